```python
import jax, jax.numpy as jnp
from jax import lax
import numpy as np

D_MODEL = 1024
BATCH = 2
SEQ = 8192
DEPTH = 2
DEC_BATCH = 128
DEC_SEQ = 1
PAST_LEN = 16384
PAGE_SIZE = 128

W_POOL = D_MODEL // 2
W_CONV = D_MODEL // 2
POOL_WINDOWS = (2, 4, 8, 16)
N_POOL_GROUPS = len(POOL_WINDOWS)
POOL_GROUP = W_POOL // N_POOL_GROUPS
POOL_BUF = max(POOL_WINDOWS) - 1
CONV_WIDTH = 3
CONV_BUF = CONV_WIDTH - 1
IN0_COLS = 2 * W_POOL + 4 * W_CONV
EVEN_SPLITS = (W_POOL, 2 * W_POOL, 2 * W_POOL + W_CONV, 2 * W_POOL + 2 * W_CONV, 2 * W_POOL + 3 * W_CONV)
N_HEADS = 16
HEAD_DIM = 64
N_KV_HEADS = 4
GROUP = N_HEADS // N_KV_HEADS
WINDOW = 128
BLOCK = 128
ROPE_THETA = 10000.0
ATTN_W = N_HEADS * HEAD_DIM
KV_W = N_KV_HEADS * HEAD_DIM
IN1_COLS = 2 * ATTN_W + 2 * KV_W
ODD_SPLITS = (ATTN_W, ATTN_W + KV_W, ATTN_W + 2 * KV_W)
RMS_EPS = 1e-6
N_EVEN = (DEPTH + 1) // 2
N_ODD = DEPTH // 2

kernel_name = "hybrid_pool_conv_swa_decoder_step"


def rms_norm(x, g):
    xf = x.astype(jnp.float32)
    y = xf * lax.rsqrt(jnp.mean(xf * xf, axis=-1, keepdims=True) + RMS_EPS)
    return (y * g.astype(jnp.float32)).astype(x.dtype)


def rope(x, pos):
    half = HEAD_DIM // 2
    inv = ROPE_THETA ** (-jnp.arange(half, dtype=jnp.float32) / half)
    ang = pos.astype(jnp.float32)[:, None] * inv[None, :]
    cos = jnp.cos(ang)[:, None, :]
    sin = jnp.sin(ang)[:, None, :]
    xf = x.astype(jnp.float32)
    x1, x2 = xf[..., :half], xf[..., half:]
    return jnp.concatenate([x1 * cos - x2 * sin, x2 * cos + x1 * sin], axis=-1).astype(x.dtype)


def pool_mix(u, prefix, pos0):
    T = u.shape[1]
    ext = jnp.concatenate([prefix, u], axis=1).astype(jnp.float32)
    cs = jnp.pad(jnp.cumsum(ext, axis=1), ((0, 0), (1, 0), (0, 0)))
    pos = pos0 + jnp.arange(T)
    ends = cs[:, POOL_BUF + 1:POOL_BUF + 1 + T]
    outs = []
    for g, w in enumerate(POOL_WINDOWS):
        lo, hi = g * POOL_GROUP, (g + 1) * POOL_GROUP
        start = cs[:, POOL_BUF + 1 - w:POOL_BUF + 1 - w + T, lo:hi]
        cnt = jnp.minimum(pos + 1, w).astype(jnp.float32)[None, :, None]
        outs.append((ends[..., lo:hi] - start) / cnt)
    mean = jnp.concatenate(outs, axis=-1)
    return (mean - u.astype(jnp.float32)).astype(u.dtype)


def short_conv(v, prefix, conv_w):
    T = v.shape[1]
    ext = jnp.concatenate([prefix, v], axis=1)
    return sum(ext[:, k:k + T] * conv_w[k] for k in range(CONV_WIDTH))


def even_layer(x, pool_prefix, conv_prefix, pos0, g_norm, w_in, pool_w, pool_scale, conv_w, w_out):
    B, T = x.shape[0], x.shape[1]
    h = rms_norm(x, g_norm)
    p = jnp.einsum('btd,de->bte', h, w_in)
    u, z_a, b_gate, c_gate, v_in, z_b = jnp.split(p, EVEN_SPLITS, axis=-1)
    d = pool_mix(u, pool_prefix, pos0).reshape(B, T, N_POOL_GROUPS, POOL_GROUP)
    y_a = jnp.einsum('btgc,gce->btge', d, pool_w).reshape(B, T, W_POOL) * pool_scale * jax.nn.silu(z_a)
    v = c_gate * v_in
    y_b = b_gate * short_conv(v, conv_prefix, conv_w) * jax.nn.silu(z_b)
    y = jnp.einsum('bte,ed->btd', jnp.concatenate([y_a, y_b], axis=-1), w_out)
    new_pool = jnp.concatenate([pool_prefix, u], axis=1)[:, -POOL_BUF:]
    new_conv = jnp.concatenate([conv_prefix, v], axis=1)[:, -CONV_BUF:]
    return x + y, new_pool, new_conv


def qkv_proj(x, g_norm, w_in, q_gain, k_gain, pos):
    B, T = x.shape[0], x.shape[1]
    h = rms_norm(x, g_norm)
    p = jnp.einsum('btd,de->bte', h, w_in)
    q, k, v, z = jnp.split(p, ODD_SPLITS, axis=-1)
    q = rope(rms_norm(q.reshape(B, T, N_HEADS, HEAD_DIM), q_gain), pos)
    k = rope(rms_norm(k.reshape(B, T, N_KV_HEADS, HEAD_DIM), k_gain), pos)
    v = v.reshape(B, T, N_KV_HEADS, HEAD_DIM)
    return q, k, v, z


def sink_softmax(s, mask, sinks):
    sk = sinks.astype(jnp.float32).reshape(N_KV_HEADS, GROUP, 1, 1)
    s = jnp.where(mask, s, -jnp.inf)
    m = jnp.maximum(jnp.max(s, axis=-1, keepdims=True), sk)
    p = jnp.exp(s - m)
    return p / (jnp.sum(p, axis=-1, keepdims=True) + jnp.exp(sk - m))


def swa_prompt(q, k, v, sinks):
    B, T = q.shape[0], q.shape[1]
    nb = T // BLOCK
    qb = q.reshape(B, nb, BLOCK, N_KV_HEADS, GROUP, HEAD_DIM)
    kb = k.reshape(B, nb, BLOCK, N_KV_HEADS, HEAD_DIM)
    vb = v.reshape(B, nb, BLOCK, N_KV_HEADS, HEAD_DIM)
    pad = ((0, 0), (1, 0), (0, 0), (0, 0), (0, 0))
    kk = jnp.concatenate([jnp.pad(kb, pad)[:, :-1], kb], axis=2)
    vv = jnp.concatenate([jnp.pad(vb, pad)[:, :-1], vb], axis=2)
    s = jnp.einsum('bnqhgd,bnkhd->bnhgqk', qb, kk, preferred_element_type=jnp.float32) * (HEAD_DIM ** -0.5)
    qi = jnp.arange(BLOCK)[:, None] + BLOCK
    ki = jnp.arange(2 * BLOCK)[None, :]
    band = (ki <= qi) & (qi - ki < WINDOW)
    valid = (jnp.arange(nb)[:, None, None] > 0) | (ki[None] >= BLOCK)
    mask = (band[None] & valid)[None, :, None, None]
    pr = sink_softmax(s, mask, sinks).astype(v.dtype)
    o = jnp.einsum('bnhgqk,bnkhd->bnqhgd', pr, vv)
    return o.reshape(B, T, ATTN_W)


def swa_sample(q, k, v, k_buf, v_buf, sinks, pos0):
    B, T = q.shape[0], q.shape[1]
    P = k_buf.shape[1]
    kk = jnp.concatenate([k_buf, k], axis=1)
    vv = jnp.concatenate([v_buf, v], axis=1)
    qg = q.reshape(B, T, N_KV_HEADS, GROUP, HEAD_DIM)
    s = jnp.einsum('bqhgd,bkhd->bhgqk', qg, kk, preferred_element_type=jnp.float32) * (HEAD_DIM ** -0.5)
    qpos = pos0 + jnp.arange(T)[:, None]
    kpos = pos0 - P + jnp.arange(P + T)[None, :]
    mask = ((kpos <= qpos) & (qpos - kpos < WINDOW))[None, None, None]
    pr = sink_softmax(s, mask, sinks).astype(v.dtype)
    o = jnp.einsum('bhgqk,bkhd->bqhgd', pr, vv).reshape(B, T, ATTN_W)
    return o, kk[:, -P:], vv[:, -P:]


def setup_inputs(seed: int = 0) -> dict:
    key = jax.random.key(seed)
    ks = jax.random.split(key, 20)
    win_buf = min(WINDOW, PAST_LEN)
    nrm = jax.random.normal
    return {
        "x_prompt": nrm(ks[0], (BATCH, SEQ, D_MODEL), jnp.float32),
        "x_sample": nrm(ks[1], (DEC_BATCH, DEC_SEQ, D_MODEL), jnp.float32),
        "state_pool": nrm(ks[2], (N_EVEN, DEC_BATCH, POOL_BUF, W_POOL), jnp.float32),
        "state_conv": nrm(ks[3], (N_EVEN, DEC_BATCH, CONV_BUF, W_CONV), jnp.float32),
        "cache_k": nrm(ks[4], (N_ODD, DEC_BATCH, win_buf, N_KV_HEADS, HEAD_DIM), jnp.float32),
        "cache_v": nrm(ks[5], (N_ODD, DEC_BATCH, win_buf, N_KV_HEADS, HEAD_DIM), jnp.float32),
        "norm_g": 1.0 + 0.02 * nrm(ks[6], (DEPTH, D_MODEL), jnp.float32),
        "w_in_even": nrm(ks[7], (N_EVEN, D_MODEL, IN0_COLS), jnp.float32) * D_MODEL ** -0.5,
        "pool_w": nrm(ks[8], (N_EVEN, N_POOL_GROUPS, POOL_GROUP, POOL_GROUP), jnp.float32) * POOL_GROUP ** -0.5,
        "pool_scale": 1.0 + 0.02 * nrm(ks[9], (N_EVEN, W_POOL), jnp.float32),
        "conv_w": nrm(ks[10], (N_EVEN, CONV_WIDTH, W_CONV), jnp.float32) * CONV_WIDTH ** -0.5,
        "w_out_even": nrm(ks[11], (N_EVEN, W_POOL + W_CONV, D_MODEL), jnp.float32) * (W_POOL + W_CONV) ** -0.5,
        "w_in_odd": nrm(ks[12], (N_ODD, D_MODEL, IN1_COLS), jnp.float32) * D_MODEL ** -0.5,
        "q_norm_g": 1.0 + 0.02 * nrm(ks[13], (N_ODD, HEAD_DIM), jnp.float32),
        "k_norm_g": 1.0 + 0.02 * nrm(ks[14], (N_ODD, HEAD_DIM), jnp.float32),
        "attn_sinks": 0.5 * nrm(ks[15], (N_ODD, N_HEADS), jnp.float32),
        "w_out_odd": nrm(ks[16], (N_ODD, ATTN_W, D_MODEL), jnp.float32) * ATTN_W ** -0.5,
    }


def reference(x_prompt, x_sample, state_pool, state_conv, cache_k, cache_v, norm_g, w_in_even, pool_w,
              pool_scale, conv_w, w_out_even, w_in_odd, q_norm_g, k_norm_g, attn_sinks, w_out_odd):
    xp, xs = x_prompt, x_sample
    Bp, Tp = xp.shape[0], xp.shape[1]
    Ts = xs.shape[1]
    win_buf = cache_k.shape[2]
    pos_p = jnp.arange(Tp)
    pos_s = PAST_LEN + jnp.arange(Ts)
    pool_p, pool_s, conv_p, conv_s = [], [], [], []
    kp_l, vp_l, ks_l, vs_l = [], [], [], []
    for layer in range(DEPTH):
        i = layer // 2
        if layer % 2 == 0:
            wts = (norm_g[layer], w_in_even[i], pool_w[i], pool_scale[i], conv_w[i], w_out_even[i])
            zp = jnp.zeros((Bp, POOL_BUF, W_POOL), xp.dtype)
            zc = jnp.zeros((Bp, CONV_BUF, W_CONV), xp.dtype)
            xp, np_, nc_ = even_layer(xp, zp, zc, 0, *wts)
            xs, ns_, ncs_ = even_layer(xs, state_pool[i], state_conv[i], PAST_LEN, *wts)
            pool_p.append(np_); conv_p.append(nc_); pool_s.append(ns_); conv_s.append(ncs_)
        else:
            q, k, v, z = qkv_proj(xp, norm_g[layer], w_in_odd[i], q_norm_g[i], k_norm_g[i], pos_p)
            o = swa_prompt(q, k, v, attn_sinks[i])
            xp = xp + jnp.einsum('bte,ed->btd', o * jax.nn.silu(z), w_out_odd[i])
            kp_l.append(k[:, -win_buf:]); vp_l.append(v[:, -win_buf:])
            q, k, v, z = qkv_proj(xs, norm_g[layer], w_in_odd[i], q_norm_g[i], k_norm_g[i], pos_s)
            o, nk, nv = swa_sample(q, k, v, cache_k[i], cache_v[i], attn_sinks[i], PAST_LEN)
            xs = xs + jnp.einsum('bte,ed->btd', o * jax.nn.silu(z), w_out_odd[i])
            ks_l.append(nk); vs_l.append(nv)
    return (xp, xs, jnp.stack(pool_p), jnp.stack(pool_s), jnp.stack(conv_p), jnp.stack(conv_s),
            jnp.stack(kp_l), jnp.stack(vp_l), jnp.stack(ks_l), jnp.stack(vs_l))
```

```python
import functools

import jax
import jax.numpy as jnp
from jax import lax
from jax.experimental import pallas as pl
from jax.experimental.pallas import tpu as pltpu

F32 = jnp.float32
BF16 = jnp.bfloat16

POOL_WINDOWS = (2, 4, 8, 16)
POOL_GROUP = 128
POOL_BUF = max(POOL_WINDOWS) - 1
CONV_WIDTH = 3
CONV_BUF = CONV_WIDTH - 1
N_HEADS = 16
HEAD_DIM = 64
N_KV_HEADS = 4
GROUP = N_HEADS // N_KV_HEADS
WINDOW = 128
ROPE_THETA = 10000.0
RMS_EPS = 1e-6
PAST_LEN = 16384
ATTN_W = N_HEADS * HEAD_DIM
KV_W = N_KV_HEADS * HEAD_DIM

LANES = 128
POOL_HALO = 16
CONV_HALO = 8
VMEM_LIMIT_BYTES = 56 * 1024 * 1024


def _rms(x, g):
    ms = jnp.mean(x * x, axis=-1, keepdims=True)
    return x * lax.rsqrt(ms + RMS_EPS) * g


def _silu(z):
    return z * jax.nn.sigmoid(z)


def _head_norm_rope(x, gain, cos, sin):
    lane = lax.broadcasted_iota(jnp.int32, x.shape, 1)
    first = lane < HEAD_DIM
    x2 = x * x
    ss0 = jnp.sum(jnp.where(first, x2, 0.0), axis=-1, keepdims=True)
    ss1 = jnp.sum(jnp.where(first, 0.0, x2), axis=-1, keepdims=True)
    r = jnp.where(first, lax.rsqrt(ss0 * (1.0 / HEAD_DIM) + RMS_EPS), lax.rsqrt(ss1 * (1.0 / HEAD_DIM) + RMS_EPS))
    xn = x * r * gain
    half = HEAD_DIM // 2
    swapped = jnp.where((lane % HEAD_DIM) < half, pltpu.roll(xn, LANES - half, 1), pltpu.roll(xn, half, 1))
    return xn * cos + swapped * sin


def _even_prompt_kernel(x_ref, g_ref, win_ref, pw_ref, ps_ref, cw_ref, wout_ref,
                        o_ref, pool_ref, conv_ref, uext, vext, sa, sb, y_scr, *, tm):
    t = pl.program_id(1)
    w_pool = uext.shape[1]
    w_conv = vext.shape[1]
    base = 2 * POOL_HALO

    @pl.when(t == 0)
    def _():
        uext[0:base, :] = jnp.zeros((base, w_pool), F32)
        vext[0:CONV_HALO, :] = jnp.zeros((CONV_HALO, w_conv), F32)
        sa[0:POOL_HALO, :] = jnp.zeros((POOL_HALO, POOL_GROUP), F32)
        sb[0:POOL_HALO, :] = jnp.zeros((POOL_HALO, POOL_GROUP), F32)

    x = x_ref[0]
    h = _rms(x, g_ref[...]).astype(BF16)

    def proj(c, width):
        return jnp.dot(h, win_ref[:, c:c + width], preferred_element_type=F32)

    u = proj(0, w_pool)
    uext[base:base + tm, :] = u
    pos = t * tm + lax.broadcasted_iota(jnp.int32, (tm, 1), 0)
    z_a = proj(w_pool, w_pool)
    n_ext = POOL_HALO + tm

    for g, w in enumerate(POOL_WINDOWS):
        lo = g * POOL_GROUP
        cols = slice(lo, lo + POOL_GROUP)
        src = uext
        src_cols = cols
        step = 1
        bufs = (sa, sb)
        nbuf = 0
        while 2 * step < w:
            dst = bufs[nbuf % 2]
            dst[POOL_HALO:POOL_HALO + n_ext, :] = (src[POOL_HALO:POOL_HALO + n_ext, src_cols]
                                                   + src[POOL_HALO - step:POOL_HALO - step + n_ext, src_cols])
            src, src_cols = dst, slice(0, POOL_GROUP)
            step *= 2
            nbuf += 1
        wsum = src[base:base + tm, src_cols] + src[base - step:base - step + tm, src_cols]
        cnt = jnp.minimum(pos + 1, w).astype(F32)
        d = wsum * (1.0 / cnt) - u[:, cols]
        ya = jnp.dot(d.astype(BF16), pw_ref[g], preferred_element_type=F32)
        ya = ya * ps_ref[:, cols] * _silu(z_a[:, cols])
        y_scr[:, cols] = ya.astype(BF16)

    b_gate = proj(2 * w_pool, w_conv)
    c_gate = proj(2 * w_pool + w_conv, w_conv)
    v_in = proj(2 * w_pool + 2 * w_conv, w_conv)
    v = c_gate * v_in
    vext[CONV_HALO:CONV_HALO + tm, :] = v
    conv = v * cw_ref[CONV_WIDTH - 1:CONV_WIDTH, :]
    for k in range(CONV_WIDTH - 1):
        shift = CONV_WIDTH - 1 - k
        conv = conv + vext[CONV_HALO - shift:CONV_HALO - shift + tm, :] * cw_ref[k:k + 1, :]
    z_b = proj(2 * w_pool + 3 * w_conv, w_conv)
    y_b = b_gate * conv * _silu(z_b)
    y_scr[:, w_pool:w_pool + w_conv] = y_b.astype(BF16)

    o_ref[0] = x + jnp.dot(y_scr[...], wout_ref[...], preferred_element_type=F32)
    pool_ref[0] = uext[base + tm - POOL_BUF:base + tm, :]
    conv_ref[0] = vext[CONV_HALO + tm - CONV_BUF:CONV_HALO + tm, :]
    uext[POOL_HALO:base, :] = uext[POOL_HALO + tm:base + tm, :]
    vext[0:CONV_HALO, :] = vext[tm:tm + CONV_HALO, :]


def _even_prompt(x, g, win, pw, ps, cw, wout, *, tm):
    B, T, D = x.shape
    w_pool = ps.shape[-1]
    w_conv = cw.shape[-1]
    assert T % tm == 0 and tm % 16 == 0 and tm >= POOL_HALO
    const = lambda *shape: pl.BlockSpec(shape, lambda b, t: (0,) * len(shape))
    return pl.pallas_call(
        functools.partial(_even_prompt_kernel, tm=tm),
        grid=(B, T // tm),
        in_specs=[
            pl.BlockSpec((1, tm, D), lambda b, t: (b, t, 0)),
            const(1, D),
            const(*win.shape),
            const(*pw.shape),
            const(1, w_pool),
            const(*cw.shape),
            const(*wout.shape),
        ],
        out_specs=[
            pl.BlockSpec((1, tm, D), lambda b, t: (b, t, 0)),
            pl.BlockSpec((1, POOL_BUF, w_pool), lambda b, t: (b, 0, 0)),
            pl.BlockSpec((1, CONV_BUF, w_conv), lambda b, t: (b, 0, 0)),
        ],
        out_shape=[
            jax.ShapeDtypeStruct((B, T, D), F32),
            jax.ShapeDtypeStruct((B, POOL_BUF, w_pool), F32),
            jax.ShapeDtypeStruct((B, CONV_BUF, w_conv), F32),
        ],
        scratch_shapes=[
            pltpu.VMEM((2 * POOL_HALO + tm, w_pool), F32),
            pltpu.VMEM((CONV_HALO + tm, w_conv), F32),
            pltpu.VMEM((2 * POOL_HALO + tm, POOL_GROUP), F32),
            pltpu.VMEM((2 * POOL_HALO + tm, POOL_GROUP), F32),
            pltpu.VMEM((tm, w_pool + w_conv), BF16),
        ],
        compiler_params=pltpu.CompilerParams(
            dimension_semantics=("arbitrary", "arbitrary"), vmem_limit_bytes=VMEM_LIMIT_BYTES),
        name="even_prompt",
    )(x, g, win, pw, ps, cw, wout)


def _odd_prompt_kernel(x_ref, g_ref, win_ref, qg_ref, kg_ref, cos_ref, sin_ref, sink_ref, wout_ref,
                       o_ref, knew_ref, vnew_ref, q_scr, kext, vext, gate_scr, y_scr, *, tm):
    t = pl.program_id(1)
    nblk = tm // WINDOW

    @pl.when(t == 0)
    def _():
        kext[0:WINDOW, :] = jnp.zeros((WINDOW, KV_W), BF16)
        vext[0:WINDOW, :] = jnp.zeros((WINDOW, KV_W), BF16)

    x = x_ref[0]
    h = _rms(x, g_ref[...]).astype(BF16)
    cos = cos_ref[...]
    sin = sin_ref[...]

    for c in range(ATTN_W // LANES):
        qc = jnp.dot(h, win_ref[:, c * LANES:(c + 1) * LANES], preferred_element_type=F32)
        qc = _head_norm_rope(qc, qg_ref[...], cos, sin) * (HEAD_DIM ** -0.5)
        q_scr[:, c * LANES:(c + 1) * LANES] = qc.astype(BF16)
    for c in range(KV_W // LANES):
        kc = jnp.dot(h, win_ref[:, ATTN_W + c * LANES:ATTN_W + (c + 1) * LANES], preferred_element_type=F32)
        kc = _head_norm_rope(kc, kg_ref[...], cos, sin)
        knew_ref[0, :, c * LANES:(c + 1) * LANES] = kc[tm - WINDOW:tm, :]
        kext[WINDOW:WINDOW + tm, c * LANES:(c + 1) * LANES] = kc.astype(BF16)
    v = jnp.dot(h, win_ref[:, ATTN_W + KV_W:ATTN_W + 2 * KV_W], preferred_element_type=F32)
    vnew_ref[0] = v[tm - WINDOW:tm, :]
    vext[WINDOW:WINDOW + tm, :] = v.astype(BF16)
    z = jnp.dot(h, win_ref[:, ATTN_W + 2 * KV_W:2 * ATTN_W + 2 * KV_W], preferred_element_type=F32)
    gate_scr[...] = _silu(z)

    qi = lax.broadcasted_iota(jnp.int32, (WINDOW, 2 * WINDOW), 0) + WINDOW
    ki = lax.broadcasted_iota(jnp.int32, (WINDOW, 2 * WINDOW), 1)
    band = (ki <= qi) & (qi - ki < WINDOW)

    def block(i, carry):
        r0 = pl.multiple_of(i * WINDOW, WINDOW)
        first_key = jnp.where((t == 0) & (i == 0), WINDOW, 0)
        mask = band & (ki >= first_key)
        for hd in range(N_HEADS):
            kv = hd // GROUP
            hs = slice(hd * HEAD_DIM, (hd + 1) * HEAD_DIM)
            ks = slice(kv * HEAD_DIM, (kv + 1) * HEAD_DIM)
            if hd % GROUP == 0:
                kblk = kext[pl.ds(r0, 2 * WINDOW), ks]
                vblk = vext[pl.ds(r0, 2 * WINDOW), ks]
            qh = q_scr[pl.ds(r0, WINDOW), hs]
            s = lax.dot_general(qh, kblk, (((1,), (1,)), ((), ())), preferred_element_type=F32)
            s = jnp.where(mask, s, -jnp.inf)
            sink = sink_ref[hd]
            m = jnp.maximum(jnp.max(s, axis=-1, keepdims=True), sink)
            p = jnp.exp(s - m)
            denom = jnp.sum(p, axis=-1, keepdims=True) + jnp.exp(sink - m)
            o = jnp.dot(p.astype(BF16), vblk, preferred_element_type=F32) * (1.0 / denom)
            y_scr[pl.ds(r0, WINDOW), hs] = (o * gate_scr[pl.ds(r0, WINDOW), hs]).astype(BF16)
        return carry

    lax.fori_loop(0, nblk, block, 0)

    o_ref[0] = x + jnp.dot(y_scr[...], wout_ref[...], preferred_element_type=F32)
    kext[0:WINDOW, :] = kext[tm:tm + WINDOW, :]
    vext[0:WINDOW, :] = vext[tm:tm + WINDOW, :]


def _odd_prompt(x, g, win, qg, kg, cos, sin, sinks, wout, *, tm):
    B, T, D = x.shape
    assert T % tm == 0 and tm % WINDOW == 0
    const = lambda *shape: pl.BlockSpec(shape, lambda b, t: (0,) * len(shape))
    return pl.pallas_call(
        functools.partial(_odd_prompt_kernel, tm=tm),
        grid=(B, T // tm),
        in_specs=[
            pl.BlockSpec((1, tm, D), lambda b, t: (b, t, 0)),
            const(1, D),
            const(*win.shape),
            const(1, LANES),
            const(1, LANES),
            pl.BlockSpec((tm, LANES), lambda b, t: (t, 0)),
            pl.BlockSpec((tm, LANES), lambda b, t: (t, 0)),
            pl.BlockSpec(memory_space=pltpu.SMEM),
            const(*wout.shape),
        ],
        out_specs=[
            pl.BlockSpec((1, tm, D), lambda b, t: (b, t, 0)),
            pl.BlockSpec((1, WINDOW, KV_W), lambda b, t: (b, 0, 0)),
            pl.BlockSpec((1, WINDOW, KV_W), lambda b, t: (b, 0, 0)),
        ],
        out_shape=[
            jax.ShapeDtypeStruct((B, T, D), F32),
            jax.ShapeDtypeStruct((B, WINDOW, KV_W), F32),
            jax.ShapeDtypeStruct((B, WINDOW, KV_W), F32),
        ],
        scratch_shapes=[
            pltpu.VMEM((tm, ATTN_W), BF16),
            pltpu.VMEM((WINDOW + tm, KV_W), BF16),
            pltpu.VMEM((WINDOW + tm, KV_W), BF16),
            pltpu.VMEM((tm, ATTN_W), F32),
            pltpu.VMEM((tm, ATTN_W), BF16),
        ],
        compiler_params=pltpu.CompilerParams(
            dimension_semantics=("arbitrary", "arbitrary"), vmem_limit_bytes=VMEM_LIMIT_BYTES),
        name="odd_prompt",
    )(x, g, win, qg, kg, cos, sin, sinks, wout)


def _sample_dense_kernel(x_ref, sp_ref, sc_ref, g0_ref, win0_ref, pw_ref, ps_ref, cw_ref, wout0_ref,
                         g1_ref, win1_ref, qg_ref, kg_ref, cos_ref, sin_ref,
                         x1_ref, pool_ref, conv_ref, q_ref, k_ref, v_ref, gate_ref):
    w_pool = ps_ref.shape[1]
    w_conv = cw_ref.shape[1]

    x = x_ref[...]
    h = _rms(x, g0_ref[...]).astype(BF16)

    def proj(c, width):
        return jnp.dot(h, win0_ref[:, c:c + width], preferred_element_type=F32)

    u = proj(0, w_pool)
    z_a = proj(w_pool, w_pool)
    ys = []
    for g, w in enumerate(POOL_WINDOWS):
        cols = slice(g * POOL_GROUP, (g + 1) * POOL_GROUP)
        wsum = u[:, cols]
        for r in range(POOL_BUF - (w - 1), POOL_BUF):
            wsum = wsum + sp_ref[r, :, cols]
        cnt = float(min(PAST_LEN + 1, w))
        d = wsum * (1.0 / cnt) - u[:, cols]
        ya = jnp.dot(d.astype(BF16), pw_ref[g], preferred_element_type=F32)
        ys.append((ya * ps_ref[:, cols] * _silu(z_a[:, cols])).astype(BF16))
    for r in range(POOL_BUF - 1):
        pool_ref[r] = sp_ref[r + 1]
    pool_ref[POOL_BUF - 1] = u

    b_gate = proj(2 * w_pool, w_conv)
    c_gate = proj(2 * w_pool + w_conv, w_conv)
    v_in = proj(2 * w_pool + 2 * w_conv, w_conv)
    v = c_gate * v_in
    conv = v * cw_ref[CONV_WIDTH - 1:CONV_WIDTH, :]
    for k in range(CONV_BUF):
        conv = conv + sc_ref[k] * cw_ref[k:k + 1, :]
    for r in range(CONV_BUF - 1):
        conv_ref[r] = sc_ref[r + 1]
    conv_ref[CONV_BUF - 1] = v
    z_b = proj(2 * w_pool + 3 * w_conv, w_conv)
    ys.append((b_gate * conv * _silu(z_b)).astype(BF16))
    y = jnp.concatenate(ys, axis=1)
    x1 = x + jnp.dot(y, wout0_ref[...], preferred_element_type=F32)
    x1_ref[...] = x1

    h1 = _rms(x1, g1_ref[...]).astype(BF16)
    cos = cos_ref[...]
    sin = sin_ref[...]
    for c in range(ATTN_W // LANES):
        qc = jnp.dot(h1, win1_ref[:, c * LANES:(c + 1) * LANES], preferred_element_type=F32)
        q_ref[:, c * LANES:(c + 1) * LANES] = _head_norm_rope(qc, qg_ref[...], cos, sin) * (HEAD_DIM ** -0.5)
    for c in range(KV_W // LANES):
        kc = jnp.dot(h1, win1_ref[:, ATTN_W + c * LANES:ATTN_W + (c + 1) * LANES], preferred_element_type=F32)
        k_ref[:, c * LANES:(c + 1) * LANES] = _head_norm_rope(kc, kg_ref[...], cos, sin)
    v_ref[...] = jnp.dot(h1, win1_ref[:, ATTN_W + KV_W:ATTN_W + 2 * KV_W], preferred_element_type=F32)
    z = jnp.dot(h1, win1_ref[:, ATTN_W + 2 * KV_W:2 * ATTN_W + 2 * KV_W], preferred_element_type=F32)
    gate_ref[...] = _silu(z)


def _sample_dense(x, sp, sc, g0, win0, pw, ps, cw, wout0, g1, win1, qg, kg, cos, sin):
    n, D = x.shape
    vmem = pl.BlockSpec(memory_space=pltpu.VMEM)
    return pl.pallas_call(
        _sample_dense_kernel,
        in_specs=[vmem] * 15,
        out_specs=[vmem] * 7,
        out_shape=[
            jax.ShapeDtypeStruct((n, D), F32),
            jax.ShapeDtypeStruct(sp.shape, F32),
            jax.ShapeDtypeStruct(sc.shape, F32),
            jax.ShapeDtypeStruct((n, ATTN_W), F32),
            jax.ShapeDtypeStruct((n, KV_W), F32),
            jax.ShapeDtypeStruct((n, KV_W), F32),
            jax.ShapeDtypeStruct((n, ATTN_W), F32),
        ],
        compiler_params=pltpu.CompilerParams(vmem_limit_bytes=VMEM_LIMIT_BYTES),
        name="sample_dense",
    )(x, sp, sc, g0, win0, pw, ps, cw, wout0, g1, win1, qg, kg, cos, sin)


def _dup_heads(x):
    lane = lax.broadcasted_iota(jnp.int32, x.shape, 1)
    rolled = pltpu.roll(x, HEAD_DIM, 1)
    first = lane < HEAD_DIM
    return jnp.where(first, x, rolled), jnp.where(first, rolled, x)


def _expand_kv(x):
    chunks = []
    for c in range(KV_W // LANES):
        d0, d1 = _dup_heads(x[:, c * LANES:(c + 1) * LANES])
        chunks += [d0] * (GROUP // 2) + [d1] * (GROUP // 2)
    return chunks


def _sample_attn_kernel(q_ref, kn_ref, vn_ref, ck_ref, cv_ref, sink_ref, seg_ref, segt_ref, gate_ref, x1_ref,
                        wout_ref, nk_ref, nv_ref, y_ref, o_scr, *, bb):
    step = pl.program_id(0)
    win = ck_ref.shape[1]
    seg = seg_ref[...]
    segt = segt_ref[...]
    sink = sink_ref[...]
    nchunk = ATTN_W // LANES

    o_rows = []
    for i in range(bb):
        nk_ref[i, 0:win - 1, :] = ck_ref[i, 1:win, :]
        nk_ref[i, win - 1:win, :] = kn_ref[i:i + 1, :]
        nv_ref[i, 0:win - 1, :] = cv_ref[i, 1:win, :]
        nv_ref[i, win - 1:win, :] = vn_ref[i:i + 1, :]
        kexp = _expand_kv(nk_ref[i])
        prod = jnp.concatenate(
            [(kexp[c] * q_ref[i:i + 1, c * LANES:(c + 1) * LANES]).astype(BF16) for c in range(nchunk)], axis=1)
        s = jnp.dot(prod, seg, preferred_element_type=F32)
        m = jnp.maximum(jnp.max(s, axis=0, keepdims=True), sink)
        p = jnp.exp(s - m)
        denom = jnp.sum(p, axis=0, keepdims=True) + jnp.exp(sink - m)
        pexp = jnp.dot((p * (1.0 / denom)).astype(BF16), segt, preferred_element_type=F32)
        vexp = _expand_kv(nv_ref[i])
        o_rows.append(jnp.concatenate(
            [jnp.sum(pexp[:, c * LANES:(c + 1) * LANES] * vexp[c], axis=0, keepdims=True) for c in range(nchunk)],
            axis=1))
    o_scr[pl.ds(pl.multiple_of(step * bb, bb), bb), :] = jnp.concatenate(o_rows, axis=0)

    @pl.when(step == pl.num_programs(0) - 1)
    def _():
        y = (o_scr[...] * gate_ref[...]).astype(BF16)
        y_ref[...] = x1_ref[...] + jnp.dot(y, wout_ref[...], preferred_element_type=F32)


def _sample_attn(q, kn, vn, ck, cv, sink_row, seg, segt, gate, x1, wout, *, bb):
    n, win, kvw = ck.shape
    D = x1.shape[1]
    assert n % bb == 0 and bb % 8 == 0
    const = lambda *shape: pl.BlockSpec(shape, lambda s: (0,) * len(shape))
    return pl.pallas_call(
        functools.partial(_sample_attn_kernel, bb=bb),
        grid=(n // bb,),
        in_specs=[
            pl.BlockSpec((bb, ATTN_W), lambda s: (s, 0)),
            pl.BlockSpec((bb, kvw), lambda s: (s, 0)),
            pl.BlockSpec((bb, kvw), lambda s: (s, 0)),
            pl.BlockSpec((bb, win, kvw), lambda s: (s, 0, 0)),
            pl.BlockSpec((bb, win, kvw), lambda s: (s, 0, 0)),
            const(1, LANES),
            const(*seg.shape),
            const(*segt.shape),
            const(n, ATTN_W),
            const(n, D),
            const(*wout.shape),
        ],
        out_specs=[
            pl.BlockSpec((bb, win, kvw), lambda s: (s, 0, 0)),
            pl.BlockSpec((bb, win, kvw), lambda s: (s, 0, 0)),
            const(n, D),
        ],
        out_shape=[
            jax.ShapeDtypeStruct((n, win, kvw), F32),
            jax.ShapeDtypeStruct((n, win, kvw), F32),
            jax.ShapeDtypeStruct((n, D), F32),
        ],
        scratch_shapes=[pltpu.VMEM((n, ATTN_W), F32)],
        compiler_params=pltpu.CompilerParams(
            dimension_semantics=("arbitrary",), vmem_limit_bytes=VMEM_LIMIT_BYTES),
        name="sample_attn",
    )(q, kn, vn, ck, cv, sink_row, seg, segt, gate, x1, wout)


def _rope_tables(pos):
    half = HEAD_DIM // 2
    inv = ROPE_THETA ** (-jnp.arange(half, dtype=F32) / half)
    ang = pos.astype(F32)[:, None] * inv[None, :]
    cos, sin = jnp.cos(ang), jnp.sin(ang)
    reps = LANES // HEAD_DIM
    return jnp.tile(jnp.concatenate([cos, cos], axis=1), (1, reps)), jnp.tile(jnp.concatenate([-sin, sin], axis=1), (1, reps))


def kernel(x_prompt, x_sample, state_pool, state_conv, cache_k, cache_v, norm_g, w_in_even, pool_w, pool_scale,
           conv_w, w_out_even, w_in_odd, q_norm_g, k_norm_g, attn_sinks, w_out_odd):
    B, T, D = x_prompt.shape
    n_s, t_s, _ = x_sample.shape
    assert norm_g.shape[0] == 2 and w_in_even.shape[0] == 1 and w_in_odd.shape[0] == 1
    assert t_s == 1 and cache_k.shape[2] == WINDOW and T >= WINDOW
    assert cache_k.shape[3] * cache_k.shape[4] == KV_W and pool_w.shape[1:] == (len(POOL_WINDOWS), POOL_GROUP, POOL_GROUP)

    g0 = norm_g[0][None, :]
    g1 = norm_g[1][None, :]
    win0 = w_in_even[0].astype(BF16)
    wout0 = w_out_even[0].astype(BF16)
    win1 = w_in_odd[0].astype(BF16)
    wout1 = w_out_odd[0].astype(BF16)
    pw = pool_w[0].astype(BF16)
    ps = pool_scale[0][None, :]
    cw = conv_w[0]
    reps = LANES // HEAD_DIM
    qg = jnp.tile(q_norm_g[0], reps)[None, :]
    kg = jnp.tile(k_norm_g[0], reps)[None, :]
    sinks = attn_sinks[0]

    cos_p, sin_p = _rope_tables(jnp.arange(T))
    cos_s, sin_s = _rope_tables(PAST_LEN + jnp.arange(t_s))

    x1p, pool_p, conv_p = _even_prompt(x_prompt, g0, win0, pw, ps, cw, wout0, tm=512)
    y_p, k_p, v_p = _odd_prompt(x1p, g1, win1, qg, kg, cos_p, sin_p, sinks, wout1, tm=512)

    sp = jnp.transpose(state_pool[0], (1, 0, 2))
    sc = jnp.transpose(state_conv[0], (1, 0, 2))
    x1s, pool_s, conv_s, q_s, k_s, v_s, gate_s = _sample_dense(
        x_sample[:, 0, :], sp, sc, g0, win0, pw, ps, cw, wout0, g1, win1, qg, kg, cos_s, sin_s)
    pool_s = jnp.transpose(pool_s, (1, 0, 2))
    conv_s = jnp.transpose(conv_s, (1, 0, 2))

    col = jnp.arange(ATTN_W) // HEAD_DIM
    seg = (col[:, None] == jnp.arange(LANES)[None, :]).astype(BF16)
    segt = seg.T
    sink_row = jnp.pad(sinks, (0, LANES - N_HEADS))[None, :]
    ck = cache_k[0].reshape(n_s, WINDOW, KV_W)
    cv = cache_v[0].reshape(n_s, WINDOW, KV_W)
    nk_s, nv_s, y_s = _sample_attn(q_s, k_s, v_s, ck, cv, sink_row, seg, segt, gate_s, x1s, wout1, bb=8)

    kv_shape = (1, -1, WINDOW, N_KV_HEADS, HEAD_DIM)
    return (y_p, y_s[:, None, :], pool_p[None], pool_s[None], conv_p[None], conv_s[None],
            k_p.reshape(kv_shape), v_p.reshape(kv_shape), nk_s.reshape(kv_shape), nv_s.reshape(kv_shape))
```

```python
import functools

import jax
import jax.numpy as jnp
from jax import lax
from jax.experimental import pallas as pl
from jax.experimental.pallas import tpu as pltpu

F32 = jnp.float32
BF16 = jnp.bfloat16

POOL_WINDOWS = (2, 4, 8, 16)
POOL_GROUP = 128
POOL_BUF = max(POOL_WINDOWS) - 1
CONV_WIDTH = 3
CONV_BUF = CONV_WIDTH - 1
N_HEADS = 16
HEAD_DIM = 64
N_KV_HEADS = 4
GROUP = N_HEADS // N_KV_HEADS
WINDOW = 128
ROPE_THETA = 10000.0
RMS_EPS = 1e-6
PAST_LEN = 16384
ATTN_W = N_HEADS * HEAD_DIM
KV_W = N_KV_HEADS * HEAD_DIM

LANES = 128
POOL_HALO = 16
CONV_HALO = 8
VMEM_LIMIT_BYTES = 56 * 1024 * 1024


def _rms(x, g):
    ms = jnp.mean(x * x, axis=-1, keepdims=True)
    return x * lax.rsqrt(ms + RMS_EPS) * g


def _silu(z):
    return z * jax.nn.sigmoid(z)


def _head_norm_rope(x, gain, cos, sin):
    lane = lax.broadcasted_iota(jnp.int32, x.shape, 1)
    first = lane < HEAD_DIM
    x2 = x * x
    ss0 = jnp.sum(jnp.where(first, x2, 0.0), axis=-1, keepdims=True)
    ss1 = jnp.sum(jnp.where(first, 0.0, x2), axis=-1, keepdims=True)
    r = jnp.where(first, lax.rsqrt(ss0 * (1.0 / HEAD_DIM) + RMS_EPS), lax.rsqrt(ss1 * (1.0 / HEAD_DIM) + RMS_EPS))
    xn = x * r * gain
    half = HEAD_DIM // 2
    swapped = jnp.where((lane % HEAD_DIM) < half, pltpu.roll(xn, LANES - half, 1), pltpu.roll(xn, half, 1))
    return xn * cos + swapped * sin


def _even_prompt_kernel(x_ref, g_ref, win_ref, pw_ref, ps_ref, cw_ref, wout_ref,
                        o_ref, pool_ref, conv_ref, uext, vext, sa, sb, y_scr, *, tm):
    t = pl.program_id(1)
    w_pool = uext.shape[1]
    w_conv = vext.shape[1]
    base = 2 * POOL_HALO

    @pl.when(t == 0)
    def _():
        uext[0:base, :] = jnp.zeros((base, w_pool), F32)
        vext[0:CONV_HALO, :] = jnp.zeros((CONV_HALO, w_conv), F32)
        sa[0:POOL_HALO, :] = jnp.zeros((POOL_HALO, POOL_GROUP), F32)
        sb[0:POOL_HALO, :] = jnp.zeros((POOL_HALO, POOL_GROUP), F32)

    x = x_ref[0]
    h = _rms(x, g_ref[...]).astype(BF16)

    def proj(c, width):
        return jnp.dot(h, win_ref[:, c:c + width], preferred_element_type=F32)

    u = proj(0, w_pool)
    uext[base:base + tm, :] = u
    pos = t * tm + lax.broadcasted_iota(jnp.int32, (tm, 1), 0)
    z_a = proj(w_pool, w_pool)
    n_ext = POOL_HALO + tm

    for g, w in enumerate(POOL_WINDOWS):
        lo = g * POOL_GROUP
        cols = slice(lo, lo + POOL_GROUP)
        src = uext
        src_cols = cols
        step = 1
        bufs = (sa, sb)
        nbuf = 0
        while 2 * step < w:
            dst = bufs[nbuf % 2]
            dst[POOL_HALO:POOL_HALO + n_ext, :] = (src[POOL_HALO:POOL_HALO + n_ext, src_cols]
                                                   + src[POOL_HALO - step:POOL_HALO - step + n_ext, src_cols])
            src, src_cols = dst, slice(0, POOL_GROUP)
            step *= 2
            nbuf += 1
        wsum = src[base:base + tm, src_cols] + src[base - step:base - step + tm, src_cols]
        cnt = jnp.minimum(pos + 1, w).astype(F32)
        d = wsum * (1.0 / cnt) - u[:, cols]
        ya = jnp.dot(d.astype(BF16), pw_ref[g], preferred_element_type=F32)
        ya = ya * ps_ref[:, cols] * _silu(z_a[:, cols])
        y_scr[:, cols] = ya.astype(BF16)

    b_gate = proj(2 * w_pool, w_conv)
    c_gate = proj(2 * w_pool + w_conv, w_conv)
    v_in = proj(2 * w_pool + 2 * w_conv, w_conv)
    v = c_gate * v_in
    vext[CONV_HALO:CONV_HALO + tm, :] = v
    conv = v * cw_ref[CONV_WIDTH - 1:CONV_WIDTH, :]
    for k in range(CONV_WIDTH - 1):
        shift = CONV_WIDTH - 1 - k
        conv = conv + vext[CONV_HALO - shift:CONV_HALO - shift + tm, :] * cw_ref[k:k + 1, :]
    z_b = proj(2 * w_pool + 3 * w_conv, w_conv)
    y_b = b_gate * conv * _silu(z_b)
    y_scr[:, w_pool:w_pool + w_conv] = y_b.astype(BF16)

    o_ref[0] = x + jnp.dot(y_scr[...], wout_ref[...], preferred_element_type=F32)
    pool_ref[0] = uext[base + tm - POOL_BUF:base + tm, :]
    conv_ref[0] = vext[CONV_HALO + tm - CONV_BUF:CONV_HALO + tm, :]
    uext[POOL_HALO:base, :] = uext[POOL_HALO + tm:base + tm, :]
    vext[0:CONV_HALO, :] = vext[tm:tm + CONV_HALO, :]


def _even_prompt(x, g, win, pw, ps, cw, wout, *, tm):
    B, T, D = x.shape
    w_pool = ps.shape[-1]
    w_conv = cw.shape[-1]
    assert T % tm == 0 and tm % 16 == 0 and tm >= POOL_HALO
    const = lambda *shape: pl.BlockSpec(shape, lambda b, t: (0,) * len(shape))
    return pl.pallas_call(
        functools.partial(_even_prompt_kernel, tm=tm),
        grid=(B, T // tm),
        in_specs=[
            pl.BlockSpec((1, tm, D), lambda b, t: (b, t, 0)),
            const(1, D),
            const(*win.shape),
            const(*pw.shape),
            const(1, w_pool),
            const(*cw.shape),
            const(*wout.shape),
        ],
        out_specs=[
            pl.BlockSpec((1, tm, D), lambda b, t: (b, t, 0)),
            pl.BlockSpec((1, POOL_BUF, w_pool), lambda b, t: (b, 0, 0)),
            pl.BlockSpec((1, CONV_BUF, w_conv), lambda b, t: (b, 0, 0)),
        ],
        out_shape=[
            jax.ShapeDtypeStruct((B, T, D), F32),
            jax.ShapeDtypeStruct((B, POOL_BUF, w_pool), F32),
            jax.ShapeDtypeStruct((B, CONV_BUF, w_conv), F32),
        ],
        scratch_shapes=[
            pltpu.VMEM((2 * POOL_HALO + tm, w_pool), F32),
            pltpu.VMEM((CONV_HALO + tm, w_conv), F32),
            pltpu.VMEM((2 * POOL_HALO + tm, POOL_GROUP), F32),
            pltpu.VMEM((2 * POOL_HALO + tm, POOL_GROUP), F32),
            pltpu.VMEM((tm, w_pool + w_conv), BF16),
        ],
        compiler_params=pltpu.CompilerParams(
            dimension_semantics=("arbitrary", "arbitrary"), vmem_limit_bytes=VMEM_LIMIT_BYTES),
        name="even_prompt",
    )(x, g, win, pw, ps, cw, wout)


def _nt_dot(a, b):
    return lax.dot_general(a, b, (((1,), (1,)), ((), ())), preferred_element_type=F32)


def _odd_prompt_kernel(x_ref, g_ref, wk_ref, wqvz_ref, qg_ref, kg_ref, cos_ref, sin_ref, cost_ref, sint_ref,
                       sink_ref, wout_ref, o_ref, knew_ref, vnew_ref,
                       qt_scr, kext, vt_ext, gate_scr, yt_scr, s_scr, p_scr, inv_scr, *, tm):
    t = pl.program_id(1)
    nblk = tm // WINDOW
    half = HEAD_DIM // 2

    @pl.when(t == 0)
    def _():
        kext[0:WINDOW, :] = jnp.zeros((WINDOW, KV_W), BF16)
        vt_ext[:, 0:WINDOW] = jnp.zeros((KV_W, WINDOW), BF16)

    x = x_ref[0]
    h = _rms(x, g_ref[...]).astype(BF16)

    for c in range(KV_W // LANES):
        kc = jnp.dot(h, wk_ref[:, c * LANES:(c + 1) * LANES], preferred_element_type=F32)
        kc = _head_norm_rope(kc, kg_ref[...], cos_ref[...], sin_ref[...])
        knew_ref[0, :, c * LANES:(c + 1) * LANES] = kc[tm - WINDOW:tm, :]
        kext[WINDOW:WINDOW + tm, c * LANES:(c + 1) * LANES] = kc.astype(BF16)

    vt = _nt_dot(wqvz_ref[ATTN_W:ATTN_W + KV_W, :], h)
    vnew_ref[0] = vt[:, tm - WINDOW:tm].T
    vt_ext[:, WINDOW:WINDOW + tm] = vt.astype(BF16)

    zt = _nt_dot(wqvz_ref[ATTN_W + KV_W:2 * ATTN_W + KV_W, :], h)
    gate_scr[...] = _silu(zt)

    qt = _nt_dot(wqvz_ref[0:ATTN_W, :], h)
    reps = tm // LANES
    gain = jnp.concatenate([qg_ref[...]] * reps, axis=1)
    cos_t = cost_ref[...]
    sin_t = sint_ref[...]
    for hd in range(N_HEADS):
        qh = qt[hd * HEAD_DIM:(hd + 1) * HEAD_DIM, :]
        ms = jnp.sum(qh * qh, axis=0, keepdims=True) * (1.0 / HEAD_DIM)
        qn = qh * lax.rsqrt(ms + RMS_EPS) * gain
        x1, x2 = qn[0:half, :], qn[half:HEAD_DIM, :]
        rot = jnp.concatenate([x1 * cos_t - x2 * sin_t, x2 * cos_t + x1 * sin_t], axis=0)
        qt_scr[hd * HEAD_DIM:(hd + 1) * HEAD_DIM, :] = (rot * (HEAD_DIM ** -0.5)).astype(BF16)

    ki = lax.broadcasted_iota(jnp.int32, (2 * WINDOW, 2 * WINDOW), 0)
    qi = lax.broadcasted_iota(jnp.int32, (2 * WINDOW, 2 * WINDOW), 1) % WINDOW + WINDOW
    band = (ki <= qi) & (qi - ki < WINDOW)
    lane = lax.broadcasted_iota(jnp.int32, (1, 2 * WINDOW), 1)
    zeros = jnp.zeros((HEAD_DIM, 2 * WINDOW), BF16)

    for i in range(nblk):
        c0 = i * WINDOW
        qcols = slice(c0, c0 + WINDOW)
        if i == 0:
            mask = band & (ki >= jnp.where(t == 0, WINDOW, 0))
        else:
            mask = band
        pairs = [(kv, kv * GROUP + 2 * pr) for kv in range(N_KV_HEADS) for pr in range(GROUP // 2)]
        for j, (kv, ha) in enumerate(pairs):
            chunk, pos = divmod(kv, LANES // HEAD_DIM)
            kblk = kext[c0:c0 + 2 * WINDOW, chunk * LANES:(chunk + 1) * LANES]
            qpair = jnp.concatenate([qt_scr[ha * HEAD_DIM:(ha + 1) * HEAD_DIM, qcols],
                                     qt_scr[(ha + 1) * HEAD_DIM:(ha + 2) * HEAD_DIM, qcols]], axis=1)
            rhs = jnp.concatenate([qpair, zeros] if pos == 0 else [zeros, qpair], axis=0)
            s_scr[j] = jnp.dot(kblk, rhs, preferred_element_type=F32)
        for j, (kv, ha) in enumerate(pairs):
            s = jnp.where(mask, s_scr[j], -jnp.inf)
            sink = jnp.where(lane < WINDOW, sink_ref[ha], sink_ref[ha + 1])
            m = jnp.maximum(jnp.max(s, axis=0, keepdims=True), sink)
            p = jnp.exp(s - m)
            denom = jnp.sum(p, axis=0, keepdims=True) + jnp.exp(sink - m)
            p_scr[j] = p.astype(BF16)
            inv_scr[j:j + 1, :] = 1.0 / denom
        for j, (kv, ha) in enumerate(pairs):
            vg = vt_ext[kv * HEAD_DIM:(kv + 1) * HEAD_DIM, c0:c0 + 2 * WINDOW]
            o = jnp.dot(vg, p_scr[j], preferred_element_type=F32) * inv_scr[j:j + 1, :]
            for hd, part in ((ha, o[:, 0:WINDOW]), (ha + 1, o[:, WINDOW:2 * WINDOW])):
                rows = slice(hd * HEAD_DIM, (hd + 1) * HEAD_DIM)
                yt_scr[rows, qcols] = (part * gate_scr[rows, qcols]).astype(BF16)

    out_t = jnp.dot(wout_ref[...], yt_scr[...], preferred_element_type=F32)
    o_ref[0] = x + out_t.T
    kext[0:WINDOW, :] = kext[tm:tm + WINDOW, :]
    vt_ext[:, 0:WINDOW] = vt_ext[:, tm:tm + WINDOW]


def _odd_prompt(x, g, wk, wqvz_t, qg_t, kg, cos, sin, cos_t, sin_t, sinks, wout_t, *, tm):
    B, T, D = x.shape
    assert T % tm == 0 and tm % WINDOW == 0
    half = HEAD_DIM // 2
    const = lambda *shape: pl.BlockSpec(shape, lambda b, t: (0,) * len(shape))
    return pl.pallas_call(
        functools.partial(_odd_prompt_kernel, tm=tm),
        grid=(B, T // tm),
        in_specs=[
            pl.BlockSpec((1, tm, D), lambda b, t: (b, t, 0)),
            const(1, D),
            const(*wk.shape),
            const(*wqvz_t.shape),
            const(HEAD_DIM, LANES),
            const(1, LANES),
            pl.BlockSpec((tm, LANES), lambda b, t: (t, 0)),
            pl.BlockSpec((tm, LANES), lambda b, t: (t, 0)),
            pl.BlockSpec((half, tm), lambda b, t: (0, t)),
            pl.BlockSpec((half, tm), lambda b, t: (0, t)),
            pl.BlockSpec(memory_space=pltpu.SMEM),
            const(*wout_t.shape),
        ],
        out_specs=[
            pl.BlockSpec((1, tm, D), lambda b, t: (b, t, 0)),
            pl.BlockSpec((1, WINDOW, KV_W), lambda b, t: (b, 0, 0)),
            pl.BlockSpec((1, WINDOW, KV_W), lambda b, t: (b, 0, 0)),
        ],
        out_shape=[
            jax.ShapeDtypeStruct((B, T, D), F32),
            jax.ShapeDtypeStruct((B, WINDOW, KV_W), F32),
            jax.ShapeDtypeStruct((B, WINDOW, KV_W), F32),
        ],
        scratch_shapes=[
            pltpu.VMEM((ATTN_W, tm), BF16),
            pltpu.VMEM((WINDOW + tm, KV_W), BF16),
            pltpu.VMEM((KV_W, WINDOW + tm), BF16),
            pltpu.VMEM((ATTN_W, tm), F32),
            pltpu.VMEM((ATTN_W, tm), BF16),
            pltpu.VMEM((N_HEADS // 2, 2 * WINDOW, 2 * WINDOW), F32),
            pltpu.VMEM((N_HEADS // 2, 2 * WINDOW, 2 * WINDOW), BF16),
            pltpu.VMEM((N_HEADS // 2, 2 * WINDOW), F32),
        ],
        compiler_params=pltpu.CompilerParams(
            dimension_semantics=("arbitrary", "arbitrary"), vmem_limit_bytes=VMEM_LIMIT_BYTES),
        name="odd_prompt",
    )(x, g, wk, wqvz_t, qg_t, kg, cos, sin, cos_t, sin_t, sinks, wout_t)


def _sample_dense_kernel(x_ref, sp_ref, sc_ref, g0_ref, win0_ref, pw_ref, ps_ref, cw_ref, wout0_ref,
                         g1_ref, win1_ref, qg_ref, kg_ref, cos_ref, sin_ref,
                         x1_ref, pool_ref, conv_ref, q_ref, k_ref, v_ref, gate_ref):
    w_pool = ps_ref.shape[1]
    w_conv = cw_ref.shape[1]

    x = x_ref[...]
    h = _rms(x, g0_ref[...]).astype(BF16)

    def proj(c, width):
        return jnp.dot(h, win0_ref[:, c:c + width], preferred_element_type=F32)

    u = proj(0, w_pool)
    z_a = proj(w_pool, w_pool)
    ys = []
    for g, w in enumerate(POOL_WINDOWS):
        cols = slice(g * POOL_GROUP, (g + 1) * POOL_GROUP)
        wsum = u[:, cols]
        for r in range(POOL_BUF - (w - 1), POOL_BUF):
            wsum = wsum + sp_ref[r, :, cols]
        cnt = float(min(PAST_LEN + 1, w))
        d = wsum * (1.0 / cnt) - u[:, cols]
        ya = jnp.dot(d.astype(BF16), pw_ref[g], preferred_element_type=F32)
        ys.append((ya * ps_ref[:, cols] * _silu(z_a[:, cols])).astype(BF16))
    for r in range(POOL_BUF - 1):
        pool_ref[r] = sp_ref[r + 1]
    pool_ref[POOL_BUF - 1] = u

    b_gate = proj(2 * w_pool, w_conv)
    c_gate = proj(2 * w_pool + w_conv, w_conv)
    v_in = proj(2 * w_pool + 2 * w_conv, w_conv)
    v = c_gate * v_in
    conv = v * cw_ref[CONV_WIDTH - 1:CONV_WIDTH, :]
    for k in range(CONV_BUF):
        conv = conv + sc_ref[k] * cw_ref[k:k + 1, :]
    for r in range(CONV_BUF - 1):
        conv_ref[r] = sc_ref[r + 1]
    conv_ref[CONV_BUF - 1] = v
    z_b = proj(2 * w_pool + 3 * w_conv, w_conv)
    ys.append((b_gate * conv * _silu(z_b)).astype(BF16))
    y = jnp.concatenate(ys, axis=1)
    x1 = x + jnp.dot(y, wout0_ref[...], preferred_element_type=F32)
    x1_ref[...] = x1

    h1 = _rms(x1, g1_ref[...]).astype(BF16)
    cos = cos_ref[...]
    sin = sin_ref[...]
    for c in range(ATTN_W // LANES):
        qc = jnp.dot(h1, win1_ref[:, c * LANES:(c + 1) * LANES], preferred_element_type=F32)
        q_ref[:, c * LANES:(c + 1) * LANES] = _head_norm_rope(qc, qg_ref[...], cos, sin) * (HEAD_DIM ** -0.5)
    for c in range(KV_W // LANES):
        kc = jnp.dot(h1, win1_ref[:, ATTN_W + c * LANES:ATTN_W + (c + 1) * LANES], preferred_element_type=F32)
        k_ref[:, c * LANES:(c + 1) * LANES] = _head_norm_rope(kc, kg_ref[...], cos, sin)
    v_ref[...] = jnp.dot(h1, win1_ref[:, ATTN_W + KV_W:ATTN_W + 2 * KV_W], preferred_element_type=F32)
    z = jnp.dot(h1, win1_ref[:, ATTN_W + 2 * KV_W:2 * ATTN_W + 2 * KV_W], preferred_element_type=F32)
    gate_ref[...] = _silu(z)


def _sample_dense(x, sp, sc, g0, win0, pw, ps, cw, wout0, g1, win1, qg, kg, cos, sin):
    n, D = x.shape
    vmem = pl.BlockSpec(memory_space=pltpu.VMEM)
    return pl.pallas_call(
        _sample_dense_kernel,
        in_specs=[vmem] * 15,
        out_specs=[vmem] * 7,
        out_shape=[
            jax.ShapeDtypeStruct((n, D), F32),
            jax.ShapeDtypeStruct(sp.shape, F32),
            jax.ShapeDtypeStruct(sc.shape, F32),
            jax.ShapeDtypeStruct((n, ATTN_W), F32),
            jax.ShapeDtypeStruct((n, KV_W), F32),
            jax.ShapeDtypeStruct((n, KV_W), F32),
            jax.ShapeDtypeStruct((n, ATTN_W), F32),
        ],
        compiler_params=pltpu.CompilerParams(vmem_limit_bytes=VMEM_LIMIT_BYTES),
        name="sample_dense",
    )(x, sp, sc, g0, win0, pw, ps, cw, wout0, g1, win1, qg, kg, cos, sin)


def _dup_heads(x):
    lane = lax.broadcasted_iota(jnp.int32, x.shape, 1)
    rolled = pltpu.roll(x, HEAD_DIM, 1)
    first = lane < HEAD_DIM
    return jnp.where(first, x, rolled), jnp.where(first, rolled, x)


def _expand_kv(x):
    chunks = []
    for c in range(KV_W // LANES):
        d0, d1 = _dup_heads(x[:, c * LANES:(c + 1) * LANES])
        chunks += [d0] * (GROUP // 2) + [d1] * (GROUP // 2)
    return chunks


def _sample_attn_kernel(q_ref, kn_ref, vn_ref, ck_ref, cv_ref, sink_ref, seg_ref, segt_ref, gate_ref, x1_ref,
                        wout_ref, nk_ref, nv_ref, y_ref, o_scr, *, bb):
    step = pl.program_id(0)
    win = ck_ref.shape[1]
    seg = seg_ref[...]
    segt = segt_ref[...]
    sink = sink_ref[...]
    nchunk = ATTN_W // LANES

    o_rows = []
    for i in range(bb):
        nk_ref[i, 0:win - 1, :] = ck_ref[i, 1:win, :]
        nk_ref[i, win - 1:win, :] = kn_ref[i:i + 1, :]
        nv_ref[i, 0:win - 1, :] = cv_ref[i, 1:win, :]
        nv_ref[i, win - 1:win, :] = vn_ref[i:i + 1, :]
        kexp = _expand_kv(nk_ref[i])
        prod = jnp.concatenate(
            [(kexp[c] * q_ref[i:i + 1, c * LANES:(c + 1) * LANES]).astype(BF16) for c in range(nchunk)], axis=1)
        s = jnp.dot(prod, seg, preferred_element_type=F32)
        m = jnp.maximum(jnp.max(s, axis=0, keepdims=True), sink)
        p = jnp.exp(s - m)
        denom = jnp.sum(p, axis=0, keepdims=True) + jnp.exp(sink - m)
        pexp = jnp.dot((p * (1.0 / denom)).astype(BF16), segt, preferred_element_type=F32)
        vexp = _expand_kv(nv_ref[i])
        o_rows.append(jnp.concatenate(
            [jnp.sum(pexp[:, c * LANES:(c + 1) * LANES] * vexp[c], axis=0, keepdims=True) for c in range(nchunk)],
            axis=1))
    o_scr[pl.ds(pl.multiple_of(step * bb, bb), bb), :] = jnp.concatenate(o_rows, axis=0)

    @pl.when(step == pl.num_programs(0) - 1)
    def _():
        y = (o_scr[...] * gate_ref[...]).astype(BF16)
        y_ref[...] = x1_ref[...] + jnp.dot(y, wout_ref[...], preferred_element_type=F32)


def _sample_attn(q, kn, vn, ck, cv, sink_row, seg, segt, gate, x1, wout, *, bb):
    n, win, kvw = ck.shape
    D = x1.shape[1]
    assert n % bb == 0 and bb % 8 == 0
    const = lambda *shape: pl.BlockSpec(shape, lambda s: (0,) * len(shape))
    return pl.pallas_call(
        functools.partial(_sample_attn_kernel, bb=bb),
        grid=(n // bb,),
        in_specs=[
            pl.BlockSpec((bb, ATTN_W), lambda s: (s, 0)),
            pl.BlockSpec((bb, kvw), lambda s: (s, 0)),
            pl.BlockSpec((bb, kvw), lambda s: (s, 0)),
            pl.BlockSpec((bb, win, kvw), lambda s: (s, 0, 0)),
            pl.BlockSpec((bb, win, kvw), lambda s: (s, 0, 0)),
            const(1, LANES),
            const(*seg.shape),
            const(*segt.shape),
            const(n, ATTN_W),
            const(n, D),
            const(*wout.shape),
        ],
        out_specs=[
            pl.BlockSpec((bb, win, kvw), lambda s: (s, 0, 0)),
            pl.BlockSpec((bb, win, kvw), lambda s: (s, 0, 0)),
            const(n, D),
        ],
        out_shape=[
            jax.ShapeDtypeStruct((n, win, kvw), F32),
            jax.ShapeDtypeStruct((n, win, kvw), F32),
            jax.ShapeDtypeStruct((n, D), F32),
        ],
        scratch_shapes=[pltpu.VMEM((n, ATTN_W), F32)],
        compiler_params=pltpu.CompilerParams(
            dimension_semantics=("arbitrary",), vmem_limit_bytes=VMEM_LIMIT_BYTES),
        name="sample_attn",
    )(q, kn, vn, ck, cv, sink_row, seg, segt, gate, x1, wout)


def _rope_tables(pos):
    half = HEAD_DIM // 2
    inv = ROPE_THETA ** (-jnp.arange(half, dtype=F32) / half)
    ang = pos.astype(F32)[:, None] * inv[None, :]
    cos, sin = jnp.cos(ang), jnp.sin(ang)
    reps = LANES // HEAD_DIM
    return (jnp.tile(jnp.concatenate([cos, cos], axis=1), (1, reps)),
            jnp.tile(jnp.concatenate([-sin, sin], axis=1), (1, reps)), cos.T, sin.T)


def kernel(x_prompt, x_sample, state_pool, state_conv, cache_k, cache_v, norm_g, w_in_even, pool_w, pool_scale,
           conv_w, w_out_even, w_in_odd, q_norm_g, k_norm_g, attn_sinks, w_out_odd):
    B, T, D = x_prompt.shape
    n_s, t_s, _ = x_sample.shape
    assert norm_g.shape[0] == 2 and w_in_even.shape[0] == 1 and w_in_odd.shape[0] == 1
    assert t_s == 1 and cache_k.shape[2] == WINDOW and T >= WINDOW
    assert cache_k.shape[3] * cache_k.shape[4] == KV_W and pool_w.shape[1:] == (len(POOL_WINDOWS), POOL_GROUP, POOL_GROUP)

    g0 = norm_g[0][None, :]
    g1 = norm_g[1][None, :]
    win0 = w_in_even[0].astype(BF16)
    wout0 = w_out_even[0].astype(BF16)
    win1 = w_in_odd[0].astype(BF16)
    wout1 = w_out_odd[0].astype(BF16)
    pw = pool_w[0].astype(BF16)
    ps = pool_scale[0][None, :]
    cw = conv_w[0]
    reps = LANES // HEAD_DIM
    qg = jnp.tile(q_norm_g[0], reps)[None, :]
    kg = jnp.tile(k_norm_g[0], reps)[None, :]
    sinks = attn_sinks[0]

    cos_p, sin_p, cos_pt, sin_pt = _rope_tables(jnp.arange(T))
    cos_s, sin_s, _, _ = _rope_tables(PAST_LEN + jnp.arange(t_s))

    w1 = w_in_odd[0]
    wk = w1[:, ATTN_W:ATTN_W + KV_W].astype(BF16)
    wqvz_t = jnp.concatenate([w1[:, :ATTN_W], w1[:, ATTN_W + KV_W:]], axis=1).T.astype(BF16)
    wout1_t = w_out_odd[0].T.astype(BF16)
    qg_t = jnp.broadcast_to(q_norm_g[0][:, None], (HEAD_DIM, LANES))

    x1p, pool_p, conv_p = _even_prompt(x_prompt, g0, win0, pw, ps, cw, wout0, tm=512)
    y_p, k_p, v_p = _odd_prompt(x1p, g1, wk, wqvz_t, qg_t, kg, cos_p, sin_p, cos_pt, sin_pt, sinks, wout1_t, tm=512)

    sp = jnp.transpose(state_pool[0], (1, 0, 2))
    sc = jnp.transpose(state_conv[0], (1, 0, 2))
    x1s, pool_s, conv_s, q_s, k_s, v_s, gate_s = _sample_dense(
        x_sample[:, 0, :], sp, sc, g0, win0, pw, ps, cw, wout0, g1, win1, qg, kg, cos_s, sin_s)
    pool_s = jnp.transpose(pool_s, (1, 0, 2))
    conv_s = jnp.transpose(conv_s, (1, 0, 2))

    col = jnp.arange(ATTN_W) // HEAD_DIM
    seg = (col[:, None] == jnp.arange(LANES)[None, :]).astype(BF16)
    segt = seg.T
    sink_row = jnp.pad(sinks, (0, LANES - N_HEADS))[None, :]
    ck = cache_k[0].reshape(n_s, WINDOW, KV_W)
    cv = cache_v[0].reshape(n_s, WINDOW, KV_W)
    nk_s, nv_s, y_s = _sample_attn(q_s, k_s, v_s, ck, cv, sink_row, seg, segt, gate_s, x1s, wout1, bb=8)

    kv_shape = (1, -1, WINDOW, N_KV_HEADS, HEAD_DIM)
    return (y_p, y_s[:, None, :], pool_p[None], pool_s[None], conv_p[None], conv_s[None],
            k_p.reshape(kv_shape), v_p.reshape(kv_shape), nk_s.reshape(kv_shape), nv_s.reshape(kv_shape))
```

```python
import functools

import jax
import jax.numpy as jnp
from jax import lax
from jax.experimental import pallas as pl
from jax.experimental.pallas import tpu as pltpu

F32 = jnp.float32
BF16 = jnp.bfloat16

POOL_WINDOWS = (2, 4, 8, 16)
POOL_GROUP = 128
POOL_BUF = max(POOL_WINDOWS) - 1
CONV_WIDTH = 3
CONV_BUF = CONV_WIDTH - 1
N_HEADS = 16
HEAD_DIM = 64
N_KV_HEADS = 4
GROUP = N_HEADS // N_KV_HEADS
WINDOW = 128
ROPE_THETA = 10000.0
RMS_EPS = 1e-6
PAST_LEN = 16384
ATTN_W = N_HEADS * HEAD_DIM
KV_W = N_KV_HEADS * HEAD_DIM

LANES = 128
POOL_HALO = 16
CONV_HALO = 8
VMEM_LIMIT_BYTES = 56 * 1024 * 1024
LOG2_E = 1.4426950408889634


def _rms(x, g):
    ms = jnp.mean(x * x, axis=-1, keepdims=True)
    return x * lax.rsqrt(ms + RMS_EPS) * g


def _silu(z):
    return z * jax.nn.sigmoid(z)


def _head_norm_rope(x, gain, cos, sin):
    lane = lax.broadcasted_iota(jnp.int32, x.shape, 1)
    first = lane < HEAD_DIM
    x2 = x * x
    ss0 = jnp.sum(jnp.where(first, x2, 0.0), axis=-1, keepdims=True)
    ss1 = jnp.sum(jnp.where(first, 0.0, x2), axis=-1, keepdims=True)
    r = jnp.where(first, lax.rsqrt(ss0 * (1.0 / HEAD_DIM) + RMS_EPS), lax.rsqrt(ss1 * (1.0 / HEAD_DIM) + RMS_EPS))
    xn = x * r * gain
    half = HEAD_DIM // 2
    swapped = jnp.where((lane % HEAD_DIM) < half, pltpu.roll(xn, LANES - half, 1), pltpu.roll(xn, half, 1))
    return xn * cos + swapped * sin


def _even_prompt_kernel(x_ref, g_ref, win_ref, pw_ref, ps_ref, cw_ref, wout_ref,
                        o_ref, pool_ref, conv_ref, uext, vext, sa, sb, y_scr, *, tm):
    t = pl.program_id(1)
    w_pool = uext.shape[1]
    w_conv = vext.shape[1]
    base = 2 * POOL_HALO

    @pl.when(t == 0)
    def _():
        uext[0:base, :] = jnp.zeros((base, w_pool), F32)
        vext[0:CONV_HALO, :] = jnp.zeros((CONV_HALO, w_conv), F32)
        sa[0:POOL_HALO, :] = jnp.zeros((POOL_HALO, POOL_GROUP), F32)
        sb[0:POOL_HALO, :] = jnp.zeros((POOL_HALO, POOL_GROUP), F32)

    x = x_ref[0]
    h = _rms(x, g_ref[...]).astype(BF16)

    def proj(c, width):
        return jnp.dot(h, win_ref[:, c:c + width], preferred_element_type=F32)

    u = proj(0, w_pool)
    uext[base:base + tm, :] = u
    pos = t * tm + lax.broadcasted_iota(jnp.int32, (tm, 1), 0)
    z_a = proj(w_pool, w_pool)
    n_ext = POOL_HALO + tm

    for g, w in enumerate(POOL_WINDOWS):
        lo = g * POOL_GROUP
        cols = slice(lo, lo + POOL_GROUP)
        src = uext
        src_cols = cols
        step = 1
        bufs = (sa, sb)
        nbuf = 0
        while 2 * step < w:
            dst = bufs[nbuf % 2]
            dst[POOL_HALO:POOL_HALO + n_ext, :] = (src[POOL_HALO:POOL_HALO + n_ext, src_cols]
                                                   + src[POOL_HALO - step:POOL_HALO - step + n_ext, src_cols])
            src, src_cols = dst, slice(0, POOL_GROUP)
            step *= 2
            nbuf += 1
        wsum = src[base:base + tm, src_cols] + src[base - step:base - step + tm, src_cols]
        cnt = jnp.minimum(pos + 1, w).astype(F32)
        d = wsum * (1.0 / cnt) - u[:, cols]
        ya = jnp.dot(d.astype(BF16), pw_ref[g], preferred_element_type=F32)
        ya = ya * ps_ref[:, cols] * _silu(z_a[:, cols])
        y_scr[:, cols] = ya.astype(BF16)

    b_gate = proj(2 * w_pool, w_conv)
    c_gate = proj(2 * w_pool + w_conv, w_conv)
    v_in = proj(2 * w_pool + 2 * w_conv, w_conv)
    v = c_gate * v_in
    vext[CONV_HALO:CONV_HALO + tm, :] = v
    conv = v * cw_ref[CONV_WIDTH - 1:CONV_WIDTH, :]
    for k in range(CONV_WIDTH - 1):
        shift = CONV_WIDTH - 1 - k
        conv = conv + vext[CONV_HALO - shift:CONV_HALO - shift + tm, :] * cw_ref[k:k + 1, :]
    z_b = proj(2 * w_pool + 3 * w_conv, w_conv)
    y_b = b_gate * conv * _silu(z_b)
    y_scr[:, w_pool:w_pool + w_conv] = y_b.astype(BF16)

    o_ref[0] = x + jnp.dot(y_scr[...], wout_ref[...], preferred_element_type=F32)
    pool_ref[0] = uext[base + tm - POOL_BUF:base + tm, :]
    conv_ref[0] = vext[CONV_HALO + tm - CONV_BUF:CONV_HALO + tm, :]
    uext[POOL_HALO:base, :] = uext[POOL_HALO + tm:base + tm, :]
    vext[0:CONV_HALO, :] = vext[tm:tm + CONV_HALO, :]


def _even_prompt(x, g, win, pw, ps, cw, wout, *, tm):
    B, T, D = x.shape
    w_pool = ps.shape[-1]
    w_conv = cw.shape[-1]
    assert T % tm == 0 and tm % 16 == 0 and tm >= POOL_HALO
    const = lambda *shape: pl.BlockSpec(shape, lambda b, t: (0,) * len(shape))
    return pl.pallas_call(
        functools.partial(_even_prompt_kernel, tm=tm),
        grid=(B, T // tm),
        in_specs=[
            pl.BlockSpec((1, tm, D), lambda b, t: (b, t, 0)),
            const(1, D),
            const(*win.shape),
            const(*pw.shape),
            const(1, w_pool),
            const(*cw.shape),
            const(*wout.shape),
        ],
        out_specs=[
            pl.BlockSpec((1, tm, D), lambda b, t: (b, t, 0)),
            pl.BlockSpec((1, POOL_BUF, w_pool), lambda b, t: (b, 0, 0)),
            pl.BlockSpec((1, CONV_BUF, w_conv), lambda b, t: (b, 0, 0)),
        ],
        out_shape=[
            jax.ShapeDtypeStruct((B, T, D), F32),
            jax.ShapeDtypeStruct((B, POOL_BUF, w_pool), F32),
            jax.ShapeDtypeStruct((B, CONV_BUF, w_conv), F32),
        ],
        scratch_shapes=[
            pltpu.VMEM((2 * POOL_HALO + tm, w_pool), F32),
            pltpu.VMEM((CONV_HALO + tm, w_conv), F32),
            pltpu.VMEM((2 * POOL_HALO + tm, POOL_GROUP), F32),
            pltpu.VMEM((2 * POOL_HALO + tm, POOL_GROUP), F32),
            pltpu.VMEM((tm, w_pool + w_conv), BF16),
        ],
        compiler_params=pltpu.CompilerParams(
            dimension_semantics=("arbitrary", "arbitrary"), vmem_limit_bytes=VMEM_LIMIT_BYTES),
        name="even_prompt",
    )(x, g, win, pw, ps, cw, wout)


def _nt_dot(a, b):
    return lax.dot_general(a, b, (((1,), (1,)), ((), ())), preferred_element_type=F32)


def _odd_prompt_kernel(x_ref, g_ref, wk_ref, wqvz_ref, qg_ref, kg_ref, cos_ref, sin_ref, cost_ref, sint_ref,
                       sink_ref, wout_ref, o_ref, knew_ref, vnew_ref,
                       qt_scr, kext, vt_ext, gate_scr, yt_scr, s_scr, p_scr, esink_scr, *, tm):
    t = pl.program_id(1)
    nblk = tm // WINDOW
    half = HEAD_DIM // 2

    @pl.when(t == 0)
    def _():
        kext[0:WINDOW, :] = jnp.zeros((WINDOW, KV_W), BF16)
        vt_ext[:, 0:WINDOW] = jnp.zeros((KV_W, WINDOW), BF16)

    x = x_ref[0]
    h = _rms(x, g_ref[...]).astype(BF16)

    for c in range(KV_W // LANES):
        kc = jnp.dot(h, wk_ref[:, c * LANES:(c + 1) * LANES], preferred_element_type=F32)
        kc = _head_norm_rope(kc, kg_ref[...], cos_ref[...], sin_ref[...])
        knew_ref[0, :, c * LANES:(c + 1) * LANES] = kc[tm - WINDOW:tm, :]
        kext[WINDOW:WINDOW + tm, c * LANES:(c + 1) * LANES] = kc.astype(BF16)

    vt = _nt_dot(wqvz_ref[ATTN_W:ATTN_W + KV_W, :], h)
    vnew_ref[0] = vt[:, tm - WINDOW:tm].T
    vt_ext[:, WINDOW:WINDOW + tm] = vt.astype(BF16)

    zt = _nt_dot(wqvz_ref[ATTN_W + KV_W:2 * ATTN_W + KV_W, :], h)
    gate_scr[...] = _silu(zt)

    qt = _nt_dot(wqvz_ref[0:ATTN_W, :], h)
    reps = tm // LANES
    gain = jnp.concatenate([qg_ref[...]] * reps, axis=1)
    cos_t = cost_ref[...]
    sin_t = sint_ref[...]
    for hd in range(N_HEADS):
        qh = qt[hd * HEAD_DIM:(hd + 1) * HEAD_DIM, :]
        ms = jnp.sum(qh * qh, axis=0, keepdims=True) * (1.0 / HEAD_DIM)
        qn = qh * lax.rsqrt(ms + RMS_EPS) * gain
        x1, x2 = qn[0:half, :], qn[half:HEAD_DIM, :]
        rot = jnp.concatenate([x1 * cos_t - x2 * sin_t, x2 * cos_t + x1 * sin_t], axis=0)
        qt_scr[hd * HEAD_DIM:(hd + 1) * HEAD_DIM, :] = (rot * (LOG2_E * HEAD_DIM ** -0.5)).astype(BF16)

    ki = lax.broadcasted_iota(jnp.int32, (2 * WINDOW, 2 * WINDOW), 0)
    qi = lax.broadcasted_iota(jnp.int32, (2 * WINDOW, 2 * WINDOW), 1) % WINDOW + WINDOW
    band = (ki <= qi) & (qi - ki < WINDOW)
    lane = lax.broadcasted_iota(jnp.int32, (1, 2 * WINDOW), 1)
    zeros = jnp.zeros((HEAD_DIM, 2 * WINDOW), BF16)
    ones = jnp.ones((16, 2 * WINDOW), BF16)

    for i in range(nblk):
        c0 = i * WINDOW
        qcols = slice(c0, c0 + WINDOW)
        if i == 0:
            mask = band & (ki >= jnp.where(t == 0, WINDOW, 0))
        else:
            mask = band
        pairs = [(kv, kv * GROUP + 2 * pr) for kv in range(N_KV_HEADS) for pr in range(GROUP // 2)]
        for j, (kv, ha) in enumerate(pairs):
            chunk, pos = divmod(kv, LANES // HEAD_DIM)
            kblk = kext[c0:c0 + 2 * WINDOW, chunk * LANES:(chunk + 1) * LANES]
            qpair = jnp.concatenate([qt_scr[ha * HEAD_DIM:(ha + 1) * HEAD_DIM, qcols],
                                     qt_scr[(ha + 1) * HEAD_DIM:(ha + 2) * HEAD_DIM, qcols]], axis=1)
            rhs = jnp.concatenate([qpair, zeros] if pos == 0 else [zeros, qpair], axis=0)
            s_scr[j] = jnp.dot(kblk, rhs, preferred_element_type=F32)
        for j, (kv, ha) in enumerate(pairs):
            s = jnp.where(mask, s_scr[j], -jnp.inf)
            sink = jnp.where(lane < WINDOW, sink_ref[ha], sink_ref[ha + 1]) * LOG2_E
            m = jnp.maximum(jnp.max(s, axis=0, keepdims=True), sink)
            p_scr[j] = jnp.exp2(s - m).astype(BF16)
            esink_scr[j:j + 1, :] = jnp.exp2(sink - m)
        for j, (kv, ha) in enumerate(pairs):
            vg = vt_ext[kv * HEAD_DIM:(kv + 1) * HEAD_DIM, c0:c0 + 2 * WINDOW]
            oa = jnp.dot(jnp.concatenate([vg, ones], axis=0), p_scr[j], preferred_element_type=F32)
            denom = oa[HEAD_DIM:HEAD_DIM + 1, :] + esink_scr[j:j + 1, :]
            o = oa[0:HEAD_DIM, :] * (1.0 / denom)
            for hd, part in ((ha, o[:, 0:WINDOW]), (ha + 1, o[:, WINDOW:2 * WINDOW])):
                rows = slice(hd * HEAD_DIM, (hd + 1) * HEAD_DIM)
                yt_scr[rows, qcols] = (part * gate_scr[rows, qcols]).astype(BF16)

    out_t = jnp.dot(wout_ref[...], yt_scr[...], preferred_element_type=F32)
    o_ref[0] = x + out_t.T
    kext[0:WINDOW, :] = kext[tm:tm + WINDOW, :]
    vt_ext[:, 0:WINDOW] = vt_ext[:, tm:tm + WINDOW]


def _odd_prompt(x, g, wk, wqvz_t, qg_t, kg, cos, sin, cos_t, sin_t, sinks, wout_t, *, tm):
    B, T, D = x.shape
    assert T % tm == 0 and tm % WINDOW == 0
    half = HEAD_DIM // 2
    const = lambda *shape: pl.BlockSpec(shape, lambda b, t: (0,) * len(shape))
    return pl.pallas_call(
        functools.partial(_odd_prompt_kernel, tm=tm),
        grid=(B, T // tm),
        in_specs=[
            pl.BlockSpec((1, tm, D), lambda b, t: (b, t, 0)),
            const(1, D),
            const(*wk.shape),
            const(*wqvz_t.shape),
            const(HEAD_DIM, LANES),
            const(1, LANES),
            pl.BlockSpec((tm, LANES), lambda b, t: (t, 0)),
            pl.BlockSpec((tm, LANES), lambda b, t: (t, 0)),
            pl.BlockSpec((half, tm), lambda b, t: (0, t)),
            pl.BlockSpec((half, tm), lambda b, t: (0, t)),
            pl.BlockSpec(memory_space=pltpu.SMEM),
            const(*wout_t.shape),
        ],
        out_specs=[
            pl.BlockSpec((1, tm, D), lambda b, t: (b, t, 0)),
            pl.BlockSpec((1, WINDOW, KV_W), lambda b, t: (b, 0, 0)),
            pl.BlockSpec((1, WINDOW, KV_W), lambda b, t: (b, 0, 0)),
        ],
        out_shape=[
            jax.ShapeDtypeStruct((B, T, D), F32),
            jax.ShapeDtypeStruct((B, WINDOW, KV_W), F32),
            jax.ShapeDtypeStruct((B, WINDOW, KV_W), F32),
        ],
        scratch_shapes=[
            pltpu.VMEM((ATTN_W, tm), BF16),
            pltpu.VMEM((WINDOW + tm, KV_W), BF16),
            pltpu.VMEM((KV_W, WINDOW + tm), BF16),
            pltpu.VMEM((ATTN_W, tm), F32),
            pltpu.VMEM((ATTN_W, tm), BF16),
            pltpu.VMEM((N_HEADS // 2, 2 * WINDOW, 2 * WINDOW), F32),
            pltpu.VMEM((N_HEADS // 2, 2 * WINDOW, 2 * WINDOW), BF16),
            pltpu.VMEM((N_HEADS // 2, 2 * WINDOW), F32),
        ],
        compiler_params=pltpu.CompilerParams(
            dimension_semantics=("arbitrary", "arbitrary"), vmem_limit_bytes=VMEM_LIMIT_BYTES),
        name="odd_prompt",
    )(x, g, wk, wqvz_t, qg_t, kg, cos, sin, cos_t, sin_t, sinks, wout_t)


def _dup_heads(x):
    lane = lax.broadcasted_iota(jnp.int32, x.shape, 1)
    rolled = pltpu.roll(x, HEAD_DIM, 1)
    first = lane < HEAD_DIM
    return jnp.where(first, x, rolled), jnp.where(first, rolled, x)


def _sample_dense_kernel(x_ref, sp_ref, sc_ref, g0_ref, win0_ref, pw_ref, ps_ref, cw_ref, wout0_ref,
                         g1_ref, wqvz_ref, wkt_ref, qg_ref, kgt_ref, cos_ref, sin_ref, cost_ref, sint_ref,
                         x1_ref, pool_ref, conv_ref, qh_ref, kt_ref, vt_ref, gate_ref):
    w_pool = ps_ref.shape[1]
    w_conv = cw_ref.shape[1]

    x = x_ref[...]
    h = _rms(x, g0_ref[...]).astype(BF16)

    def proj(c, width):
        return jnp.dot(h, win0_ref[:, c:c + width], preferred_element_type=F32)

    u = proj(0, w_pool)
    z_a = proj(w_pool, w_pool)
    ys = []
    for g, w in enumerate(POOL_WINDOWS):
        cols = slice(g * POOL_GROUP, (g + 1) * POOL_GROUP)
        wsum = u[:, cols]
        for r in range(POOL_BUF - (w - 1), POOL_BUF):
            wsum = wsum + sp_ref[r, :, cols]
        cnt = float(min(PAST_LEN + 1, w))
        d = wsum * (1.0 / cnt) - u[:, cols]
        ya = jnp.dot(d.astype(BF16), pw_ref[g], preferred_element_type=F32)
        ys.append((ya * ps_ref[:, cols] * _silu(z_a[:, cols])).astype(BF16))
    for r in range(POOL_BUF - 1):
        pool_ref[r] = sp_ref[r + 1]
    pool_ref[POOL_BUF - 1] = u

    b_gate = proj(2 * w_pool, w_conv)
    c_gate = proj(2 * w_pool + w_conv, w_conv)
    v_in = proj(2 * w_pool + 2 * w_conv, w_conv)
    v = c_gate * v_in
    conv = v * cw_ref[CONV_WIDTH - 1:CONV_WIDTH, :]
    for k in range(CONV_BUF):
        conv = conv + sc_ref[k] * cw_ref[k:k + 1, :]
    for r in range(CONV_BUF - 1):
        conv_ref[r] = sc_ref[r + 1]
    conv_ref[CONV_BUF - 1] = v
    z_b = proj(2 * w_pool + 3 * w_conv, w_conv)
    ys.append((b_gate * conv * _silu(z_b)).astype(BF16))
    y = jnp.concatenate(ys, axis=1)
    x1 = x + jnp.dot(y, wout0_ref[...], preferred_element_type=F32)
    x1_ref[...] = x1

    n = x.shape[0]
    half = HEAD_DIM // 2
    h1 = _rms(x1, g1_ref[...]).astype(BF16)
    q = _nt_dot(h1, wqvz_ref[0:ATTN_W, :])
    for c in range(ATTN_W // LANES):
        qc = _head_norm_rope(q[:, c * LANES:(c + 1) * LANES], qg_ref[...], cos_ref[...], sin_ref[...])
        d0, d1 = _dup_heads(qc * (HEAD_DIM ** -0.5))
        qh_ref[2 * c * n:(2 * c + 1) * n, :] = d0
        qh_ref[(2 * c + 1) * n:(2 * c + 2) * n, :] = d1
    kt = _nt_dot(wkt_ref[...], h1)
    cos_t = cost_ref[...]
    sin_t = sint_ref[...]
    for kv in range(N_KV_HEADS):
        kh = kt[kv * HEAD_DIM:(kv + 1) * HEAD_DIM, :]
        ms = jnp.sum(kh * kh, axis=0, keepdims=True) * (1.0 / HEAD_DIM)
        kn = kh * lax.rsqrt(ms + RMS_EPS) * kgt_ref[...]
        x1h, x2h = kn[0:half, :], kn[half:HEAD_DIM, :]
        kt_ref[kv * HEAD_DIM:(kv + 1) * HEAD_DIM, :] = jnp.concatenate(
            [x1h * cos_t - x2h * sin_t, x2h * cos_t + x1h * sin_t], axis=0)
    vt_ref[...] = _nt_dot(wqvz_ref[ATTN_W:ATTN_W + KV_W, :], h1)
    z = _nt_dot(h1, wqvz_ref[ATTN_W + KV_W:2 * ATTN_W + KV_W, :])
    gate_ref[...] = _silu(z)


def _sample_dense(x, sp, sc, g0, win0, pw, ps, cw, wout0, g1, wqvz_t, wk_t, qg, kg_t, cos, sin, cos_t, sin_t):
    n, D = x.shape
    assert n == LANES
    vmem = pl.BlockSpec(memory_space=pltpu.VMEM)
    return pl.pallas_call(
        _sample_dense_kernel,
        in_specs=[vmem] * 18,
        out_specs=[vmem] * 7,
        out_shape=[
            jax.ShapeDtypeStruct((n, D), F32),
            jax.ShapeDtypeStruct(sp.shape, F32),
            jax.ShapeDtypeStruct(sc.shape, F32),
            jax.ShapeDtypeStruct((N_HEADS * n, LANES), F32),
            jax.ShapeDtypeStruct((KV_W, n), F32),
            jax.ShapeDtypeStruct((KV_W, n), F32),
            jax.ShapeDtypeStruct((n, ATTN_W), F32),
        ],
        compiler_params=pltpu.CompilerParams(vmem_limit_bytes=VMEM_LIMIT_BYTES),
        name="sample_dense",
    )(x, sp, sc, g0, win0, pw, ps, cw, wout0, g1, wqvz_t, wk_t, qg, kg_t, cos, sin, cos_t, sin_t)


def _sample_attn_kernel(qh_ref, knt_ref, vnt_ref, ck_ref, cv_ref, sink_ref, gate_ref, x1_ref, wout_ref,
                        nk_ref, nv_ref, y_ref, o_lo, o_hi, *, bb):
    step = pl.program_id(0)
    n = x1_ref.shape[0]
    win = ck_ref.shape[2]
    blocks = LANES // HEAD_DIM

    lane = lax.broadcasted_iota(jnp.int32, (KV_W, win), 1)
    newest = lane == win - 1
    hrow = lax.broadcasted_iota(jnp.int32, (N_HEADS, LANES), 0) // GROUP
    hcol = lax.broadcasted_iota(jnp.int32, (N_HEADS, LANES), 1) // HEAD_DIM
    sink = sink_ref[...]

    for i in range(bb):
        b = step * bb + i
        kt = jnp.where(newest, pltpu.roll(knt_ref[...], win - 1 - b, 1), pltpu.roll(ck_ref[i], win - 1, 1))
        vt = jnp.where(newest, pltpu.roll(vnt_ref[...], win - 1 - b, 1), pltpu.roll(cv_ref[i], win - 1, 1))
        nk_ref[i] = kt
        nv_ref[i] = vt
        qd = qh_ref[pl.ds(b, N_HEADS, stride=n), :]
        qx = jnp.concatenate([jnp.where(hcol + blocks * c == hrow, qd, 0.0) for c in range(KV_W // LANES)], axis=1)
        s = jnp.dot(qx.astype(BF16), kt.astype(BF16), preferred_element_type=F32)
        m = jnp.maximum(jnp.max(s, axis=-1, keepdims=True), sink)
        p = jnp.exp(s - m)
        denom = jnp.sum(p, axis=-1, keepdims=True) + jnp.exp(sink - m)
        o = _nt_dot((p * (1.0 / denom)).astype(BF16), vt.astype(BF16))
        rows = pl.ds(pl.multiple_of(b * N_HEADS, N_HEADS), N_HEADS)
        o_lo[rows, :] = o[:, 0:LANES]
        o_hi[rows, :] = o[:, LANES:2 * LANES]

    @pl.when(step == pl.num_programs(0) - 1)
    def _():
        lane_n = lax.broadcasted_iota(jnp.int32, (n, LANES), 1)
        chunks = []
        for c in range(ATTN_W // LANES):
            parts = []
            for hd in (2 * c, 2 * c + 1):
                kv = hd // GROUP
                slab = (o_lo, o_hi)[kv // blocks]
                a = slab[pl.ds(hd, n, stride=N_HEADS), :]
                parts.append(a if kv % blocks == hd % blocks else pltpu.roll(a, HEAD_DIM, 1))
            chunks.append(jnp.where(lane_n < HEAD_DIM, parts[0], parts[1]))
        y = (jnp.concatenate(chunks, axis=1) * gate_ref[...]).astype(BF16)
        y_ref[...] = x1_ref[...] + _nt_dot(y, wout_ref[...])


def _sample_attn(qh, knt, vnt, ckt, cvt, sink_col, gate, x1, wout_t, *, bb):
    n, kvw, win = ckt.shape
    D = x1.shape[1]
    assert n % bb == 0 and kvw == KV_W == 2 * LANES and win == LANES and n == LANES
    const = lambda *shape: pl.BlockSpec(shape, lambda s: (0,) * len(shape))
    return pl.pallas_call(
        functools.partial(_sample_attn_kernel, bb=bb),
        grid=(n // bb,),
        in_specs=[
            const(*qh.shape),
            const(*knt.shape),
            const(*vnt.shape),
            pl.BlockSpec((bb, kvw, win), lambda s: (s, 0, 0)),
            pl.BlockSpec((bb, kvw, win), lambda s: (s, 0, 0)),
            const(*sink_col.shape),
            const(n, ATTN_W),
            const(n, D),
            const(*wout_t.shape),
        ],
        out_specs=[
            pl.BlockSpec((bb, kvw, win), lambda s: (s, 0, 0)),
            pl.BlockSpec((bb, kvw, win), lambda s: (s, 0, 0)),
            const(n, D),
        ],
        out_shape=[
            jax.ShapeDtypeStruct((n, kvw, win), F32),
            jax.ShapeDtypeStruct((n, kvw, win), F32),
            jax.ShapeDtypeStruct((n, D), F32),
        ],
        scratch_shapes=[pltpu.VMEM((n * N_HEADS, LANES), F32), pltpu.VMEM((n * N_HEADS, LANES), F32)],
        compiler_params=pltpu.CompilerParams(
            dimension_semantics=("arbitrary",), vmem_limit_bytes=VMEM_LIMIT_BYTES),
        name="sample_attn",
    )(qh, knt, vnt, ckt, cvt, sink_col, gate, x1, wout_t)


def _rope_tables(pos):
    half = HEAD_DIM // 2
    inv = ROPE_THETA ** (-jnp.arange(half, dtype=F32) / half)
    ang = pos.astype(F32)[:, None] * inv[None, :]
    cos, sin = jnp.cos(ang), jnp.sin(ang)
    reps = LANES // HEAD_DIM
    return (jnp.tile(jnp.concatenate([cos, cos], axis=1), (1, reps)),
            jnp.tile(jnp.concatenate([-sin, sin], axis=1), (1, reps)), cos.T, sin.T)


def kernel(x_prompt, x_sample, state_pool, state_conv, cache_k, cache_v, norm_g, w_in_even, pool_w, pool_scale,
           conv_w, w_out_even, w_in_odd, q_norm_g, k_norm_g, attn_sinks, w_out_odd):
    B, T, D = x_prompt.shape
    n_s, t_s, _ = x_sample.shape
    assert norm_g.shape[0] == 2 and w_in_even.shape[0] == 1 and w_in_odd.shape[0] == 1
    assert t_s == 1 and cache_k.shape[2] == WINDOW and T >= WINDOW
    assert cache_k.shape[3] * cache_k.shape[4] == KV_W and pool_w.shape[1:] == (len(POOL_WINDOWS), POOL_GROUP, POOL_GROUP)

    g0 = norm_g[0][None, :]
    g1 = norm_g[1][None, :]
    win0 = w_in_even[0].astype(BF16)
    wout0 = w_out_even[0].astype(BF16)
    pw = pool_w[0].astype(BF16)
    ps = pool_scale[0][None, :]
    cw = conv_w[0]
    reps = LANES // HEAD_DIM
    qg = jnp.tile(q_norm_g[0], reps)[None, :]
    kg = jnp.tile(k_norm_g[0], reps)[None, :]
    sinks = attn_sinks[0]

    cos_p, sin_p, cos_pt, sin_pt = _rope_tables(jnp.arange(T))
    cos_s, sin_s, cos_st, sin_st = _rope_tables(PAST_LEN + jnp.arange(t_s))
    cos_st = jnp.broadcast_to(cos_st, (HEAD_DIM // 2, LANES))
    sin_st = jnp.broadcast_to(sin_st, (HEAD_DIM // 2, LANES))

    w1 = w_in_odd[0]
    wk = w1[:, ATTN_W:ATTN_W + KV_W].astype(BF16)
    wk_t = w1[:, ATTN_W:ATTN_W + KV_W].T.astype(BF16)
    wqvz_t = jnp.concatenate([w1[:, :ATTN_W], w1[:, ATTN_W + KV_W:]], axis=1).T.astype(BF16)
    wout1_t = w_out_odd[0].T.astype(BF16)
    qg_t = jnp.broadcast_to(q_norm_g[0][:, None], (HEAD_DIM, LANES))
    kg_t = jnp.broadcast_to(k_norm_g[0][:, None], (HEAD_DIM, LANES))

    x1p, pool_p, conv_p = _even_prompt(x_prompt, g0, win0, pw, ps, cw, wout0, tm=1024)
    y_p, k_p, v_p = _odd_prompt(x1p, g1, wk, wqvz_t, qg_t, kg, cos_p, sin_p, cos_pt, sin_pt, sinks, wout1_t, tm=512)

    sp = jnp.transpose(state_pool[0], (1, 0, 2))
    sc = jnp.transpose(state_conv[0], (1, 0, 2))
    x1s, pool_s, conv_s, qh_s, kt_s, vt_s, gate_s = _sample_dense(
        x_sample[:, 0, :], sp, sc, g0, win0, pw, ps, cw, wout0, g1, wqvz_t, wk_t, qg, kg_t, cos_s, sin_s,
        cos_st, sin_st)
    pool_s = jnp.transpose(pool_s, (1, 0, 2))
    conv_s = jnp.transpose(conv_s, (1, 0, 2))

    ckt = jnp.transpose(cache_k[0], (0, 2, 3, 1)).reshape(n_s, KV_W, WINDOW)
    cvt = jnp.transpose(cache_v[0], (0, 2, 3, 1)).reshape(n_s, KV_W, WINDOW)
    nkt_s, nvt_s, y_s = _sample_attn(qh_s, kt_s, vt_s, ckt, cvt, sinks[:, None], gate_s, x1s, wout1_t, bb=8)
    nk_s = jnp.transpose(nkt_s.reshape(n_s, N_KV_HEADS, HEAD_DIM, WINDOW), (0, 3, 1, 2))
    nv_s = jnp.transpose(nvt_s.reshape(n_s, N_KV_HEADS, HEAD_DIM, WINDOW), (0, 3, 1, 2))

    kv_shape = (1, -1, WINDOW, N_KV_HEADS, HEAD_DIM)
    return (y_p, y_s[:, None, :], pool_p[None], pool_s[None], conv_p[None], conv_s[None],
            k_p.reshape(kv_shape), v_p.reshape(kv_shape), nk_s[None], nv_s[None])
```

```python
import functools

import jax
import jax.numpy as jnp
from jax import lax
from jax.experimental import pallas as pl
from jax.experimental.pallas import tpu as pltpu

F32 = jnp.float32
BF16 = jnp.bfloat16

POOL_WINDOWS = (2, 4, 8, 16)
POOL_GROUP = 128
POOL_BUF = max(POOL_WINDOWS) - 1
CONV_WIDTH = 3
CONV_BUF = CONV_WIDTH - 1
N_HEADS = 16
HEAD_DIM = 64
N_KV_HEADS = 4
GROUP = N_HEADS // N_KV_HEADS
WINDOW = 128
ROPE_THETA = 10000.0
RMS_EPS = 1e-6
PAST_LEN = 16384
ATTN_W = N_HEADS * HEAD_DIM
KV_W = N_KV_HEADS * HEAD_DIM

LANES = 128
POOL_HALO = 16
CONV_HALO = 8
VMEM_LIMIT_BYTES = 56 * 1024 * 1024
LOG2_E = 1.4426950408889634


def _rms(x, g):
    ms = jnp.mean(x * x, axis=-1, keepdims=True)
    return x * lax.rsqrt(ms + RMS_EPS) * g


def _silu(z):
    return z * jax.nn.sigmoid(z)


def _head_norm_rope(x, gain, cos, sin):
    lane = lax.broadcasted_iota(jnp.int32, x.shape, 1)
    first = lane < HEAD_DIM
    x2 = x * x
    ss0 = jnp.sum(jnp.where(first, x2, 0.0), axis=-1, keepdims=True)
    ss1 = jnp.sum(jnp.where(first, 0.0, x2), axis=-1, keepdims=True)
    r = jnp.where(first, lax.rsqrt(ss0 * (1.0 / HEAD_DIM) + RMS_EPS), lax.rsqrt(ss1 * (1.0 / HEAD_DIM) + RMS_EPS))
    xn = x * r * gain
    half = HEAD_DIM // 2
    swapped = jnp.where((lane % HEAD_DIM) < half, pltpu.roll(xn, LANES - half, 1), pltpu.roll(xn, half, 1))
    return xn * cos + swapped * sin


def _even_prompt_kernel(x_ref, g_ref, win_ref, pw_ref, ps_ref, cw_ref, wout_ref,
                        o_ref, pool_ref, conv_ref, uext, vext, sa, sb, y_scr, *, tm):
    t = pl.program_id(1)
    w_pool = uext.shape[1]
    w_conv = vext.shape[1]
    base = 2 * POOL_HALO

    @pl.when(t == 0)
    def _():
        uext[0:base, :] = jnp.zeros((base, w_pool), F32)
        vext[0:CONV_HALO, :] = jnp.zeros((CONV_HALO, w_conv), F32)
        sa[0:POOL_HALO, :] = jnp.zeros((POOL_HALO, POOL_GROUP), F32)
        sb[0:POOL_HALO, :] = jnp.zeros((POOL_HALO, POOL_GROUP), F32)

    x = x_ref[0]
    h = _rms(x, g_ref[...]).astype(BF16)

    def proj(c, width):
        return jnp.dot(h, win_ref[:, c:c + width], preferred_element_type=F32)

    u = proj(0, w_pool)
    uext[base:base + tm, :] = u
    pos = t * tm + lax.broadcasted_iota(jnp.int32, (tm, 1), 0)
    z_a = proj(w_pool, w_pool)
    n_ext = POOL_HALO + tm

    for g, w in enumerate(POOL_WINDOWS):
        lo = g * POOL_GROUP
        cols = slice(lo, lo + POOL_GROUP)
        src = uext
        src_cols = cols
        step = 1
        bufs = (sa, sb)
        nbuf = 0
        while 2 * step < w:
            dst = bufs[nbuf % 2]
            dst[POOL_HALO:POOL_HALO + n_ext, :] = (src[POOL_HALO:POOL_HALO + n_ext, src_cols]
                                                   + src[POOL_HALO - step:POOL_HALO - step + n_ext, src_cols])
            src, src_cols = dst, slice(0, POOL_GROUP)
            step *= 2
            nbuf += 1
        wsum = src[base:base + tm, src_cols] + src[base - step:base - step + tm, src_cols]
        cnt = jnp.minimum(pos + 1, w).astype(F32)
        d = wsum * (1.0 / cnt) - u[:, cols]
        ya = jnp.dot(d.astype(BF16), pw_ref[g], preferred_element_type=F32)
        ya = ya * ps_ref[:, cols] * _silu(z_a[:, cols])
        y_scr[:, cols] = ya.astype(BF16)

    b_gate = proj(2 * w_pool, w_conv)
    c_gate = proj(2 * w_pool + w_conv, w_conv)
    v_in = proj(2 * w_pool + 2 * w_conv, w_conv)
    v = c_gate * v_in
    vext[CONV_HALO:CONV_HALO + tm, :] = v
    conv = v * cw_ref[CONV_WIDTH - 1:CONV_WIDTH, :]
    for k in range(CONV_WIDTH - 1):
        shift = CONV_WIDTH - 1 - k
        conv = conv + vext[CONV_HALO - shift:CONV_HALO - shift + tm, :] * cw_ref[k:k + 1, :]
    z_b = proj(2 * w_pool + 3 * w_conv, w_conv)
    y_b = b_gate * conv * _silu(z_b)
    y_scr[:, w_pool:w_pool + w_conv] = y_b.astype(BF16)

    o_ref[0] = x + jnp.dot(y_scr[...], wout_ref[...], preferred_element_type=F32)
    pool_ref[0] = uext[base + tm - POOL_BUF:base + tm, :]
    conv_ref[0] = vext[CONV_HALO + tm - CONV_BUF:CONV_HALO + tm, :]
    uext[POOL_HALO:base, :] = uext[POOL_HALO + tm:base + tm, :]
    vext[0:CONV_HALO, :] = vext[tm:tm + CONV_HALO, :]


def _even_prompt(x, g, win, pw, ps, cw, wout, *, tm):
    B, T, D = x.shape
    w_pool = ps.shape[-1]
    w_conv = cw.shape[-1]
    assert T % tm == 0 and tm % 16 == 0 and tm >= POOL_HALO
    const = lambda *shape: pl.BlockSpec(shape, lambda b, t: (0,) * len(shape))
    return pl.pallas_call(
        functools.partial(_even_prompt_kernel, tm=tm),
        grid=(B, T // tm),
        in_specs=[
            pl.BlockSpec((1, tm, D), lambda b, t: (b, t, 0)),
            const(1, D),
            const(*win.shape),
            const(*pw.shape),
            const(1, w_pool),
            const(*cw.shape),
            const(*wout.shape),
        ],
        out_specs=[
            pl.BlockSpec((1, tm, D), lambda b, t: (b, t, 0)),
            pl.BlockSpec((1, POOL_BUF, w_pool), lambda b, t: (b, 0, 0)),
            pl.BlockSpec((1, CONV_BUF, w_conv), lambda b, t: (b, 0, 0)),
        ],
        out_shape=[
            jax.ShapeDtypeStruct((B, T, D), F32),
            jax.ShapeDtypeStruct((B, POOL_BUF, w_pool), F32),
            jax.ShapeDtypeStruct((B, CONV_BUF, w_conv), F32),
        ],
        scratch_shapes=[
            pltpu.VMEM((2 * POOL_HALO + tm, w_pool), F32),
            pltpu.VMEM((CONV_HALO + tm, w_conv), F32),
            pltpu.VMEM((2 * POOL_HALO + tm, POOL_GROUP), F32),
            pltpu.VMEM((2 * POOL_HALO + tm, POOL_GROUP), F32),
            pltpu.VMEM((tm, w_pool + w_conv), BF16),
        ],
        compiler_params=pltpu.CompilerParams(
            dimension_semantics=("arbitrary", "arbitrary"), vmem_limit_bytes=VMEM_LIMIT_BYTES),
        name="even_prompt",
    )(x, g, win, pw, ps, cw, wout)


def _nt_dot(a, b):
    return lax.dot_general(a, b, (((1,), (1,)), ((), ())), preferred_element_type=F32)


def _odd_prompt_kernel(*refs, tm, nt):
    qt_scr, kext, vt_ext, gate_scr = refs[14:18]

    @pl.when(pl.program_id(0) == 0)
    def _():
        qt_scr[...] = jnp.zeros(qt_scr.shape, BF16)
        gate_scr[...] = jnp.zeros(gate_scr.shape, F32)
        kext[...] = jnp.zeros(kext.shape, BF16)
        vt_ext[...] = jnp.zeros(vt_ext.shape, BF16)

    for parity in range(2):
        @pl.when(pl.program_id(0) % 2 == parity)
        def _():
            _odd_prompt_step(*refs, tm=tm, nt=nt, cur=parity, prev=1 - parity)


def _odd_prompt_step(x_ref, xres_ref, g_ref, wkt_ref, wqvz_ref, qg_ref, kg_ref, cost_ref, sint_ref, sink_ref,
                     wout_ref, o_ref, knew_ref, vnew_ref,
                     qt_scr, kext, vt_ext, gate_scr, yt_scr, s_scr, p_scr, esink_scr, *, tm, nt, cur, prev):
    step = pl.program_id(0)
    proj_t = jnp.minimum(step, pl.num_programs(0) - 2) % nt
    attn_t = jnp.maximum(step - 1, 0) % nt
    nblk = tm // WINDOW
    half = HEAD_DIM // 2

    h = _rms(x_ref[0], g_ref[...]).astype(BF16)
    reps = tm // LANES
    cos_t = cost_ref[...]
    sin_t = sint_ref[...]

    def norm_rope(xt, gain):
        ms = jnp.sum(xt * xt, axis=0, keepdims=True) * (1.0 / HEAD_DIM)
        xn = xt * lax.rsqrt(ms + RMS_EPS) * gain
        x1, x2 = xn[0:half, :], xn[half:HEAD_DIM, :]
        return jnp.concatenate([x1 * cos_t - x2 * sin_t, x2 * cos_t + x1 * sin_t], axis=0)

    has_past = proj_t > 0

    def proj_kv():
        kt = _nt_dot(wkt_ref[...], h)
        kgain = jnp.concatenate([kg_ref[...]] * reps, axis=1)
        kt = jnp.concatenate(
            [norm_rope(kt[kv * HEAD_DIM:(kv + 1) * HEAD_DIM, :], kgain) for kv in range(N_KV_HEADS)], axis=0)
        k_rows = kt.T
        knew_ref[0] = k_rows[tm - WINDOW:tm, :]
        kext[cur, 0:WINDOW, :] = jnp.where(has_past, kext[prev, tm:tm + WINDOW, :], jnp.zeros((WINDOW, KV_W), BF16))
        kext[cur, WINDOW:WINDOW + tm, :] = k_rows.astype(BF16)
        vt = _nt_dot(wqvz_ref[ATTN_W:ATTN_W + KV_W, :], h)
        vnew_ref[0] = vt[:, tm - WINDOW:tm].T
        vt_ext[cur, :, 0:WINDOW] = jnp.where(has_past, vt_ext[prev, :, tm:tm + WINDOW],
                                             jnp.zeros((KV_W, WINDOW), BF16))
        vt_ext[cur, :, WINDOW:WINDOW + tm] = vt.astype(BF16)

    def proj_gate(lo, hi):
        zt = _nt_dot(wqvz_ref[ATTN_W + KV_W + lo:ATTN_W + KV_W + hi, :], h)
        gate_scr[cur, lo:hi, :] = _silu(zt)

    def proj_q(lo, hi):
        qt = _nt_dot(wqvz_ref[lo:hi, :], h)
        qgain = jnp.concatenate([qg_ref[...]] * reps, axis=1)
        for r0 in range(0, hi - lo, HEAD_DIM):
            qt_scr[cur, lo + r0:lo + r0 + HEAD_DIM, :] = (
                norm_rope(qt[r0:r0 + HEAD_DIM, :], qgain) * (LOG2_E * HEAD_DIM ** -0.5)).astype(BF16)

    quarter = ATTN_W // 4
    pieces = [lambda: (proj_kv(), proj_gate(0, quarter)),
              lambda: proj_gate(quarter, ATTN_W),
              lambda: proj_q(0, ATTN_W // 2),
              lambda: proj_q(ATTN_W // 2, ATTN_W)]

    ki = lax.broadcasted_iota(jnp.int32, (2 * WINDOW, 2 * WINDOW), 0)
    qi = lax.broadcasted_iota(jnp.int32, (2 * WINDOW, 2 * WINDOW), 1) % WINDOW + WINDOW
    band = (ki <= qi) & (qi - ki < WINDOW)
    lane = lax.broadcasted_iota(jnp.int32, (1, 2 * WINDOW), 1)
    zeros = jnp.zeros((HEAD_DIM, 2 * WINDOW), BF16)
    ones = jnp.ones((16, 2 * WINDOW), BF16)

    for i in range(nblk):
        c0 = i * WINDOW
        qcols = slice(c0, c0 + WINDOW)
        if i == 0:
            mask = band & (ki >= jnp.where(attn_t == 0, WINDOW, 0))
        else:
            mask = band
        pairs = [(kv, kv * GROUP + 2 * pr) for kv in range(N_KV_HEADS) for pr in range(GROUP // 2)]
        for j, (kv, ha) in enumerate(pairs):
            chunk, pos = divmod(kv, LANES // HEAD_DIM)
            kblk = kext[prev, c0:c0 + 2 * WINDOW, chunk * LANES:(chunk + 1) * LANES]
            qpair = jnp.concatenate([qt_scr[prev, ha * HEAD_DIM:(ha + 1) * HEAD_DIM, qcols],
                                     qt_scr[prev, (ha + 1) * HEAD_DIM:(ha + 2) * HEAD_DIM, qcols]], axis=1)
            rhs = jnp.concatenate([qpair, zeros] if pos == 0 else [zeros, qpair], axis=0)
            s_scr[j] = jnp.dot(kblk, rhs, preferred_element_type=F32)
        for c, piece in enumerate(pieces):
            if c * nblk // len(pieces) == i:
                piece()
        for j, (kv, ha) in enumerate(pairs):
            s = jnp.where(mask, s_scr[j], -jnp.inf)
            sink = jnp.where(lane < WINDOW, sink_ref[ha], sink_ref[ha + 1]) * LOG2_E
            m = jnp.maximum(jnp.max(s, axis=0, keepdims=True), sink)
            p_scr[j] = jnp.exp2(s - m).astype(BF16)
            esink_scr[j:j + 1, :] = jnp.exp2(sink - m)
        for j, (kv, ha) in enumerate(pairs):
            vg = vt_ext[prev, kv * HEAD_DIM:(kv + 1) * HEAD_DIM, c0:c0 + 2 * WINDOW]
            oa = jnp.dot(jnp.concatenate([vg, ones], axis=0), p_scr[j], preferred_element_type=F32)
            denom = oa[HEAD_DIM:HEAD_DIM + 1, :] + esink_scr[j:j + 1, :]
            o = oa[0:HEAD_DIM, :] * (1.0 / denom)
            for hd, part in ((ha, o[:, 0:WINDOW]), (ha + 1, o[:, WINDOW:2 * WINDOW])):
                rows = slice(hd * HEAD_DIM, (hd + 1) * HEAD_DIM)
                yt_scr[rows, qcols] = (part * gate_scr[prev, rows, qcols]).astype(BF16)

    out_t = jnp.dot(wout_ref[...], yt_scr[...], preferred_element_type=F32)
    o_ref[0] = xres_ref[0] + out_t.T


def _odd_prompt(x, g, wk_t, wqvz_t, qg_t, kg_t, cos_t, sin_t, sinks, wout_t, *, tm):
    B, T, D = x.shape
    assert T % tm == 0 and tm % WINDOW == 0
    nt = T // tm
    half = HEAD_DIM // 2
    const = lambda *shape: pl.BlockSpec(shape, lambda s: (0,) * len(shape))
    n_tiles = B * nt
    proj_tile = lambda s: jnp.minimum(s, n_tiles - 1)
    attn_tile = lambda s: jnp.maximum(s - 1, 0)
    return pl.pallas_call(
        functools.partial(_odd_prompt_kernel, tm=tm, nt=nt),
        grid=(n_tiles + 1,),
        in_specs=[
            pl.BlockSpec((1, tm, D), lambda s: (proj_tile(s) // nt, proj_tile(s) % nt, 0)),
            pl.BlockSpec((1, tm, D), lambda s: (attn_tile(s) // nt, attn_tile(s) % nt, 0)),
            const(1, D),
            const(*wk_t.shape),
            const(*wqvz_t.shape),
            const(HEAD_DIM, LANES),
            const(HEAD_DIM, LANES),
            pl.BlockSpec((half, tm), lambda s: (0, proj_tile(s) % nt)),
            pl.BlockSpec((half, tm), lambda s: (0, proj_tile(s) % nt)),
            pl.BlockSpec(memory_space=pltpu.SMEM),
            const(*wout_t.shape),
        ],
        out_specs=[
            pl.BlockSpec((1, tm, D), lambda s: (attn_tile(s) // nt, attn_tile(s) % nt, 0)),
            pl.BlockSpec((1, WINDOW, KV_W), lambda s: (proj_tile(s) // nt, 0, 0)),
            pl.BlockSpec((1, WINDOW, KV_W), lambda s: (proj_tile(s) // nt, 0, 0)),
        ],
        out_shape=[
            jax.ShapeDtypeStruct((B, T, D), F32),
            jax.ShapeDtypeStruct((B, WINDOW, KV_W), F32),
            jax.ShapeDtypeStruct((B, WINDOW, KV_W), F32),
        ],
        scratch_shapes=[
            pltpu.VMEM((2, ATTN_W, tm), BF16),
            pltpu.VMEM((2, WINDOW + tm, KV_W), BF16),
            pltpu.VMEM((2, KV_W, WINDOW + tm), BF16),
            pltpu.VMEM((2, ATTN_W, tm), F32),
            pltpu.VMEM((ATTN_W, tm), BF16),
            pltpu.VMEM((N_HEADS // 2, 2 * WINDOW, 2 * WINDOW), F32),
            pltpu.VMEM((N_HEADS // 2, 2 * WINDOW, 2 * WINDOW), BF16),
            pltpu.VMEM((N_HEADS // 2, 2 * WINDOW), F32),
        ],
        compiler_params=pltpu.CompilerParams(
            dimension_semantics=("arbitrary",), vmem_limit_bytes=VMEM_LIMIT_BYTES),
        name="odd_prompt",
    )(x, x, g, wk_t, wqvz_t, qg_t, kg_t, cos_t, sin_t, sinks, wout_t)


def _dup_heads(x):
    lane = lax.broadcasted_iota(jnp.int32, x.shape, 1)
    rolled = pltpu.roll(x, HEAD_DIM, 1)
    first = lane < HEAD_DIM
    return jnp.where(first, x, rolled), jnp.where(first, rolled, x)


def _sample_dense_kernel(x_ref, sp_ref, sc_ref, g0_ref, win0_ref, pw_ref, ps_ref, cw_ref, wout0_ref,
                         g1_ref, wqvz_ref, wkt_ref, qg_ref, kgt_ref, cos_ref, sin_ref, cost_ref, sint_ref,
                         x1_ref, pool_ref, conv_ref, qh_ref, kt_ref, vt_ref, gate_ref):
    w_pool = ps_ref.shape[1]
    w_conv = cw_ref.shape[1]

    x = x_ref[...]
    h = _rms(x, g0_ref[...]).astype(BF16)

    def proj(c, width):
        return jnp.dot(h, win0_ref[:, c:c + width], preferred_element_type=F32)

    u = proj(0, w_pool)
    z_a = proj(w_pool, w_pool)
    ys = []
    for g, w in enumerate(POOL_WINDOWS):
        cols = slice(g * POOL_GROUP, (g + 1) * POOL_GROUP)
        wsum = u[:, cols]
        for r in range(POOL_BUF - (w - 1), POOL_BUF):
            wsum = wsum + sp_ref[r, :, cols]
        cnt = float(min(PAST_LEN + 1, w))
        d = wsum * (1.0 / cnt) - u[:, cols]
        ya = jnp.dot(d.astype(BF16), pw_ref[g], preferred_element_type=F32)
        ys.append((ya * ps_ref[:, cols] * _silu(z_a[:, cols])).astype(BF16))
    for r in range(POOL_BUF - 1):
        pool_ref[r] = sp_ref[r + 1]
    pool_ref[POOL_BUF - 1] = u

    b_gate = proj(2 * w_pool, w_conv)
    c_gate = proj(2 * w_pool + w_conv, w_conv)
    v_in = proj(2 * w_pool + 2 * w_conv, w_conv)
    v = c_gate * v_in
    conv = v * cw_ref[CONV_WIDTH - 1:CONV_WIDTH, :]
    for k in range(CONV_BUF):
        conv = conv + sc_ref[k] * cw_ref[k:k + 1, :]
    for r in range(CONV_BUF - 1):
        conv_ref[r] = sc_ref[r + 1]
    conv_ref[CONV_BUF - 1] = v
    z_b = proj(2 * w_pool + 3 * w_conv, w_conv)
    ys.append((b_gate * conv * _silu(z_b)).astype(BF16))
    y = jnp.concatenate(ys, axis=1)
    x1 = x + jnp.dot(y, wout0_ref[...], preferred_element_type=F32)
    x1_ref[...] = x1

    n = x.shape[0]
    half = HEAD_DIM // 2
    h1 = _rms(x1, g1_ref[...]).astype(BF16)
    q = _nt_dot(h1, wqvz_ref[0:ATTN_W, :])
    for c in range(ATTN_W // LANES):
        qc = _head_norm_rope(q[:, c * LANES:(c + 1) * LANES], qg_ref[...], cos_ref[...], sin_ref[...])
        d0, d1 = _dup_heads(qc * (HEAD_DIM ** -0.5))
        qh_ref[2 * c * n:(2 * c + 1) * n, :] = d0
        qh_ref[(2 * c + 1) * n:(2 * c + 2) * n, :] = d1
    kt = _nt_dot(wkt_ref[...], h1)
    cos_t = cost_ref[...]
    sin_t = sint_ref[...]
    for kv in range(N_KV_HEADS):
        kh = kt[kv * HEAD_DIM:(kv + 1) * HEAD_DIM, :]
        ms = jnp.sum(kh * kh, axis=0, keepdims=True) * (1.0 / HEAD_DIM)
        kn = kh * lax.rsqrt(ms + RMS_EPS) * kgt_ref[...]
        x1h, x2h = kn[0:half, :], kn[half:HEAD_DIM, :]
        kt_ref[kv * HEAD_DIM:(kv + 1) * HEAD_DIM, :] = jnp.concatenate(
            [x1h * cos_t - x2h * sin_t, x2h * cos_t + x1h * sin_t], axis=0)
    vt_ref[...] = _nt_dot(wqvz_ref[ATTN_W:ATTN_W + KV_W, :], h1)
    z = _nt_dot(h1, wqvz_ref[ATTN_W + KV_W:2 * ATTN_W + KV_W, :])
    gate_ref[...] = _silu(z)


def _sample_dense(x, sp, sc, g0, win0, pw, ps, cw, wout0, g1, wqvz_t, wk_t, qg, kg_t, cos, sin, cos_t, sin_t):
    n, D = x.shape
    assert n == LANES
    vmem = pl.BlockSpec(memory_space=pltpu.VMEM)
    return pl.pallas_call(
        _sample_dense_kernel,
        in_specs=[vmem] * 18,
        out_specs=[vmem] * 7,
        out_shape=[
            jax.ShapeDtypeStruct((n, D), F32),
            jax.ShapeDtypeStruct(sp.shape, F32),
            jax.ShapeDtypeStruct(sc.shape, F32),
            jax.ShapeDtypeStruct((N_HEADS * n, LANES), F32),
            jax.ShapeDtypeStruct((KV_W, n), F32),
            jax.ShapeDtypeStruct((KV_W, n), F32),
            jax.ShapeDtypeStruct((n, ATTN_W), F32),
        ],
        compiler_params=pltpu.CompilerParams(vmem_limit_bytes=VMEM_LIMIT_BYTES),
        name="sample_dense",
    )(x, sp, sc, g0, win0, pw, ps, cw, wout0, g1, wqvz_t, wk_t, qg, kg_t, cos, sin, cos_t, sin_t)


def _sample_attn_kernel(qh_ref, knt_ref, vnt_ref, ck_ref, cv_ref, sink_ref, gate_ref, x1_ref, wout_ref,
                        nk_ref, nv_ref, y_ref, o_lo, o_hi, *, bb):
    step = pl.program_id(0)
    n = x1_ref.shape[0]
    win = ck_ref.shape[2]
    blocks = LANES // HEAD_DIM

    lane = lax.broadcasted_iota(jnp.int32, (KV_W, win), 1)
    newest = lane == win - 1
    hrow = lax.broadcasted_iota(jnp.int32, (N_HEADS, LANES), 0) // GROUP
    hcol = lax.broadcasted_iota(jnp.int32, (N_HEADS, LANES), 1) // HEAD_DIM
    sink = sink_ref[...]

    for i in range(bb):
        b = step * bb + i
        kt = jnp.where(newest, pltpu.roll(knt_ref[...], win - 1 - b, 1), pltpu.roll(ck_ref[i], win - 1, 1))
        vt = jnp.where(newest, pltpu.roll(vnt_ref[...], win - 1 - b, 1), pltpu.roll(cv_ref[i], win - 1, 1))
        nk_ref[i] = kt
        nv_ref[i] = vt
        qd = qh_ref[pl.ds(b, N_HEADS, stride=n), :]
        qx = jnp.concatenate([jnp.where(hcol + blocks * c == hrow, qd, 0.0) for c in range(KV_W // LANES)], axis=1)
        s = jnp.dot(qx.astype(BF16), kt.astype(BF16), preferred_element_type=F32)
        m = jnp.maximum(jnp.max(s, axis=-1, keepdims=True), sink)
        p = jnp.exp(s - m)
        denom = jnp.sum(p, axis=-1, keepdims=True) + jnp.exp(sink - m)
        o = _nt_dot((p * (1.0 / denom)).astype(BF16), vt.astype(BF16))
        rows = pl.ds(pl.multiple_of(b * N_HEADS, N_HEADS), N_HEADS)
        o_lo[rows, :] = o[:, 0:LANES]
        o_hi[rows, :] = o[:, LANES:2 * LANES]

    @pl.when(step == pl.num_programs(0) - 1)
    def _():
        lane_n = lax.broadcasted_iota(jnp.int32, (n, LANES), 1)
        chunks = []
        for c in range(ATTN_W // LANES):
            parts = []
            for hd in (2 * c, 2 * c + 1):
                kv = hd // GROUP
                slab = (o_lo, o_hi)[kv // blocks]
                a = slab[pl.ds(hd, n, stride=N_HEADS), :]
                parts.append(a if kv % blocks == hd % blocks else pltpu.roll(a, HEAD_DIM, 1))
            chunks.append(jnp.where(lane_n < HEAD_DIM, parts[0], parts[1]))
        y = (jnp.concatenate(chunks, axis=1) * gate_ref[...]).astype(BF16)
        y_ref[...] = x1_ref[...] + _nt_dot(y, wout_ref[...])


def _sample_attn(qh, knt, vnt, ckt, cvt, sink_col, gate, x1, wout_t, *, bb):
    n, kvw, win = ckt.shape
    D = x1.shape[1]
    assert n % bb == 0 and kvw == KV_W == 2 * LANES and win == LANES and n == LANES
    const = lambda *shape: pl.BlockSpec(shape, lambda s: (0,) * len(shape))
    return pl.pallas_call(
        functools.partial(_sample_attn_kernel, bb=bb),
        grid=(n // bb,),
        in_specs=[
            const(*qh.shape),
            const(*knt.shape),
            const(*vnt.shape),
            pl.BlockSpec((bb, kvw, win), lambda s: (s, 0, 0)),
            pl.BlockSpec((bb, kvw, win), lambda s: (s, 0, 0)),
            const(*sink_col.shape),
            const(n, ATTN_W),
            const(n, D),
            const(*wout_t.shape),
        ],
        out_specs=[
            pl.BlockSpec((bb, kvw, win), lambda s: (s, 0, 0)),
            pl.BlockSpec((bb, kvw, win), lambda s: (s, 0, 0)),
            const(n, D),
        ],
        out_shape=[
            jax.ShapeDtypeStruct((n, kvw, win), F32),
            jax.ShapeDtypeStruct((n, kvw, win), F32),
            jax.ShapeDtypeStruct((n, D), F32),
        ],
        scratch_shapes=[pltpu.VMEM((n * N_HEADS, LANES), F32), pltpu.VMEM((n * N_HEADS, LANES), F32)],
        compiler_params=pltpu.CompilerParams(
            dimension_semantics=("arbitrary",), vmem_limit_bytes=VMEM_LIMIT_BYTES),
        name="sample_attn",
    )(qh, knt, vnt, ckt, cvt, sink_col, gate, x1, wout_t)


def _rope_tables(pos):
    half = HEAD_DIM // 2
    inv = ROPE_THETA ** (-jnp.arange(half, dtype=F32) / half)
    ang = pos.astype(F32)[:, None] * inv[None, :]
    return jnp.cos(ang), jnp.sin(ang)


def kernel(x_prompt, x_sample, state_pool, state_conv, cache_k, cache_v, norm_g, w_in_even, pool_w, pool_scale,
           conv_w, w_out_even, w_in_odd, q_norm_g, k_norm_g, attn_sinks, w_out_odd):
    B, T, D = x_prompt.shape
    n_s, t_s, _ = x_sample.shape
    assert norm_g.shape[0] == 2 and w_in_even.shape[0] == 1 and w_in_odd.shape[0] == 1
    assert t_s == 1 and cache_k.shape[2] == WINDOW and T >= WINDOW
    assert cache_k.shape[3] * cache_k.shape[4] == KV_W and pool_w.shape[1:] == (len(POOL_WINDOWS), POOL_GROUP, POOL_GROUP)

    g0 = norm_g[0][None, :]
    g1 = norm_g[1][None, :]
    win0 = w_in_even[0].astype(BF16)
    wout0 = w_out_even[0].astype(BF16)
    pw = pool_w[0].astype(BF16)
    ps = pool_scale[0][None, :]
    cw = conv_w[0]
    reps = LANES // HEAD_DIM
    qg = jnp.tile(q_norm_g[0], reps)[None, :]
    sinks = attn_sinks[0]

    cos_p, sin_p = _rope_tables(jnp.arange(T))
    cos_pt, sin_pt = cos_p.T, sin_p.T
    cos_1, sin_1 = _rope_tables(PAST_LEN + jnp.arange(t_s))
    cos_s = jnp.tile(jnp.concatenate([cos_1, cos_1], axis=1), (1, reps))
    sin_s = jnp.tile(jnp.concatenate([-sin_1, sin_1], axis=1), (1, reps))
    cos_st = jnp.broadcast_to(cos_1.T, (HEAD_DIM // 2, LANES))
    sin_st = jnp.broadcast_to(sin_1.T, (HEAD_DIM // 2, LANES))

    w1 = w_in_odd[0]
    wk_t = w1[:, ATTN_W:ATTN_W + KV_W].T.astype(BF16)
    wqvz_t = jnp.concatenate([w1[:, :ATTN_W], w1[:, ATTN_W + KV_W:]], axis=1).T.astype(BF16)
    wout1_t = w_out_odd[0].T.astype(BF16)
    qg_t = jnp.broadcast_to(q_norm_g[0][:, None], (HEAD_DIM, LANES))
    kg_t = jnp.broadcast_to(k_norm_g[0][:, None], (HEAD_DIM, LANES))

    x1p, pool_p, conv_p = _even_prompt(x_prompt, g0, win0, pw, ps, cw, wout0, tm=1024)
    y_p, k_p, v_p = _odd_prompt(x1p, g1, wk_t, wqvz_t, qg_t, kg_t, cos_pt, sin_pt, sinks, wout1_t, tm=512)

    sp = jnp.transpose(state_pool[0], (1, 0, 2))
    sc = jnp.transpose(state_conv[0], (1, 0, 2))
    x1s, pool_s, conv_s, qh_s, kt_s, vt_s, gate_s = _sample_dense(
        x_sample[:, 0, :], sp, sc, g0, win0, pw, ps, cw, wout0, g1, wqvz_t, wk_t, qg, kg_t, cos_s, sin_s,
        cos_st, sin_st)
    pool_s = jnp.transpose(pool_s, (1, 0, 2))
    conv_s = jnp.transpose(conv_s, (1, 0, 2))

    ckt = jnp.transpose(cache_k[0], (0, 2, 3, 1)).reshape(n_s, KV_W, WINDOW)
    cvt = jnp.transpose(cache_v[0], (0, 2, 3, 1)).reshape(n_s, KV_W, WINDOW)
    nkt_s, nvt_s, y_s = _sample_attn(qh_s, kt_s, vt_s, ckt, cvt, sinks[:, None], gate_s, x1s, wout1_t, bb=8)
    nk_s = jnp.transpose(nkt_s.reshape(n_s, N_KV_HEADS, HEAD_DIM, WINDOW), (0, 3, 1, 2))
    nv_s = jnp.transpose(nvt_s.reshape(n_s, N_KV_HEADS, HEAD_DIM, WINDOW), (0, 3, 1, 2))

    kv_shape = (1, -1, WINDOW, N_KV_HEADS, HEAD_DIM)
    return (y_p, y_s[:, None, :], pool_p[None], pool_s[None], conv_p[None], conv_s[None],
            k_p.reshape(kv_shape), v_p.reshape(kv_shape), nk_s[None], nv_s[None])
```

```python
import functools

import jax
import jax.numpy as jnp
from jax import lax
from jax.experimental import pallas as pl
from jax.experimental.pallas import tpu as pltpu

F32 = jnp.float32
BF16 = jnp.bfloat16

POOL_WINDOWS = (2, 4, 8, 16)
POOL_GROUP = 128
POOL_BUF = max(POOL_WINDOWS) - 1
CONV_WIDTH = 3
CONV_BUF = CONV_WIDTH - 1
N_HEADS = 16
HEAD_DIM = 64
N_KV_HEADS = 4
GROUP = N_HEADS // N_KV_HEADS
WINDOW = 128
ROPE_THETA = 10000.0
RMS_EPS = 1e-6
PAST_LEN = 16384
ATTN_W = N_HEADS * HEAD_DIM
KV_W = N_KV_HEADS * HEAD_DIM

LANES = 128
POOL_HALO = 16
CONV_HALO = 8
VMEM_LIMIT_BYTES = 56 * 1024 * 1024
LOG2_E = 1.4426950408889634


def _rms(x, g):
    ms = jnp.mean(x * x, axis=-1, keepdims=True)
    return x * lax.rsqrt(ms + RMS_EPS) * g


def _silu(z):
    return z * jax.nn.sigmoid(z)


def _head_norm_rope(x, gain, cos, sin):
    lane = lax.broadcasted_iota(jnp.int32, x.shape, 1)
    first = lane < HEAD_DIM
    x2 = x * x
    ss0 = jnp.sum(jnp.where(first, x2, 0.0), axis=-1, keepdims=True)
    ss1 = jnp.sum(jnp.where(first, 0.0, x2), axis=-1, keepdims=True)
    r = jnp.where(first, lax.rsqrt(ss0 * (1.0 / HEAD_DIM) + RMS_EPS), lax.rsqrt(ss1 * (1.0 / HEAD_DIM) + RMS_EPS))
    xn = x * r * gain
    half = HEAD_DIM // 2
    swapped = jnp.where((lane % HEAD_DIM) < half, pltpu.roll(xn, LANES - half, 1), pltpu.roll(xn, half, 1))
    return xn * cos + swapped * sin


def _even_prompt_kernel(x_ref, g_ref, win_ref, pw_ref, ps_ref, cw_ref, wout_ref,
                        o_ref, pool_ref, conv_ref, uext, vext, sa, sb, y_scr, *, tm):
    t = pl.program_id(1)
    w_pool = uext.shape[1]
    w_conv = vext.shape[1]
    base = 2 * POOL_HALO

    @pl.when(t == 0)
    def _():
        uext[0:base, :] = jnp.zeros((base, w_pool), F32)
        vext[0:CONV_HALO, :] = jnp.zeros((CONV_HALO, w_conv), F32)
        sa[0:POOL_HALO, :] = jnp.zeros((POOL_HALO, POOL_GROUP), F32)
        sb[0:POOL_HALO, :] = jnp.zeros((POOL_HALO, POOL_GROUP), F32)

    x = x_ref[0]
    h = _rms(x, g_ref[...]).astype(win_ref.dtype)

    def proj(c, width):
        return jnp.dot(h, win_ref[:, c:c + width], preferred_element_type=F32)

    u = proj(0, w_pool)
    uext[base:base + tm, :] = u
    pos = t * tm + lax.broadcasted_iota(jnp.int32, (tm, 1), 0)
    z_a = proj(w_pool, w_pool)
    n_ext = POOL_HALO + tm

    for g, w in enumerate(POOL_WINDOWS):
        lo = g * POOL_GROUP
        cols = slice(lo, lo + POOL_GROUP)
        src = uext
        src_cols = cols
        step = 1
        bufs = (sa, sb)
        nbuf = 0
        while 2 * step < w:
            dst = bufs[nbuf % 2]
            dst[POOL_HALO:POOL_HALO + n_ext, :] = (src[POOL_HALO:POOL_HALO + n_ext, src_cols]
                                                   + src[POOL_HALO - step:POOL_HALO - step + n_ext, src_cols])
            src, src_cols = dst, slice(0, POOL_GROUP)
            step *= 2
            nbuf += 1
        wsum = src[base:base + tm, src_cols] + src[base - step:base - step + tm, src_cols]
        cnt = jnp.minimum(pos + 1, w).astype(F32)
        d = wsum * (1.0 / cnt) - u[:, cols]
        ya = jnp.dot(d.astype(pw_ref.dtype), pw_ref[g], preferred_element_type=F32)
        ya = ya * ps_ref[:, cols] * _silu(z_a[:, cols])
        y_scr[:, cols] = ya.astype(y_scr.dtype)

    b_gate = proj(2 * w_pool, w_conv)
    c_gate = proj(2 * w_pool + w_conv, w_conv)
    v_in = proj(2 * w_pool + 2 * w_conv, w_conv)
    v = c_gate * v_in
    vext[CONV_HALO:CONV_HALO + tm, :] = v
    conv = v * cw_ref[CONV_WIDTH - 1:CONV_WIDTH, :]
    for k in range(CONV_WIDTH - 1):
        shift = CONV_WIDTH - 1 - k
        conv = conv + vext[CONV_HALO - shift:CONV_HALO - shift + tm, :] * cw_ref[k:k + 1, :]
    z_b = proj(2 * w_pool + 3 * w_conv, w_conv)
    y_b = b_gate * conv * _silu(z_b)
    y_scr[:, w_pool:w_pool + w_conv] = y_b.astype(y_scr.dtype)

    o_ref[0] = x + jnp.dot(y_scr[...], wout_ref[...], preferred_element_type=F32)
    pool_ref[0] = uext[base + tm - POOL_BUF:base + tm, :]
    conv_ref[0] = vext[CONV_HALO + tm - CONV_BUF:CONV_HALO + tm, :]
    uext[POOL_HALO:base, :] = uext[POOL_HALO + tm:base + tm, :]
    vext[0:CONV_HALO, :] = vext[tm:tm + CONV_HALO, :]


def _even_prompt(x, g, win, pw, ps, cw, wout, *, tm):
    B, T, D = x.shape
    w_pool = ps.shape[-1]
    w_conv = cw.shape[-1]
    assert T % tm == 0 and tm % 16 == 0 and tm >= POOL_HALO
    const = lambda *shape: pl.BlockSpec(shape, lambda b, t: (0,) * len(shape))
    return pl.pallas_call(
        functools.partial(_even_prompt_kernel, tm=tm),
        grid=(B, T // tm),
        in_specs=[
            pl.BlockSpec((1, tm, D), lambda b, t: (b, t, 0)),
            const(1, D),
            const(*win.shape),
            const(*pw.shape),
            const(1, w_pool),
            const(*cw.shape),
            const(*wout.shape),
        ],
        out_specs=[
            pl.BlockSpec((1, tm, D), lambda b, t: (b, t, 0)),
            pl.BlockSpec((1, POOL_BUF, w_pool), lambda b, t: (b, 0, 0)),
            pl.BlockSpec((1, CONV_BUF, w_conv), lambda b, t: (b, 0, 0)),
        ],
        out_shape=[
            jax.ShapeDtypeStruct((B, T, D), F32),
            jax.ShapeDtypeStruct((B, POOL_BUF, w_pool), F32),
            jax.ShapeDtypeStruct((B, CONV_BUF, w_conv), F32),
        ],
        scratch_shapes=[
            pltpu.VMEM((2 * POOL_HALO + tm, w_pool), F32),
            pltpu.VMEM((CONV_HALO + tm, w_conv), F32),
            pltpu.VMEM((2 * POOL_HALO + tm, POOL_GROUP), F32),
            pltpu.VMEM((2 * POOL_HALO + tm, POOL_GROUP), F32),
            pltpu.VMEM((tm, w_pool + w_conv), wout.dtype),
        ],
        compiler_params=pltpu.CompilerParams(
            dimension_semantics=("arbitrary", "arbitrary"), vmem_limit_bytes=VMEM_LIMIT_BYTES),
        name="even_prompt",
    )(x, g, win, pw, ps, cw, wout)


def _nt_dot(a, b):
    return lax.dot_general(a, b, (((1,), (1,)), ((), ())), preferred_element_type=F32)


def _odd_prompt_kernel(*refs, tm, nt):
    (x_ref, xres_ref, g_ref, win_ref, wo_ref, qg_ref, kg_ref, cost_ref, sint_ref, sink_ref,
     o_ref, knew_ref, vnew_ref, wkt_scr, wqvz_scr, wout_scr, qt_scr, kext, vt_ext, gate_scr, *rest) = refs

    @pl.when(pl.program_id(0) == 0)
    def _():
        qt_scr[...] = jnp.zeros(qt_scr.shape, BF16)
        gate_scr[...] = jnp.zeros(gate_scr.shape, F32)
        kext[...] = jnp.zeros(kext.shape, BF16)
        vt_ext[...] = jnp.zeros(vt_ext.shape, BF16)
        blk = 2 * LANES
        for src, dst, width in ((0, 0, ATTN_W), (ATTN_W + KV_W, ATTN_W, KV_W + ATTN_W)):
            for c in range(0, width, blk):
                wqvz_scr[dst + c:dst + c + blk, :] = win_ref[:, src + c:src + c + blk].T.astype(BF16)
        wkt_scr[...] = win_ref[:, ATTN_W:ATTN_W + KV_W].T.astype(BF16)
        for c in range(0, wo_ref.shape[1], blk):
            wout_scr[c:c + blk, :] = wo_ref[:, c:c + blk].T.astype(BF16)

    for parity in range(2):
        @pl.when(pl.program_id(0) % 2 == parity)
        def _():
            _odd_prompt_step(x_ref, xres_ref, g_ref, wkt_scr, wqvz_scr, qg_ref, kg_ref, cost_ref, sint_ref, sink_ref,
                             wout_scr, o_ref, knew_ref, vnew_ref, qt_scr, kext, vt_ext, gate_scr, *rest,
                             tm=tm, nt=nt, cur=parity, prev=1 - parity)


def _odd_prompt_step(x_ref, xres_ref, g_ref, wkt_ref, wqvz_ref, qg_ref, kg_ref, cost_ref, sint_ref, sink_ref,
                     wout_ref, o_ref, knew_ref, vnew_ref,
                     qt_scr, kext, vt_ext, gate_scr, yt_scr, s_scr, p_scr, esink_scr, *, tm, nt, cur, prev):
    step = pl.program_id(0)
    proj_t = jnp.minimum(step, pl.num_programs(0) - 2) % nt
    attn_t = jnp.maximum(step - 1, 0) % nt
    nblk = tm // WINDOW
    half = HEAD_DIM // 2

    h = _rms(x_ref[0], g_ref[...]).astype(BF16)
    reps = tm // LANES
    cos_t = cost_ref[...]
    sin_t = sint_ref[...]

    def norm_rope(xt, gain):
        ms = jnp.sum(xt * xt, axis=0, keepdims=True) * (1.0 / HEAD_DIM)
        xn = xt * lax.rsqrt(ms + RMS_EPS) * gain
        x1, x2 = xn[0:half, :], xn[half:HEAD_DIM, :]
        return jnp.concatenate([x1 * cos_t - x2 * sin_t, x2 * cos_t + x1 * sin_t], axis=0)

    has_past = proj_t > 0

    def proj_kv():
        kt = _nt_dot(wkt_ref[...], h)
        kgain = jnp.concatenate([kg_ref[...]] * reps, axis=1)
        kt = jnp.concatenate(
            [norm_rope(kt[kv * HEAD_DIM:(kv + 1) * HEAD_DIM, :], kgain) for kv in range(N_KV_HEADS)], axis=0)
        k_rows = kt.T
        knew_ref[0] = k_rows[tm - WINDOW:tm, :]
        kext[cur, 0:WINDOW, :] = jnp.where(has_past, kext[prev, tm:tm + WINDOW, :], jnp.zeros((WINDOW, KV_W), BF16))
        kext[cur, WINDOW:WINDOW + tm, :] = k_rows.astype(BF16)
        vt = _nt_dot(wqvz_ref[ATTN_W:ATTN_W + KV_W, :], h)
        vnew_ref[0] = vt[:, tm - WINDOW:tm].T
        vt_ext[cur, :, 0:WINDOW] = jnp.where(has_past, vt_ext[prev, :, tm:tm + WINDOW],
                                             jnp.zeros((KV_W, WINDOW), BF16))
        vt_ext[cur, :, WINDOW:WINDOW + tm] = vt.astype(BF16)

    def proj_gate(lo, hi):
        zt = _nt_dot(wqvz_ref[ATTN_W + KV_W + lo:ATTN_W + KV_W + hi, :], h)
        gate_scr[cur, lo:hi, :] = _silu(zt)

    def proj_q(lo, hi):
        qt = _nt_dot(wqvz_ref[lo:hi, :], h)
        qgain = jnp.concatenate([qg_ref[...]] * reps, axis=1)
        for r0 in range(0, hi - lo, HEAD_DIM):
            qt_scr[cur, lo + r0:lo + r0 + HEAD_DIM, :] = (
                norm_rope(qt[r0:r0 + HEAD_DIM, :], qgain) * (LOG2_E * HEAD_DIM ** -0.5)).astype(BF16)

    pieces = [proj_kv,
              lambda: proj_gate(0, ATTN_W // 2),
              lambda: proj_gate(ATTN_W // 2, ATTN_W),
              lambda: proj_q(0, ATTN_W // 2)]
    last_piece = lambda: proj_q(ATTN_W // 2, ATTN_W)

    ki = lax.broadcasted_iota(jnp.int32, (2 * WINDOW, 2 * WINDOW), 0)
    qi = lax.broadcasted_iota(jnp.int32, (2 * WINDOW, 2 * WINDOW), 1) % WINDOW + WINDOW
    band = (ki <= qi) & (qi - ki < WINDOW)
    lane = lax.broadcasted_iota(jnp.int32, (1, 2 * WINDOW), 1)
    zeros = jnp.zeros((HEAD_DIM, 2 * WINDOW), BF16)
    ones = jnp.ones((16, 2 * WINDOW), BF16)

    for i in range(nblk):
        c0 = i * WINDOW
        qcols = slice(c0, c0 + WINDOW)
        if i == 0:
            mask = band & (ki >= jnp.where(attn_t == 0, WINDOW, 0))
        else:
            mask = band
        pairs = [(kv, kv * GROUP + 2 * pr) for kv in range(N_KV_HEADS) for pr in range(GROUP // 2)]
        for j, (kv, ha) in enumerate(pairs):
            chunk, pos = divmod(kv, LANES // HEAD_DIM)
            kblk = kext[prev, c0:c0 + 2 * WINDOW, chunk * LANES:(chunk + 1) * LANES]
            qpair = jnp.concatenate([qt_scr[prev, ha * HEAD_DIM:(ha + 1) * HEAD_DIM, qcols],
                                     qt_scr[prev, (ha + 1) * HEAD_DIM:(ha + 2) * HEAD_DIM, qcols]], axis=1)
            rhs = jnp.concatenate([qpair, zeros] if pos == 0 else [zeros, qpair], axis=0)
            s_scr[j] = jnp.dot(kblk, rhs, preferred_element_type=F32)
        for c, piece in enumerate(pieces):
            if c * nblk // len(pieces) == i:
                piece()
        for j, (kv, ha) in enumerate(pairs):
            s = jnp.where(mask, s_scr[j], -jnp.inf)
            sink = jnp.where(lane < WINDOW, sink_ref[ha], sink_ref[ha + 1]) * LOG2_E
            m = jnp.maximum(jnp.max(s, axis=0, keepdims=True), sink)
            p_scr[j] = jnp.exp2(s - m).astype(BF16)
            esink_scr[j:j + 1, :] = jnp.exp2(sink - m)
        for j, (kv, ha) in enumerate(pairs):
            vg = vt_ext[prev, kv * HEAD_DIM:(kv + 1) * HEAD_DIM, c0:c0 + 2 * WINDOW]
            oa = jnp.dot(jnp.concatenate([vg, ones], axis=0), p_scr[j], preferred_element_type=F32)
            denom = oa[HEAD_DIM:HEAD_DIM + 1, :] + esink_scr[j:j + 1, :]
            o = oa[0:HEAD_DIM, :] * (1.0 / denom)
            for hd, part in ((ha, o[:, 0:WINDOW]), (ha + 1, o[:, WINDOW:2 * WINDOW])):
                rows = slice(hd * HEAD_DIM, (hd + 1) * HEAD_DIM)
                yt_scr[rows, qcols] = (part * gate_scr[prev, rows, qcols]).astype(BF16)

    out_t = jnp.dot(wout_ref[...], yt_scr[...], preferred_element_type=F32)
    last_piece()
    o_ref[0] = xres_ref[0] + out_t.T


def _odd_prompt(x, g, w_in, w_out, qg_t, kg_t, cos_t, sin_t, sinks, *, tm):
    B, T, D = x.shape
    assert T % tm == 0 and tm % WINDOW == 0 and w_in.shape == (D, 2 * ATTN_W + 2 * KV_W) and w_out.shape == (ATTN_W, D)
    nt = T // tm
    half = HEAD_DIM // 2
    const = lambda *shape: pl.BlockSpec(shape, lambda s: (0,) * len(shape))
    n_tiles = B * nt
    proj_tile = lambda s: jnp.minimum(s, n_tiles - 1)
    attn_tile = lambda s: jnp.maximum(s - 1, 0)
    return pl.pallas_call(
        functools.partial(_odd_prompt_kernel, tm=tm, nt=nt),
        grid=(n_tiles + 1,),
        in_specs=[
            pl.BlockSpec((1, tm, D), lambda s: (proj_tile(s) // nt, proj_tile(s) % nt, 0)),
            pl.BlockSpec((1, tm, D), lambda s: (attn_tile(s) // nt, attn_tile(s) % nt, 0)),
            const(1, D),
            const(*w_in.shape),
            const(*w_out.shape),
            const(HEAD_DIM, LANES),
            const(HEAD_DIM, LANES),
            pl.BlockSpec((half, tm), lambda s: (0, proj_tile(s) % nt)),
            pl.BlockSpec((half, tm), lambda s: (0, proj_tile(s) % nt)),
            pl.BlockSpec(memory_space=pltpu.SMEM),
        ],
        out_specs=[
            pl.BlockSpec((1, tm, D), lambda s: (attn_tile(s) // nt, attn_tile(s) % nt, 0)),
            pl.BlockSpec((1, WINDOW, KV_W), lambda s: (proj_tile(s) // nt, 0, 0)),
            pl.BlockSpec((1, WINDOW, KV_W), lambda s: (proj_tile(s) // nt, 0, 0)),
        ],
        out_shape=[
            jax.ShapeDtypeStruct((B, T, D), F32),
            jax.ShapeDtypeStruct((B, WINDOW, KV_W), F32),
            jax.ShapeDtypeStruct((B, WINDOW, KV_W), F32),
        ],
        scratch_shapes=[
            pltpu.VMEM((KV_W, D), BF16),
            pltpu.VMEM((2 * ATTN_W + KV_W, D), BF16),
            pltpu.VMEM((D, ATTN_W), BF16),
            pltpu.VMEM((2, ATTN_W, tm), BF16),
            pltpu.VMEM((2, WINDOW + tm, KV_W), BF16),
            pltpu.VMEM((2, KV_W, WINDOW + tm), BF16),
            pltpu.VMEM((2, ATTN_W, tm), F32),
            pltpu.VMEM((ATTN_W, tm), BF16),
            pltpu.VMEM((N_HEADS // 2, 2 * WINDOW, 2 * WINDOW), F32),
            pltpu.VMEM((N_HEADS // 2, 2 * WINDOW, 2 * WINDOW), BF16),
            pltpu.VMEM((N_HEADS // 2, 2 * WINDOW), F32),
        ],
        compiler_params=pltpu.CompilerParams(
            dimension_semantics=("arbitrary",), vmem_limit_bytes=VMEM_LIMIT_BYTES),
        name="odd_prompt",
    )(x, x, g, w_in, w_out, qg_t, kg_t, cos_t, sin_t, sinks)


def _dup_heads(x):
    lane = lax.broadcasted_iota(jnp.int32, x.shape, 1)
    rolled = pltpu.roll(x, HEAD_DIM, 1)
    first = lane < HEAD_DIM
    return jnp.where(first, x, rolled), jnp.where(first, rolled, x)


def _sample_dense_kernel(x_ref, sp_ref, sc_ref, g0_ref, win0_ref, pw_ref, ps_ref, cw_ref, wout0_ref,
                         g1_ref, win1_ref, qg_ref, kgt_ref, cos_ref, sin_ref, cost_ref, sint_ref,
                         x1_ref, pool_ref, conv_ref, qh_ref, kt_ref, vt_ref, gate_ref):
    w_pool = ps_ref.shape[1]
    w_conv = cw_ref.shape[1]

    x = x_ref[...]
    h = _rms(x, g0_ref[...]).astype(win0_ref.dtype)

    def proj(c, width):
        return jnp.dot(h, win0_ref[:, c:c + width], preferred_element_type=F32)

    u = proj(0, w_pool)
    z_a = proj(w_pool, w_pool)
    ys = []
    for g, w in enumerate(POOL_WINDOWS):
        cols = slice(g * POOL_GROUP, (g + 1) * POOL_GROUP)
        wsum = u[:, cols]
        for r in range(POOL_BUF - (w - 1), POOL_BUF):
            wsum = wsum + sp_ref[r, :, cols]
        cnt = float(min(PAST_LEN + 1, w))
        d = wsum * (1.0 / cnt) - u[:, cols]
        ya = jnp.dot(d.astype(pw_ref.dtype), pw_ref[g], preferred_element_type=F32)
        ys.append((ya * ps_ref[:, cols] * _silu(z_a[:, cols])).astype(wout0_ref.dtype))
    for r in range(POOL_BUF - 1):
        pool_ref[r] = sp_ref[r + 1]
    pool_ref[POOL_BUF - 1] = u

    b_gate = proj(2 * w_pool, w_conv)
    c_gate = proj(2 * w_pool + w_conv, w_conv)
    v_in = proj(2 * w_pool + 2 * w_conv, w_conv)
    v = c_gate * v_in
    conv = v * cw_ref[CONV_WIDTH - 1:CONV_WIDTH, :]
    for k in range(CONV_BUF):
        conv = conv + sc_ref[k] * cw_ref[k:k + 1, :]
    for r in range(CONV_BUF - 1):
        conv_ref[r] = sc_ref[r + 1]
    conv_ref[CONV_BUF - 1] = v
    z_b = proj(2 * w_pool + 3 * w_conv, w_conv)
    ys.append((b_gate * conv * _silu(z_b)).astype(wout0_ref.dtype))
    y = jnp.concatenate(ys, axis=1)
    x1 = x + jnp.dot(y, wout0_ref[...], preferred_element_type=F32)
    x1_ref[...] = x1

    n = x.shape[0]
    half = HEAD_DIM // 2
    h1 = _rms(x1, g1_ref[...])

    def proj1(lo, width):
        return jnp.dot(h1, win1_ref[:, lo:lo + width], preferred_element_type=F32)

    q = proj1(0, ATTN_W)
    for c in range(ATTN_W // LANES):
        qc = _head_norm_rope(q[:, c * LANES:(c + 1) * LANES], qg_ref[...], cos_ref[...], sin_ref[...])
        d0, d1 = _dup_heads(qc * (HEAD_DIM ** -0.5))
        qh_ref[2 * c * n:(2 * c + 1) * n, :] = d0
        qh_ref[(2 * c + 1) * n:(2 * c + 2) * n, :] = d1
    kt = proj1(ATTN_W, KV_W).T
    cos_t = cost_ref[...]
    sin_t = sint_ref[...]
    for kv in range(N_KV_HEADS):
        kh = kt[kv * HEAD_DIM:(kv + 1) * HEAD_DIM, :]
        ms = jnp.sum(kh * kh, axis=0, keepdims=True) * (1.0 / HEAD_DIM)
        kn = kh * lax.rsqrt(ms + RMS_EPS) * kgt_ref[...]
        x1h, x2h = kn[0:half, :], kn[half:HEAD_DIM, :]
        kt_ref[kv * HEAD_DIM:(kv + 1) * HEAD_DIM, :] = jnp.concatenate(
            [x1h * cos_t - x2h * sin_t, x2h * cos_t + x1h * sin_t], axis=0)
    vt_ref[...] = proj1(ATTN_W + KV_W, KV_W).T
    gate_ref[...] = _silu(proj1(ATTN_W + 2 * KV_W, ATTN_W))


def _sample_dense(x, sp, sc, g0, win0, pw, ps, cw, wout0, g1, win1, qg, kg_t, cos, sin, cos_t, sin_t):
    n, D = x.shape
    assert n == LANES
    vmem = pl.BlockSpec(memory_space=pltpu.VMEM)
    return pl.pallas_call(
        _sample_dense_kernel,
        in_specs=[vmem] * 17,
        out_specs=[vmem] * 7,
        out_shape=[
            jax.ShapeDtypeStruct((n, D), F32),
            jax.ShapeDtypeStruct(sp.shape, F32),
            jax.ShapeDtypeStruct(sc.shape, F32),
            jax.ShapeDtypeStruct((N_HEADS * n, LANES), F32),
            jax.ShapeDtypeStruct((KV_W, n), F32),
            jax.ShapeDtypeStruct((KV_W, n), F32),
            jax.ShapeDtypeStruct((n, ATTN_W), F32),
        ],
        compiler_params=pltpu.CompilerParams(vmem_limit_bytes=VMEM_LIMIT_BYTES),
        name="sample_dense",
    )(x, sp, sc, g0, win0, pw, ps, cw, wout0, g1, win1, qg, kg_t, cos, sin, cos_t, sin_t)


def _sample_attn_kernel(qh_ref, knt_ref, vnt_ref, ck_ref, cv_ref, sink_ref, gate_ref, x1_ref, wout_ref,
                        nk_ref, nv_ref, y_ref, o_lo, o_hi, *, bb):
    step = pl.program_id(0)
    n = x1_ref.shape[0]
    win = ck_ref.shape[2]
    blocks = LANES // HEAD_DIM

    lane = lax.broadcasted_iota(jnp.int32, (KV_W, win), 1)
    newest = lane == win - 1
    hrow = lax.broadcasted_iota(jnp.int32, (N_HEADS, LANES), 0) // GROUP
    hcol = lax.broadcasted_iota(jnp.int32, (N_HEADS, LANES), 1) // HEAD_DIM
    sink = sink_ref[...]

    scores = []
    for i in range(bb):
        b = step * bb + i
        kt = jnp.where(newest, pltpu.roll(knt_ref[...], win - 1 - b, 1), pltpu.roll(ck_ref[i], win - 1, 1))
        nk_ref[i] = kt
        nv_ref[i] = jnp.where(newest, pltpu.roll(vnt_ref[...], win - 1 - b, 1), pltpu.roll(cv_ref[i], win - 1, 1))
        qd = qh_ref[pl.ds(b, N_HEADS, stride=n), :]
        qx = jnp.concatenate([jnp.where(hcol + blocks * c == hrow, qd, 0.0) for c in range(KV_W // LANES)], axis=1)
        scores.append(jnp.dot(qx.astype(BF16), kt.astype(BF16), preferred_element_type=F32))
    probs = []
    for s in scores:
        m = jnp.maximum(jnp.max(s, axis=-1, keepdims=True), sink)
        p = jnp.exp(s - m)
        denom = jnp.sum(p, axis=-1, keepdims=True) + jnp.exp(sink - m)
        probs.append((p * (1.0 / denom)).astype(BF16))
    for i in range(bb):
        o = _nt_dot(probs[i], nv_ref[i].astype(BF16))
        rows = pl.ds(pl.multiple_of((step * bb + i) * N_HEADS, N_HEADS), N_HEADS)
        o_lo[rows, :] = o[:, 0:LANES]
        o_hi[rows, :] = o[:, LANES:2 * LANES]

    @pl.when(step == pl.num_programs(0) - 1)
    def _():
        lane_n = lax.broadcasted_iota(jnp.int32, (n, LANES), 1)
        chunks = []
        for c in range(ATTN_W // LANES):
            parts = []
            for hd in (2 * c, 2 * c + 1):
                kv = hd // GROUP
                slab = (o_lo, o_hi)[kv // blocks]
                a = slab[pl.ds(hd, n, stride=N_HEADS), :]
                parts.append(a if kv % blocks == hd % blocks else pltpu.roll(a, HEAD_DIM, 1))
            chunks.append(jnp.where(lane_n < HEAD_DIM, parts[0], parts[1]))
        y = jnp.concatenate(chunks, axis=1) * gate_ref[...]
        y_ref[...] = x1_ref[...] + jnp.dot(y, wout_ref[...], preferred_element_type=F32)


def _sample_attn(qh, knt, vnt, ckt, cvt, sink_col, gate, x1, wout, *, bb):
    n, kvw, win = ckt.shape
    D = x1.shape[1]
    assert n % bb == 0 and kvw == KV_W == 2 * LANES and win == LANES and n == LANES
    const = lambda *shape: pl.BlockSpec(shape, lambda s: (0,) * len(shape))
    return pl.pallas_call(
        functools.partial(_sample_attn_kernel, bb=bb),
        grid=(n // bb,),
        in_specs=[
            const(*qh.shape),
            const(*knt.shape),
            const(*vnt.shape),
            pl.BlockSpec((bb, kvw, win), lambda s: (s, 0, 0)),
            pl.BlockSpec((bb, kvw, win), lambda s: (s, 0, 0)),
            const(*sink_col.shape),
            const(n, ATTN_W),
            const(n, D),
            const(*wout.shape),
        ],
        out_specs=[
            pl.BlockSpec((bb, kvw, win), lambda s: (s, 0, 0)),
            pl.BlockSpec((bb, kvw, win), lambda s: (s, 0, 0)),
            const(n, D),
        ],
        out_shape=[
            jax.ShapeDtypeStruct((n, kvw, win), F32),
            jax.ShapeDtypeStruct((n, kvw, win), F32),
            jax.ShapeDtypeStruct((n, D), F32),
        ],
        scratch_shapes=[pltpu.VMEM((n * N_HEADS, LANES), F32), pltpu.VMEM((n * N_HEADS, LANES), F32)],
        compiler_params=pltpu.CompilerParams(
            dimension_semantics=("arbitrary",), vmem_limit_bytes=VMEM_LIMIT_BYTES),
        name="sample_attn",
    )(qh, knt, vnt, ckt, cvt, sink_col, gate, x1, wout)


def _rope_tables(pos):
    half = HEAD_DIM // 2
    inv = ROPE_THETA ** (-jnp.arange(half, dtype=F32) / half)
    ang = pos.astype(F32)[:, None] * inv[None, :]
    return jnp.cos(ang), jnp.sin(ang)


def kernel(x_prompt, x_sample, state_pool, state_conv, cache_k, cache_v, norm_g, w_in_even, pool_w, pool_scale,
           conv_w, w_out_even, w_in_odd, q_norm_g, k_norm_g, attn_sinks, w_out_odd):
    B, T, D = x_prompt.shape
    n_s, t_s, _ = x_sample.shape
    assert norm_g.shape[0] == 2 and w_in_even.shape[0] == 1 and w_in_odd.shape[0] == 1
    assert t_s == 1 and cache_k.shape[2] == WINDOW and T >= WINDOW
    assert cache_k.shape[3] * cache_k.shape[4] == KV_W and pool_w.shape[1:] == (len(POOL_WINDOWS), POOL_GROUP, POOL_GROUP)

    g0 = norm_g[0][None, :]
    g1 = norm_g[1][None, :]
    win0 = w_in_even[0]
    wout0 = w_out_even[0]
    pw = pool_w[0]
    ps = pool_scale[0][None, :]
    cw = conv_w[0]
    reps = LANES // HEAD_DIM
    qg = jnp.tile(q_norm_g[0], reps)[None, :]
    sinks = attn_sinks[0]

    cos_p, sin_p = _rope_tables(jnp.arange(T))
    cos_pt, sin_pt = cos_p.T, sin_p.T
    cos_1, sin_1 = _rope_tables(PAST_LEN + jnp.arange(t_s))
    cos_s = jnp.tile(jnp.concatenate([cos_1, cos_1], axis=1), (1, reps))
    sin_s = jnp.tile(jnp.concatenate([-sin_1, sin_1], axis=1), (1, reps))
    cos_st = jnp.broadcast_to(cos_1.T, (HEAD_DIM // 2, LANES))
    sin_st = jnp.broadcast_to(sin_1.T, (HEAD_DIM // 2, LANES))

    win1 = w_in_odd[0]
    wout1 = w_out_odd[0]
    qg_t = jnp.broadcast_to(q_norm_g[0][:, None], (HEAD_DIM, LANES))
    kg_t = jnp.broadcast_to(k_norm_g[0][:, None], (HEAD_DIM, LANES))

    x1p, pool_p, conv_p = _even_prompt(x_prompt, g0, win0, pw, ps, cw, wout0, tm=1024)
    y_p, k_p, v_p = _odd_prompt(x1p, g1, win1, wout1, qg_t, kg_t, cos_pt, sin_pt, sinks, tm=512)

    sp = jnp.transpose(state_pool[0], (1, 0, 2))
    sc = jnp.transpose(state_conv[0], (1, 0, 2))
    x1s, pool_s, conv_s, qh_s, kt_s, vt_s, gate_s = _sample_dense(
        x_sample[:, 0, :], sp, sc, g0, win0, pw, ps, cw, wout0, g1, win1, qg, kg_t, cos_s, sin_s,
        cos_st, sin_st)
    pool_s = jnp.transpose(pool_s, (1, 0, 2))
    conv_s = jnp.transpose(conv_s, (1, 0, 2))

    ckt = jnp.transpose(cache_k[0], (0, 2, 3, 1)).reshape(n_s, KV_W, WINDOW)
    cvt = jnp.transpose(cache_v[0], (0, 2, 3, 1)).reshape(n_s, KV_W, WINDOW)
    nkt_s, nvt_s, y_s = _sample_attn(qh_s, kt_s, vt_s, ckt, cvt, sinks[:, None], gate_s, x1s, wout1, bb=8)
    nk_s = jnp.transpose(nkt_s.reshape(n_s, N_KV_HEADS, HEAD_DIM, WINDOW), (0, 3, 1, 2))
    nv_s = jnp.transpose(nvt_s.reshape(n_s, N_KV_HEADS, HEAD_DIM, WINDOW), (0, 3, 1, 2))

    kv_shape = (1, -1, WINDOW, N_KV_HEADS, HEAD_DIM)
    return (y_p, y_s[:, None, :], pool_p[None], pool_s[None], conv_p[None], conv_s[None],
            k_p.reshape(kv_shape), v_p.reshape(kv_shape), nk_s[None], nv_s[None])
```

```python
import functools

import jax
import jax.numpy as jnp
from jax import lax
from jax.experimental import pallas as pl
from jax.experimental.pallas import tpu as pltpu

F32 = jnp.float32
BF16 = jnp.bfloat16

POOL_WINDOWS = (2, 4, 8, 16)
POOL_GROUP = 128
POOL_BUF = max(POOL_WINDOWS) - 1
CONV_WIDTH = 3
CONV_BUF = CONV_WIDTH - 1
N_HEADS = 16
HEAD_DIM = 64
N_KV_HEADS = 4
GROUP = N_HEADS // N_KV_HEADS
WINDOW = 128
ROPE_THETA = 10000.0
RMS_EPS = 1e-6
PAST_LEN = 16384
ATTN_W = N_HEADS * HEAD_DIM
KV_W = N_KV_HEADS * HEAD_DIM

LANES = 128
POOL_HALO = 16
CONV_HALO = 8
VMEM_LIMIT_BYTES = 60 * 1024 * 1024
LOG2_E = 1.4426950408889634


def _rms(x, g):
    ms = jnp.mean(x * x, axis=-1, keepdims=True)
    return x * lax.rsqrt(ms + RMS_EPS) * g


def _silu(z):
    return z * jax.nn.sigmoid(z)


def _head_norm_rope(x, gain, cos, sin):
    lane = lax.broadcasted_iota(jnp.int32, x.shape, 1)
    first = lane < HEAD_DIM
    x2 = x * x
    ss0 = jnp.sum(jnp.where(first, x2, 0.0), axis=-1, keepdims=True)
    ss1 = jnp.sum(jnp.where(first, 0.0, x2), axis=-1, keepdims=True)
    r = jnp.where(first, lax.rsqrt(ss0 * (1.0 / HEAD_DIM) + RMS_EPS), lax.rsqrt(ss1 * (1.0 / HEAD_DIM) + RMS_EPS))
    xn = x * r * gain
    half = HEAD_DIM // 2
    swapped = jnp.where((lane % HEAD_DIM) < half, pltpu.roll(xn, LANES - half, 1), pltpu.roll(xn, half, 1))
    return xn * cos + swapped * sin


def _nt_dot(a, b):
    return lax.dot_general(a, b, (((1,), (1,)), ((), ())), preferred_element_type=F32)


class _SampleAttention:
    def __init__(self, qh_ref, knt_ref, vnt_ref, ck_ref, cv_ref, sink_ref, nk_ref, nv_ref, olo_ref, ohi_ref, *, step):
        self.refs = (qh_ref, knt_ref, vnt_ref, ck_ref, cv_ref, sink_ref, nk_ref, nv_ref, olo_ref, ohi_ref)
        self.bb = ck_ref.shape[0]
        self.first = step * self.bb

    def roll(self):
        _, knt_ref, vnt_ref, ck_ref, cv_ref, _, nk_ref, nv_ref, _, _ = self.refs
        win = ck_ref.shape[2]
        newest = lax.broadcasted_iota(jnp.int32, (KV_W, win), 1) == win - 1
        for i in range(self.bb):
            b = self.first + i
            nk_ref[i] = jnp.where(newest, pltpu.roll(knt_ref[...], win - 1 - b, 1), pltpu.roll(ck_ref[i], win - 1, 1))
            nv_ref[i] = jnp.where(newest, pltpu.roll(vnt_ref[...], win - 1 - b, 1), pltpu.roll(cv_ref[i], win - 1, 1))

    def scores(self):
        qh_ref, nk_ref = self.refs[0], self.refs[6]
        n = qh_ref.shape[0] // N_HEADS
        blocks = LANES // HEAD_DIM
        hrow = lax.broadcasted_iota(jnp.int32, (N_HEADS, LANES), 0) // GROUP
        hcol = lax.broadcasted_iota(jnp.int32, (N_HEADS, LANES), 1) // HEAD_DIM
        self.s = []
        for i in range(self.bb):
            qd = qh_ref[pl.ds(self.first + i, N_HEADS, stride=n), :]
            qx = jnp.concatenate(
                [jnp.where(hcol + blocks * c == hrow, qd, 0.0) for c in range(KV_W // LANES)], axis=1)
            self.s.append(jnp.dot(qx.astype(BF16), nk_ref[i].astype(BF16), preferred_element_type=F32))

    def softmax(self):
        sink = self.refs[5][...]
        self.p = []
        for s in self.s:
            m = jnp.maximum(jnp.max(s, axis=-1, keepdims=True), sink)
            p = jnp.exp(s - m)
            denom = jnp.sum(p, axis=-1, keepdims=True) + jnp.exp(sink - m)
            self.p.append((p * (1.0 / denom)).astype(BF16))

    def outputs(self):
        nv_ref, olo_ref, ohi_ref = self.refs[7:10]
        for i in range(self.bb):
            o = _nt_dot(self.p[i], nv_ref[i].astype(BF16))
            rows = pl.ds(pl.multiple_of((self.first + i) * N_HEADS, N_HEADS), N_HEADS)
            olo_ref[rows, :] = o[:, 0:LANES]
            ohi_ref[rows, :] = o[:, LANES:2 * LANES]


def _even_prompt_kernel(x_ref, g_ref, win_ref, pw_ref, ps_ref, cw_ref, wout_ref,
                        qh_ref, knt_ref, vnt_ref, ck_ref, cv_ref, sink_ref,
                        o_ref, pool_ref, conv_ref, nk_ref, nv_ref, olo_ref, ohi_ref,
                        uext, vext, sa, sb, y_scr, *, tm):
    t = pl.program_id(1)
    w_pool = uext.shape[1]
    w_conv = vext.shape[1]
    base = 2 * POOL_HALO
    side = _SampleAttention(qh_ref, knt_ref, vnt_ref, ck_ref, cv_ref, sink_ref, nk_ref, nv_ref, olo_ref, ohi_ref,
                            step=pl.program_id(0) * pl.num_programs(1) + t)

    @pl.when(t == 0)
    def _():
        uext[0:base, :] = jnp.zeros((base, w_pool), F32)
        vext[0:CONV_HALO, :] = jnp.zeros((CONV_HALO, w_conv), F32)
        sa[0:POOL_HALO, :] = jnp.zeros((POOL_HALO, POOL_GROUP), F32)
        sb[0:POOL_HALO, :] = jnp.zeros((POOL_HALO, POOL_GROUP), F32)

    x = x_ref[0]
    h = _rms(x, g_ref[...]).astype(win_ref.dtype)

    def proj(c, width):
        return jnp.dot(h, win_ref[:, c:c + width], preferred_element_type=F32)

    side.roll()
    u = proj(0, w_pool)
    uext[base:base + tm, :] = u
    pos = t * tm + lax.broadcasted_iota(jnp.int32, (tm, 1), 0)
    side.scores()
    z_a = proj(w_pool, w_pool)
    n_ext = POOL_HALO + tm

    for g, w in enumerate(POOL_WINDOWS):
        lo = g * POOL_GROUP
        cols = slice(lo, lo + POOL_GROUP)
        src = uext
        src_cols = cols
        step = 1
        bufs = (sa, sb)
        nbuf = 0
        while 2 * step < w:
            dst = bufs[nbuf % 2]
            dst[POOL_HALO:POOL_HALO + n_ext, :] = (src[POOL_HALO:POOL_HALO + n_ext, src_cols]
                                                   + src[POOL_HALO - step:POOL_HALO - step + n_ext, src_cols])
            src, src_cols = dst, slice(0, POOL_GROUP)
            step *= 2
            nbuf += 1
        wsum = src[base:base + tm, src_cols] + src[base - step:base - step + tm, src_cols]
        cnt = jnp.minimum(pos + 1, w).astype(F32)
        d = wsum * (1.0 / cnt) - u[:, cols]
        ya = jnp.dot(d.astype(pw_ref.dtype), pw_ref[g], preferred_element_type=F32)
        ya = ya * ps_ref[:, cols] * _silu(z_a[:, cols])
        y_scr[:, cols] = ya.astype(y_scr.dtype)

    b_gate = proj(2 * w_pool, w_conv)
    side.softmax()
    c_gate = proj(2 * w_pool + w_conv, w_conv)
    side.outputs()
    v_in = proj(2 * w_pool + 2 * w_conv, w_conv)
    v = c_gate * v_in
    vext[CONV_HALO:CONV_HALO + tm, :] = v
    conv = v * cw_ref[CONV_WIDTH - 1:CONV_WIDTH, :]
    for k in range(CONV_WIDTH - 1):
        shift = CONV_WIDTH - 1 - k
        conv = conv + vext[CONV_HALO - shift:CONV_HALO - shift + tm, :] * cw_ref[k:k + 1, :]
    z_b = proj(2 * w_pool + 3 * w_conv, w_conv)
    y_b = b_gate * conv * _silu(z_b)
    y_scr[:, w_pool:w_pool + w_conv] = y_b.astype(y_scr.dtype)

    o_ref[0] = x + jnp.dot(y_scr[...], wout_ref[...], preferred_element_type=F32)
    pool_ref[0] = uext[base + tm - POOL_BUF:base + tm, :]
    conv_ref[0] = vext[CONV_HALO + tm - CONV_BUF:CONV_HALO + tm, :]
    uext[POOL_HALO:base, :] = uext[POOL_HALO + tm:base + tm, :]
    vext[0:CONV_HALO, :] = vext[tm:tm + CONV_HALO, :]


def _even_prompt(x, g, win, pw, ps, cw, wout, qh, knt, vnt, ckt, cvt, sink_col, *, tm):
    B, T, D = x.shape
    w_pool = ps.shape[-1]
    w_conv = cw.shape[-1]
    assert T % tm == 0 and tm % 16 == 0 and tm >= POOL_HALO
    nt = T // tm
    n, kvw, win_len = ckt.shape
    assert n % (B * nt) == 0 and kvw == KV_W == 2 * LANES and win_len == LANES and n == LANES
    bb = n // (B * nt)
    const = lambda *shape: pl.BlockSpec(shape, lambda b, t: (0,) * len(shape))
    cache_block = pl.BlockSpec((bb, kvw, win_len), lambda b, t: (b * nt + t, 0, 0))
    return pl.pallas_call(
        functools.partial(_even_prompt_kernel, tm=tm),
        grid=(B, nt),
        in_specs=[
            pl.BlockSpec((1, tm, D), lambda b, t: (b, t, 0)),
            const(1, D),
            const(*win.shape),
            const(*pw.shape),
            const(1, w_pool),
            const(*cw.shape),
            const(*wout.shape),
            const(*qh.shape),
            const(*knt.shape),
            const(*vnt.shape),
            cache_block,
            cache_block,
            const(*sink_col.shape),
        ],
        out_specs=[
            pl.BlockSpec((1, tm, D), lambda b, t: (b, t, 0)),
            pl.BlockSpec((1, POOL_BUF, w_pool), lambda b, t: (b, 0, 0)),
            pl.BlockSpec((1, CONV_BUF, w_conv), lambda b, t: (b, 0, 0)),
            cache_block,
            cache_block,
            const(n * N_HEADS, LANES),
            const(n * N_HEADS, LANES),
        ],
        out_shape=[
            jax.ShapeDtypeStruct((B, T, D), F32),
            jax.ShapeDtypeStruct((B, POOL_BUF, w_pool), F32),
            jax.ShapeDtypeStruct((B, CONV_BUF, w_conv), F32),
            jax.ShapeDtypeStruct((n, kvw, win_len), F32),
            jax.ShapeDtypeStruct((n, kvw, win_len), F32),
            jax.ShapeDtypeStruct((n * N_HEADS, LANES), F32),
            jax.ShapeDtypeStruct((n * N_HEADS, LANES), F32),
        ],
        scratch_shapes=[
            pltpu.VMEM((2 * POOL_HALO + tm, w_pool), F32),
            pltpu.VMEM((CONV_HALO + tm, w_conv), F32),
            pltpu.VMEM((2 * POOL_HALO + tm, POOL_GROUP), F32),
            pltpu.VMEM((2 * POOL_HALO + tm, POOL_GROUP), F32),
            pltpu.VMEM((tm, w_pool + w_conv), wout.dtype),
        ],
        compiler_params=pltpu.CompilerParams(
            dimension_semantics=("arbitrary", "arbitrary"), vmem_limit_bytes=VMEM_LIMIT_BYTES),
        name="even_prompt",
    )(x, g, win, pw, ps, cw, wout, qh, knt, vnt, ckt, cvt, sink_col)


def _odd_prompt_kernel(*refs, tm, nt):
    (x_ref, xres_ref, g_ref, win_ref, wo_ref, qg_ref, kg_ref, cost_ref, sint_ref, sink_ref,
     o_ref, knew_ref, vnew_ref, wkt_scr, wqvz_scr, wout_scr, qt_scr, kext, vt_ext, gate_scr, *rest) = refs

    @pl.when(pl.program_id(0) == 0)
    def _():
        qt_scr[...] = jnp.zeros(qt_scr.shape, BF16)
        gate_scr[...] = jnp.zeros(gate_scr.shape, F32)
        kext[...] = jnp.zeros(kext.shape, BF16)
        vt_ext[...] = jnp.zeros(vt_ext.shape, BF16)
        blk = 2 * LANES
        for src, dst, width in ((0, 0, ATTN_W), (ATTN_W + KV_W, ATTN_W, KV_W + ATTN_W)):
            for c in range(0, width, blk):
                wqvz_scr[dst + c:dst + c + blk, :] = win_ref[:, src + c:src + c + blk].T.astype(BF16)
        wkt_scr[...] = win_ref[:, ATTN_W:ATTN_W + KV_W].T.astype(BF16)
        for c in range(0, wo_ref.shape[1], blk):
            wout_scr[c:c + blk, :] = wo_ref[:, c:c + blk].T.astype(BF16)

    for parity in range(2):
        @pl.when(pl.program_id(0) % 2 == parity)
        def _():
            _odd_prompt_step(x_ref, xres_ref, g_ref, wkt_scr, wqvz_scr, qg_ref, kg_ref, cost_ref, sint_ref, sink_ref,
                             wout_scr, o_ref, knew_ref, vnew_ref, qt_scr, kext, vt_ext, gate_scr, *rest,
                             tm=tm, nt=nt, cur=parity, prev=1 - parity)


def _odd_prompt_step(x_ref, xres_ref, g_ref, wkt_ref, wqvz_ref, qg_ref, kg_ref, cost_ref, sint_ref, sink_ref,
                     wout_ref, o_ref, knew_ref, vnew_ref,
                     qt_scr, kext, vt_ext, gate_scr, yt_scr, s_scr, p_scr, esink_scr, *, tm, nt, cur, prev):
    step = pl.program_id(0)
    proj_t = jnp.minimum(step, pl.num_programs(0) - 2) % nt
    attn_t = jnp.maximum(step - 1, 0) % nt
    nblk = tm // WINDOW
    half = HEAD_DIM // 2

    h = _rms(x_ref[0], g_ref[...]).astype(BF16)
    reps = tm // LANES
    cos_t = cost_ref[...]
    sin_t = sint_ref[...]

    def norm_rope(xt, gain):
        ms = jnp.sum(xt * xt, axis=0, keepdims=True) * (1.0 / HEAD_DIM)
        xn = xt * lax.rsqrt(ms + RMS_EPS) * gain
        x1, x2 = xn[0:half, :], xn[half:HEAD_DIM, :]
        return jnp.concatenate([x1 * cos_t - x2 * sin_t, x2 * cos_t + x1 * sin_t], axis=0)

    has_past = proj_t > 0

    def proj_kv():
        kt = _nt_dot(wkt_ref[...], h)
        kgain = jnp.concatenate([kg_ref[...]] * reps, axis=1)
        kt = jnp.concatenate(
            [norm_rope(kt[kv * HEAD_DIM:(kv + 1) * HEAD_DIM, :], kgain) for kv in range(N_KV_HEADS)], axis=0)
        k_rows = kt.T
        knew_ref[0] = k_rows[tm - WINDOW:tm, :]
        kext[cur, 0:WINDOW, :] = jnp.where(has_past, kext[prev, tm:tm + WINDOW, :], jnp.zeros((WINDOW, KV_W), BF16))
        kext[cur, WINDOW:WINDOW + tm, :] = k_rows.astype(BF16)
        vt = _nt_dot(wqvz_ref[ATTN_W:ATTN_W + KV_W, :], h)
        vnew_ref[0] = vt[:, tm - WINDOW:tm].T
        vt_ext[cur, :, 0:WINDOW] = jnp.where(has_past, vt_ext[prev, :, tm:tm + WINDOW],
                                             jnp.zeros((KV_W, WINDOW), BF16))
        vt_ext[cur, :, WINDOW:WINDOW + tm] = vt.astype(BF16)

    def proj_gate(lo, hi):
        zt = _nt_dot(wqvz_ref[ATTN_W + KV_W + lo:ATTN_W + KV_W + hi, :], h)
        gate_scr[cur, lo:hi, :] = _silu(zt)

    def proj_q(lo, hi):
        qt = _nt_dot(wqvz_ref[lo:hi, :], h)
        qgain = jnp.concatenate([qg_ref[...]] * reps, axis=1)
        for r0 in range(0, hi - lo, HEAD_DIM):
            qt_scr[cur, lo + r0:lo + r0 + HEAD_DIM, :] = (
                norm_rope(qt[r0:r0 + HEAD_DIM, :], qgain) * (LOG2_E * HEAD_DIM ** -0.5)).astype(BF16)

    pieces = [proj_kv,
              lambda: proj_gate(0, ATTN_W // 2),
              lambda: proj_gate(ATTN_W // 2, ATTN_W),
              lambda: proj_q(0, ATTN_W // 2)]
    last_piece = lambda: proj_q(ATTN_W // 2, ATTN_W)

    ki = lax.broadcasted_iota(jnp.int32, (2 * WINDOW, 2 * WINDOW), 0)
    qi = lax.broadcasted_iota(jnp.int32, (2 * WINDOW, 2 * WINDOW), 1) % WINDOW + WINDOW
    band = (ki <= qi) & (qi - ki < WINDOW)
    lane = lax.broadcasted_iota(jnp.int32, (1, 2 * WINDOW), 1)
    zeros = jnp.zeros((HEAD_DIM, 2 * WINDOW), BF16)
    ones = jnp.ones((16, 2 * WINDOW), BF16)

    for i in range(nblk):
        c0 = i * WINDOW
        qcols = slice(c0, c0 + WINDOW)
        if i == 0:
            mask = band & (ki >= jnp.where(attn_t == 0, WINDOW, 0))
        else:
            mask = band
        pairs = [(kv, kv * GROUP + 2 * pr) for kv in range(N_KV_HEADS) for pr in range(GROUP // 2)]
        for j, (kv, ha) in enumerate(pairs):
            chunk, pos = divmod(kv, LANES // HEAD_DIM)
            kblk = kext[prev, c0:c0 + 2 * WINDOW, chunk * LANES:(chunk + 1) * LANES]
            qpair = jnp.concatenate([qt_scr[prev, ha * HEAD_DIM:(ha + 1) * HEAD_DIM, qcols],
                                     qt_scr[prev, (ha + 1) * HEAD_DIM:(ha + 2) * HEAD_DIM, qcols]], axis=1)
            rhs = jnp.concatenate([qpair, zeros] if pos == 0 else [zeros, qpair], axis=0)
            s_scr[j] = jnp.dot(kblk, rhs, preferred_element_type=F32)
        for c, piece in enumerate(pieces):
            if c * nblk // len(pieces) == i:
                piece()
        for j, (kv, ha) in enumerate(pairs):
            s = jnp.where(mask, s_scr[j], -jnp.inf)
            sink = jnp.where(lane < WINDOW, sink_ref[ha], sink_ref[ha + 1]) * LOG2_E
            m = jnp.maximum(jnp.max(s, axis=0, keepdims=True), sink)
            p_scr[j] = jnp.exp2(s - m).astype(BF16)
            esink_scr[j:j + 1, :] = jnp.exp2(sink - m)
        for j, (kv, ha) in enumerate(pairs):
            vg = vt_ext[prev, kv * HEAD_DIM:(kv + 1) * HEAD_DIM, c0:c0 + 2 * WINDOW]
            oa = jnp.dot(jnp.concatenate([vg, ones], axis=0), p_scr[j], preferred_element_type=F32)
            denom = oa[HEAD_DIM:HEAD_DIM + 1, :] + esink_scr[j:j + 1, :]
            o = oa[0:HEAD_DIM, :] * (1.0 / denom)
            for hd, part in ((ha, o[:, 0:WINDOW]), (ha + 1, o[:, WINDOW:2 * WINDOW])):
                rows = slice(hd * HEAD_DIM, (hd + 1) * HEAD_DIM)
                yt_scr[rows, qcols] = (part * gate_scr[prev, rows, qcols]).astype(BF16)

    out_t = jnp.dot(wout_ref[...], yt_scr[...], preferred_element_type=F32)
    last_piece()
    o_ref[0] = xres_ref[0] + out_t.T


def _odd_prompt(x, g, w_in, w_out, qg_t, kg_t, cos_t, sin_t, sinks, *, tm):
    B, T, D = x.shape
    assert T % tm == 0 and tm % WINDOW == 0 and w_in.shape == (D, 2 * ATTN_W + 2 * KV_W) and w_out.shape == (ATTN_W, D)
    nt = T // tm
    half = HEAD_DIM // 2
    const = lambda *shape: pl.BlockSpec(shape, lambda s: (0,) * len(shape))
    n_tiles = B * nt
    proj_tile = lambda s: jnp.minimum(s, n_tiles - 1)
    attn_tile = lambda s: jnp.maximum(s - 1, 0)
    return pl.pallas_call(
        functools.partial(_odd_prompt_kernel, tm=tm, nt=nt),
        grid=(n_tiles + 1,),
        in_specs=[
            pl.BlockSpec((1, tm, D), lambda s: (proj_tile(s) // nt, proj_tile(s) % nt, 0)),
            pl.BlockSpec((1, tm, D), lambda s: (attn_tile(s) // nt, attn_tile(s) % nt, 0)),
            const(1, D),
            const(*w_in.shape),
            const(*w_out.shape),
            const(HEAD_DIM, LANES),
            const(HEAD_DIM, LANES),
            pl.BlockSpec((half, tm), lambda s: (0, proj_tile(s) % nt)),
            pl.BlockSpec((half, tm), lambda s: (0, proj_tile(s) % nt)),
            pl.BlockSpec(memory_space=pltpu.SMEM),
        ],
        out_specs=[
            pl.BlockSpec((1, tm, D), lambda s: (attn_tile(s) // nt, attn_tile(s) % nt, 0)),
            pl.BlockSpec((1, WINDOW, KV_W), lambda s: (proj_tile(s) // nt, 0, 0)),
            pl.BlockSpec((1, WINDOW, KV_W), lambda s: (proj_tile(s) // nt, 0, 0)),
        ],
        out_shape=[
            jax.ShapeDtypeStruct((B, T, D), F32),
            jax.ShapeDtypeStruct((B, WINDOW, KV_W), F32),
            jax.ShapeDtypeStruct((B, WINDOW, KV_W), F32),
        ],
        scratch_shapes=[
            pltpu.VMEM((KV_W, D), BF16),
            pltpu.VMEM((2 * ATTN_W + KV_W, D), BF16),
            pltpu.VMEM((D, ATTN_W), BF16),
            pltpu.VMEM((2, ATTN_W, tm), BF16),
            pltpu.VMEM((2, WINDOW + tm, KV_W), BF16),
            pltpu.VMEM((2, KV_W, WINDOW + tm), BF16),
            pltpu.VMEM((2, ATTN_W, tm), F32),
            pltpu.VMEM((ATTN_W, tm), BF16),
            pltpu.VMEM((N_HEADS // 2, 2 * WINDOW, 2 * WINDOW), F32),
            pltpu.VMEM((N_HEADS // 2, 2 * WINDOW, 2 * WINDOW), BF16),
            pltpu.VMEM((N_HEADS // 2, 2 * WINDOW), F32),
        ],
        compiler_params=pltpu.CompilerParams(
            dimension_semantics=("arbitrary",), vmem_limit_bytes=VMEM_LIMIT_BYTES),
        name="odd_prompt",
    )(x, x, g, w_in, w_out, qg_t, kg_t, cos_t, sin_t, sinks)


def _dup_heads(x):
    lane = lax.broadcasted_iota(jnp.int32, x.shape, 1)
    rolled = pltpu.roll(x, HEAD_DIM, 1)
    first = lane < HEAD_DIM
    return jnp.where(first, x, rolled), jnp.where(first, rolled, x)


def _sample_dense_kernel(x_ref, sp_ref, sc_ref, g0_ref, win0_ref, pw_ref, ps_ref, cw_ref, wout0_ref,
                         g1_ref, win1_ref, qg_ref, kgt_ref, cos_ref, sin_ref, cost_ref, sint_ref,
                         x1_ref, pool_ref, conv_ref, qh_ref, kt_ref, vt_ref, gate_ref):
    w_pool = ps_ref.shape[1]
    w_conv = cw_ref.shape[1]

    x = x_ref[...]
    h = _rms(x, g0_ref[...]).astype(win0_ref.dtype)

    def proj(c, width):
        return jnp.dot(h, win0_ref[:, c:c + width], preferred_element_type=F32)

    u = proj(0, w_pool)
    z_a = proj(w_pool, w_pool)
    ys = []
    for g, w in enumerate(POOL_WINDOWS):
        cols = slice(g * POOL_GROUP, (g + 1) * POOL_GROUP)
        wsum = u[:, cols]
        for r in range(POOL_BUF - (w - 1), POOL_BUF):
            wsum = wsum + sp_ref[r, :, cols]
        cnt = float(min(PAST_LEN + 1, w))
        d = wsum * (1.0 / cnt) - u[:, cols]
        ya = jnp.dot(d.astype(pw_ref.dtype), pw_ref[g], preferred_element_type=F32)
        ys.append((ya * ps_ref[:, cols] * _silu(z_a[:, cols])).astype(wout0_ref.dtype))
    for r in range(POOL_BUF - 1):
        pool_ref[r] = sp_ref[r + 1]
    pool_ref[POOL_BUF - 1] = u

    b_gate = proj(2 * w_pool, w_conv)
    c_gate = proj(2 * w_pool + w_conv, w_conv)
    v_in = proj(2 * w_pool + 2 * w_conv, w_conv)
    v = c_gate * v_in
    conv = v * cw_ref[CONV_WIDTH - 1:CONV_WIDTH, :]
    for k in range(CONV_BUF):
        conv = conv + sc_ref[k] * cw_ref[k:k + 1, :]
    for r in range(CONV_BUF - 1):
        conv_ref[r] = sc_ref[r + 1]
    conv_ref[CONV_BUF - 1] = v
    z_b = proj(2 * w_pool + 3 * w_conv, w_conv)
    ys.append((b_gate * conv * _silu(z_b)).astype(wout0_ref.dtype))
    y = jnp.concatenate(ys, axis=1)
    x1 = x + jnp.dot(y, wout0_ref[...], preferred_element_type=F32)
    x1_ref[...] = x1

    n = x.shape[0]
    half = HEAD_DIM // 2
    h1 = _rms(x1, g1_ref[...])

    def proj1(lo, width):
        return jnp.dot(h1, win1_ref[:, lo:lo + width], preferred_element_type=F32)

    q = proj1(0, ATTN_W)
    for c in range(ATTN_W // LANES):
        qc = _head_norm_rope(q[:, c * LANES:(c + 1) * LANES], qg_ref[...], cos_ref[...], sin_ref[...])
        d0, d1 = _dup_heads(qc * (HEAD_DIM ** -0.5))
        qh_ref[2 * c * n:(2 * c + 1) * n, :] = d0
        qh_ref[(2 * c + 1) * n:(2 * c + 2) * n, :] = d1
    kt = proj1(ATTN_W, KV_W).T
    cos_t = cost_ref[...]
    sin_t = sint_ref[...]
    for kv in range(N_KV_HEADS):
        kh = kt[kv * HEAD_DIM:(kv + 1) * HEAD_DIM, :]
        ms = jnp.sum(kh * kh, axis=0, keepdims=True) * (1.0 / HEAD_DIM)
        kn = kh * lax.rsqrt(ms + RMS_EPS) * kgt_ref[...]
        x1h, x2h = kn[0:half, :], kn[half:HEAD_DIM, :]
        kt_ref[kv * HEAD_DIM:(kv + 1) * HEAD_DIM, :] = jnp.concatenate(
            [x1h * cos_t - x2h * sin_t, x2h * cos_t + x1h * sin_t], axis=0)
    vt_ref[...] = proj1(ATTN_W + KV_W, KV_W).T
    gate_ref[...] = _silu(proj1(ATTN_W + 2 * KV_W, ATTN_W))


def _sample_dense(x, sp, sc, g0, win0, pw, ps, cw, wout0, g1, win1, qg, kg_t, cos, sin, cos_t, sin_t):
    n, D = x.shape
    assert n == LANES
    vmem = pl.BlockSpec(memory_space=pltpu.VMEM)
    return pl.pallas_call(
        _sample_dense_kernel,
        in_specs=[vmem] * 17,
        out_specs=[vmem] * 7,
        out_shape=[
            jax.ShapeDtypeStruct((n, D), F32),
            jax.ShapeDtypeStruct(sp.shape, F32),
            jax.ShapeDtypeStruct(sc.shape, F32),
            jax.ShapeDtypeStruct((N_HEADS * n, LANES), F32),
            jax.ShapeDtypeStruct((KV_W, n), F32),
            jax.ShapeDtypeStruct((KV_W, n), F32),
            jax.ShapeDtypeStruct((n, ATTN_W), F32),
        ],
        compiler_params=pltpu.CompilerParams(vmem_limit_bytes=VMEM_LIMIT_BYTES),
        name="sample_dense",
    )(x, sp, sc, g0, win0, pw, ps, cw, wout0, g1, win1, qg, kg_t, cos, sin, cos_t, sin_t)


def _sample_out_kernel(olo_ref, ohi_ref, gate_ref, x1_ref, wout_ref, y_ref):
    n = x1_ref.shape[0]
    blocks = LANES // HEAD_DIM
    lane_n = lax.broadcasted_iota(jnp.int32, (n, LANES), 1)
    chunks = []
    for c in range(ATTN_W // LANES):
        parts = []
        for hd in (2 * c, 2 * c + 1):
            kv = hd // GROUP
            slab = (olo_ref, ohi_ref)[kv // blocks]
            a = slab[pl.ds(hd, n, stride=N_HEADS), :]
            parts.append(a if kv % blocks == hd % blocks else pltpu.roll(a, HEAD_DIM, 1))
        chunks.append(jnp.where(lane_n < HEAD_DIM, parts[0], parts[1]))
    y = jnp.concatenate(chunks, axis=1) * gate_ref[...]
    y_ref[...] = x1_ref[...] + jnp.dot(y, wout_ref[...], preferred_element_type=F32)


def _sample_out(olo, ohi, gate, x1, wout):
    vmem = pl.BlockSpec(memory_space=pltpu.VMEM)
    return pl.pallas_call(
        _sample_out_kernel,
        in_specs=[vmem] * 5,
        out_specs=vmem,
        out_shape=jax.ShapeDtypeStruct(x1.shape, F32),
        compiler_params=pltpu.CompilerParams(vmem_limit_bytes=VMEM_LIMIT_BYTES),
        name="sample_out",
    )(olo, ohi, gate, x1, wout)


def _rope_tables(pos):
    half = HEAD_DIM // 2
    inv = ROPE_THETA ** (-jnp.arange(half, dtype=F32) / half)
    ang = pos.astype(F32)[:, None] * inv[None, :]
    return jnp.cos(ang), jnp.sin(ang)


def kernel(x_prompt, x_sample, state_pool, state_conv, cache_k, cache_v, norm_g, w_in_even, pool_w, pool_scale,
           conv_w, w_out_even, w_in_odd, q_norm_g, k_norm_g, attn_sinks, w_out_odd):
    B, T, D = x_prompt.shape
    n_s, t_s, _ = x_sample.shape
    assert norm_g.shape[0] == 2 and w_in_even.shape[0] == 1 and w_in_odd.shape[0] == 1
    assert t_s == 1 and cache_k.shape[2] == WINDOW and T >= WINDOW
    assert cache_k.shape[3] * cache_k.shape[4] == KV_W and pool_w.shape[1:] == (len(POOL_WINDOWS), POOL_GROUP, POOL_GROUP)

    g0 = norm_g[0][None, :]
    g1 = norm_g[1][None, :]
    win0 = w_in_even[0]
    wout0 = w_out_even[0]
    pw = pool_w[0]
    ps = pool_scale[0][None, :]
    cw = conv_w[0]
    reps = LANES // HEAD_DIM
    qg = jnp.tile(q_norm_g[0], reps)[None, :]
    sinks = attn_sinks[0]

    cos_p, sin_p = _rope_tables(jnp.arange(T))
    cos_pt, sin_pt = cos_p.T, sin_p.T
    cos_1, sin_1 = _rope_tables(PAST_LEN + jnp.arange(t_s))
    cos_s = jnp.tile(jnp.concatenate([cos_1, cos_1], axis=1), (1, reps))
    sin_s = jnp.tile(jnp.concatenate([-sin_1, sin_1], axis=1), (1, reps))
    cos_st = jnp.broadcast_to(cos_1.T, (HEAD_DIM // 2, LANES))
    sin_st = jnp.broadcast_to(sin_1.T, (HEAD_DIM // 2, LANES))

    win1 = w_in_odd[0]
    wout1 = w_out_odd[0]
    qg_t = jnp.broadcast_to(q_norm_g[0][:, None], (HEAD_DIM, LANES))
    kg_t = jnp.broadcast_to(k_norm_g[0][:, None], (HEAD_DIM, LANES))

    sp = jnp.transpose(state_pool[0], (1, 0, 2))
    sc = jnp.transpose(state_conv[0], (1, 0, 2))
    x1s, pool_s, conv_s, qh_s, kt_s, vt_s, gate_s = _sample_dense(
        x_sample[:, 0, :], sp, sc, g0, win0, pw, ps, cw, wout0, g1, win1, qg, kg_t, cos_s, sin_s,
        cos_st, sin_st)
    pool_s = jnp.transpose(pool_s, (1, 0, 2))
    conv_s = jnp.transpose(conv_s, (1, 0, 2))

    ckt = jnp.transpose(cache_k[0], (0, 2, 3, 1)).reshape(n_s, KV_W, WINDOW)
    cvt = jnp.transpose(cache_v[0], (0, 2, 3, 1)).reshape(n_s, KV_W, WINDOW)
    x1p, pool_p, conv_p, nkt_s, nvt_s, olo_s, ohi_s = _even_prompt(
        x_prompt, g0, win0, pw, ps, cw, wout0, qh_s, kt_s, vt_s, ckt, cvt, sinks[:, None], tm=1024)
    y_p, k_p, v_p = _odd_prompt(x1p, g1, win1, wout1, qg_t, kg_t, cos_pt, sin_pt, sinks, tm=512)
    y_s = _sample_out(olo_s, ohi_s, gate_s, x1s, wout1)
    nk_s = jnp.transpose(nkt_s.reshape(n_s, N_KV_HEADS, HEAD_DIM, WINDOW), (0, 3, 1, 2))
    nv_s = jnp.transpose(nvt_s.reshape(n_s, N_KV_HEADS, HEAD_DIM, WINDOW), (0, 3, 1, 2))

    kv_shape = (1, -1, WINDOW, N_KV_HEADS, HEAD_DIM)
    return (y_p, y_s[:, None, :], pool_p[None], pool_s[None], conv_p[None], conv_s[None],
            k_p.reshape(kv_shape), v_p.reshape(kv_shape), nk_s[None], nv_s[None])
```

```python
import functools

import jax
import jax.numpy as jnp
from jax import lax
from jax.experimental import pallas as pl
from jax.experimental.pallas import tpu as pltpu

F32 = jnp.float32
BF16 = jnp.bfloat16

POOL_WINDOWS = (2, 4, 8, 16)
POOL_GROUP = 128
POOL_BUF = max(POOL_WINDOWS) - 1
CONV_WIDTH = 3
CONV_BUF = CONV_WIDTH - 1
N_HEADS = 16
HEAD_DIM = 64
N_KV_HEADS = 4
GROUP = N_HEADS // N_KV_HEADS
WINDOW = 128
ROPE_THETA = 10000.0
RMS_EPS = 1e-6
PAST_LEN = 16384
ATTN_W = N_HEADS * HEAD_DIM
KV_W = N_KV_HEADS * HEAD_DIM

LANES = 128
POOL_HALO = 16
CONV_HALO = 8
VMEM_LIMIT_BYTES = 60 * 1024 * 1024
LOG2_E = 1.4426950408889634


def _rms(x, g):
    ms = jnp.mean(x * x, axis=-1, keepdims=True)
    return x * lax.rsqrt(ms + RMS_EPS) * g


def _silu(z):
    return z * jax.nn.sigmoid(z)


def _head_norm_rope(x, gain, cos, sin):
    lane = lax.broadcasted_iota(jnp.int32, x.shape, 1)
    first = lane < HEAD_DIM
    x2 = x * x
    ss0 = jnp.sum(jnp.where(first, x2, 0.0), axis=-1, keepdims=True)
    ss1 = jnp.sum(jnp.where(first, 0.0, x2), axis=-1, keepdims=True)
    r = jnp.where(first, lax.rsqrt(ss0 * (1.0 / HEAD_DIM) + RMS_EPS), lax.rsqrt(ss1 * (1.0 / HEAD_DIM) + RMS_EPS))
    xn = x * r * gain
    half = HEAD_DIM // 2
    swapped = jnp.where((lane % HEAD_DIM) < half, pltpu.roll(xn, LANES - half, 1), pltpu.roll(xn, half, 1))
    return xn * cos + swapped * sin


def _nt_dot(a, b):
    return lax.dot_general(a, b, (((1,), (1,)), ((), ())), preferred_element_type=F32)


class _SampleAttention:
    def __init__(self, qh_ref, knt_ref, vnt_ref, ck_ref, cv_ref, sink_ref, nk_ref, nv_ref, olo_ref, ohi_ref, *, step):
        self.refs = (qh_ref, knt_ref, vnt_ref, ck_ref, cv_ref, sink_ref, nk_ref, nv_ref, olo_ref, ohi_ref)
        self.bb = ck_ref.shape[0]
        self.first = step * self.bb

    def roll(self):
        _, knt_ref, vnt_ref, ck_ref, cv_ref, _, nk_ref, nv_ref, _, _ = self.refs
        win = ck_ref.shape[2]
        newest = lax.broadcasted_iota(jnp.int32, (KV_W, win), 1) == win - 1
        for i in range(self.bb):
            b = self.first + i
            nk_ref[i] = jnp.where(newest, pltpu.roll(knt_ref[...], win - 1 - b, 1), pltpu.roll(ck_ref[i], win - 1, 1))
            nv_ref[i] = jnp.where(newest, pltpu.roll(vnt_ref[...], win - 1 - b, 1), pltpu.roll(cv_ref[i], win - 1, 1))

    def scores(self):
        qh_ref, nk_ref = self.refs[0], self.refs[6]
        n = qh_ref.shape[0] // N_HEADS
        blocks = LANES // HEAD_DIM
        hrow = lax.broadcasted_iota(jnp.int32, (N_HEADS, LANES), 0) // GROUP
        hcol = lax.broadcasted_iota(jnp.int32, (N_HEADS, LANES), 1) // HEAD_DIM
        self.s = []
        for i in range(self.bb):
            qd = qh_ref[pl.ds(self.first + i, N_HEADS, stride=n), :]
            qx = jnp.concatenate(
                [jnp.where(hcol + blocks * c == hrow, qd, 0.0) for c in range(KV_W // LANES)], axis=1)
            self.s.append(jnp.dot(qx.astype(BF16), nk_ref[i].astype(BF16), preferred_element_type=F32))

    def softmax(self):
        sink = self.refs[5][...]
        self.p = []
        for s in self.s:
            m = jnp.maximum(jnp.max(s, axis=-1, keepdims=True), sink)
            p = jnp.exp(s - m)
            denom = jnp.sum(p, axis=-1, keepdims=True) + jnp.exp(sink - m)
            self.p.append((p * (1.0 / denom)).astype(BF16))

    def outputs(self):
        nv_ref, olo_ref, ohi_ref = self.refs[7:10]
        for i in range(self.bb):
            o = _nt_dot(self.p[i], nv_ref[i].astype(BF16))
            rows = pl.ds(pl.multiple_of((self.first + i) * N_HEADS, N_HEADS), N_HEADS)
            olo_ref[rows, :] = o[:, 0:LANES]
            ohi_ref[rows, :] = o[:, LANES:2 * LANES]


def _even_prompt_kernel(x_ref, g_ref, win_ref, pw_ref, ps_ref, cw_ref, wout_ref,
                        qh_ref, knt_ref, vnt_ref, ck_ref, cv_ref, sink_ref,
                        o_ref, pool_ref, conv_ref, nk_ref, nv_ref, olo_ref, ohi_ref,
                        uext, vext, sa, sb, y_scr, *, tm):
    t = pl.program_id(1)
    w_pool = uext.shape[1]
    w_conv = vext.shape[1]
    base = 2 * POOL_HALO
    side = _SampleAttention(qh_ref, knt_ref, vnt_ref, ck_ref, cv_ref, sink_ref, nk_ref, nv_ref, olo_ref, ohi_ref,
                            step=pl.program_id(0) * pl.num_programs(1) + t)

    @pl.when(t == 0)
    def _():
        uext[0:base, :] = jnp.zeros((base, w_pool), F32)
        vext[0:CONV_HALO, :] = jnp.zeros((CONV_HALO, w_conv), F32)
        sa[0:POOL_HALO, :] = jnp.zeros((POOL_HALO, POOL_GROUP), F32)
        sb[0:POOL_HALO, :] = jnp.zeros((POOL_HALO, POOL_GROUP), F32)

    x = x_ref[0]
    h = _rms(x, g_ref[...]).astype(win_ref.dtype)

    def proj(c, width):
        return jnp.dot(h, win_ref[:, c:c + width], preferred_element_type=F32)

    side.roll()
    u = proj(0, w_pool)
    uext[base:base + tm, :] = u
    pos = t * tm + lax.broadcasted_iota(jnp.int32, (tm, 1), 0)
    side.scores()
    z_a = proj(w_pool, w_pool)
    n_ext = POOL_HALO + tm

    for g, w in enumerate(POOL_WINDOWS):
        lo = g * POOL_GROUP
        cols = slice(lo, lo + POOL_GROUP)
        src = uext
        src_cols = cols
        step = 1
        bufs = (sa, sb)
        nbuf = 0
        while 2 * step < w:
            dst = bufs[nbuf % 2]
            dst[POOL_HALO:POOL_HALO + n_ext, :] = (src[POOL_HALO:POOL_HALO + n_ext, src_cols]
                                                   + src[POOL_HALO - step:POOL_HALO - step + n_ext, src_cols])
            src, src_cols = dst, slice(0, POOL_GROUP)
            step *= 2
            nbuf += 1
        wsum = src[base:base + tm, src_cols] + src[base - step:base - step + tm, src_cols]
        cnt = jnp.minimum(pos + 1, w).astype(F32)
        d = wsum * (1.0 / cnt) - u[:, cols]
        ya = jnp.dot(d.astype(pw_ref.dtype), pw_ref[g], preferred_element_type=F32)
        ya = ya * ps_ref[:, cols] * _silu(z_a[:, cols])
        y_scr[:, cols] = ya.astype(y_scr.dtype)

    b_gate = proj(2 * w_pool, w_conv)
    side.softmax()
    c_gate = proj(2 * w_pool + w_conv, w_conv)
    side.outputs()
    v_in = proj(2 * w_pool + 2 * w_conv, w_conv)
    v = c_gate * v_in
    vext[CONV_HALO:CONV_HALO + tm, :] = v
    conv = v * cw_ref[CONV_WIDTH - 1:CONV_WIDTH, :]
    for k in range(CONV_WIDTH - 1):
        shift = CONV_WIDTH - 1 - k
        conv = conv + vext[CONV_HALO - shift:CONV_HALO - shift + tm, :] * cw_ref[k:k + 1, :]
    z_b = proj(2 * w_pool + 3 * w_conv, w_conv)
    y_b = b_gate * conv * _silu(z_b)
    y_scr[:, w_pool:w_pool + w_conv] = y_b.astype(y_scr.dtype)

    o_ref[0] = x + jnp.dot(y_scr[...], wout_ref[...], preferred_element_type=F32)
    pool_ref[0] = uext[base + tm - POOL_BUF:base + tm, :]
    conv_ref[0] = vext[CONV_HALO + tm - CONV_BUF:CONV_HALO + tm, :]
    uext[POOL_HALO:base, :] = uext[POOL_HALO + tm:base + tm, :]
    vext[0:CONV_HALO, :] = vext[tm:tm + CONV_HALO, :]


def _even_prompt(x, g, win, pw, ps, cw, wout, qh, knt, vnt, ckt, cvt, sink_col, *, tm):
    B, T, D = x.shape
    w_pool = ps.shape[-1]
    w_conv = cw.shape[-1]
    assert T % tm == 0 and tm % 16 == 0 and tm >= POOL_HALO
    nt = T // tm
    n, kvw, win_len = ckt.shape
    assert n % (B * nt) == 0 and kvw == KV_W == 2 * LANES and win_len == LANES and n == LANES
    bb = n // (B * nt)
    const = lambda *shape: pl.BlockSpec(shape, lambda b, t: (0,) * len(shape))
    cache_block = pl.BlockSpec((bb, kvw, win_len), lambda b, t: (b * nt + t, 0, 0))
    return pl.pallas_call(
        functools.partial(_even_prompt_kernel, tm=tm),
        grid=(B, nt),
        in_specs=[
            pl.BlockSpec((1, tm, D), lambda b, t: (b, t, 0)),
            const(1, D),
            const(*win.shape),
            const(*pw.shape),
            const(1, w_pool),
            const(*cw.shape),
            const(*wout.shape),
            const(*qh.shape),
            const(*knt.shape),
            const(*vnt.shape),
            cache_block,
            cache_block,
            const(*sink_col.shape),
        ],
        out_specs=[
            pl.BlockSpec((1, tm, D), lambda b, t: (b, t, 0)),
            pl.BlockSpec((1, POOL_BUF, w_pool), lambda b, t: (b, 0, 0)),
            pl.BlockSpec((1, CONV_BUF, w_conv), lambda b, t: (b, 0, 0)),
            cache_block,
            cache_block,
            const(n * N_HEADS, LANES),
            const(n * N_HEADS, LANES),
        ],
        out_shape=[
            jax.ShapeDtypeStruct((B, T, D), F32),
            jax.ShapeDtypeStruct((B, POOL_BUF, w_pool), F32),
            jax.ShapeDtypeStruct((B, CONV_BUF, w_conv), F32),
            jax.ShapeDtypeStruct((n, kvw, win_len), F32),
            jax.ShapeDtypeStruct((n, kvw, win_len), F32),
            jax.ShapeDtypeStruct((n * N_HEADS, LANES), F32),
            jax.ShapeDtypeStruct((n * N_HEADS, LANES), F32),
        ],
        scratch_shapes=[
            pltpu.VMEM((2 * POOL_HALO + tm, w_pool), F32),
            pltpu.VMEM((CONV_HALO + tm, w_conv), F32),
            pltpu.VMEM((2 * POOL_HALO + tm, POOL_GROUP), F32),
            pltpu.VMEM((2 * POOL_HALO + tm, POOL_GROUP), F32),
            pltpu.VMEM((tm, w_pool + w_conv), wout.dtype),
        ],
        compiler_params=pltpu.CompilerParams(
            dimension_semantics=("arbitrary", "arbitrary"), vmem_limit_bytes=VMEM_LIMIT_BYTES),
        name="even_prompt",
    )(x, g, win, pw, ps, cw, wout, qh, knt, vnt, ckt, cvt, sink_col)


def _odd_prompt_kernel(*refs, tm, nt):
    (x_ref, xres_ref, g_ref, win_ref, wo_ref, qg_ref, kg_ref, cost_ref, sint_ref, sink_ref,
     o_ref, knew_ref, vnew_ref, wkt_scr, wqvz_scr, wout_scr, qt_scr, kext, vt_ext, gate_scr, *rest) = refs

    @pl.when(pl.program_id(0) == 0)
    def _():
        qt_scr[...] = jnp.zeros(qt_scr.shape, BF16)
        gate_scr[...] = jnp.zeros(gate_scr.shape, F32)
        kext[...] = jnp.zeros(kext.shape, BF16)
        vt_ext[...] = jnp.zeros(vt_ext.shape, BF16)
        blk = 2 * LANES
        for src, dst, width in ((0, 0, ATTN_W), (ATTN_W + KV_W, ATTN_W, KV_W + ATTN_W)):
            for c in range(0, width, blk):
                wqvz_scr[dst + c:dst + c + blk, :] = win_ref[:, src + c:src + c + blk].T.astype(BF16)
        wkt_scr[...] = win_ref[:, ATTN_W:ATTN_W + KV_W].T.astype(BF16)
        for c in range(0, wo_ref.shape[1], blk):
            wout_scr[c:c + blk, :] = wo_ref[:, c:c + blk].T.astype(BF16)

    for parity in range(2):
        @pl.when(pl.program_id(0) % 2 == parity)
        def _():
            _odd_prompt_step(x_ref, xres_ref, g_ref, wkt_scr, wqvz_scr, qg_ref, kg_ref, cost_ref, sint_ref, sink_ref,
                             wout_scr, o_ref, knew_ref, vnew_ref, qt_scr, kext, vt_ext, gate_scr, *rest,
                             tm=tm, nt=nt, cur=parity, prev=1 - parity)


def _odd_prompt_step(x_ref, xres_ref, g_ref, wkt_ref, wqvz_ref, qg_ref, kg_ref, cost_ref, sint_ref, sink_ref,
                     wout_ref, o_ref, knew_ref, vnew_ref,
                     qt_scr, kext, vt_ext, gate_scr, yt_scr, s_scr, p_scr, esink_scr, *, tm, nt, cur, prev):
    step = pl.program_id(0)
    proj_t = jnp.minimum(step, pl.num_programs(0) - 2) % nt
    attn_t = jnp.maximum(step - 1, 0) % nt
    nblk = tm // WINDOW
    half = HEAD_DIM // 2

    h = _rms(x_ref[0], g_ref[...]).astype(BF16)
    reps = tm // LANES
    cos_t = cost_ref[...]
    sin_t = sint_ref[...]

    def norm_rope(xt, gain):
        ms = jnp.sum(xt * xt, axis=0, keepdims=True) * (1.0 / HEAD_DIM)
        xn = xt * lax.rsqrt(ms + RMS_EPS) * gain
        x1, x2 = xn[0:half, :], xn[half:HEAD_DIM, :]
        return jnp.concatenate([x1 * cos_t - x2 * sin_t, x2 * cos_t + x1 * sin_t], axis=0)

    has_past = proj_t > 0

    def proj_kv():
        kt = _nt_dot(wkt_ref[...], h)
        kgain = jnp.concatenate([kg_ref[...]] * reps, axis=1)
        kt = jnp.concatenate(
            [norm_rope(kt[kv * HEAD_DIM:(kv + 1) * HEAD_DIM, :], kgain) for kv in range(N_KV_HEADS)], axis=0)
        knew_ref[0] = kt[:, tm - WINDOW:tm]
        kext[cur, 0:WINDOW, :] = jnp.where(has_past, kext[prev, tm:tm + WINDOW, :], jnp.zeros((WINDOW, KV_W), BF16))
        kext[cur, WINDOW:WINDOW + tm, :] = kt.T.astype(BF16)
        vt = _nt_dot(wqvz_ref[ATTN_W:ATTN_W + KV_W, :], h)
        vnew_ref[0] = vt[:, tm - WINDOW:tm]
        vt_ext[cur, :, 0:WINDOW] = jnp.where(has_past, vt_ext[prev, :, tm:tm + WINDOW],
                                             jnp.zeros((KV_W, WINDOW), BF16))
        vt_ext[cur, :, WINDOW:WINDOW + tm] = vt.astype(BF16)

    def proj_gate(lo, hi):
        zt = _nt_dot(wqvz_ref[ATTN_W + KV_W + lo:ATTN_W + KV_W + hi, :], h)
        gate_scr[cur, lo:hi, :] = _silu(zt)

    q_raw = {}

    def proj_q(lo, hi):
        q_raw[lo] = _nt_dot(wqvz_ref[lo:hi, :], h)

    def finish_q():
        qgain = jnp.concatenate([qg_ref[...]] * reps, axis=1)
        for lo, qt in q_raw.items():
            for r0 in range(0, qt.shape[0], HEAD_DIM):
                qt_scr[cur, lo + r0:lo + r0 + HEAD_DIM, :] = (
                    norm_rope(qt[r0:r0 + HEAD_DIM, :], qgain) * (LOG2_E * HEAD_DIM ** -0.5)).astype(BF16)
        q_raw.clear()

    pieces = [proj_kv,
              lambda: proj_gate(0, ATTN_W // 2),
              lambda: proj_gate(ATTN_W // 2, ATTN_W),
              lambda: (proj_q(0, ATTN_W // 2), finish_q())]
    last_piece = lambda: proj_q(ATTN_W // 2, ATTN_W)

    ri = lax.broadcasted_iota(jnp.int32, (WINDOW, 2 * WINDOW), 0)
    qq = lax.broadcasted_iota(jnp.int32, (WINDOW, 2 * WINDOW), 1) % WINDOW
    from_prev = ri > qq
    keep_prev = from_prev.astype(BF16)
    keep_cur = 1.0 - keep_prev
    lane = lax.broadcasted_iota(jnp.int32, (1, 2 * WINDOW), 1)
    zeros = jnp.zeros((HEAD_DIM, 2 * WINDOW), BF16)
    ones = jnp.ones((16, 2 * WINDOW), BF16)

    for i in range(nblk):
        c0 = i * WINDOW
        qcols = slice(c0, c0 + WINDOW)
        pairs = [(kv, kv * GROUP + 2 * pr) for kv in range(N_KV_HEADS) for pr in range(GROUP // 2)]
        for j, (kv, ha) in enumerate(pairs):
            chunk, pos = divmod(kv, LANES // HEAD_DIM)
            kblk = kext[prev, c0:c0 + 2 * WINDOW, chunk * LANES:(chunk + 1) * LANES]
            qpair = jnp.concatenate([qt_scr[prev, ha * HEAD_DIM:(ha + 1) * HEAD_DIM, qcols],
                                     qt_scr[prev, (ha + 1) * HEAD_DIM:(ha + 2) * HEAD_DIM, qcols]], axis=1)
            rhs = jnp.concatenate([qpair, zeros] if pos == 0 else [zeros, qpair], axis=0)
            s = jnp.dot(kblk, rhs, preferred_element_type=F32)
            s_prev = s[0:WINDOW, :]
            if i == 0:
                s_prev = jnp.where(attn_t == 0, -jnp.inf, s_prev)
            s_scr[j] = jnp.where(from_prev, s_prev, s[WINDOW:2 * WINDOW, :])
        for c, piece in enumerate(pieces):
            if c * nblk // len(pieces) == i:
                piece()
        for j, (kv, ha) in enumerate(pairs):
            s = s_scr[j]
            sink = jnp.where(lane < WINDOW, sink_ref[ha], sink_ref[ha + 1]) * LOG2_E
            m = jnp.maximum(jnp.max(s, axis=0, keepdims=True), sink)
            p_scr[j] = jnp.exp2(s - m).astype(BF16)
            esink_scr[j:j + 1, :] = jnp.exp2(sink - m)
        for j, (kv, ha) in enumerate(pairs):
            vg = vt_ext[prev, kv * HEAD_DIM:(kv + 1) * HEAD_DIM, c0:c0 + 2 * WINDOW]
            p = p_scr[j]
            p_keys = jnp.concatenate([p * keep_prev, p * keep_cur], axis=0)
            oa = jnp.dot(jnp.concatenate([vg, ones], axis=0), p_keys, preferred_element_type=F32)
            denom = oa[HEAD_DIM:HEAD_DIM + 1, :] + esink_scr[j:j + 1, :]
            o = oa[0:HEAD_DIM, :] * (1.0 / denom)
            for hd, part in ((ha, o[:, 0:WINDOW]), (ha + 1, o[:, WINDOW:2 * WINDOW])):
                rows = slice(hd * HEAD_DIM, (hd + 1) * HEAD_DIM)
                yt_scr[rows, qcols] = (part * gate_scr[prev, rows, qcols]).astype(BF16)

    out_t = jnp.dot(wout_ref[...], yt_scr[...], preferred_element_type=F32)
    last_piece()
    finish_q()
    o_ref[0] = xres_ref[0] + out_t.T


def _odd_prompt(x, g, w_in, w_out, qg_t, kg_t, cos_t, sin_t, sinks, *, tm):
    B, T, D = x.shape
    assert T % tm == 0 and tm % WINDOW == 0 and w_in.shape == (D, 2 * ATTN_W + 2 * KV_W) and w_out.shape == (ATTN_W, D)
    nt = T // tm
    half = HEAD_DIM // 2
    const = lambda *shape: pl.BlockSpec(shape, lambda s: (0,) * len(shape))
    n_tiles = B * nt
    proj_tile = lambda s: jnp.minimum(s, n_tiles - 1)
    attn_tile = lambda s: jnp.maximum(s - 1, 0)
    return pl.pallas_call(
        functools.partial(_odd_prompt_kernel, tm=tm, nt=nt),
        grid=(n_tiles + 1,),
        in_specs=[
            pl.BlockSpec((1, tm, D), lambda s: (proj_tile(s) // nt, proj_tile(s) % nt, 0)),
            pl.BlockSpec((1, tm, D), lambda s: (attn_tile(s) // nt, attn_tile(s) % nt, 0)),
            const(1, D),
            const(*w_in.shape),
            const(*w_out.shape),
            const(HEAD_DIM, LANES),
            const(HEAD_DIM, LANES),
            pl.BlockSpec((half, tm), lambda s: (0, proj_tile(s) % nt)),
            pl.BlockSpec((half, tm), lambda s: (0, proj_tile(s) % nt)),
            pl.BlockSpec(memory_space=pltpu.SMEM),
        ],
        out_specs=[
            pl.BlockSpec((1, tm, D), lambda s: (attn_tile(s) // nt, attn_tile(s) % nt, 0)),
            pl.BlockSpec((1, KV_W, WINDOW), lambda s: (proj_tile(s) // nt, 0, 0)),
            pl.BlockSpec((1, KV_W, WINDOW), lambda s: (proj_tile(s) // nt, 0, 0)),
        ],
        out_shape=[
            jax.ShapeDtypeStruct((B, T, D), F32),
            jax.ShapeDtypeStruct((B, KV_W, WINDOW), F32),
            jax.ShapeDtypeStruct((B, KV_W, WINDOW), F32),
        ],
        scratch_shapes=[
            pltpu.VMEM((KV_W, D), BF16),
            pltpu.VMEM((2 * ATTN_W + KV_W, D), BF16),
            pltpu.VMEM((D, ATTN_W), BF16),
            pltpu.VMEM((2, ATTN_W, tm), BF16),
            pltpu.VMEM((2, WINDOW + tm, KV_W), BF16),
            pltpu.VMEM((2, KV_W, WINDOW + tm), BF16),
            pltpu.VMEM((2, ATTN_W, tm), F32),
            pltpu.VMEM((ATTN_W, tm), BF16),
            pltpu.VMEM((N_HEADS // 2, WINDOW, 2 * WINDOW), F32),
            pltpu.VMEM((N_HEADS // 2, WINDOW, 2 * WINDOW), BF16),
            pltpu.VMEM((N_HEADS // 2, 2 * WINDOW), F32),
        ],
        compiler_params=pltpu.CompilerParams(
            dimension_semantics=("arbitrary",), vmem_limit_bytes=VMEM_LIMIT_BYTES),
        name="odd_prompt",
    )(x, x, g, w_in, w_out, qg_t, kg_t, cos_t, sin_t, sinks)


def _dup_heads(x):
    lane = lax.broadcasted_iota(jnp.int32, x.shape, 1)
    rolled = pltpu.roll(x, HEAD_DIM, 1)
    first = lane < HEAD_DIM
    return jnp.where(first, x, rolled), jnp.where(first, rolled, x)


def _sample_dense_kernel(x_ref, sp_ref, sc_ref, g0_ref, win0_ref, pw_ref, ps_ref, cw_ref, wout0_ref,
                         g1_ref, win1_ref, qg_ref, kgt_ref, cos_ref, sin_ref, cost_ref, sint_ref,
                         x1_ref, pool_ref, conv_ref, qh_ref, kt_ref, vt_ref, gate_ref):
    w_pool = ps_ref.shape[1]
    w_conv = cw_ref.shape[1]

    x = x_ref[...]
    h = _rms(x, g0_ref[...]).astype(win0_ref.dtype)

    def proj(c, width):
        return jnp.dot(h, win0_ref[:, c:c + width], preferred_element_type=F32)

    u = proj(0, w_pool)
    z_a = proj(w_pool, w_pool)
    ys = []
    for g, w in enumerate(POOL_WINDOWS):
        cols = slice(g * POOL_GROUP, (g + 1) * POOL_GROUP)
        wsum = u[:, cols]
        for r in range(POOL_BUF - (w - 1), POOL_BUF):
            wsum = wsum + sp_ref[r, :, cols]
        cnt = float(min(PAST_LEN + 1, w))
        d = wsum * (1.0 / cnt) - u[:, cols]
        ya = jnp.dot(d.astype(pw_ref.dtype), pw_ref[g], preferred_element_type=F32)
        ys.append((ya * ps_ref[:, cols] * _silu(z_a[:, cols])).astype(wout0_ref.dtype))
    for r in range(POOL_BUF - 1):
        pool_ref[r] = sp_ref[r + 1]
    pool_ref[POOL_BUF - 1] = u

    b_gate = proj(2 * w_pool, w_conv)
    c_gate = proj(2 * w_pool + w_conv, w_conv)
    v_in = proj(2 * w_pool + 2 * w_conv, w_conv)
    v = c_gate * v_in
    conv = v * cw_ref[CONV_WIDTH - 1:CONV_WIDTH, :]
    for k in range(CONV_BUF):
        conv = conv + sc_ref[k] * cw_ref[k:k + 1, :]
    for r in range(CONV_BUF - 1):
        conv_ref[r] = sc_ref[r + 1]
    conv_ref[CONV_BUF - 1] = v
    z_b = proj(2 * w_pool + 3 * w_conv, w_conv)
    ys.append((b_gate * conv * _silu(z_b)).astype(wout0_ref.dtype))
    y = jnp.concatenate(ys, axis=1)
    x1 = x + jnp.dot(y, wout0_ref[...], preferred_element_type=F32)
    x1_ref[...] = x1

    n = x.shape[0]
    half = HEAD_DIM // 2
    h1 = _rms(x1, g1_ref[...])

    def proj1(lo, width):
        return jnp.dot(h1, win1_ref[:, lo:lo + width], preferred_element_type=F32)

    q = proj1(0, ATTN_W)
    for c in range(ATTN_W // LANES):
        qc = _head_norm_rope(q[:, c * LANES:(c + 1) * LANES], qg_ref[...], cos_ref[...], sin_ref[...])
        d0, d1 = _dup_heads(qc * (HEAD_DIM ** -0.5))
        qh_ref[2 * c * n:(2 * c + 1) * n, :] = d0
        qh_ref[(2 * c + 1) * n:(2 * c + 2) * n, :] = d1
    kt = proj1(ATTN_W, KV_W).T
    cos_t = cost_ref[...]
    sin_t = sint_ref[...]
    for kv in range(N_KV_HEADS):
        kh = kt[kv * HEAD_DIM:(kv + 1) * HEAD_DIM, :]
        ms = jnp.sum(kh * kh, axis=0, keepdims=True) * (1.0 / HEAD_DIM)
        kn = kh * lax.rsqrt(ms + RMS_EPS) * kgt_ref[...]
        x1h, x2h = kn[0:half, :], kn[half:HEAD_DIM, :]
        kt_ref[kv * HEAD_DIM:(kv + 1) * HEAD_DIM, :] = jnp.concatenate(
            [x1h * cos_t - x2h * sin_t, x2h * cos_t + x1h * sin_t], axis=0)
    vt_ref[...] = proj1(ATTN_W + KV_W, KV_W).T
    gate_ref[...] = _silu(proj1(ATTN_W + 2 * KV_W, ATTN_W))


def _sample_dense(x, sp, sc, g0, win0, pw, ps, cw, wout0, g1, win1, qg, kg_t, cos, sin, cos_t, sin_t):
    n, D = x.shape
    assert n == LANES
    vmem = pl.BlockSpec(memory_space=pltpu.VMEM)
    return pl.pallas_call(
        _sample_dense_kernel,
        in_specs=[vmem] * 17,
        out_specs=[vmem] * 7,
        out_shape=[
            jax.ShapeDtypeStruct((n, D), F32),
            jax.ShapeDtypeStruct(sp.shape, F32),
            jax.ShapeDtypeStruct(sc.shape, F32),
            jax.ShapeDtypeStruct((N_HEADS * n, LANES), F32),
            jax.ShapeDtypeStruct((KV_W, n), F32),
            jax.ShapeDtypeStruct((KV_W, n), F32),
            jax.ShapeDtypeStruct((n, ATTN_W), F32),
        ],
        compiler_params=pltpu.CompilerParams(vmem_limit_bytes=VMEM_LIMIT_BYTES),
        name="sample_dense",
    )(x, sp, sc, g0, win0, pw, ps, cw, wout0, g1, win1, qg, kg_t, cos, sin, cos_t, sin_t)


def _sample_out_kernel(olo_ref, ohi_ref, gate_ref, x1_ref, wout_ref, y_ref):
    n = x1_ref.shape[0]
    blocks = LANES // HEAD_DIM
    lane_n = lax.broadcasted_iota(jnp.int32, (n, LANES), 1)
    chunks = []
    for c in range(ATTN_W // LANES):
        parts = []
        for hd in (2 * c, 2 * c + 1):
            kv = hd // GROUP
            slab = (olo_ref, ohi_ref)[kv // blocks]
            a = slab[pl.ds(hd, n, stride=N_HEADS), :]
            parts.append(a if kv % blocks == hd % blocks else pltpu.roll(a, HEAD_DIM, 1))
        chunks.append(jnp.where(lane_n < HEAD_DIM, parts[0], parts[1]))
    y = jnp.concatenate(chunks, axis=1) * gate_ref[...]
    y_ref[...] = x1_ref[...] + jnp.dot(y, wout_ref[...], preferred_element_type=F32)


def _sample_out(olo, ohi, gate, x1, wout):
    vmem = pl.BlockSpec(memory_space=pltpu.VMEM)
    return pl.pallas_call(
        _sample_out_kernel,
        in_specs=[vmem] * 5,
        out_specs=vmem,
        out_shape=jax.ShapeDtypeStruct(x1.shape, F32),
        compiler_params=pltpu.CompilerParams(vmem_limit_bytes=VMEM_LIMIT_BYTES),
        name="sample_out",
    )(olo, ohi, gate, x1, wout)


def _rope_tables(pos):
    half = HEAD_DIM // 2
    inv = ROPE_THETA ** (-jnp.arange(half, dtype=F32) / half)
    ang = pos.astype(F32)[:, None] * inv[None, :]
    return jnp.cos(ang), jnp.sin(ang)


def kernel(x_prompt, x_sample, state_pool, state_conv, cache_k, cache_v, norm_g, w_in_even, pool_w, pool_scale,
           conv_w, w_out_even, w_in_odd, q_norm_g, k_norm_g, attn_sinks, w_out_odd):
    B, T, D = x_prompt.shape
    n_s, t_s, _ = x_sample.shape
    assert norm_g.shape[0] == 2 and w_in_even.shape[0] == 1 and w_in_odd.shape[0] == 1
    assert t_s == 1 and cache_k.shape[2] == WINDOW and T >= WINDOW
    assert cache_k.shape[3] * cache_k.shape[4] == KV_W and pool_w.shape[1:] == (len(POOL_WINDOWS), POOL_GROUP, POOL_GROUP)

    g0 = norm_g[0][None, :]
    g1 = norm_g[1][None, :]
    win0 = w_in_even[0]
    wout0 = w_out_even[0]
    pw = pool_w[0]
    ps = pool_scale[0][None, :]
    cw = conv_w[0]
    reps = LANES // HEAD_DIM
    qg = jnp.tile(q_norm_g[0], reps)[None, :]
    sinks = attn_sinks[0]

    cos_p, sin_p = _rope_tables(jnp.arange(T))
    cos_pt, sin_pt = cos_p.T, sin_p.T
    cos_1, sin_1 = _rope_tables(PAST_LEN + jnp.arange(t_s))
    cos_s = jnp.tile(jnp.concatenate([cos_1, cos_1], axis=1), (1, reps))
    sin_s = jnp.tile(jnp.concatenate([-sin_1, sin_1], axis=1), (1, reps))
    cos_st = jnp.broadcast_to(cos_1.T, (HEAD_DIM // 2, LANES))
    sin_st = jnp.broadcast_to(sin_1.T, (HEAD_DIM // 2, LANES))

    win1 = w_in_odd[0]
    wout1 = w_out_odd[0]
    qg_t = jnp.broadcast_to(q_norm_g[0][:, None], (HEAD_DIM, LANES))
    kg_t = jnp.broadcast_to(k_norm_g[0][:, None], (HEAD_DIM, LANES))

    sp = jnp.transpose(state_pool[0], (1, 0, 2))
    sc = jnp.transpose(state_conv[0], (1, 0, 2))
    x1s, pool_s, conv_s, qh_s, kt_s, vt_s, gate_s = _sample_dense(
        x_sample[:, 0, :], sp, sc, g0, win0, pw, ps, cw, wout0, g1, win1, qg, kg_t, cos_s, sin_s,
        cos_st, sin_st)
    pool_s = jnp.transpose(pool_s, (1, 0, 2))
    conv_s = jnp.transpose(conv_s, (1, 0, 2))

    ckt = jnp.transpose(cache_k[0], (0, 2, 3, 1)).reshape(n_s, KV_W, WINDOW)
    cvt = jnp.transpose(cache_v[0], (0, 2, 3, 1)).reshape(n_s, KV_W, WINDOW)
    x1p, pool_p, conv_p, nkt_s, nvt_s, olo_s, ohi_s = _even_prompt(
        x_prompt, g0, win0, pw, ps, cw, wout0, qh_s, kt_s, vt_s, ckt, cvt, sinks[:, None], tm=1024)
    y_p, k_p, v_p = _odd_prompt(x1p, g1, win1, wout1, qg_t, kg_t, cos_pt, sin_pt, sinks, tm=512)
    y_s = _sample_out(olo_s, ohi_s, gate_s, x1s, wout1)

    def window_major(a):
        return jnp.transpose(a.reshape(-1, N_KV_HEADS, HEAD_DIM, WINDOW), (0, 3, 1, 2))[None]

    return (y_p, y_s[:, None, :], pool_p[None], pool_s[None], conv_p[None], conv_s[None],
            window_major(k_p), window_major(v_p), window_major(nkt_s), window_major(nvt_s))
```

```python
import functools

import jax
import jax.numpy as jnp
from jax import lax
from jax.experimental import pallas as pl
from jax.experimental.pallas import tpu as pltpu

F32 = jnp.float32
BF16 = jnp.bfloat16

POOL_WINDOWS = (2, 4, 8, 16)
POOL_GROUP = 128
POOL_BUF = max(POOL_WINDOWS) - 1
CONV_WIDTH = 3
CONV_BUF = CONV_WIDTH - 1
N_HEADS = 16
HEAD_DIM = 64
N_KV_HEADS = 4
GROUP = N_HEADS // N_KV_HEADS
WINDOW = 128
ROPE_THETA = 10000.0
RMS_EPS = 1e-6
PAST_LEN = 16384
ATTN_W = N_HEADS * HEAD_DIM
KV_W = N_KV_HEADS * HEAD_DIM

LANES = 128
POOL_HALO = 16
CONV_HALO = 8
VMEM_LIMIT_BYTES = 60 * 1024 * 1024
LOG2_E = 1.4426950408889634
_QG_T, _KG_T, _COS_T, _SIN_T = 0, HEAD_DIM, 2 * HEAD_DIM, 2 * HEAD_DIM + HEAD_DIM // 2
_QG_ROW, _COS_ROW, _SIN_ROW, _SMALL_ROWS = 3 * HEAD_DIM, 3 * HEAD_DIM + 1, 3 * HEAD_DIM + 2, 3 * HEAD_DIM + 8


def _rms(x, g):
    ms = jnp.mean(x * x, axis=-1, keepdims=True)
    return x * lax.rsqrt(ms + RMS_EPS) * g


def _silu(z):
    return z * jax.nn.sigmoid(z)


def _head_norm_rope(x, gain, cos, sin):
    lane = lax.broadcasted_iota(jnp.int32, x.shape, 1)
    first = lane < HEAD_DIM
    x2 = x * x
    ss0 = jnp.sum(jnp.where(first, x2, 0.0), axis=-1, keepdims=True)
    ss1 = jnp.sum(jnp.where(first, 0.0, x2), axis=-1, keepdims=True)
    r = jnp.where(first, lax.rsqrt(ss0 * (1.0 / HEAD_DIM) + RMS_EPS), lax.rsqrt(ss1 * (1.0 / HEAD_DIM) + RMS_EPS))
    xn = x * r * gain
    half = HEAD_DIM // 2
    swapped = jnp.where((lane % HEAD_DIM) < half, pltpu.roll(xn, LANES - half, 1), pltpu.roll(xn, half, 1))
    return xn * cos + swapped * sin


def _nt_dot(a, b):
    return lax.dot_general(a, b, (((1,), (1,)), ((), ())), preferred_element_type=F32)


class _SampleAttention:
    def __init__(self, qh_ref, knt_ref, vnt_ref, ck_ref, cv_ref, sink_ref, nk_ref, nv_ref, olo_ref, ohi_ref, *, step):
        self.refs = (qh_ref, knt_ref, vnt_ref, ck_ref, cv_ref, sink_ref, nk_ref, nv_ref, olo_ref, ohi_ref)
        self.bb = ck_ref.shape[0]
        self.first = step * self.bb

    def roll(self):
        _, knt_ref, vnt_ref, ck_ref, cv_ref, _, nk_ref, nv_ref, _, _ = self.refs
        win = ck_ref.shape[2]
        newest = lax.broadcasted_iota(jnp.int32, (KV_W, win), 1) == win - 1
        for i in range(self.bb):
            b = self.first + i
            nk_ref[i] = jnp.where(newest, pltpu.roll(knt_ref[...], win - 1 - b, 1), pltpu.roll(ck_ref[i], win - 1, 1))
            nv_ref[i] = jnp.where(newest, pltpu.roll(vnt_ref[...], win - 1 - b, 1), pltpu.roll(cv_ref[i], win - 1, 1))

    def scores(self):
        qh_ref, nk_ref = self.refs[0], self.refs[6]
        n = qh_ref.shape[0] // N_HEADS
        blocks = LANES // HEAD_DIM
        hrow = lax.broadcasted_iota(jnp.int32, (N_HEADS, LANES), 0) // GROUP
        hcol = lax.broadcasted_iota(jnp.int32, (N_HEADS, LANES), 1) // HEAD_DIM
        self.s = []
        for i in range(self.bb):
            qd = qh_ref[pl.ds(self.first + i, N_HEADS, stride=n), :]
            qx = jnp.concatenate(
                [jnp.where(hcol + blocks * c == hrow, qd, 0.0) for c in range(KV_W // LANES)], axis=1)
            self.s.append(jnp.dot(qx.astype(BF16), nk_ref[i].astype(BF16), preferred_element_type=F32))

    def softmax(self):
        sink = self.refs[5][...]
        self.p = []
        for s in self.s:
            m = jnp.maximum(jnp.max(s, axis=-1, keepdims=True), sink)
            p = jnp.exp(s - m)
            denom = jnp.sum(p, axis=-1, keepdims=True) + jnp.exp(sink - m)
            self.p.append((p * (1.0 / denom)).astype(BF16))

    def outputs(self):
        nv_ref, olo_ref, ohi_ref = self.refs[7:10]
        for i in range(self.bb):
            o = _nt_dot(self.p[i], nv_ref[i].astype(BF16))
            rows = pl.ds(pl.multiple_of((self.first + i) * N_HEADS, N_HEADS), N_HEADS)
            olo_ref[rows, :] = o[:, 0:LANES]
            ohi_ref[rows, :] = o[:, LANES:2 * LANES]


def _even_prompt_kernel(x_ref, g_ref, win_ref, pw_ref, ps_ref, cw_ref, wout_ref,
                        qh_ref, knt_ref, vnt_ref, ck_ref, cv_ref, sink_ref,
                        o_ref, pool_ref, conv_ref, nk_ref, nv_ref, olo_ref, ohi_ref,
                        uext, vext, sa, sb, y_scr, *, tm):
    t = pl.program_id(1)
    w_pool = uext.shape[1]
    w_conv = vext.shape[1]
    base = 2 * POOL_HALO
    side = _SampleAttention(qh_ref, knt_ref, vnt_ref, ck_ref, cv_ref, sink_ref, nk_ref, nv_ref, olo_ref, ohi_ref,
                            step=pl.program_id(0) * pl.num_programs(1) + t)

    @pl.when(t == 0)
    def _():
        uext[0:base, :] = jnp.zeros((base, w_pool), F32)
        vext[0:CONV_HALO, :] = jnp.zeros((CONV_HALO, w_conv), F32)
        sa[0:POOL_HALO, :] = jnp.zeros((POOL_HALO, POOL_GROUP), F32)
        sb[0:POOL_HALO, :] = jnp.zeros((POOL_HALO, POOL_GROUP), F32)

    x = x_ref[0]
    h = _rms(x, g_ref[0:1, :]).astype(win_ref.dtype)

    def proj(c, width):
        return jnp.dot(h, win_ref[:, c:c + width], preferred_element_type=F32)

    side.roll()
    u = proj(0, w_pool)
    uext[base:base + tm, :] = u
    pos = t * tm + lax.broadcasted_iota(jnp.int32, (tm, 1), 0)
    side.scores()
    z_a = proj(w_pool, w_pool)
    n_ext = POOL_HALO + tm

    for g, w in enumerate(POOL_WINDOWS):
        lo = g * POOL_GROUP
        cols = slice(lo, lo + POOL_GROUP)
        src = uext
        src_cols = cols
        step = 1
        bufs = (sa, sb)
        nbuf = 0
        while 2 * step < w:
            dst = bufs[nbuf % 2]
            dst[POOL_HALO:POOL_HALO + n_ext, :] = (src[POOL_HALO:POOL_HALO + n_ext, src_cols]
                                                   + src[POOL_HALO - step:POOL_HALO - step + n_ext, src_cols])
            src, src_cols = dst, slice(0, POOL_GROUP)
            step *= 2
            nbuf += 1
        wsum = src[base:base + tm, src_cols] + src[base - step:base - step + tm, src_cols]
        cnt = jnp.minimum(pos + 1, w).astype(F32)
        d = wsum * (1.0 / cnt) - u[:, cols]
        ya = jnp.dot(d.astype(pw_ref.dtype), pw_ref[g], preferred_element_type=F32)
        ya = ya * ps_ref[:, cols] * _silu(z_a[:, cols])
        y_scr[:, cols] = ya.astype(y_scr.dtype)

    b_gate = proj(2 * w_pool, w_conv)
    side.softmax()
    c_gate = proj(2 * w_pool + w_conv, w_conv)
    side.outputs()
    v_in = proj(2 * w_pool + 2 * w_conv, w_conv)
    v = c_gate * v_in
    vext[CONV_HALO:CONV_HALO + tm, :] = v
    conv = v * cw_ref[CONV_WIDTH - 1:CONV_WIDTH, :]
    for k in range(CONV_WIDTH - 1):
        shift = CONV_WIDTH - 1 - k
        conv = conv + vext[CONV_HALO - shift:CONV_HALO - shift + tm, :] * cw_ref[k:k + 1, :]
    z_b = proj(2 * w_pool + 3 * w_conv, w_conv)
    y_b = b_gate * conv * _silu(z_b)
    y_scr[:, w_pool:w_pool + w_conv] = y_b.astype(y_scr.dtype)

    o_ref[0] = x + jnp.dot(y_scr[...], wout_ref[...], preferred_element_type=F32)
    pool_ref[0] = uext[base + tm - POOL_BUF:base + tm, :]
    conv_ref[0] = vext[CONV_HALO + tm - CONV_BUF:CONV_HALO + tm, :]
    uext[POOL_HALO:base, :] = uext[POOL_HALO + tm:base + tm, :]
    vext[0:CONV_HALO, :] = vext[tm:tm + CONV_HALO, :]


def _even_prompt(x, g, win, pw, ps, cw, wout, qh, knt, vnt, ckt, cvt, sink_col, *, tm):
    B, T, D = x.shape
    w_pool = ps.shape[-1]
    w_conv = cw.shape[-1]
    assert T % tm == 0 and tm % 16 == 0 and tm >= POOL_HALO
    nt = T // tm
    n, kvw, win_len = ckt.shape
    assert n % (B * nt) == 0 and kvw == KV_W == 2 * LANES and win_len == LANES and n == LANES
    bb = n // (B * nt)
    const = lambda *shape: pl.BlockSpec(shape, lambda b, t: (0,) * len(shape))
    cache_block = pl.BlockSpec((bb, kvw, win_len), lambda b, t: (b * nt + t, 0, 0))
    return pl.pallas_call(
        functools.partial(_even_prompt_kernel, tm=tm),
        grid=(B, nt),
        in_specs=[
            pl.BlockSpec((1, tm, D), lambda b, t: (b, t, 0)),
            const(*g.shape),
            const(*win.shape),
            const(*pw.shape),
            const(1, w_pool),
            const(*cw.shape),
            const(*wout.shape),
            const(*qh.shape),
            const(*knt.shape),
            const(*vnt.shape),
            cache_block,
            cache_block,
            const(*sink_col.shape),
        ],
        out_specs=[
            pl.BlockSpec((1, tm, D), lambda b, t: (b, t, 0)),
            pl.BlockSpec((1, POOL_BUF, w_pool), lambda b, t: (b, 0, 0)),
            pl.BlockSpec((1, CONV_BUF, w_conv), lambda b, t: (b, 0, 0)),
            cache_block,
            cache_block,
            const(n * N_HEADS, LANES),
            const(n * N_HEADS, LANES),
        ],
        out_shape=[
            jax.ShapeDtypeStruct((B, T, D), F32),
            jax.ShapeDtypeStruct((B, POOL_BUF, w_pool), F32),
            jax.ShapeDtypeStruct((B, CONV_BUF, w_conv), F32),
            jax.ShapeDtypeStruct((n, kvw, win_len), F32),
            jax.ShapeDtypeStruct((n, kvw, win_len), F32),
            jax.ShapeDtypeStruct((n * N_HEADS, LANES), F32),
            jax.ShapeDtypeStruct((n * N_HEADS, LANES), F32),
        ],
        scratch_shapes=[
            pltpu.VMEM((2 * POOL_HALO + tm, w_pool), F32),
            pltpu.VMEM((CONV_HALO + tm, w_conv), F32),
            pltpu.VMEM((2 * POOL_HALO + tm, POOL_GROUP), F32),
            pltpu.VMEM((2 * POOL_HALO + tm, POOL_GROUP), F32),
            pltpu.VMEM((tm, w_pool + w_conv), wout.dtype),
        ],
        compiler_params=pltpu.CompilerParams(
            dimension_semantics=("arbitrary", "arbitrary"), vmem_limit_bytes=VMEM_LIMIT_BYTES),
        name="even_prompt",
    )(x, g, win, pw, ps, cw, wout, qh, knt, vnt, ckt, cvt, sink_col)


def _odd_prompt_kernel(*refs, tm, nt):
    (x_ref, xres_ref, ng_ref, win_ref, wo_ref, small_ref, cost_ref, sint_ref, sink_ref,
     olo_ref, ohi_ref, sgate_ref, sx1_ref,
     o_ref, knew_ref, vnew_ref, sy_ref, wkt_scr, wqvz_scr, wout_scr, qt_scr, kext, vt_ext, gate_scr, *rest) = refs
    g_ref = ng_ref.at[1:2]
    qg_ref, kg_ref = small_ref.at[_QG_T:_QG_T + HEAD_DIM], small_ref.at[_KG_T:_KG_T + HEAD_DIM]

    @pl.when(pl.program_id(0) == 0)
    def _():
        _sample_out(olo_ref, ohi_ref, sgate_ref, sx1_ref, wo_ref, sy_ref)
        qt_scr[...] = jnp.zeros(qt_scr.shape, BF16)
        gate_scr[...] = jnp.zeros(gate_scr.shape, F32)
        kext[...] = jnp.zeros(kext.shape, BF16)
        vt_ext[...] = jnp.zeros(vt_ext.shape, BF16)
        blk = 2 * LANES
        for src, dst, width in ((0, 0, ATTN_W), (ATTN_W + KV_W, ATTN_W, KV_W + ATTN_W)):
            for c in range(0, width, blk):
                wqvz_scr[dst + c:dst + c + blk, :] = win_ref[:, src + c:src + c + blk].T.astype(BF16)
        wkt_scr[...] = win_ref[:, ATTN_W:ATTN_W + KV_W].T.astype(BF16)
        for c in range(0, wo_ref.shape[1], blk):
            wout_scr[c:c + blk, :] = wo_ref[:, c:c + blk].T.astype(BF16)

    for parity in range(2):
        @pl.when(pl.program_id(0) % 2 == parity)
        def _():
            _odd_prompt_step(x_ref, xres_ref, g_ref, wkt_scr, wqvz_scr, qg_ref, kg_ref, cost_ref, sint_ref, sink_ref,
                             wout_scr, o_ref, knew_ref, vnew_ref, qt_scr, kext, vt_ext, gate_scr, *rest,
                             tm=tm, nt=nt, cur=parity, prev=1 - parity)


def _odd_prompt_step(x_ref, xres_ref, g_ref, wkt_ref, wqvz_ref, qg_ref, kg_ref, cost_ref, sint_ref, sink_ref,
                     wout_ref, o_ref, knew_ref, vnew_ref,
                     qt_scr, kext, vt_ext, gate_scr, yt_scr, s_scr, p_scr, esink_scr, *, tm, nt, cur, prev):
    step = pl.program_id(0)
    proj_t = jnp.minimum(step, pl.num_programs(0) - 2) % nt
    attn_t = jnp.maximum(step - 1, 0) % nt
    nblk = tm // WINDOW
    half = HEAD_DIM // 2

    h = _rms(x_ref[0], g_ref[...]).astype(BF16)
    reps = tm // LANES
    cos_t = cost_ref[...]
    sin_t = sint_ref[...]

    def norm_rope(xt, gain):
        ms = jnp.sum(xt * xt, axis=0, keepdims=True) * (1.0 / HEAD_DIM)
        xn = xt * lax.rsqrt(ms + RMS_EPS) * gain
        x1, x2 = xn[0:half, :], xn[half:HEAD_DIM, :]
        return jnp.concatenate([x1 * cos_t - x2 * sin_t, x2 * cos_t + x1 * sin_t], axis=0)

    has_past = proj_t > 0

    def proj_kv():
        kt = _nt_dot(wkt_ref[...], h)
        kgain = jnp.concatenate([kg_ref[...]] * reps, axis=1)
        kt = jnp.concatenate(
            [norm_rope(kt[kv * HEAD_DIM:(kv + 1) * HEAD_DIM, :], kgain) for kv in range(N_KV_HEADS)], axis=0)
        knew_ref[0] = kt[:, tm - WINDOW:tm]
        kext[cur, 0:WINDOW, :] = jnp.where(has_past, kext[prev, tm:tm + WINDOW, :], jnp.zeros((WINDOW, KV_W), BF16))
        kext[cur, WINDOW:WINDOW + tm, :] = kt.T.astype(BF16)
        vt = _nt_dot(wqvz_ref[ATTN_W:ATTN_W + KV_W, :], h)
        vnew_ref[0] = vt[:, tm - WINDOW:tm]
        vt_ext[cur, :, 0:WINDOW] = jnp.where(has_past, vt_ext[prev, :, tm:tm + WINDOW],
                                             jnp.zeros((KV_W, WINDOW), BF16))
        vt_ext[cur, :, WINDOW:WINDOW + tm] = vt.astype(BF16)

    def proj_gate(lo, hi):
        zt = _nt_dot(wqvz_ref[ATTN_W + KV_W + lo:ATTN_W + KV_W + hi, :], h)
        gate_scr[cur, lo:hi, :] = _silu(zt)

    q_raw = {}

    def proj_q(lo, hi):
        q_raw[lo] = _nt_dot(wqvz_ref[lo:hi, :], h)

    def finish_q():
        qgain = jnp.concatenate([qg_ref[...]] * reps, axis=1)
        for lo, qt in q_raw.items():
            for r0 in range(0, qt.shape[0], HEAD_DIM):
                qt_scr[cur, lo + r0:lo + r0 + HEAD_DIM, :] = (
                    norm_rope(qt[r0:r0 + HEAD_DIM, :], qgain) * (LOG2_E * HEAD_DIM ** -0.5)).astype(BF16)
        q_raw.clear()

    pieces = [proj_kv,
              lambda: proj_gate(0, ATTN_W // 2),
              lambda: proj_gate(ATTN_W // 2, ATTN_W),
              lambda: proj_q(0, ATTN_W // 2)]
    last_piece = lambda: proj_q(ATTN_W // 2, ATTN_W)

    ri = lax.broadcasted_iota(jnp.int32, (WINDOW, 2 * WINDOW), 0)
    qq = lax.broadcasted_iota(jnp.int32, (WINDOW, 2 * WINDOW), 1) % WINDOW
    from_prev = ri > qq
    keep_prev = from_prev.astype(BF16)
    keep_cur = 1.0 - keep_prev
    lane = lax.broadcasted_iota(jnp.int32, (1, 2 * WINDOW), 1)
    zeros = jnp.zeros((HEAD_DIM, 2 * WINDOW), BF16)
    ones = jnp.ones((16, 2 * WINDOW), BF16)

    for i in range(nblk):
        c0 = i * WINDOW
        qcols = slice(c0, c0 + WINDOW)
        pairs = [(kv, kv * GROUP + 2 * pr) for kv in range(N_KV_HEADS) for pr in range(GROUP // 2)]
        for j, (kv, ha) in enumerate(pairs):
            chunk, pos = divmod(kv, LANES // HEAD_DIM)
            kblk = kext[prev, c0:c0 + 2 * WINDOW, chunk * LANES:(chunk + 1) * LANES]
            qpair = jnp.concatenate([qt_scr[prev, ha * HEAD_DIM:(ha + 1) * HEAD_DIM, qcols],
                                     qt_scr[prev, (ha + 1) * HEAD_DIM:(ha + 2) * HEAD_DIM, qcols]], axis=1)
            rhs = jnp.concatenate([qpair, zeros] if pos == 0 else [zeros, qpair], axis=0)
            s = jnp.dot(kblk, rhs, preferred_element_type=F32)
            s_prev = s[0:WINDOW, :]
            if i == 0:
                s_prev = jnp.where(attn_t == 0, -jnp.inf, s_prev)
            s_scr[j] = jnp.where(from_prev, s_prev, s[WINDOW:2 * WINDOW, :])
        for c, piece in enumerate(pieces):
            if c * nblk // len(pieces) == i:
                piece()
        for j, (kv, ha) in enumerate(pairs):
            s = s_scr[j]
            sink = jnp.where(lane < WINDOW, sink_ref[ha], sink_ref[ha + 1]) * LOG2_E
            m = jnp.maximum(jnp.max(s, axis=0, keepdims=True), sink)
            p_scr[j] = jnp.exp2(s - m).astype(BF16)
            esink_scr[j:j + 1, :] = jnp.exp2(sink - m)
        for j, (kv, ha) in enumerate(pairs):
            vg = vt_ext[prev, kv * HEAD_DIM:(kv + 1) * HEAD_DIM, c0:c0 + 2 * WINDOW]
            p = p_scr[j]
            p_keys = jnp.concatenate([p * keep_prev, p * keep_cur], axis=0)
            oa = jnp.dot(jnp.concatenate([vg, ones], axis=0), p_keys, preferred_element_type=F32)
            denom = oa[HEAD_DIM:HEAD_DIM + 1, :] + esink_scr[j:j + 1, :]
            o = oa[0:HEAD_DIM, :] * (1.0 / denom)
            for hd, part in ((ha, o[:, 0:WINDOW]), (ha + 1, o[:, WINDOW:2 * WINDOW])):
                rows = slice(hd * HEAD_DIM, (hd + 1) * HEAD_DIM)
                yt_scr[rows, qcols] = (part * gate_scr[prev, rows, qcols]).astype(BF16)

    out_t = jnp.dot(wout_ref[...], yt_scr[...], preferred_element_type=F32)
    finish_q()
    last_piece()
    finish_q()
    o_ref[0] = xres_ref[0] + out_t.T


def _odd_prompt(x, ng, w_in, w_out, small, cos_t, sin_t, sinks, s_olo, s_ohi, s_gate, s_x1, *, tm):
    B, T, D = x.shape
    assert T % tm == 0 and tm % WINDOW == 0 and w_in.shape == (D, 2 * ATTN_W + 2 * KV_W) and w_out.shape == (ATTN_W, D)
    nt = T // tm
    half = HEAD_DIM // 2
    const = lambda *shape: pl.BlockSpec(shape, lambda s: (0,) * len(shape))
    n_tiles = B * nt
    proj_tile = lambda s: jnp.minimum(s, n_tiles - 1)
    attn_tile = lambda s: jnp.maximum(s - 1, 0)
    return pl.pallas_call(
        functools.partial(_odd_prompt_kernel, tm=tm, nt=nt),
        grid=(n_tiles + 1,),
        in_specs=[
            pl.BlockSpec((1, tm, D), lambda s: (proj_tile(s) // nt, proj_tile(s) % nt, 0)),
            pl.BlockSpec((1, tm, D), lambda s: (attn_tile(s) // nt, attn_tile(s) % nt, 0)),
            const(*ng.shape),
            const(*w_in.shape),
            const(*w_out.shape),
            const(*small.shape),
            pl.BlockSpec((half, tm), lambda s: (0, proj_tile(s) % nt)),
            pl.BlockSpec((half, tm), lambda s: (0, proj_tile(s) % nt)),
            pl.BlockSpec(memory_space=pltpu.SMEM),
            const(*s_olo.shape),
            const(*s_ohi.shape),
            const(*s_gate.shape),
            const(*s_x1.shape),
        ],
        out_specs=[
            pl.BlockSpec((1, tm, D), lambda s: (attn_tile(s) // nt, attn_tile(s) % nt, 0)),
            pl.BlockSpec((1, KV_W, WINDOW), lambda s: (proj_tile(s) // nt, 0, 0)),
            pl.BlockSpec((1, KV_W, WINDOW), lambda s: (proj_tile(s) // nt, 0, 0)),
            const(s_x1.size // LANES, LANES),
        ],
        out_shape=[
            jax.ShapeDtypeStruct((B, T, D), F32),
            jax.ShapeDtypeStruct((B, KV_W, WINDOW), F32),
            jax.ShapeDtypeStruct((B, KV_W, WINDOW), F32),
            jax.ShapeDtypeStruct((s_x1.size // LANES, LANES), F32),
        ],
        scratch_shapes=[
            pltpu.VMEM((KV_W, D), BF16),
            pltpu.VMEM((2 * ATTN_W + KV_W, D), BF16),
            pltpu.VMEM((D, ATTN_W), BF16),
            pltpu.VMEM((2, ATTN_W, tm), BF16),
            pltpu.VMEM((2, WINDOW + tm, KV_W), BF16),
            pltpu.VMEM((2, KV_W, WINDOW + tm), BF16),
            pltpu.VMEM((2, ATTN_W, tm), F32),
            pltpu.VMEM((ATTN_W, tm), BF16),
            pltpu.VMEM((N_HEADS // 2, WINDOW, 2 * WINDOW), F32),
            pltpu.VMEM((N_HEADS // 2, WINDOW, 2 * WINDOW), BF16),
            pltpu.VMEM((N_HEADS // 2, 2 * WINDOW), F32),
        ],
        compiler_params=pltpu.CompilerParams(
            dimension_semantics=("arbitrary",), vmem_limit_bytes=VMEM_LIMIT_BYTES),
        name="odd_prompt",
    )(x, x, ng, w_in, w_out, small, cos_t, sin_t, sinks, s_olo, s_ohi, s_gate, s_x1)


def _dup_heads(x):
    lane = lax.broadcasted_iota(jnp.int32, x.shape, 1)
    rolled = pltpu.roll(x, HEAD_DIM, 1)
    first = lane < HEAD_DIM
    return jnp.where(first, x, rolled), jnp.where(first, rolled, x)


def _sample_dense_kernel(x_ref, sp_ref, sc_ref, ng_ref, win0_ref, pw_ref, ps_ref, cw_ref, wout0_ref,
                         win1_ref, small_ref,
                         x1_ref, pool_ref, conv_ref, qh_ref, kt_ref, vt_ref, gate_ref):
    w_pool = ps_ref.shape[1]
    w_conv = cw_ref.shape[1]
    g0_ref, g1_ref = ng_ref.at[0:1], ng_ref.at[1:2]
    qg_ref, kgt_ref = small_ref.at[_QG_ROW:_QG_ROW + 1], small_ref.at[_KG_T:_KG_T + HEAD_DIM]
    cos_ref, sin_ref = small_ref.at[_COS_ROW:_COS_ROW + 1], small_ref.at[_SIN_ROW:_SIN_ROW + 1]
    cost_ref, sint_ref = small_ref.at[_COS_T:_COS_T + HEAD_DIM // 2], small_ref.at[_SIN_T:_SIN_T + HEAD_DIM // 2]

    n = x1_ref.shape[0]
    chunks = x1_ref.shape[1] // LANES
    x = jnp.concatenate([x_ref[pl.ds(c, n, stride=chunks), :] for c in range(chunks)], axis=1)
    h = _rms(x, g0_ref[...]).astype(win0_ref.dtype)

    def proj(c, width):
        return jnp.dot(h, win0_ref[:, c:c + width], preferred_element_type=F32)

    u = proj(0, w_pool)
    z_a = proj(w_pool, w_pool)
    ys = []
    for g, w in enumerate(POOL_WINDOWS):
        cols = slice(g * POOL_GROUP, (g + 1) * POOL_GROUP)
        wsum = u[:, cols]
        for r in range(POOL_BUF - (w - 1), POOL_BUF):
            wsum = wsum + sp_ref[r, :, cols]
        cnt = float(min(PAST_LEN + 1, w))
        d = wsum * (1.0 / cnt) - u[:, cols]
        ya = jnp.dot(d.astype(pw_ref.dtype), pw_ref[g], preferred_element_type=F32)
        ys.append((ya * ps_ref[:, cols] * _silu(z_a[:, cols])).astype(wout0_ref.dtype))
    for r in range(POOL_BUF - 1):
        pool_ref[r] = sp_ref[r + 1]
    pool_ref[POOL_BUF - 1] = u

    b_gate = proj(2 * w_pool, w_conv)
    c_gate = proj(2 * w_pool + w_conv, w_conv)
    v_in = proj(2 * w_pool + 2 * w_conv, w_conv)
    v = c_gate * v_in
    conv = v * cw_ref[CONV_WIDTH - 1:CONV_WIDTH, :]
    for k in range(CONV_BUF):
        conv = conv + sc_ref[k] * cw_ref[k:k + 1, :]
    for r in range(CONV_BUF - 1):
        conv_ref[r] = sc_ref[r + 1]
    conv_ref[CONV_BUF - 1] = v
    z_b = proj(2 * w_pool + 3 * w_conv, w_conv)
    ys.append((b_gate * conv * _silu(z_b)).astype(wout0_ref.dtype))
    y = jnp.concatenate(ys, axis=1)
    x1 = x + jnp.dot(y, wout0_ref[...], preferred_element_type=F32)
    x1_ref[...] = x1

    n = x.shape[0]
    half = HEAD_DIM // 2
    h1 = _rms(x1, g1_ref[...])

    def proj1(lo, width):
        return jnp.dot(h1, win1_ref[:, lo:lo + width], preferred_element_type=F32)

    q = proj1(0, ATTN_W)
    for c in range(ATTN_W // LANES):
        qc = _head_norm_rope(q[:, c * LANES:(c + 1) * LANES], qg_ref[...], cos_ref[...], sin_ref[...])
        d0, d1 = _dup_heads(qc * (HEAD_DIM ** -0.5))
        qh_ref[2 * c * n:(2 * c + 1) * n, :] = d0
        qh_ref[(2 * c + 1) * n:(2 * c + 2) * n, :] = d1
    kt = proj1(ATTN_W, KV_W).T
    cos_t = cost_ref[...]
    sin_t = sint_ref[...]
    for kv in range(N_KV_HEADS):
        kh = kt[kv * HEAD_DIM:(kv + 1) * HEAD_DIM, :]
        ms = jnp.sum(kh * kh, axis=0, keepdims=True) * (1.0 / HEAD_DIM)
        kn = kh * lax.rsqrt(ms + RMS_EPS) * kgt_ref[...]
        x1h, x2h = kn[0:half, :], kn[half:HEAD_DIM, :]
        kt_ref[kv * HEAD_DIM:(kv + 1) * HEAD_DIM, :] = jnp.concatenate(
            [x1h * cos_t - x2h * sin_t, x2h * cos_t + x1h * sin_t], axis=0)
    vt_ref[...] = proj1(ATTN_W + KV_W, KV_W).T
    gate_ref[...] = _silu(proj1(ATTN_W + 2 * KV_W, ATTN_W))


def _sample_dense(x, sp, sc, ng, win0, pw, ps, cw, wout0, win1, small):
    D = win0.shape[0]
    n = x.shape[0] * LANES // D
    assert n == LANES
    vmem = pl.BlockSpec(memory_space=pltpu.VMEM)
    return pl.pallas_call(
        _sample_dense_kernel,
        in_specs=[vmem] * 11,
        out_specs=[vmem] * 7,
        out_shape=[
            jax.ShapeDtypeStruct((n, D), F32),
            jax.ShapeDtypeStruct(sp.shape, F32),
            jax.ShapeDtypeStruct(sc.shape, F32),
            jax.ShapeDtypeStruct((N_HEADS * n, LANES), F32),
            jax.ShapeDtypeStruct((KV_W, n), F32),
            jax.ShapeDtypeStruct((KV_W, n), F32),
            jax.ShapeDtypeStruct((n, ATTN_W), F32),
        ],
        compiler_params=pltpu.CompilerParams(vmem_limit_bytes=VMEM_LIMIT_BYTES),
        name="sample_dense",
    )(x, sp, sc, ng, win0, pw, ps, cw, wout0, win1, small)


def _sample_out(olo_ref, ohi_ref, gate_ref, x1_ref, wout_ref, y_ref):
    n = x1_ref.shape[0]
    blocks = LANES // HEAD_DIM
    lane_n = lax.broadcasted_iota(jnp.int32, (n, LANES), 1)
    chunks = []
    for c in range(ATTN_W // LANES):
        parts = []
        for hd in (2 * c, 2 * c + 1):
            kv = hd // GROUP
            slab = (olo_ref, ohi_ref)[kv // blocks]
            a = slab[pl.ds(hd, n, stride=N_HEADS), :]
            parts.append(a if kv % blocks == hd % blocks else pltpu.roll(a, HEAD_DIM, 1))
        chunks.append(jnp.where(lane_n < HEAD_DIM, parts[0], parts[1]))
    y = jnp.concatenate(chunks, axis=1) * gate_ref[...]
    out = x1_ref[...] + jnp.dot(y, wout_ref[...], preferred_element_type=F32)
    chunks = out.shape[1] // LANES
    for c in range(chunks):
        y_ref[pl.ds(c, n, stride=chunks), :] = out[:, c * LANES:(c + 1) * LANES]


def _rope_tables(pos):
    half = HEAD_DIM // 2
    inv = ROPE_THETA ** (-jnp.arange(half, dtype=F32) / half)
    ang = pos.astype(F32)[:, None] * inv[None, :]
    return jnp.cos(ang), jnp.sin(ang)


def kernel(x_prompt, x_sample, state_pool, state_conv, cache_k, cache_v, norm_g, w_in_even, pool_w, pool_scale,
           conv_w, w_out_even, w_in_odd, q_norm_g, k_norm_g, attn_sinks, w_out_odd):
    B, T, D = x_prompt.shape
    n_s, t_s, _ = x_sample.shape
    assert norm_g.shape[0] == 2 and w_in_even.shape[0] == 1 and w_in_odd.shape[0] == 1
    assert t_s == 1 and cache_k.shape[2] == WINDOW and T >= WINDOW
    assert cache_k.shape[3] * cache_k.shape[4] == KV_W and pool_w.shape[1:] == (len(POOL_WINDOWS), POOL_GROUP, POOL_GROUP)

    win0 = w_in_even[0]
    wout0 = w_out_even[0]
    win1 = w_in_odd[0]
    wout1 = w_out_odd[0]
    pw = pool_w[0]
    ps = pool_scale[0][None, :]
    cw = conv_w[0]
    reps = LANES // HEAD_DIM
    sinks = attn_sinks[0]

    cos_p, sin_p = _rope_tables(jnp.arange(T))
    cos_pt, sin_pt = cos_p.T, sin_p.T
    cos_1, sin_1 = _rope_tables(PAST_LEN + jnp.arange(t_s))
    small = jnp.concatenate([
        jnp.broadcast_to(q_norm_g[0][:, None], (HEAD_DIM, LANES)),
        jnp.broadcast_to(k_norm_g[0][:, None], (HEAD_DIM, LANES)),
        jnp.broadcast_to(cos_1.T, (HEAD_DIM // 2, LANES)),
        jnp.broadcast_to(sin_1.T, (HEAD_DIM // 2, LANES)),
        jnp.tile(q_norm_g[0], reps)[None, :],
        jnp.tile(jnp.concatenate([cos_1, cos_1], axis=1), (1, reps)),
        jnp.tile(jnp.concatenate([-sin_1, sin_1], axis=1), (1, reps)),
        jnp.zeros((_SMALL_ROWS - _SIN_ROW - 1, LANES), F32)], axis=0)

    sp = jnp.transpose(state_pool[0], (1, 0, 2))
    sc = jnp.transpose(state_conv[0], (1, 0, 2))
    x1s, pool_s, conv_s, qh_s, kt_s, vt_s, gate_s = _sample_dense(
        x_sample.reshape(n_s * D // LANES, LANES), sp, sc, norm_g, win0, pw, ps, cw, wout0, win1, small)
    pool_s = jnp.transpose(pool_s, (1, 0, 2))
    conv_s = jnp.transpose(conv_s, (1, 0, 2))

    ckt = jnp.transpose(cache_k[0], (0, 2, 3, 1)).reshape(n_s, KV_W, WINDOW)
    cvt = jnp.transpose(cache_v[0], (0, 2, 3, 1)).reshape(n_s, KV_W, WINDOW)
    x1p, pool_p, conv_p, nkt_s, nvt_s, olo_s, ohi_s = _even_prompt(
        x_prompt, norm_g, win0, pw, ps, cw, wout0, qh_s, kt_s, vt_s, ckt, cvt, sinks[:, None], tm=1024)
    y_p, k_p, v_p, y_s = _odd_prompt(x1p, norm_g, win1, wout1, small, cos_pt, sin_pt, sinks,
                                     olo_s, ohi_s, gate_s, x1s, tm=512)

    def window_major(a):
        return jnp.transpose(a.reshape(-1, N_KV_HEADS, HEAD_DIM, WINDOW), (0, 3, 1, 2))[None]

    return (y_p, y_s.reshape(n_s, 1, D), pool_p[None], pool_s[None], conv_p[None], conv_s[None],
            window_major(k_p), window_major(v_p), window_major(nkt_s), window_major(nvt_s))
```

```python
import functools

import jax
import jax.numpy as jnp
from jax import lax
from jax.experimental import pallas as pl
from jax.experimental.pallas import tpu as pltpu

F32 = jnp.float32
BF16 = jnp.bfloat16

POOL_WINDOWS = (2, 4, 8, 16)
POOL_GROUP = 128
POOL_BUF = max(POOL_WINDOWS) - 1
CONV_WIDTH = 3
CONV_BUF = CONV_WIDTH - 1
N_HEADS = 16
HEAD_DIM = 64
N_KV_HEADS = 4
GROUP = N_HEADS // N_KV_HEADS
WINDOW = 128
ROPE_THETA = 10000.0
RMS_EPS = 1e-6
PAST_LEN = 16384
ATTN_W = N_HEADS * HEAD_DIM
KV_W = N_KV_HEADS * HEAD_DIM

LANES = 128
POOL_HALO = 16
CONV_HALO = 8
VMEM_LIMIT_BYTES = 60 * 1024 * 1024
LOG2_E = 1.4426950408889634
_QG_T, _KG_T, _COS_T, _SIN_T = 0, HEAD_DIM, 2 * HEAD_DIM, 2 * HEAD_DIM + HEAD_DIM // 2
_QG_ROW, _COS_ROW, _SIN_ROW, _SMALL_ROWS = 3 * HEAD_DIM, 3 * HEAD_DIM + 1, 3 * HEAD_DIM + 2, 3 * HEAD_DIM + 8


def _rms(x, g):
    ms = jnp.mean(x * x, axis=-1, keepdims=True)
    return x * lax.rsqrt(ms + RMS_EPS) * g


def _silu(z):
    return z * jax.nn.sigmoid(z)


def _head_norm_rope(x, gain, cos, sin):
    lane = lax.broadcasted_iota(jnp.int32, x.shape, 1)
    first = lane < HEAD_DIM
    x2 = x * x
    ss0 = jnp.sum(jnp.where(first, x2, 0.0), axis=-1, keepdims=True)
    ss1 = jnp.sum(jnp.where(first, 0.0, x2), axis=-1, keepdims=True)
    r = jnp.where(first, lax.rsqrt(ss0 * (1.0 / HEAD_DIM) + RMS_EPS), lax.rsqrt(ss1 * (1.0 / HEAD_DIM) + RMS_EPS))
    xn = x * r * gain
    half = HEAD_DIM // 2
    swapped = jnp.where((lane % HEAD_DIM) < half, pltpu.roll(xn, LANES - half, 1), pltpu.roll(xn, half, 1))
    return xn * cos + swapped * sin


def _nt_dot(a, b):
    return lax.dot_general(a, b, (((1,), (1,)), ((), ())), preferred_element_type=F32)


class _SampleAttention:
    def __init__(self, qh_ref, knt_ref, vnt_ref, ck_ref, cv_ref, sink_ref, nk_ref, nv_ref, olo_ref, ohi_ref, *, step):
        self.refs = (qh_ref, knt_ref, vnt_ref, ck_ref, cv_ref, sink_ref, nk_ref, nv_ref, olo_ref, ohi_ref)
        self.bb = ck_ref.shape[0]
        self.first = step * self.bb

    def roll(self):
        _, knt_ref, vnt_ref, ck_ref, cv_ref, _, nk_ref, nv_ref, _, _ = self.refs
        win = ck_ref.shape[2]
        newest = lax.broadcasted_iota(jnp.int32, (KV_W, win), 1) == win - 1
        for i in range(self.bb):
            b = self.first + i
            nk_ref[i] = jnp.where(newest, pltpu.roll(knt_ref[...], win - 1 - b, 1), pltpu.roll(ck_ref[i], win - 1, 1))
            nv_ref[i] = jnp.where(newest, pltpu.roll(vnt_ref[...], win - 1 - b, 1), pltpu.roll(cv_ref[i], win - 1, 1))

    def scores(self):
        qh_ref, nk_ref = self.refs[0], self.refs[6]
        n = qh_ref.shape[0] // N_HEADS
        blocks = LANES // HEAD_DIM
        hrow = lax.broadcasted_iota(jnp.int32, (N_HEADS, LANES), 0) // GROUP
        hcol = lax.broadcasted_iota(jnp.int32, (N_HEADS, LANES), 1) // HEAD_DIM
        self.s = []
        for i in range(self.bb):
            qd = qh_ref[pl.ds(self.first + i, N_HEADS, stride=n), :]
            qx = jnp.concatenate(
                [jnp.where(hcol + blocks * c == hrow, qd, 0.0) for c in range(KV_W // LANES)], axis=1)
            self.s.append(jnp.dot(qx.astype(BF16), nk_ref[i].astype(BF16), preferred_element_type=F32))

    def softmax(self):
        sink = self.refs[5][...]
        self.p = []
        for s in self.s:
            m = jnp.maximum(jnp.max(s, axis=-1, keepdims=True), sink)
            p = jnp.exp(s - m)
            denom = jnp.sum(p, axis=-1, keepdims=True) + jnp.exp(sink - m)
            self.p.append((p * (1.0 / denom)).astype(BF16))

    def outputs(self):
        nv_ref, olo_ref, ohi_ref = self.refs[7:10]
        for i in range(self.bb):
            o = _nt_dot(self.p[i], nv_ref[i].astype(BF16))
            rows = pl.ds(pl.multiple_of((self.first + i) * N_HEADS, N_HEADS), N_HEADS)
            olo_ref[rows, :] = o[:, 0:LANES]
            ohi_ref[rows, :] = o[:, LANES:2 * LANES]


def _even_prompt_kernel(x_ref, g_ref, win_ref, pw_ref, ps_ref, cw_ref, wout_ref,
                        qh_ref, knt_ref, vnt_ref, ck_ref, cv_ref, sink_ref,
                        o_ref, pool_ref, conv_ref, nk_ref, nv_ref, olo_ref, ohi_ref,
                        uext, vext, sa, sb, y_scr, *, tm):
    t = pl.program_id(1)
    w_pool = uext.shape[1]
    w_conv = vext.shape[1]
    base = 2 * POOL_HALO
    side = _SampleAttention(qh_ref, knt_ref, vnt_ref, ck_ref, cv_ref, sink_ref, nk_ref, nv_ref, olo_ref, ohi_ref,
                            step=pl.program_id(0) * pl.num_programs(1) + t)

    @pl.when(t == 0)
    def _():
        uext[0:base, :] = jnp.zeros((base, w_pool), F32)
        vext[0:CONV_HALO, :] = jnp.zeros((CONV_HALO, w_conv), F32)
        sa[0:POOL_HALO, :] = jnp.zeros((POOL_HALO, POOL_GROUP), F32)
        sb[0:POOL_HALO, :] = jnp.zeros((POOL_HALO, POOL_GROUP), F32)

    x = x_ref[0]
    h = _rms(x, g_ref[0:1, :]).astype(win_ref.dtype)

    def proj(c, width):
        return jnp.dot(h, win_ref[:, c:c + width], preferred_element_type=F32)

    side.roll()
    u = proj(0, w_pool)
    uext[base:base + tm, :] = u
    pos = t * tm + lax.broadcasted_iota(jnp.int32, (tm, 1), 0)
    side.scores()
    z_a = proj(w_pool, w_pool)
    n_ext = POOL_HALO + tm

    for g, w in enumerate(POOL_WINDOWS):
        lo = g * POOL_GROUP
        cols = slice(lo, lo + POOL_GROUP)
        src = uext
        src_cols = cols
        step = 1
        bufs = (sa, sb)
        nbuf = 0
        while 2 * step < w:
            dst = bufs[nbuf % 2]
            dst[POOL_HALO:POOL_HALO + n_ext, :] = (src[POOL_HALO:POOL_HALO + n_ext, src_cols]
                                                   + src[POOL_HALO - step:POOL_HALO - step + n_ext, src_cols])
            src, src_cols = dst, slice(0, POOL_GROUP)
            step *= 2
            nbuf += 1
        wsum = src[base:base + tm, src_cols] + src[base - step:base - step + tm, src_cols]
        cnt = jnp.minimum(pos + 1, w).astype(F32)
        d = wsum * (1.0 / cnt) - u[:, cols]
        ya = jnp.dot(d.astype(pw_ref.dtype), pw_ref[g], preferred_element_type=F32)
        ya = ya * ps_ref[:, cols] * _silu(z_a[:, cols])
        y_scr[:, cols] = ya.astype(y_scr.dtype)

    b_gate = proj(2 * w_pool, w_conv)
    side.softmax()
    c_gate = proj(2 * w_pool + w_conv, w_conv)
    side.outputs()
    v_in = proj(2 * w_pool + 2 * w_conv, w_conv)
    v = c_gate * v_in
    vext[CONV_HALO:CONV_HALO + tm, :] = v
    conv = v * cw_ref[CONV_WIDTH - 1:CONV_WIDTH, :]
    for k in range(CONV_WIDTH - 1):
        shift = CONV_WIDTH - 1 - k
        conv = conv + vext[CONV_HALO - shift:CONV_HALO - shift + tm, :] * cw_ref[k:k + 1, :]
    z_b = proj(2 * w_pool + 3 * w_conv, w_conv)
    y_b = b_gate * conv * _silu(z_b)
    y_scr[:, w_pool:w_pool + w_conv] = y_b.astype(y_scr.dtype)

    o_ref[0] = x + jnp.dot(y_scr[...], wout_ref[...], preferred_element_type=F32)
    pool_ref[0] = uext[base + tm - POOL_BUF:base + tm, :]
    conv_ref[0] = vext[CONV_HALO + tm - CONV_BUF:CONV_HALO + tm, :]
    uext[POOL_HALO:base, :] = uext[POOL_HALO + tm:base + tm, :]
    vext[0:CONV_HALO, :] = vext[tm:tm + CONV_HALO, :]


def _even_prompt(x, g, win, pw, ps, cw, wout, qh, knt, vnt, ckt, cvt, sink_col, *, tm):
    B, T, D = x.shape
    w_pool = ps.shape[-1]
    w_conv = cw.shape[-1]
    assert T % tm == 0 and tm % 16 == 0 and tm >= POOL_HALO
    nt = T // tm
    n, kvw, win_len = ckt.shape
    assert n % (B * nt) == 0 and kvw == KV_W == 2 * LANES and win_len == LANES and n == LANES
    bb = n // (B * nt)
    const = lambda *shape: pl.BlockSpec(shape, lambda b, t: (0,) * len(shape))
    cache_block = pl.BlockSpec((bb, kvw, win_len), lambda b, t: (b * nt + t, 0, 0))
    return pl.pallas_call(
        functools.partial(_even_prompt_kernel, tm=tm),
        grid=(B, nt),
        in_specs=[
            pl.BlockSpec((1, tm, D), lambda b, t: (b, t, 0)),
            const(*g.shape),
            const(*win.shape),
            const(*pw.shape),
            const(1, w_pool),
            const(*cw.shape),
            const(*wout.shape),
            const(*qh.shape),
            const(*knt.shape),
            const(*vnt.shape),
            cache_block,
            cache_block,
            const(*sink_col.shape),
        ],
        out_specs=[
            pl.BlockSpec((1, tm, D), lambda b, t: (b, t, 0)),
            pl.BlockSpec((1, POOL_BUF, w_pool), lambda b, t: (b, 0, 0)),
            pl.BlockSpec((1, CONV_BUF, w_conv), lambda b, t: (b, 0, 0)),
            cache_block,
            cache_block,
            const(n * N_HEADS, LANES),
            const(n * N_HEADS, LANES),
        ],
        out_shape=[
            jax.ShapeDtypeStruct((B, T, D), F32),
            jax.ShapeDtypeStruct((B, POOL_BUF, w_pool), F32),
            jax.ShapeDtypeStruct((B, CONV_BUF, w_conv), F32),
            jax.ShapeDtypeStruct((n, kvw, win_len), F32),
            jax.ShapeDtypeStruct((n, kvw, win_len), F32),
            jax.ShapeDtypeStruct((n * N_HEADS, LANES), F32),
            jax.ShapeDtypeStruct((n * N_HEADS, LANES), F32),
        ],
        scratch_shapes=[
            pltpu.VMEM((2 * POOL_HALO + tm, w_pool), F32),
            pltpu.VMEM((CONV_HALO + tm, w_conv), F32),
            pltpu.VMEM((2 * POOL_HALO + tm, POOL_GROUP), F32),
            pltpu.VMEM((2 * POOL_HALO + tm, POOL_GROUP), F32),
            pltpu.VMEM((tm, w_pool + w_conv), wout.dtype),
        ],
        compiler_params=pltpu.CompilerParams(
            dimension_semantics=("arbitrary", "arbitrary"), vmem_limit_bytes=VMEM_LIMIT_BYTES),
        name="even_prompt",
    )(x, g, win, pw, ps, cw, wout, qh, knt, vnt, ckt, cvt, sink_col)


def _odd_prompt_kernel(*refs, tm, nt, n_steps):
    (x_ref, xres_ref, ng_ref, win_ref, wo_ref, small_ref, cost_ref, sint_ref, sink_ref,
     olo_ref, ohi_ref, sgate_ref, sx1_ref,
     o_ref, knew_ref, vnew_ref, sy_ref, wkt_scr, wqvz_scr, wout_scr, qt_scr, kext, vt_ext, gate_scr, *rest) = refs
    g_ref = ng_ref.at[1:2]
    qg_ref, kg_ref = small_ref.at[_QG_T:_QG_T + HEAD_DIM], small_ref.at[_KG_T:_KG_T + HEAD_DIM]

    step = pl.program_id(0)
    last = n_steps - 1

    @pl.when(step == 0)
    def _():
        _sample_out(olo_ref, ohi_ref, sgate_ref, sx1_ref, wo_ref, sy_ref)
        kext[...] = jnp.zeros(kext.shape, BF16)
        vt_ext[...] = jnp.zeros(vt_ext.shape, BF16)
        blk = 2 * LANES
        for src, dst, width in ((0, 0, ATTN_W), (ATTN_W + KV_W, ATTN_W, KV_W + ATTN_W)):
            for c in range(0, width, blk):
                wqvz_scr[dst + c:dst + c + blk, :] = win_ref[:, src + c:src + c + blk].T.astype(BF16)
        wkt_scr[...] = win_ref[:, ATTN_W:ATTN_W + KV_W].T.astype(BF16)
        for c in range(0, wo_ref.shape[1], blk):
            wout_scr[c:c + blk, :] = wo_ref[:, c:c + blk].T.astype(BF16)

    def run(cur, **halves):
        _odd_prompt_step(x_ref, xres_ref, g_ref, wkt_scr, wqvz_scr, qg_ref, kg_ref, cost_ref, sint_ref, sink_ref,
                         wout_scr, o_ref, knew_ref, vnew_ref, qt_scr, kext, vt_ext, gate_scr, *rest,
                         tm=tm, nt=nt, cur=cur, prev=1 - cur, **halves)

    pl.when(step == 0)(lambda: run(0, do_attn=False))
    for parity in range(2):
        pl.when((step % 2 == parity) & (step > 0) & (step < last))(lambda parity=parity: run(parity))
    pl.when(step == last)(lambda: run((n_steps - 1) % 2, do_proj=False))


def _odd_prompt_step(x_ref, xres_ref, g_ref, wkt_ref, wqvz_ref, qg_ref, kg_ref, cost_ref, sint_ref, sink_ref,
                     wout_ref, o_ref, knew_ref, vnew_ref,
                     qt_scr, kext, vt_ext, gate_scr, yt_scr, s_scr, p_scr, esink_scr, *, tm, nt, cur, prev,
                     do_proj=True, do_attn=True):
    step = pl.program_id(0)
    proj_t = step % nt
    attn_t = jnp.maximum(step - 1, 0) % nt
    nblk = tm // WINDOW
    half = HEAD_DIM // 2

    h = _rms(x_ref[0], g_ref[...]).astype(BF16) if do_proj else None
    reps = tm // LANES
    cos_t = cost_ref[...]
    sin_t = sint_ref[...]

    def norm_rope(xt, gain):
        ms = jnp.sum(xt * xt, axis=0, keepdims=True) * (1.0 / HEAD_DIM)
        xn = xt * lax.rsqrt(ms + RMS_EPS) * gain
        x1, x2 = xn[0:half, :], xn[half:HEAD_DIM, :]
        return jnp.concatenate([x1 * cos_t - x2 * sin_t, x2 * cos_t + x1 * sin_t], axis=0)

    has_past = proj_t > 0

    def proj_kv():
        kt = _nt_dot(wkt_ref[...], h)
        kgain = jnp.concatenate([kg_ref[...]] * reps, axis=1)
        kt = jnp.concatenate(
            [norm_rope(kt[kv * HEAD_DIM:(kv + 1) * HEAD_DIM, :], kgain) for kv in range(N_KV_HEADS)], axis=0)
        knew_ref[0] = kt[:, tm - WINDOW:tm]
        kext[cur, 0:WINDOW, :] = jnp.where(has_past, kext[prev, tm:tm + WINDOW, :], jnp.zeros((WINDOW, KV_W), BF16))
        kext[cur, WINDOW:WINDOW + tm, :] = kt.T.astype(BF16)
        vt = _nt_dot(wqvz_ref[ATTN_W:ATTN_W + KV_W, :], h)
        vnew_ref[0] = vt[:, tm - WINDOW:tm]
        vt_ext[cur, :, 0:WINDOW] = jnp.where(has_past, vt_ext[prev, :, tm:tm + WINDOW],
                                             jnp.zeros((KV_W, WINDOW), BF16))
        vt_ext[cur, :, WINDOW:WINDOW + tm] = vt.astype(BF16)

    def proj_gate(lo, hi):
        zt = _nt_dot(wqvz_ref[ATTN_W + KV_W + lo:ATTN_W + KV_W + hi, :], h)
        gate_scr[cur, lo:hi, :] = _silu(zt)

    q_raw = {}

    def proj_q(lo, hi):
        q_raw[lo] = _nt_dot(wqvz_ref[lo:hi, :], h)

    def finish_q():
        qgain = jnp.concatenate([qg_ref[...]] * reps, axis=1)
        for lo, qt in q_raw.items():
            for r0 in range(0, qt.shape[0], HEAD_DIM):
                qt_scr[cur, lo + r0:lo + r0 + HEAD_DIM, :] = (
                    norm_rope(qt[r0:r0 + HEAD_DIM, :], qgain) * (LOG2_E * HEAD_DIM ** -0.5)).astype(BF16)
        q_raw.clear()

    pieces = [proj_kv,
              lambda: proj_gate(0, ATTN_W // 2),
              lambda: proj_gate(ATTN_W // 2, ATTN_W),
              lambda: proj_q(0, ATTN_W // 2)]
    last_piece = lambda: proj_q(ATTN_W // 2, ATTN_W)
    if not do_proj:
        pieces, last_piece = [], lambda: None
    if not do_attn:
        for piece in pieces:
            piece()
        finish_q()
        last_piece()
        finish_q()
        return

    ri = lax.broadcasted_iota(jnp.int32, (WINDOW, 2 * WINDOW), 0)
    qq = lax.broadcasted_iota(jnp.int32, (WINDOW, 2 * WINDOW), 1) % WINDOW
    from_prev = ri > qq
    keep_prev = from_prev.astype(BF16)
    keep_cur = 1.0 - keep_prev
    lane = lax.broadcasted_iota(jnp.int32, (1, 2 * WINDOW), 1)
    zeros = jnp.zeros((HEAD_DIM, 2 * WINDOW), BF16)
    ones = jnp.ones((16, 2 * WINDOW), BF16)

    for i in range(nblk):
        c0 = i * WINDOW
        qcols = slice(c0, c0 + WINDOW)
        pairs = [(kv, kv * GROUP + 2 * pr) for kv in range(N_KV_HEADS) for pr in range(GROUP // 2)]
        for j, (kv, ha) in enumerate(pairs):
            chunk, pos = divmod(kv, LANES // HEAD_DIM)
            kblk = kext[prev, c0:c0 + 2 * WINDOW, chunk * LANES:(chunk + 1) * LANES]
            qpair = jnp.concatenate([qt_scr[prev, ha * HEAD_DIM:(ha + 1) * HEAD_DIM, qcols],
                                     qt_scr[prev, (ha + 1) * HEAD_DIM:(ha + 2) * HEAD_DIM, qcols]], axis=1)
            rhs = jnp.concatenate([qpair, zeros] if pos == 0 else [zeros, qpair], axis=0)
            s = jnp.dot(kblk, rhs, preferred_element_type=F32)
            s_prev = s[0:WINDOW, :]
            if i == 0:
                s_prev = jnp.where(attn_t == 0, -jnp.inf, s_prev)
            s_scr[j] = jnp.where(from_prev, s_prev, s[WINDOW:2 * WINDOW, :])
        for c, piece in enumerate(pieces):
            if c * nblk // len(pieces) == i:
                piece()
        for j, (kv, ha) in enumerate(pairs):
            s = s_scr[j]
            sink = jnp.where(lane < WINDOW, sink_ref[ha], sink_ref[ha + 1]) * LOG2_E
            m = jnp.maximum(jnp.max(s, axis=0, keepdims=True), sink)
            p_scr[j] = jnp.exp2(s - m).astype(BF16)
            esink_scr[j:j + 1, :] = jnp.exp2(sink - m)
        for j, (kv, ha) in enumerate(pairs):
            vg = vt_ext[prev, kv * HEAD_DIM:(kv + 1) * HEAD_DIM, c0:c0 + 2 * WINDOW]
            p = p_scr[j]
            p_keys = jnp.concatenate([p * keep_prev, p * keep_cur], axis=0)
            oa = jnp.dot(jnp.concatenate([vg, ones], axis=0), p_keys, preferred_element_type=F32)
            denom = oa[HEAD_DIM:HEAD_DIM + 1, :] + esink_scr[j:j + 1, :]
            o = oa[0:HEAD_DIM, :] * (1.0 / denom)
            for hd, part in ((ha, o[:, 0:WINDOW]), (ha + 1, o[:, WINDOW:2 * WINDOW])):
                rows = slice(hd * HEAD_DIM, (hd + 1) * HEAD_DIM)
                yt_scr[rows, qcols] = (part * gate_scr[prev, rows, qcols]).astype(BF16)

    out_t = jnp.dot(wout_ref[...], yt_scr[...], preferred_element_type=F32)
    finish_q()
    last_piece()
    finish_q()
    o_ref[0] = xres_ref[0] + out_t.T


def _odd_prompt(x, ng, w_in, w_out, small, cos_t, sin_t, sinks, s_olo, s_ohi, s_gate, s_x1, *, tm):
    B, T, D = x.shape
    assert T % tm == 0 and tm % WINDOW == 0 and w_in.shape == (D, 2 * ATTN_W + 2 * KV_W) and w_out.shape == (ATTN_W, D)
    nt = T // tm
    half = HEAD_DIM // 2
    const = lambda *shape: pl.BlockSpec(shape, lambda s: (0,) * len(shape))
    n_tiles = B * nt
    proj_tile = lambda s: jnp.minimum(s, n_tiles - 1)
    attn_tile = lambda s: jnp.maximum(s - 1, 0)
    return pl.pallas_call(
        functools.partial(_odd_prompt_kernel, tm=tm, nt=nt, n_steps=n_tiles + 1),
        grid=(n_tiles + 1,),
        in_specs=[
            pl.BlockSpec((1, tm, D), lambda s: (proj_tile(s) // nt, proj_tile(s) % nt, 0)),
            pl.BlockSpec((1, tm, D), lambda s: (attn_tile(s) // nt, attn_tile(s) % nt, 0)),
            const(*ng.shape),
            const(*w_in.shape),
            const(*w_out.shape),
            const(*small.shape),
            pl.BlockSpec((half, tm), lambda s: (0, proj_tile(s) % nt)),
            pl.BlockSpec((half, tm), lambda s: (0, proj_tile(s) % nt)),
            pl.BlockSpec(memory_space=pltpu.SMEM),
            const(*s_olo.shape),
            const(*s_ohi.shape),
            const(*s_gate.shape),
            const(*s_x1.shape),
        ],
        out_specs=[
            pl.BlockSpec((1, tm, D), lambda s: (attn_tile(s) // nt, attn_tile(s) % nt, 0)),
            pl.BlockSpec((1, KV_W, WINDOW), lambda s: (proj_tile(s) // nt, 0, 0)),
            pl.BlockSpec((1, KV_W, WINDOW), lambda s: (proj_tile(s) // nt, 0, 0)),
            const(s_x1.size // LANES, LANES),
        ],
        out_shape=[
            jax.ShapeDtypeStruct((B, T, D), F32),
            jax.ShapeDtypeStruct((B, KV_W, WINDOW), F32),
            jax.ShapeDtypeStruct((B, KV_W, WINDOW), F32),
            jax.ShapeDtypeStruct((s_x1.size // LANES, LANES), F32),
        ],
        scratch_shapes=[
            pltpu.VMEM((KV_W, D), BF16),
            pltpu.VMEM((2 * ATTN_W + KV_W, D), BF16),
            pltpu.VMEM((D, ATTN_W), BF16),
            pltpu.VMEM((2, ATTN_W, tm), BF16),
            pltpu.VMEM((2, WINDOW + tm, KV_W), BF16),
            pltpu.VMEM((2, KV_W, WINDOW + tm), BF16),
            pltpu.VMEM((2, ATTN_W, tm), F32),
            pltpu.VMEM((ATTN_W, tm), BF16),
            pltpu.VMEM((N_HEADS // 2, WINDOW, 2 * WINDOW), F32),
            pltpu.VMEM((N_HEADS // 2, WINDOW, 2 * WINDOW), BF16),
            pltpu.VMEM((N_HEADS // 2, 2 * WINDOW), F32),
        ],
        compiler_params=pltpu.CompilerParams(
            dimension_semantics=("arbitrary",), vmem_limit_bytes=VMEM_LIMIT_BYTES),
        name="odd_prompt",
    )(x, x, ng, w_in, w_out, small, cos_t, sin_t, sinks, s_olo, s_ohi, s_gate, s_x1)


def _dup_heads(x):
    lane = lax.broadcasted_iota(jnp.int32, x.shape, 1)
    rolled = pltpu.roll(x, HEAD_DIM, 1)
    first = lane < HEAD_DIM
    return jnp.where(first, x, rolled), jnp.where(first, rolled, x)


def _sample_dense_kernel(x_ref, sp_ref, sc_ref, ng_ref, win0_ref, pw_ref, ps_ref, cw_ref, wout0_ref,
                         win1_ref, small_ref,
                         x1_ref, pool_ref, conv_ref, qh_ref, kt_ref, vt_ref, gate_ref):
    w_pool = ps_ref.shape[1]
    w_conv = cw_ref.shape[1]
    g0_ref, g1_ref = ng_ref.at[0:1], ng_ref.at[1:2]
    qg_ref, kgt_ref = small_ref.at[_QG_ROW:_QG_ROW + 1], small_ref.at[_KG_T:_KG_T + HEAD_DIM]
    cos_ref, sin_ref = small_ref.at[_COS_ROW:_COS_ROW + 1], small_ref.at[_SIN_ROW:_SIN_ROW + 1]
    cost_ref, sint_ref = small_ref.at[_COS_T:_COS_T + HEAD_DIM // 2], small_ref.at[_SIN_T:_SIN_T + HEAD_DIM // 2]

    n = x1_ref.shape[0]
    chunks = x1_ref.shape[1] // LANES
    x = jnp.concatenate([x_ref[pl.ds(c, n, stride=chunks), :] for c in range(chunks)], axis=1)
    h = _rms(x, g0_ref[...]).astype(win0_ref.dtype)

    def proj(c, width):
        return jnp.dot(h, win0_ref[:, c:c + width], preferred_element_type=F32)

    u = proj(0, w_pool)
    z_a = proj(w_pool, w_pool)
    ys = []
    for g, w in enumerate(POOL_WINDOWS):
        cols = slice(g * POOL_GROUP, (g + 1) * POOL_GROUP)
        wsum = u[:, cols]
        for r in range(POOL_BUF - (w - 1), POOL_BUF):
            wsum = wsum + sp_ref[r, :, cols]
        cnt = float(min(PAST_LEN + 1, w))
        d = wsum * (1.0 / cnt) - u[:, cols]
        ya = jnp.dot(d.astype(pw_ref.dtype), pw_ref[g], preferred_element_type=F32)
        ys.append((ya * ps_ref[:, cols] * _silu(z_a[:, cols])).astype(wout0_ref.dtype))
    for r in range(POOL_BUF - 1):
        pool_ref[r] = sp_ref[r + 1]
    pool_ref[POOL_BUF - 1] = u

    b_gate = proj(2 * w_pool, w_conv)
    c_gate = proj(2 * w_pool + w_conv, w_conv)
    v_in = proj(2 * w_pool + 2 * w_conv, w_conv)
    v = c_gate * v_in
    cchunks = w_conv // LANES
    cstride = cchunks * CONV_BUF

    def conv_row(r):
        return jnp.concatenate(
            [sc_ref[pl.ds(j * CONV_BUF + r, n, stride=cstride), :] for j in range(cchunks)], axis=1)

    def put_conv_row(r, val):
        for j in range(cchunks):
            conv_ref[pl.ds(j * CONV_BUF + r, n, stride=cstride), :] = val[:, j * LANES:(j + 1) * LANES]

    conv = v * cw_ref[CONV_WIDTH - 1:CONV_WIDTH, :]
    for k in range(CONV_BUF):
        conv = conv + conv_row(k) * cw_ref[k:k + 1, :]
    for r in range(CONV_BUF - 1):
        put_conv_row(r, conv_row(r + 1))
    put_conv_row(CONV_BUF - 1, v)
    z_b = proj(2 * w_pool + 3 * w_conv, w_conv)
    ys.append((b_gate * conv * _silu(z_b)).astype(wout0_ref.dtype))
    y = jnp.concatenate(ys, axis=1)
    x1 = x + jnp.dot(y, wout0_ref[...], preferred_element_type=F32)
    x1_ref[...] = x1

    n = x.shape[0]
    half = HEAD_DIM // 2
    h1 = _rms(x1, g1_ref[...])

    def proj1(lo, width):
        return jnp.dot(h1, win1_ref[:, lo:lo + width], preferred_element_type=F32)

    q = proj1(0, ATTN_W)
    for c in range(ATTN_W // LANES):
        qc = _head_norm_rope(q[:, c * LANES:(c + 1) * LANES], qg_ref[...], cos_ref[...], sin_ref[...])
        d0, d1 = _dup_heads(qc * (HEAD_DIM ** -0.5))
        qh_ref[2 * c * n:(2 * c + 1) * n, :] = d0
        qh_ref[(2 * c + 1) * n:(2 * c + 2) * n, :] = d1
    kt = proj1(ATTN_W, KV_W).T
    cos_t = cost_ref[...]
    sin_t = sint_ref[...]
    for kv in range(N_KV_HEADS):
        kh = kt[kv * HEAD_DIM:(kv + 1) * HEAD_DIM, :]
        ms = jnp.sum(kh * kh, axis=0, keepdims=True) * (1.0 / HEAD_DIM)
        kn = kh * lax.rsqrt(ms + RMS_EPS) * kgt_ref[...]
        x1h, x2h = kn[0:half, :], kn[half:HEAD_DIM, :]
        kt_ref[kv * HEAD_DIM:(kv + 1) * HEAD_DIM, :] = jnp.concatenate(
            [x1h * cos_t - x2h * sin_t, x2h * cos_t + x1h * sin_t], axis=0)
    vt_ref[...] = proj1(ATTN_W + KV_W, KV_W).T
    gate_ref[...] = _silu(proj1(ATTN_W + 2 * KV_W, ATTN_W))


def _sample_dense(x, sp, sc, ng, win0, pw, ps, cw, wout0, win1, small):
    D = win0.shape[0]
    n = x.shape[0] * LANES // D
    assert n == LANES
    vmem = pl.BlockSpec(memory_space=pltpu.VMEM)
    return pl.pallas_call(
        _sample_dense_kernel,
        in_specs=[vmem] * 11,
        out_specs=[vmem] * 7,
        out_shape=[
            jax.ShapeDtypeStruct((n, D), F32),
            jax.ShapeDtypeStruct(sp.shape, F32),
            jax.ShapeDtypeStruct(sc.shape, F32),
            jax.ShapeDtypeStruct((N_HEADS * n, LANES), F32),
            jax.ShapeDtypeStruct((KV_W, n), F32),
            jax.ShapeDtypeStruct((KV_W, n), F32),
            jax.ShapeDtypeStruct((n, ATTN_W), F32),
        ],
        compiler_params=pltpu.CompilerParams(vmem_limit_bytes=VMEM_LIMIT_BYTES),
        name="sample_dense",
    )(x, sp, sc, ng, win0, pw, ps, cw, wout0, win1, small)


def _sample_out(olo_ref, ohi_ref, gate_ref, x1_ref, wout_ref, y_ref):
    n = x1_ref.shape[0]
    blocks = LANES // HEAD_DIM
    lane_n = lax.broadcasted_iota(jnp.int32, (n, LANES), 1)
    chunks = []
    for c in range(ATTN_W // LANES):
        parts = []
        for hd in (2 * c, 2 * c + 1):
            kv = hd // GROUP
            slab = (olo_ref, ohi_ref)[kv // blocks]
            a = slab[pl.ds(hd, n, stride=N_HEADS), :]
            parts.append(a if kv % blocks == hd % blocks else pltpu.roll(a, HEAD_DIM, 1))
        chunks.append(jnp.where(lane_n < HEAD_DIM, parts[0], parts[1]))
    y = jnp.concatenate(chunks, axis=1) * gate_ref[...]
    out = x1_ref[...] + jnp.dot(y, wout_ref[...], preferred_element_type=F32)
    chunks = out.shape[1] // LANES
    for c in range(chunks):
        y_ref[pl.ds(c, n, stride=chunks), :] = out[:, c * LANES:(c + 1) * LANES]


def _rope_tables(pos):
    half = HEAD_DIM // 2
    inv = ROPE_THETA ** (-jnp.arange(half, dtype=F32) / half)
    ang = pos.astype(F32)[:, None] * inv[None, :]
    return jnp.cos(ang), jnp.sin(ang)


def kernel(x_prompt, x_sample, state_pool, state_conv, cache_k, cache_v, norm_g, w_in_even, pool_w, pool_scale,
           conv_w, w_out_even, w_in_odd, q_norm_g, k_norm_g, attn_sinks, w_out_odd):
    B, T, D = x_prompt.shape
    n_s, t_s, _ = x_sample.shape
    assert norm_g.shape[0] == 2 and w_in_even.shape[0] == 1 and w_in_odd.shape[0] == 1
    assert t_s == 1 and cache_k.shape[2] == WINDOW and T >= WINDOW
    assert cache_k.shape[3] * cache_k.shape[4] == KV_W and pool_w.shape[1:] == (len(POOL_WINDOWS), POOL_GROUP, POOL_GROUP)

    win0 = w_in_even[0]
    wout0 = w_out_even[0]
    win1 = w_in_odd[0]
    wout1 = w_out_odd[0]
    pw = pool_w[0]
    ps = pool_scale[0][None, :]
    cw = conv_w[0]
    reps = LANES // HEAD_DIM
    sinks = attn_sinks[0]

    cos_p, sin_p = _rope_tables(jnp.arange(T))
    cos_pt, sin_pt = cos_p.T, sin_p.T
    cos_1, sin_1 = _rope_tables(PAST_LEN + jnp.arange(t_s))
    small = jnp.concatenate([
        jnp.broadcast_to(q_norm_g[0][:, None], (HEAD_DIM, LANES)),
        jnp.broadcast_to(k_norm_g[0][:, None], (HEAD_DIM, LANES)),
        jnp.broadcast_to(cos_1.T, (HEAD_DIM // 2, LANES)),
        jnp.broadcast_to(sin_1.T, (HEAD_DIM // 2, LANES)),
        jnp.tile(q_norm_g[0], reps)[None, :],
        jnp.tile(jnp.concatenate([cos_1, cos_1], axis=1), (1, reps)),
        jnp.tile(jnp.concatenate([-sin_1, sin_1], axis=1), (1, reps)),
        jnp.zeros((_SMALL_ROWS - _SIN_ROW - 1, LANES), F32)], axis=0)

    w_conv = state_conv.shape[-1]
    sp = jnp.transpose(state_pool[0], (1, 0, 2))
    sc = jnp.transpose(state_conv[0].reshape(n_s, CONV_BUF, w_conv // LANES, LANES), (0, 2, 1, 3)).reshape(-1, LANES)
    x1s, pool_s, conv_s, qh_s, kt_s, vt_s, gate_s = _sample_dense(
        x_sample.reshape(n_s * D // LANES, LANES), sp, sc, norm_g, win0, pw, ps, cw, wout0, win1, small)
    pool_s = jnp.transpose(pool_s, (1, 0, 2))
    conv_s = jnp.transpose(conv_s.reshape(n_s, w_conv // LANES, CONV_BUF, LANES), (0, 2, 1, 3)).reshape(
        n_s, CONV_BUF, w_conv)

    ckt = jnp.transpose(cache_k[0], (0, 2, 3, 1)).reshape(n_s, KV_W, WINDOW)
    cvt = jnp.transpose(cache_v[0], (0, 2, 3, 1)).reshape(n_s, KV_W, WINDOW)
    x1p, pool_p, conv_p, nkt_s, nvt_s, olo_s, ohi_s = _even_prompt(
        x_prompt, norm_g, win0, pw, ps, cw, wout0, qh_s, kt_s, vt_s, ckt, cvt, sinks[:, None], tm=1024)
    y_p, k_p, v_p, y_s = _odd_prompt(x1p, norm_g, win1, wout1, small, cos_pt, sin_pt, sinks,
                                     olo_s, ohi_s, gate_s, x1s, tm=512)

    def window_major(a):
        return jnp.transpose(a.reshape(-1, N_KV_HEADS, HEAD_DIM, WINDOW), (0, 3, 1, 2))[None]

    return (y_p, y_s.reshape(n_s, 1, D), pool_p[None], pool_s[None], conv_p[None], conv_s[None],
            window_major(k_p), window_major(v_p), window_major(nkt_s), window_major(nvt_s))
```

```python
import functools

import jax
import jax.numpy as jnp
from jax import lax
from jax.experimental import pallas as pl
from jax.experimental.pallas import tpu as pltpu

F32 = jnp.float32
BF16 = jnp.bfloat16

POOL_WINDOWS = (2, 4, 8, 16)
POOL_GROUP = 128
POOL_BUF = max(POOL_WINDOWS) - 1
CONV_WIDTH = 3
CONV_BUF = CONV_WIDTH - 1
N_HEADS = 16
HEAD_DIM = 64
N_KV_HEADS = 4
GROUP = N_HEADS // N_KV_HEADS
WINDOW = 128
ROPE_THETA = 10000.0
RMS_EPS = 1e-6
PAST_LEN = 16384
ATTN_W = N_HEADS * HEAD_DIM
KV_W = N_KV_HEADS * HEAD_DIM

LANES = 128
POOL_HALO = 16
CONV_HALO = 8
VMEM_LIMIT_BYTES = 60 * 1024 * 1024
LOG2_E = 1.4426950408889634
_QG_T, _KG_T, _COS_T, _SIN_T = 0, HEAD_DIM, 2 * HEAD_DIM, 2 * HEAD_DIM + HEAD_DIM // 2
_QG_ROW, _COS_ROW, _SIN_ROW, _SMALL_ROWS = 3 * HEAD_DIM, 3 * HEAD_DIM + 1, 3 * HEAD_DIM + 2, 3 * HEAD_DIM + 8


def _rms(x, g):
    ms = jnp.mean(x * x, axis=-1, keepdims=True)
    return x * lax.rsqrt(ms + RMS_EPS) * g


def _silu(z):
    return z * jax.nn.sigmoid(z)


def _head_norm_rope(x, gain, cos, sin):
    lane = lax.broadcasted_iota(jnp.int32, x.shape, 1)
    first = lane < HEAD_DIM
    x2 = x * x
    ss0 = jnp.sum(jnp.where(first, x2, 0.0), axis=-1, keepdims=True)
    ss1 = jnp.sum(jnp.where(first, 0.0, x2), axis=-1, keepdims=True)
    r = jnp.where(first, lax.rsqrt(ss0 * (1.0 / HEAD_DIM) + RMS_EPS), lax.rsqrt(ss1 * (1.0 / HEAD_DIM) + RMS_EPS))
    xn = x * r * gain
    half = HEAD_DIM // 2
    swapped = jnp.where((lane % HEAD_DIM) < half, pltpu.roll(xn, LANES - half, 1), pltpu.roll(xn, half, 1))
    return xn * cos + swapped * sin


def _nt_dot(a, b):
    return lax.dot_general(a, b, (((1,), (1,)), ((), ())), preferred_element_type=F32)


def _col_table(row):
    m = row.shape[1]
    diag = lax.broadcasted_iota(jnp.int32, (m, m), 0) == lax.broadcasted_iota(jnp.int32, (m, m), 1)
    col = jnp.sum(jnp.where(diag, jnp.broadcast_to(row, (m, m)), 0.0), axis=1, keepdims=True)
    return jnp.broadcast_to(col, (m, LANES))


def _tiled_row(table):
    m = table.shape[0]
    pick = lax.broadcasted_iota(jnp.int32, table.shape, 0) == lax.broadcasted_iota(jnp.int32, table.shape, 1) % m
    return jnp.sum(jnp.where(pick, table, 0.0), axis=0, keepdims=True)


def _conv_tap(cw_ref, k, width):
    chunks = width // LANES
    return jnp.concatenate([cw_ref[k * chunks + j:k * chunks + j + 1, :] for j in range(chunks)], axis=1)


class _SampleAttention:
    def __init__(self, qh_ref, knt_ref, vnt_ref, ck_ref, cv_ref, sink_ref, nk_ref, nv_ref, olo_ref, ohi_ref, *, step):
        self.refs = (qh_ref, knt_ref, vnt_ref, ck_ref, cv_ref, sink_ref, nk_ref, nv_ref, olo_ref, ohi_ref)
        self.bb = ck_ref.shape[0]
        self.first = step * self.bb

    def roll(self):
        _, knt_ref, vnt_ref, ck_ref, cv_ref, _, nk_ref, nv_ref, _, _ = self.refs
        win = ck_ref.shape[2]
        newest = lax.broadcasted_iota(jnp.int32, (KV_W, win), 1) == win - 1
        for i in range(self.bb):
            b = self.first + i
            nk_ref[i] = jnp.where(newest, pltpu.roll(knt_ref[...], win - 1 - b, 1), pltpu.roll(ck_ref[i], win - 1, 1))
            nv_ref[i] = jnp.where(newest, pltpu.roll(vnt_ref[...], win - 1 - b, 1), pltpu.roll(cv_ref[i], win - 1, 1))

    def scores(self):
        qh_ref, nk_ref = self.refs[0], self.refs[6]
        n = qh_ref.shape[0] // N_HEADS
        blocks = LANES // HEAD_DIM
        hrow = lax.broadcasted_iota(jnp.int32, (N_HEADS, LANES), 0) // GROUP
        hcol = lax.broadcasted_iota(jnp.int32, (N_HEADS, LANES), 1) // HEAD_DIM
        self.s = []
        for i in range(self.bb):
            qd = qh_ref[pl.ds(self.first + i, N_HEADS, stride=n), :]
            qx = jnp.concatenate(
                [jnp.where(hcol + blocks * c == hrow, qd, 0.0) for c in range(KV_W // LANES)], axis=1)
            self.s.append(jnp.dot(qx.astype(BF16), nk_ref[i].astype(BF16), preferred_element_type=F32))

    def softmax(self):
        sink = _col_table(self.refs[5][...])[:, 0:1]
        self.p = []
        for s in self.s:
            m = jnp.maximum(jnp.max(s, axis=-1, keepdims=True), sink)
            p = jnp.exp(s - m)
            denom = jnp.sum(p, axis=-1, keepdims=True) + jnp.exp(sink - m)
            self.p.append((p * (1.0 / denom)).astype(BF16))

    def outputs(self):
        nv_ref, olo_ref, ohi_ref = self.refs[7:10]
        for i in range(self.bb):
            o = _nt_dot(self.p[i], nv_ref[i].astype(BF16))
            rows = pl.ds(pl.multiple_of((self.first + i) * N_HEADS, N_HEADS), N_HEADS)
            olo_ref[rows, :] = o[:, 0:LANES]
            ohi_ref[rows, :] = o[:, LANES:2 * LANES]


def _even_prompt_kernel(x_ref, g_ref, win_ref, pw_ref, ps_ref, cw_ref, wout_ref,
                        qh_ref, knt_ref, vnt_ref, ck_ref, cv_ref, sink_ref,
                        o_ref, pool_ref, conv_ref, nk_ref, nv_ref, olo_ref, ohi_ref,
                        uext, vext, sa, sb, y_scr, *, tm):
    t = pl.program_id(1)
    w_pool = uext.shape[1]
    w_conv = vext.shape[1]
    base = 2 * POOL_HALO
    side = _SampleAttention(qh_ref, knt_ref, vnt_ref, ck_ref, cv_ref, sink_ref, nk_ref, nv_ref, olo_ref, ohi_ref,
                            step=pl.program_id(0) * pl.num_programs(1) + t)

    @pl.when(t == 0)
    def _():
        uext[0:base, :] = jnp.zeros((base, w_pool), F32)
        vext[0:CONV_HALO, :] = jnp.zeros((CONV_HALO, w_conv), F32)
        sa[0:POOL_HALO, :] = jnp.zeros((POOL_HALO, POOL_GROUP), F32)
        sb[0:POOL_HALO, :] = jnp.zeros((POOL_HALO, POOL_GROUP), F32)

    x = x_ref[0]
    h = _rms(x, g_ref[0:1, :]).astype(win_ref.dtype)

    def proj(c, width):
        return jnp.dot(h, win_ref[:, c:c + width], preferred_element_type=F32)

    side.roll()
    u = proj(0, w_pool)
    uext[base:base + tm, :] = u
    pos = t * tm + lax.broadcasted_iota(jnp.int32, (tm, 1), 0)
    side.scores()
    z_a = proj(w_pool, w_pool)
    n_ext = POOL_HALO + tm

    for g, w in enumerate(POOL_WINDOWS):
        lo = g * POOL_GROUP
        cols = slice(lo, lo + POOL_GROUP)
        src = uext
        src_cols = cols
        step = 1
        bufs = (sa, sb)
        nbuf = 0
        while 2 * step < w:
            dst = bufs[nbuf % 2]
            dst[POOL_HALO:POOL_HALO + n_ext, :] = (src[POOL_HALO:POOL_HALO + n_ext, src_cols]
                                                   + src[POOL_HALO - step:POOL_HALO - step + n_ext, src_cols])
            src, src_cols = dst, slice(0, POOL_GROUP)
            step *= 2
            nbuf += 1
        wsum = src[base:base + tm, src_cols] + src[base - step:base - step + tm, src_cols]
        cnt = jnp.minimum(pos + 1, w).astype(F32)
        d = wsum * (1.0 / cnt) - u[:, cols]
        ya = jnp.dot(d.astype(pw_ref.dtype), pw_ref[g], preferred_element_type=F32)
        ya = ya * ps_ref[:, cols] * _silu(z_a[:, cols])
        y_scr[:, cols] = ya.astype(y_scr.dtype)

    b_gate = proj(2 * w_pool, w_conv)
    side.softmax()
    c_gate = proj(2 * w_pool + w_conv, w_conv)
    side.outputs()
    v_in = proj(2 * w_pool + 2 * w_conv, w_conv)
    v = c_gate * v_in
    vext[CONV_HALO:CONV_HALO + tm, :] = v
    conv = v * _conv_tap(cw_ref, CONV_WIDTH - 1, w_conv)
    for k in range(CONV_WIDTH - 1):
        shift = CONV_WIDTH - 1 - k
        conv = conv + vext[CONV_HALO - shift:CONV_HALO - shift + tm, :] * _conv_tap(cw_ref, k, w_conv)
    z_b = proj(2 * w_pool + 3 * w_conv, w_conv)
    y_b = b_gate * conv * _silu(z_b)
    y_scr[:, w_pool:w_pool + w_conv] = y_b.astype(y_scr.dtype)

    o_ref[0] = x + jnp.dot(y_scr[...], wout_ref[...], preferred_element_type=F32)
    pool_ref[0] = uext[base + tm - POOL_BUF:base + tm, :]
    conv_ref[0] = vext[CONV_HALO + tm - CONV_BUF:CONV_HALO + tm, :]
    uext[POOL_HALO:base, :] = uext[POOL_HALO + tm:base + tm, :]
    vext[0:CONV_HALO, :] = vext[tm:tm + CONV_HALO, :]


def _even_prompt(x, g, win, pw, ps, cw, wout, qh, knt, vnt, ckt, cvt, sink_col, *, tm):
    B, T, D = x.shape
    w_pool = ps.shape[-1]
    w_conv = (win.shape[1] - 2 * w_pool) // 4
    assert cw.shape == (CONV_WIDTH * w_conv // LANES, LANES)
    assert T % tm == 0 and tm % 16 == 0 and tm >= POOL_HALO
    nt = T // tm
    n, kvw, win_len = ckt.shape
    assert n % (B * nt) == 0 and kvw == KV_W == 2 * LANES and win_len == LANES and n == LANES
    bb = n // (B * nt)
    const = lambda *shape: pl.BlockSpec(shape, lambda b, t: (0,) * len(shape))
    cache_block = pl.BlockSpec((bb, kvw, win_len), lambda b, t: (b * nt + t, 0, 0))
    return pl.pallas_call(
        functools.partial(_even_prompt_kernel, tm=tm),
        grid=(B, nt),
        in_specs=[
            pl.BlockSpec((1, tm, D), lambda b, t: (b, t, 0)),
            const(*g.shape),
            const(*win.shape),
            const(*pw.shape),
            const(1, w_pool),
            const(*cw.shape),
            const(*wout.shape),
            const(*qh.shape),
            const(*knt.shape),
            const(*vnt.shape),
            cache_block,
            cache_block,
            const(*sink_col.shape),
        ],
        out_specs=[
            pl.BlockSpec((1, tm, D), lambda b, t: (b, t, 0)),
            pl.BlockSpec((1, POOL_BUF, w_pool), lambda b, t: (b, 0, 0)),
            pl.BlockSpec((1, CONV_BUF, w_conv), lambda b, t: (b, 0, 0)),
            cache_block,
            cache_block,
            const(n * N_HEADS, LANES),
            const(n * N_HEADS, LANES),
        ],
        out_shape=[
            jax.ShapeDtypeStruct((B, T, D), F32),
            jax.ShapeDtypeStruct((B, POOL_BUF, w_pool), F32),
            jax.ShapeDtypeStruct((B, CONV_BUF, w_conv), F32),
            jax.ShapeDtypeStruct((n, kvw, win_len), F32),
            jax.ShapeDtypeStruct((n, kvw, win_len), F32),
            jax.ShapeDtypeStruct((n * N_HEADS, LANES), F32),
            jax.ShapeDtypeStruct((n * N_HEADS, LANES), F32),
        ],
        scratch_shapes=[
            pltpu.VMEM((2 * POOL_HALO + tm, w_pool), F32),
            pltpu.VMEM((CONV_HALO + tm, w_conv), F32),
            pltpu.VMEM((2 * POOL_HALO + tm, POOL_GROUP), F32),
            pltpu.VMEM((2 * POOL_HALO + tm, POOL_GROUP), F32),
            pltpu.VMEM((tm, w_pool + w_conv), wout.dtype),
        ],
        compiler_params=pltpu.CompilerParams(
            dimension_semantics=("arbitrary", "arbitrary"), vmem_limit_bytes=VMEM_LIMIT_BYTES),
        name="even_prompt",
    )(x, g, win, pw, ps, cw, wout, qh, knt, vnt, ckt, cvt, sink_col)


def _odd_prompt_kernel(*refs, tm, nt, n_steps):
    (x_ref, xres_ref, ng_ref, win_ref, wo_ref, small_ref, ropeb_ref, ropeo_ref, sink_ref,
     olo_ref, ohi_ref, sgate_ref, sx1_ref,
     o_ref, knew_ref, vnew_ref, sy_ref, wkt_scr, wqvz_scr, wout_scr, qt_scr, kext, vt_ext, gate_scr, *rest) = refs
    g_ref = ng_ref.at[1:2]
    qg_ref, kg_ref = small_ref.at[_QG_T:_QG_T + HEAD_DIM], small_ref.at[_KG_T:_KG_T + HEAD_DIM]

    step = pl.program_id(0)
    last = n_steps - 1

    @pl.when(step == 0)
    def _():
        _sample_out(olo_ref, ohi_ref, sgate_ref, sx1_ref, wo_ref, sy_ref)
        kext[...] = jnp.zeros(kext.shape, BF16)
        vt_ext[...] = jnp.zeros(vt_ext.shape, BF16)
        blk = 2 * LANES
        for src, dst, width in ((0, 0, ATTN_W), (ATTN_W + KV_W, ATTN_W, KV_W + ATTN_W)):
            for c in range(0, width, blk):
                wqvz_scr[dst + c:dst + c + blk, :] = win_ref[:, src + c:src + c + blk].T.astype(BF16)
        wkt_scr[...] = win_ref[:, ATTN_W:ATTN_W + KV_W].T.astype(BF16)
        for c in range(0, wo_ref.shape[1], blk):
            wout_scr[c:c + blk, :] = wo_ref[:, c:c + blk].T.astype(BF16)

    def run(cur, **halves):
        _odd_prompt_step(x_ref, xres_ref, g_ref, wkt_scr, wqvz_scr, qg_ref, kg_ref, ropeb_ref, ropeo_ref, sink_ref,
                         wout_scr, o_ref, knew_ref, vnew_ref, qt_scr, kext, vt_ext, gate_scr, *rest,
                         tm=tm, nt=nt, cur=cur, prev=1 - cur, **halves)

    pl.when(step == 0)(lambda: run(0, do_attn=False))
    for parity in range(2):
        pl.when((step % 2 == parity) & (step > 0) & (step < last))(lambda parity=parity: run(parity))
    pl.when(step == last)(lambda: run((n_steps - 1) % 2, do_proj=False))


def _odd_prompt_step(x_ref, xres_ref, g_ref, wkt_ref, wqvz_ref, qg_ref, kg_ref, ropeb_ref, ropeo_ref, sink_ref,
                     wout_ref, o_ref, knew_ref, vnew_ref,
                     qt_scr, kext, vt_ext, gate_scr, yt_scr, s_scr, p_scr, esink_scr, *, tm, nt, cur, prev,
                     do_proj=True, do_attn=True):
    step = pl.program_id(0)
    proj_t = step % nt
    attn_t = jnp.maximum(step - 1, 0) % nt
    nblk = tm // WINDOW
    half = HEAD_DIM // 2

    h = _rms(x_ref[0], g_ref[...]).astype(BF16) if do_proj else None
    reps = tm // LANES
    if do_proj:
        cos_b = jnp.concatenate([ropeb_ref[0, 0:half, :]] * reps, axis=1)
        sin_b = jnp.concatenate([ropeb_ref[0, half:HEAD_DIM, :]] * reps, axis=1)
        cos_o, sin_o = ropeo_ref[0:half, :], ropeo_ref[half:HEAD_DIM, :]
        cos_t = cos_b * cos_o - sin_b * sin_o
        sin_t = sin_b * cos_o + cos_b * sin_o

    def norm_rope(xt, gain):
        ms = jnp.sum(xt * xt, axis=0, keepdims=True) * (1.0 / HEAD_DIM)
        xn = xt * lax.rsqrt(ms + RMS_EPS) * gain
        x1, x2 = xn[0:half, :], xn[half:HEAD_DIM, :]
        return jnp.concatenate([x1 * cos_t - x2 * sin_t, x2 * cos_t + x1 * sin_t], axis=0)

    has_past = proj_t > 0

    def proj_kv():
        kt = _nt_dot(wkt_ref[...], h)
        kgain = jnp.concatenate([kg_ref[...]] * reps, axis=1)
        kt = jnp.concatenate(
            [norm_rope(kt[kv * HEAD_DIM:(kv + 1) * HEAD_DIM, :], kgain) for kv in range(N_KV_HEADS)], axis=0)
        knew_ref[0] = kt[:, tm - WINDOW:tm]
        kext[cur, 0:WINDOW, :] = jnp.where(has_past, kext[prev, tm:tm + WINDOW, :], jnp.zeros((WINDOW, KV_W), BF16))
        kext[cur, WINDOW:WINDOW + tm, :] = kt.T.astype(BF16)
        vt = _nt_dot(wqvz_ref[ATTN_W:ATTN_W + KV_W, :], h)
        vnew_ref[0] = vt[:, tm - WINDOW:tm]
        vt_ext[cur, :, 0:WINDOW] = jnp.where(has_past, vt_ext[prev, :, tm:tm + WINDOW],
                                             jnp.zeros((KV_W, WINDOW), BF16))
        vt_ext[cur, :, WINDOW:WINDOW + tm] = vt.astype(BF16)

    def proj_gate(lo, hi):
        zt = _nt_dot(wqvz_ref[ATTN_W + KV_W + lo:ATTN_W + KV_W + hi, :], h)
        gate_scr[cur, lo:hi, :] = _silu(zt)

    q_raw = {}

    def proj_q(lo, hi):
        q_raw[lo] = _nt_dot(wqvz_ref[lo:hi, :], h)

    def finish_q():
        qgain = jnp.concatenate([qg_ref[...]] * reps, axis=1)
        for lo, qt in q_raw.items():
            for r0 in range(0, qt.shape[0], HEAD_DIM):
                qt_scr[cur, lo + r0:lo + r0 + HEAD_DIM, :] = (
                    norm_rope(qt[r0:r0 + HEAD_DIM, :], qgain) * (LOG2_E * HEAD_DIM ** -0.5)).astype(BF16)
        q_raw.clear()

    pieces = [proj_kv,
              lambda: proj_gate(0, ATTN_W // 2),
              lambda: proj_gate(ATTN_W // 2, ATTN_W),
              lambda: proj_q(0, ATTN_W // 2)]
    last_piece = lambda: proj_q(ATTN_W // 2, ATTN_W)
    if not do_proj:
        pieces, last_piece = [], lambda: None
    if not do_attn:
        for piece in pieces:
            piece()
        finish_q()
        last_piece()
        finish_q()
        return

    ri = lax.broadcasted_iota(jnp.int32, (WINDOW, 2 * WINDOW), 0)
    qq = lax.broadcasted_iota(jnp.int32, (WINDOW, 2 * WINDOW), 1) % WINDOW
    from_prev = ri > qq
    keep_prev = from_prev.astype(BF16)
    keep_cur = 1.0 - keep_prev
    lane = lax.broadcasted_iota(jnp.int32, (1, 2 * WINDOW), 1)
    zeros = jnp.zeros((HEAD_DIM, 2 * WINDOW), BF16)
    ones = jnp.ones((16, 2 * WINDOW), BF16)

    for i in range(nblk):
        c0 = i * WINDOW
        qcols = slice(c0, c0 + WINDOW)
        pairs = [(kv, kv * GROUP + 2 * pr) for kv in range(N_KV_HEADS) for pr in range(GROUP // 2)]
        for j, (kv, ha) in enumerate(pairs):
            chunk, pos = divmod(kv, LANES // HEAD_DIM)
            kblk = kext[prev, c0:c0 + 2 * WINDOW, chunk * LANES:(chunk + 1) * LANES]
            qpair = jnp.concatenate([qt_scr[prev, ha * HEAD_DIM:(ha + 1) * HEAD_DIM, qcols],
                                     qt_scr[prev, (ha + 1) * HEAD_DIM:(ha + 2) * HEAD_DIM, qcols]], axis=1)
            rhs = jnp.concatenate([qpair, zeros] if pos == 0 else [zeros, qpair], axis=0)
            s = jnp.dot(kblk, rhs, preferred_element_type=F32)
            s_prev = s[0:WINDOW, :]
            if i == 0:
                s_prev = jnp.where(attn_t == 0, -jnp.inf, s_prev)
            s_scr[j] = jnp.where(from_prev, s_prev, s[WINDOW:2 * WINDOW, :])
        for c, piece in enumerate(pieces):
            if c * nblk // len(pieces) == i:
                piece()
        for j, (kv, ha) in enumerate(pairs):
            s = s_scr[j]
            sink = jnp.where(lane < WINDOW, sink_ref[ha], sink_ref[ha + 1]) * LOG2_E
            m = jnp.maximum(jnp.max(s, axis=0, keepdims=True), sink)
            p_scr[j] = jnp.exp2(s - m).astype(BF16)
            esink_scr[j:j + 1, :] = jnp.exp2(sink - m)
        for j, (kv, ha) in enumerate(pairs):
            vg = vt_ext[prev, kv * HEAD_DIM:(kv + 1) * HEAD_DIM, c0:c0 + 2 * WINDOW]
            p = p_scr[j]
            p_keys = jnp.concatenate([p * keep_prev, p * keep_cur], axis=0)
            oa = jnp.dot(jnp.concatenate([vg, ones], axis=0), p_keys, preferred_element_type=F32)
            denom = oa[HEAD_DIM:HEAD_DIM + 1, :] + esink_scr[j:j + 1, :]
            o = oa[0:HEAD_DIM, :] * (1.0 / denom)
            for hd, part in ((ha, o[:, 0:WINDOW]), (ha + 1, o[:, WINDOW:2 * WINDOW])):
                rows = slice(hd * HEAD_DIM, (hd + 1) * HEAD_DIM)
                yt_scr[rows, qcols] = (part * gate_scr[prev, rows, qcols]).astype(BF16)

    out_t = jnp.dot(wout_ref[...], yt_scr[...], preferred_element_type=F32)
    finish_q()
    last_piece()
    finish_q()
    o_ref[0] = xres_ref[0] + out_t.T


def _odd_prompt(x, ng, w_in, w_out, small, cos_t, sin_t, sinks, s_olo, s_ohi, s_gate, s_x1, *, tm):
    B, T, D = x.shape
    assert T % tm == 0 and tm % WINDOW == 0 and w_in.shape == (D, 2 * ATTN_W + 2 * KV_W) and w_out.shape == (ATTN_W, D)
    nt = T // tm
    half = HEAD_DIM // 2
    const = lambda *shape: pl.BlockSpec(shape, lambda s: (0,) * len(shape))
    n_tiles = B * nt
    proj_tile = lambda s: jnp.minimum(s, n_tiles - 1)
    attn_tile = lambda s: jnp.maximum(s - 1, 0)
    return pl.pallas_call(
        functools.partial(_odd_prompt_kernel, tm=tm, nt=nt, n_steps=n_tiles + 1),
        grid=(n_tiles + 1,),
        in_specs=[
            pl.BlockSpec((1, tm, D), lambda s: (proj_tile(s) // nt, proj_tile(s) % nt, 0)),
            pl.BlockSpec((1, tm, D), lambda s: (attn_tile(s) // nt, attn_tile(s) % nt, 0)),
            const(*ng.shape),
            const(*w_in.shape),
            const(*w_out.shape),
            const(*small.shape),
            pl.BlockSpec((1, HEAD_DIM, LANES), lambda s: (proj_tile(s) % nt, 0, 0)),
            const(HEAD_DIM, tm),
            pl.BlockSpec(memory_space=pltpu.SMEM),
            const(*s_olo.shape),
            const(*s_ohi.shape),
            const(*s_gate.shape),
            const(*s_x1.shape),
        ],
        out_specs=[
            pl.BlockSpec((1, tm, D), lambda s: (attn_tile(s) // nt, attn_tile(s) % nt, 0)),
            pl.BlockSpec((1, KV_W, WINDOW), lambda s: (proj_tile(s) // nt, 0, 0)),
            pl.BlockSpec((1, KV_W, WINDOW), lambda s: (proj_tile(s) // nt, 0, 0)),
            const(s_x1.size // LANES, LANES),
        ],
        out_shape=[
            jax.ShapeDtypeStruct((B, T, D), F32),
            jax.ShapeDtypeStruct((B, KV_W, WINDOW), F32),
            jax.ShapeDtypeStruct((B, KV_W, WINDOW), F32),
            jax.ShapeDtypeStruct((s_x1.size // LANES, LANES), F32),
        ],
        scratch_shapes=[
            pltpu.VMEM((KV_W, D), BF16),
            pltpu.VMEM((2 * ATTN_W + KV_W, D), BF16),
            pltpu.VMEM((D, ATTN_W), BF16),
            pltpu.VMEM((2, ATTN_W, tm), BF16),
            pltpu.VMEM((2, WINDOW + tm, KV_W), BF16),
            pltpu.VMEM((2, KV_W, WINDOW + tm), BF16),
            pltpu.VMEM((2, ATTN_W, tm), F32),
            pltpu.VMEM((ATTN_W, tm), BF16),
            pltpu.VMEM((N_HEADS // 2, WINDOW, 2 * WINDOW), F32),
            pltpu.VMEM((N_HEADS // 2, WINDOW, 2 * WINDOW), BF16),
            pltpu.VMEM((N_HEADS // 2, 2 * WINDOW), F32),
        ],
        compiler_params=pltpu.CompilerParams(
            dimension_semantics=("arbitrary",), vmem_limit_bytes=VMEM_LIMIT_BYTES),
        name="odd_prompt",
    )(x, x, ng, w_in, w_out, small, cos_t, sin_t, sinks, s_olo, s_ohi, s_gate, s_x1)


def _dup_heads(x):
    lane = lax.broadcasted_iota(jnp.int32, x.shape, 1)
    rolled = pltpu.roll(x, HEAD_DIM, 1)
    first = lane < HEAD_DIM
    return jnp.where(first, x, rolled), jnp.where(first, rolled, x)


def _sample_dense_kernel(x_ref, sp_ref, sc_ref, ng_ref, win0_ref, pw_ref, ps_ref, cw_ref, wout0_ref,
                         win1_ref, qgain_ref, kgain_ref, cos1_ref, sin1_ref,
                         x1_ref, pool_ref, conv_ref, qh_ref, kt_ref, vt_ref, gate_ref, small_ref):
    w_pool = ps_ref.shape[1]
    w_conv = (win0_ref.shape[1] - 2 * w_pool) // 4
    half = HEAD_DIM // 2
    qg_t, cos_t1, sin_t1 = _col_table(qgain_ref[...]), _col_table(cos1_ref[...]), _col_table(sin1_ref[...])
    small_ref[_QG_T:_QG_T + HEAD_DIM, :] = qg_t
    small_ref[_KG_T:_KG_T + HEAD_DIM, :] = _col_table(kgain_ref[...])
    small_ref[_COS_T:_COS_T + half, :] = cos_t1
    small_ref[_SIN_T:_SIN_T + half, :] = sin_t1
    second_half = lax.broadcasted_iota(jnp.int32, (1, LANES), 1) % HEAD_DIM >= half
    small_ref[_QG_ROW:_SMALL_ROWS, :] = jnp.concatenate(
        [_tiled_row(qg_t), _tiled_row(cos_t1), jnp.where(second_half, 1.0, -1.0) * _tiled_row(sin_t1),
         jnp.zeros((_SMALL_ROWS - _SIN_ROW - 1, LANES), F32)], axis=0)
    g0_ref, g1_ref = ng_ref.at[0:1], ng_ref.at[1:2]
    qg_ref, kgt_ref = small_ref.at[_QG_ROW:_QG_ROW + 1], small_ref.at[_KG_T:_KG_T + HEAD_DIM]
    cos_ref, sin_ref = small_ref.at[_COS_ROW:_COS_ROW + 1], small_ref.at[_SIN_ROW:_SIN_ROW + 1]
    cost_ref, sint_ref = small_ref.at[_COS_T:_COS_T + HEAD_DIM // 2], small_ref.at[_SIN_T:_SIN_T + HEAD_DIM // 2]

    n = x1_ref.shape[0]
    chunks = x1_ref.shape[1] // LANES
    x = jnp.concatenate([x_ref[pl.ds(c, n, stride=chunks), :] for c in range(chunks)], axis=1)
    h = _rms(x, g0_ref[...]).astype(win0_ref.dtype)

    def proj(c, width):
        return jnp.dot(h, win0_ref[:, c:c + width], preferred_element_type=F32)

    u = proj(0, w_pool)
    z_a = proj(w_pool, w_pool)
    ys = []
    for g, w in enumerate(POOL_WINDOWS):
        cols = slice(g * POOL_GROUP, (g + 1) * POOL_GROUP)
        wsum = u[:, cols]
        for r in range(POOL_BUF - (w - 1), POOL_BUF):
            wsum = wsum + sp_ref[r, :, cols]
        cnt = float(min(PAST_LEN + 1, w))
        d = wsum * (1.0 / cnt) - u[:, cols]
        ya = jnp.dot(d.astype(pw_ref.dtype), pw_ref[g], preferred_element_type=F32)
        ys.append((ya * ps_ref[:, cols] * _silu(z_a[:, cols])).astype(wout0_ref.dtype))
    for r in range(POOL_BUF - 1):
        pool_ref[r] = sp_ref[r + 1]
    pool_ref[POOL_BUF - 1] = u

    b_gate = proj(2 * w_pool, w_conv)
    c_gate = proj(2 * w_pool + w_conv, w_conv)
    v_in = proj(2 * w_pool + 2 * w_conv, w_conv)
    v = c_gate * v_in
    cchunks = w_conv // LANES
    cstride = cchunks * CONV_BUF

    def conv_row(r):
        return jnp.concatenate(
            [sc_ref[pl.ds(j * CONV_BUF + r, n, stride=cstride), :] for j in range(cchunks)], axis=1)

    def put_conv_row(r, val):
        for j in range(cchunks):
            conv_ref[pl.ds(j * CONV_BUF + r, n, stride=cstride), :] = val[:, j * LANES:(j + 1) * LANES]

    conv = v * _conv_tap(cw_ref, CONV_WIDTH - 1, w_conv)
    for k in range(CONV_BUF):
        conv = conv + conv_row(k) * _conv_tap(cw_ref, k, w_conv)
    for r in range(CONV_BUF - 1):
        put_conv_row(r, conv_row(r + 1))
    put_conv_row(CONV_BUF - 1, v)
    z_b = proj(2 * w_pool + 3 * w_conv, w_conv)
    ys.append((b_gate * conv * _silu(z_b)).astype(wout0_ref.dtype))
    y = jnp.concatenate(ys, axis=1)
    x1 = x + jnp.dot(y, wout0_ref[...], preferred_element_type=F32)
    x1_ref[...] = x1

    n = x.shape[0]
    half = HEAD_DIM // 2
    h1 = _rms(x1, g1_ref[...])

    def proj1(lo, width):
        return jnp.dot(h1, win1_ref[:, lo:lo + width], preferred_element_type=F32)

    q = proj1(0, ATTN_W)
    for c in range(ATTN_W // LANES):
        qc = _head_norm_rope(q[:, c * LANES:(c + 1) * LANES], qg_ref[...], cos_ref[...], sin_ref[...])
        d0, d1 = _dup_heads(qc * (HEAD_DIM ** -0.5))
        qh_ref[2 * c * n:(2 * c + 1) * n, :] = d0
        qh_ref[(2 * c + 1) * n:(2 * c + 2) * n, :] = d1
    kt = proj1(ATTN_W, KV_W).T
    cos_t = cost_ref[...]
    sin_t = sint_ref[...]
    for kv in range(N_KV_HEADS):
        kh = kt[kv * HEAD_DIM:(kv + 1) * HEAD_DIM, :]
        ms = jnp.sum(kh * kh, axis=0, keepdims=True) * (1.0 / HEAD_DIM)
        kn = kh * lax.rsqrt(ms + RMS_EPS) * kgt_ref[...]
        x1h, x2h = kn[0:half, :], kn[half:HEAD_DIM, :]
        kt_ref[kv * HEAD_DIM:(kv + 1) * HEAD_DIM, :] = jnp.concatenate(
            [x1h * cos_t - x2h * sin_t, x2h * cos_t + x1h * sin_t], axis=0)
    vt_ref[...] = proj1(ATTN_W + KV_W, KV_W).T
    gate_ref[...] = _silu(proj1(ATTN_W + 2 * KV_W, ATTN_W))


def _sample_dense(x, sp, sc, ng, win0, pw, ps, cw, wout0, win1, qgain, kgain, cos1, sin1):
    D = win0.shape[0]
    n = x.shape[0] * LANES // D
    assert n == LANES
    assert qgain.shape == kgain.shape == (1, HEAD_DIM) and cos1.shape == sin1.shape == (1, HEAD_DIM // 2)
    vmem = pl.BlockSpec(memory_space=pltpu.VMEM)
    return pl.pallas_call(
        _sample_dense_kernel,
        in_specs=[vmem] * 14,
        out_specs=[vmem] * 8,
        out_shape=[
            jax.ShapeDtypeStruct((n, D), F32),
            jax.ShapeDtypeStruct(sp.shape, F32),
            jax.ShapeDtypeStruct(sc.shape, F32),
            jax.ShapeDtypeStruct((N_HEADS * n, LANES), F32),
            jax.ShapeDtypeStruct((KV_W, n), F32),
            jax.ShapeDtypeStruct((KV_W, n), F32),
            jax.ShapeDtypeStruct((n, ATTN_W), F32),
            jax.ShapeDtypeStruct((_SMALL_ROWS, LANES), F32),
        ],
        compiler_params=pltpu.CompilerParams(vmem_limit_bytes=VMEM_LIMIT_BYTES),
        name="sample_dense",
    )(x, sp, sc, ng, win0, pw, ps, cw, wout0, win1, qgain, kgain, cos1, sin1)


def _sample_out(olo_ref, ohi_ref, gate_ref, x1_ref, wout_ref, y_ref):
    n = x1_ref.shape[0]
    blocks = LANES // HEAD_DIM
    lane_n = lax.broadcasted_iota(jnp.int32, (n, LANES), 1)
    chunks = []
    for c in range(ATTN_W // LANES):
        parts = []
        for hd in (2 * c, 2 * c + 1):
            kv = hd // GROUP
            slab = (olo_ref, ohi_ref)[kv // blocks]
            a = slab[pl.ds(hd, n, stride=N_HEADS), :]
            parts.append(a if kv % blocks == hd % blocks else pltpu.roll(a, HEAD_DIM, 1))
        chunks.append(jnp.where(lane_n < HEAD_DIM, parts[0], parts[1]))
    y = jnp.concatenate(chunks, axis=1) * gate_ref[...]
    out = x1_ref[...] + jnp.dot(y, wout_ref[...], preferred_element_type=F32)
    chunks = out.shape[1] // LANES
    for c in range(chunks):
        y_ref[pl.ds(c, n, stride=chunks), :] = out[:, c * LANES:(c + 1) * LANES]


def _rope_tables(pos):
    half = HEAD_DIM // 2
    inv = ROPE_THETA ** (-jnp.arange(half, dtype=F32) / half)
    ang = pos.astype(F32)[:, None] * inv[None, :]
    return jnp.cos(ang), jnp.sin(ang)


def kernel(x_prompt, x_sample, state_pool, state_conv, cache_k, cache_v, norm_g, w_in_even, pool_w, pool_scale,
           conv_w, w_out_even, w_in_odd, q_norm_g, k_norm_g, attn_sinks, w_out_odd):
    B, T, D = x_prompt.shape
    n_s, t_s, _ = x_sample.shape
    assert norm_g.shape[0] == 2 and w_in_even.shape[0] == 1 and w_in_odd.shape[0] == 1
    assert t_s == 1 and cache_k.shape[2] == WINDOW and T >= WINDOW
    assert cache_k.shape[3] * cache_k.shape[4] == KV_W and pool_w.shape[1:] == (len(POOL_WINDOWS), POOL_GROUP, POOL_GROUP)

    win0 = w_in_even[0]
    wout0 = w_out_even[0]
    win1 = w_in_odd[0]
    wout1 = w_out_odd[0]
    pw = pool_w[0]
    ps = pool_scale[0][None, :]
    cw = conv_w[0].reshape(-1, LANES)
    sinks = attn_sinks[0]

    tm_odd = 512
    cos_b, sin_b = _rope_tables(jnp.arange(T // tm_odd) * tm_odd)
    rope_base = jnp.broadcast_to(jnp.concatenate([cos_b, sin_b], axis=1)[:, :, None], (T // tm_odd, HEAD_DIM, LANES))
    cos_o, sin_o = _rope_tables(jnp.arange(tm_odd))
    rope_off = jnp.concatenate([cos_o, sin_o], axis=1).T
    cos_1, sin_1 = _rope_tables(PAST_LEN + jnp.arange(t_s))

    w_conv = state_conv.shape[-1]
    sp = jnp.transpose(state_pool[0], (1, 0, 2))
    sc = jnp.transpose(state_conv[0].reshape(n_s, CONV_BUF, w_conv // LANES, LANES), (0, 2, 1, 3)).reshape(-1, LANES)
    x1s, pool_s, conv_s, qh_s, kt_s, vt_s, gate_s, small = _sample_dense(
        x_sample.reshape(n_s * D // LANES, LANES), sp, sc, norm_g, win0, pw, ps, cw, wout0, win1,
        q_norm_g, k_norm_g, cos_1, sin_1)
    pool_s = jnp.transpose(pool_s, (1, 0, 2))
    conv_s = jnp.transpose(conv_s.reshape(n_s, w_conv // LANES, CONV_BUF, LANES), (0, 2, 1, 3)).reshape(
        n_s, CONV_BUF, w_conv)

    ckt = jnp.transpose(cache_k[0], (0, 2, 3, 1)).reshape(n_s, KV_W, WINDOW)
    cvt = jnp.transpose(cache_v[0], (0, 2, 3, 1)).reshape(n_s, KV_W, WINDOW)
    x1p, pool_p, conv_p, nkt_s, nvt_s, olo_s, ohi_s = _even_prompt(
        x_prompt, norm_g, win0, pw, ps, cw, wout0, qh_s, kt_s, vt_s, ckt, cvt, attn_sinks, tm=1024)
    y_p, k_p, v_p, y_s = _odd_prompt(x1p, norm_g, win1, wout1, small, rope_base, rope_off, sinks,
                                     olo_s, ohi_s, gate_s, x1s, tm=tm_odd)

    def window_major(a):
        return jnp.transpose(a.reshape(-1, N_KV_HEADS, HEAD_DIM, WINDOW), (0, 3, 1, 2))[None]

    return (y_p, y_s.reshape(n_s, 1, D), pool_p[None], pool_s[None], conv_p[None], conv_s[None],
            window_major(k_p), window_major(v_p), window_major(nkt_s), window_major(nvt_s))
```

```python
import functools

import jax
import jax.numpy as jnp
from jax import lax
from jax.experimental import pallas as pl
from jax.experimental.pallas import tpu as pltpu

F32 = jnp.float32
BF16 = jnp.bfloat16

POOL_WINDOWS = (2, 4, 8, 16)
POOL_GROUP = 128
POOL_BUF = max(POOL_WINDOWS) - 1
CONV_WIDTH = 3
CONV_BUF = CONV_WIDTH - 1
N_HEADS = 16
HEAD_DIM = 64
N_KV_HEADS = 4
GROUP = N_HEADS // N_KV_HEADS
WINDOW = 128
ROPE_THETA = 10000.0
RMS_EPS = 1e-6
PAST_LEN = 16384
ATTN_W = N_HEADS * HEAD_DIM
KV_W = N_KV_HEADS * HEAD_DIM

LANES = 128
POOL_HALO = 16
CONV_HALO = 8
VMEM_LIMIT_BYTES = 60 * 1024 * 1024
LOG2_E = 1.4426950408889634
_QG_T, _KG_T, _COS_T, _SIN_T = 0, HEAD_DIM, 2 * HEAD_DIM, 2 * HEAD_DIM + HEAD_DIM // 2
_QG_ROW, _COS_ROW, _SIN_ROW, _SMALL_ROWS = 3 * HEAD_DIM, 3 * HEAD_DIM + 1, 3 * HEAD_DIM + 2, 3 * HEAD_DIM + 8


def _rms(x, g):
    ms = jnp.mean(x * x, axis=-1, keepdims=True)
    return x * lax.rsqrt(ms + RMS_EPS) * g


def _silu(z):
    return z * jax.nn.sigmoid(z)


def _head_norm_rope(x, gain, cos, sin):
    lane = lax.broadcasted_iota(jnp.int32, x.shape, 1)
    first = lane < HEAD_DIM
    x2 = x * x
    ss0 = jnp.sum(jnp.where(first, x2, 0.0), axis=-1, keepdims=True)
    ss1 = jnp.sum(jnp.where(first, 0.0, x2), axis=-1, keepdims=True)
    r = jnp.where(first, lax.rsqrt(ss0 * (1.0 / HEAD_DIM) + RMS_EPS), lax.rsqrt(ss1 * (1.0 / HEAD_DIM) + RMS_EPS))
    xn = x * r * gain
    half = HEAD_DIM // 2
    swapped = jnp.where((lane % HEAD_DIM) < half, pltpu.roll(xn, LANES - half, 1), pltpu.roll(xn, half, 1))
    return xn * cos + swapped * sin


def _nt_dot(a, b):
    return lax.dot_general(a, b, (((1,), (1,)), ((), ())), preferred_element_type=F32)


def _col_table(row):
    m = row.shape[1]
    diag = lax.broadcasted_iota(jnp.int32, (m, m), 0) == lax.broadcasted_iota(jnp.int32, (m, m), 1)
    col = jnp.sum(jnp.where(diag, jnp.broadcast_to(row, (m, m)), 0.0), axis=1, keepdims=True)
    return jnp.broadcast_to(col, (m, LANES))


def _tiled_row(table):
    m = table.shape[0]
    pick = lax.broadcasted_iota(jnp.int32, table.shape, 0) == lax.broadcasted_iota(jnp.int32, table.shape, 1) % m
    return jnp.sum(jnp.where(pick, table, 0.0), axis=0, keepdims=True)


def _conv_tap(cw_ref, k, width):
    chunks = width // LANES
    return jnp.concatenate([cw_ref[k * chunks + j:k * chunks + j + 1, :] for j in range(chunks)], axis=1)


class _SampleAttention:
    def __init__(self, qh_ref, knt_ref, vnt_ref, ck_ref, cv_ref, sink_ref, nk_ref, nv_ref, olo_ref, ohi_ref, *, step):
        self.refs = (qh_ref, knt_ref, vnt_ref, ck_ref, cv_ref, sink_ref, nk_ref, nv_ref, olo_ref, ohi_ref)
        self.bb = ck_ref.shape[0]
        self.first = step * self.bb

    def roll(self):
        _, knt_ref, vnt_ref, ck_ref, cv_ref, _, nk_ref, nv_ref, _, _ = self.refs
        win = ck_ref.shape[2]
        newest = lax.broadcasted_iota(jnp.int32, (KV_W, win), 1) == win - 1
        for i in range(self.bb):
            b = self.first + i
            nk_ref[i] = jnp.where(newest, pltpu.roll(knt_ref[...], win - 1 - b, 1), pltpu.roll(ck_ref[i], win - 1, 1))
            nv_ref[i] = jnp.where(newest, pltpu.roll(vnt_ref[...], win - 1 - b, 1), pltpu.roll(cv_ref[i], win - 1, 1))

    def scores(self):
        qh_ref, nk_ref = self.refs[0], self.refs[6]
        n = qh_ref.shape[0] // N_HEADS
        blocks = LANES // HEAD_DIM
        hrow = lax.broadcasted_iota(jnp.int32, (N_HEADS, LANES), 0) // GROUP
        hcol = lax.broadcasted_iota(jnp.int32, (N_HEADS, LANES), 1) // HEAD_DIM
        self.s = []
        for i in range(self.bb):
            qd = qh_ref[pl.ds(self.first + i, N_HEADS, stride=n), :]
            qx = jnp.concatenate(
                [jnp.where(hcol + blocks * c == hrow, qd, 0.0) for c in range(KV_W // LANES)], axis=1)
            self.s.append(jnp.dot(qx.astype(BF16), nk_ref[i].astype(BF16), preferred_element_type=F32))

    def softmax(self):
        sink = _col_table(self.refs[5][...])[:, 0:1]
        self.p = []
        for s in self.s:
            m = jnp.maximum(jnp.max(s, axis=-1, keepdims=True), sink)
            p = jnp.exp(s - m)
            denom = jnp.sum(p, axis=-1, keepdims=True) + jnp.exp(sink - m)
            self.p.append((p * (1.0 / denom)).astype(BF16))

    def outputs(self):
        nv_ref, olo_ref, ohi_ref = self.refs[7:10]
        for i in range(self.bb):
            o = _nt_dot(self.p[i], nv_ref[i].astype(BF16))
            rows = pl.ds(pl.multiple_of((self.first + i) * N_HEADS, N_HEADS), N_HEADS)
            olo_ref[rows, :] = o[:, 0:LANES]
            ohi_ref[rows, :] = o[:, LANES:2 * LANES]


def _even_prompt_kernel(x_ref, g_ref, win_ref, pw_ref, ps_ref, cw_ref, wout_ref,
                        qh_ref, knt_ref, vnt_ref, ck_ref, cv_ref, sink_ref,
                        o_ref, pool_ref, conv_ref, nk_ref, nv_ref, olo_ref, ohi_ref,
                        uext, vext, sa, sb, y_scr, *, tm):
    t = pl.program_id(1)
    w_pool = uext.shape[1]
    w_conv = vext.shape[1]
    base = 2 * POOL_HALO
    side = _SampleAttention(qh_ref, knt_ref, vnt_ref, ck_ref, cv_ref, sink_ref, nk_ref, nv_ref, olo_ref, ohi_ref,
                            step=pl.program_id(0) * pl.num_programs(1) + t)

    @pl.when(t == 0)
    def _():
        uext[0:base, :] = jnp.zeros((base, w_pool), F32)
        vext[0:CONV_HALO, :] = jnp.zeros((CONV_HALO, w_conv), F32)
        sa[0:POOL_HALO, :] = jnp.zeros((POOL_HALO, POOL_GROUP), F32)
        sb[0:POOL_HALO, :] = jnp.zeros((POOL_HALO, POOL_GROUP), F32)

    x = x_ref[0]
    h = _rms(x, g_ref[0:1, :]).astype(win_ref.dtype)

    def proj(c, width):
        return jnp.dot(h, win_ref[:, c:c + width], preferred_element_type=F32)

    side.roll()
    u = proj(0, w_pool)
    uext[base:base + tm, :] = u
    pos = t * tm + lax.broadcasted_iota(jnp.int32, (tm, 1), 0)
    side.scores()
    z_a = proj(w_pool, w_pool)
    n_ext = POOL_HALO + tm

    for g, w in enumerate(POOL_WINDOWS):
        lo = g * POOL_GROUP
        cols = slice(lo, lo + POOL_GROUP)
        src = uext
        src_cols = cols
        step = 1
        bufs = (sa, sb)
        nbuf = 0
        while 2 * step < w:
            dst = bufs[nbuf % 2]
            dst[POOL_HALO:POOL_HALO + n_ext, :] = (src[POOL_HALO:POOL_HALO + n_ext, src_cols]
                                                   + src[POOL_HALO - step:POOL_HALO - step + n_ext, src_cols])
            src, src_cols = dst, slice(0, POOL_GROUP)
            step *= 2
            nbuf += 1
        wsum = src[base:base + tm, src_cols] + src[base - step:base - step + tm, src_cols]
        cnt = jnp.minimum(pos + 1, w).astype(F32)
        d = wsum * (1.0 / cnt) - u[:, cols]
        ya = jnp.dot(d.astype(pw_ref.dtype), pw_ref[g], preferred_element_type=F32)
        ya = ya * ps_ref[:, cols] * _silu(z_a[:, cols])
        y_scr[:, cols] = ya.astype(y_scr.dtype)

    b_gate = proj(2 * w_pool, w_conv)
    side.softmax()
    c_gate = proj(2 * w_pool + w_conv, w_conv)
    side.outputs()
    v_in = proj(2 * w_pool + 2 * w_conv, w_conv)
    v = c_gate * v_in
    vext[CONV_HALO:CONV_HALO + tm, :] = v
    conv = v * _conv_tap(cw_ref, CONV_WIDTH - 1, w_conv)
    for k in range(CONV_WIDTH - 1):
        shift = CONV_WIDTH - 1 - k
        conv = conv + vext[CONV_HALO - shift:CONV_HALO - shift + tm, :] * _conv_tap(cw_ref, k, w_conv)
    z_b = proj(2 * w_pool + 3 * w_conv, w_conv)
    y_b = b_gate * conv * _silu(z_b)
    y_scr[:, w_pool:w_pool + w_conv] = y_b.astype(y_scr.dtype)

    o_ref[0] = x + jnp.dot(y_scr[...], wout_ref[...], preferred_element_type=F32)
    pool_ref[0] = uext[base + tm - POOL_BUF:base + tm, :]
    conv_ref[0] = vext[CONV_HALO + tm - CONV_BUF:CONV_HALO + tm, :]
    uext[POOL_HALO:base, :] = uext[POOL_HALO + tm:base + tm, :]
    vext[0:CONV_HALO, :] = vext[tm:tm + CONV_HALO, :]


def _even_prompt(x, g, win, pw, ps, cw, wout, qh, knt, vnt, ckt, cvt, sink_col, *, tm):
    B, T, D = x.shape
    w_pool = ps.shape[-1]
    w_conv = (win.shape[1] - 2 * w_pool) // 4
    assert cw.shape == (CONV_WIDTH * w_conv // LANES, LANES)
    assert T % tm == 0 and tm % 16 == 0 and tm >= POOL_HALO
    nt = T // tm
    n, kvw, win_len = ckt.shape
    assert n % (B * nt) == 0 and kvw == KV_W == 2 * LANES and win_len == LANES and n == LANES
    bb = n // (B * nt)
    const = lambda *shape: pl.BlockSpec(shape, lambda b, t: (0,) * len(shape))
    cache_block = pl.BlockSpec((bb, kvw, win_len), lambda b, t: (b * nt + t, 0, 0))
    return pl.pallas_call(
        functools.partial(_even_prompt_kernel, tm=tm),
        grid=(B, nt),
        in_specs=[
            pl.BlockSpec((1, tm, D), lambda b, t: (b, t, 0)),
            const(*g.shape),
            const(*win.shape),
            const(*pw.shape),
            const(1, w_pool),
            const(*cw.shape),
            const(*wout.shape),
            const(*qh.shape),
            const(*knt.shape),
            const(*vnt.shape),
            cache_block,
            cache_block,
            const(*sink_col.shape),
        ],
        out_specs=[
            pl.BlockSpec((1, tm, D), lambda b, t: (b, t, 0)),
            pl.BlockSpec((1, POOL_BUF, w_pool), lambda b, t: (b, 0, 0)),
            pl.BlockSpec((1, CONV_BUF, w_conv), lambda b, t: (b, 0, 0)),
            cache_block,
            cache_block,
            const(n * N_HEADS, LANES),
            const(n * N_HEADS, LANES),
        ],
        out_shape=[
            jax.ShapeDtypeStruct((B, T, D), F32),
            jax.ShapeDtypeStruct((B, POOL_BUF, w_pool), F32),
            jax.ShapeDtypeStruct((B, CONV_BUF, w_conv), F32),
            jax.ShapeDtypeStruct((n, kvw, win_len), F32),
            jax.ShapeDtypeStruct((n, kvw, win_len), F32),
            jax.ShapeDtypeStruct((n * N_HEADS, LANES), F32),
            jax.ShapeDtypeStruct((n * N_HEADS, LANES), F32),
        ],
        scratch_shapes=[
            pltpu.VMEM((2 * POOL_HALO + tm, w_pool), F32),
            pltpu.VMEM((CONV_HALO + tm, w_conv), F32),
            pltpu.VMEM((2 * POOL_HALO + tm, POOL_GROUP), F32),
            pltpu.VMEM((2 * POOL_HALO + tm, POOL_GROUP), F32),
            pltpu.VMEM((tm, w_pool + w_conv), wout.dtype),
        ],
        compiler_params=pltpu.CompilerParams(
            dimension_semantics=("arbitrary", "arbitrary"), vmem_limit_bytes=VMEM_LIMIT_BYTES),
        name="even_prompt",
    )(x, g, win, pw, ps, cw, wout, qh, knt, vnt, ckt, cvt, sink_col)


def _odd_prompt_kernel(*refs, tm, nt, n_steps):
    (x_ref, xres_ref, ng_ref, win_ref, wo_ref, small_ref, ropeb_ref, ropeo_ref, sink_ref,
     olo_ref, ohi_ref, sgate_ref, sx1_ref,
     o_ref, knew_ref, vnew_ref, sy_ref, wkt_scr, wqvz_scr, wout_scr, qt_scr, kext, vt_ext, gate_scr, *rest) = refs
    g_ref = ng_ref.at[1:2]
    qg_ref, kg_ref = small_ref.at[_QG_T:_QG_T + HEAD_DIM], small_ref.at[_KG_T:_KG_T + HEAD_DIM]

    step = pl.program_id(0)
    last = n_steps - 1

    @pl.when(step == 0)
    def _():
        _sample_out(olo_ref, ohi_ref, sgate_ref, sx1_ref, wo_ref, sy_ref)
        kext[...] = jnp.zeros(kext.shape, BF16)
        vt_ext[...] = jnp.zeros(vt_ext.shape, BF16)
        blk = 2 * LANES
        for src, dst, width in ((0, 0, ATTN_W), (ATTN_W + KV_W, ATTN_W, KV_W + ATTN_W)):
            for c in range(0, width, blk):
                wqvz_scr[dst + c:dst + c + blk, :] = win_ref[:, src + c:src + c + blk].T.astype(BF16)
        wkt_scr[...] = win_ref[:, ATTN_W:ATTN_W + KV_W].T.astype(BF16)
        for c in range(0, wo_ref.shape[1], blk):
            wout_scr[c:c + blk, :] = wo_ref[:, c:c + blk].T.astype(BF16)

    def run(cur, **halves):
        _odd_prompt_step(x_ref, xres_ref, g_ref, wkt_scr, wqvz_scr, qg_ref, kg_ref, ropeb_ref, ropeo_ref, sink_ref,
                         wout_scr, o_ref, knew_ref, vnew_ref, qt_scr, kext, vt_ext, gate_scr, *rest,
                         tm=tm, nt=nt, cur=cur, prev=1 - cur, **halves)

    pl.when(step == 0)(lambda: run(0, do_attn=False))
    for parity in range(2):
        pl.when((step % 2 == parity) & (step > 0) & (step < last))(lambda parity=parity: run(parity))
    pl.when(step == last)(lambda: run((n_steps - 1) % 2, do_proj=False))


def _odd_prompt_step(x_ref, xres_ref, g_ref, wkt_ref, wqvz_ref, qg_ref, kg_ref, ropeb_ref, ropeo_ref, sink_ref,
                     wout_ref, o_ref, knew_ref, vnew_ref,
                     qt_scr, kext, vt_ext, gate_scr, yt_scr, s_scr, p_scr, esink_scr, *, tm, nt, cur, prev,
                     do_proj=True, do_attn=True):
    step = pl.program_id(0)
    proj_t = step % nt
    attn_t = jnp.maximum(step - 1, 0) % nt
    nblk = tm // WINDOW
    half = HEAD_DIM // 2

    h = _rms(x_ref[0], g_ref[...]).astype(BF16) if do_proj else None
    reps = tm // LANES
    if do_proj:
        cos_b = jnp.concatenate([ropeb_ref[0, 0:half, :]] * reps, axis=1)
        sin_b = jnp.concatenate([ropeb_ref[0, half:HEAD_DIM, :]] * reps, axis=1)
        cos_o, sin_o = ropeo_ref[0:half, :], ropeo_ref[half:HEAD_DIM, :]
        cos_t = cos_b * cos_o - sin_b * sin_o
        sin_t = sin_b * cos_o + cos_b * sin_o

    def norm_rope(xt, gain):
        ms = jnp.sum(xt * xt, axis=0, keepdims=True) * (1.0 / HEAD_DIM)
        xn = xt * lax.rsqrt(ms + RMS_EPS) * gain
        x1, x2 = xn[0:half, :], xn[half:HEAD_DIM, :]
        return jnp.concatenate([x1 * cos_t - x2 * sin_t, x2 * cos_t + x1 * sin_t], axis=0)

    has_past = proj_t > 0

    def proj_kv():
        kt = _nt_dot(wkt_ref[...], h)
        kgain = jnp.concatenate([kg_ref[...]] * reps, axis=1)
        kt = jnp.concatenate(
            [norm_rope(kt[kv * HEAD_DIM:(kv + 1) * HEAD_DIM, :], kgain) for kv in range(N_KV_HEADS)], axis=0)
        knew_ref[0] = kt[:, tm - WINDOW:tm]
        kext[cur, 0:WINDOW, :] = jnp.where(has_past, kext[prev, tm:tm + WINDOW, :], jnp.zeros((WINDOW, KV_W), BF16))
        kext[cur, WINDOW:WINDOW + tm, :] = kt.T.astype(BF16)
        vt = _nt_dot(wqvz_ref[ATTN_W:ATTN_W + KV_W, :], h)
        vnew_ref[0] = vt[:, tm - WINDOW:tm]
        vt_ext[cur, :, 0:WINDOW] = jnp.where(has_past, vt_ext[prev, :, tm:tm + WINDOW],
                                             jnp.zeros((KV_W, WINDOW), BF16))
        vt_ext[cur, :, WINDOW:WINDOW + tm] = vt.astype(BF16)

    def proj_gate(lo, hi):
        zt = _nt_dot(wqvz_ref[ATTN_W + KV_W + lo:ATTN_W + KV_W + hi, :], h)
        gate_scr[cur, lo:hi, :] = _silu(zt)

    q_raw = {}

    def proj_q(lo, hi):
        q_raw[lo] = _nt_dot(wqvz_ref[lo:hi, :], h)

    def finish_q():
        qgain = jnp.concatenate([qg_ref[...]] * reps, axis=1)
        for lo, qt in q_raw.items():
            for r0 in range(0, qt.shape[0], HEAD_DIM):
                qt_scr[cur, lo + r0:lo + r0 + HEAD_DIM, :] = (
                    norm_rope(qt[r0:r0 + HEAD_DIM, :], qgain) * (LOG2_E * HEAD_DIM ** -0.5)).astype(BF16)
        q_raw.clear()

    pieces = [proj_kv,
              lambda: proj_gate(0, ATTN_W // 2),
              lambda: proj_gate(ATTN_W // 2, ATTN_W),
              lambda: proj_q(0, ATTN_W // 2)]
    last_piece = lambda: proj_q(ATTN_W // 2, ATTN_W)
    if not do_proj:
        pieces, last_piece = [], lambda: None
    if not do_attn:
        for piece in pieces:
            piece()
        finish_q()
        last_piece()
        finish_q()
        return

    ri = lax.broadcasted_iota(jnp.int32, (WINDOW, 2 * WINDOW), 0)
    qq = lax.broadcasted_iota(jnp.int32, (WINDOW, 2 * WINDOW), 1) % WINDOW
    from_prev = ri > qq
    keep_prev = from_prev.astype(BF16)
    keep_cur = 1.0 - keep_prev
    lane = lax.broadcasted_iota(jnp.int32, (1, 2 * WINDOW), 1)
    zeros = jnp.zeros((HEAD_DIM, 2 * WINDOW), BF16)
    ones = jnp.ones((16, 2 * WINDOW), BF16)

    for i in range(nblk):
        c0 = i * WINDOW
        qcols = slice(c0, c0 + WINDOW)
        pairs = [(kv, kv * GROUP + 2 * pr) for kv in range(N_KV_HEADS) for pr in range(GROUP // 2)]
        for j, (kv, ha) in enumerate(pairs):
            chunk, pos = divmod(kv, LANES // HEAD_DIM)
            kblk = kext[prev, c0:c0 + 2 * WINDOW, chunk * LANES:(chunk + 1) * LANES]
            qpair = jnp.concatenate([qt_scr[prev, ha * HEAD_DIM:(ha + 1) * HEAD_DIM, qcols],
                                     qt_scr[prev, (ha + 1) * HEAD_DIM:(ha + 2) * HEAD_DIM, qcols]], axis=1)
            rhs = jnp.concatenate([qpair, zeros] if pos == 0 else [zeros, qpair], axis=0)
            s = jnp.dot(kblk, rhs, preferred_element_type=F32)
            s_prev = s[0:WINDOW, :]
            if i == 0:
                s_prev = jnp.where(attn_t == 0, -jnp.inf, s_prev)
            s_scr[j] = jnp.where(from_prev, s_prev, s[WINDOW:2 * WINDOW, :])
        for c, piece in enumerate(pieces):
            if c * nblk // len(pieces) == i:
                piece()
        for j, (kv, ha) in enumerate(pairs):
            s = s_scr[j]
            sink = jnp.where(lane < WINDOW, sink_ref[ha], sink_ref[ha + 1]) * LOG2_E
            m = jnp.maximum(jnp.max(s, axis=0, keepdims=True), sink)
            p_scr[j] = jnp.exp2(s - m).astype(BF16)
            esink_scr[j:j + 1, :] = jnp.exp2(sink - m)
        for j, (kv, ha) in enumerate(pairs):
            vg = vt_ext[prev, kv * HEAD_DIM:(kv + 1) * HEAD_DIM, c0:c0 + 2 * WINDOW]
            p = p_scr[j]
            p_keys = jnp.concatenate([p * keep_prev, p * keep_cur], axis=0)
            oa = jnp.dot(jnp.concatenate([vg, ones], axis=0), p_keys, preferred_element_type=F32)
            denom = oa[HEAD_DIM:HEAD_DIM + 1, :] + esink_scr[j:j + 1, :]
            o = oa[0:HEAD_DIM, :] * (1.0 / denom)
            for hd, part in ((ha, o[:, 0:WINDOW]), (ha + 1, o[:, WINDOW:2 * WINDOW])):
                rows = slice(hd * HEAD_DIM, (hd + 1) * HEAD_DIM)
                yt_scr[rows, qcols] = (part * gate_scr[prev, rows, qcols]).astype(BF16)

    out_t = jnp.dot(wout_ref[...], yt_scr[...], preferred_element_type=F32)
    finish_q()
    last_piece()
    finish_q()
    o_ref[0] = xres_ref[0] + out_t.T


def _odd_prompt(x, ng, w_in, w_out, small, cos_t, sin_t, sinks, s_olo, s_ohi, s_gate, s_x1, *, tm):
    B, T, D = x.shape
    assert T % tm == 0 and tm % WINDOW == 0 and w_in.shape == (D, 2 * ATTN_W + 2 * KV_W) and w_out.shape == (ATTN_W, D)
    nt = T // tm
    half = HEAD_DIM // 2
    const = lambda *shape: pl.BlockSpec(shape, lambda s: (0,) * len(shape))
    n_tiles = B * nt
    proj_tile = lambda s: jnp.minimum(s, n_tiles - 1)
    attn_tile = lambda s: jnp.maximum(s - 1, 0)
    return pl.pallas_call(
        functools.partial(_odd_prompt_kernel, tm=tm, nt=nt, n_steps=n_tiles + 1),
        grid=(n_tiles + 1,),
        in_specs=[
            pl.BlockSpec((1, tm, D), lambda s: (proj_tile(s) // nt, proj_tile(s) % nt, 0)),
            pl.BlockSpec((1, tm, D), lambda s: (attn_tile(s) // nt, attn_tile(s) % nt, 0)),
            const(*ng.shape),
            const(*w_in.shape),
            const(*w_out.shape),
            const(*small.shape),
            pl.BlockSpec((1, HEAD_DIM, LANES), lambda s: (proj_tile(s) % nt, 0, 0)),
            const(HEAD_DIM, tm),
            pl.BlockSpec(memory_space=pltpu.SMEM),
            const(*s_olo.shape),
            const(*s_ohi.shape),
            const(*s_gate.shape),
            const(*s_x1.shape),
        ],
        out_specs=[
            pl.BlockSpec((1, tm, D), lambda s: (attn_tile(s) // nt, attn_tile(s) % nt, 0)),
            pl.BlockSpec((1, KV_W, WINDOW), lambda s: (proj_tile(s) // nt, 0, 0)),
            pl.BlockSpec((1, KV_W, WINDOW), lambda s: (proj_tile(s) // nt, 0, 0)),
            const(s_x1.size // LANES, LANES),
        ],
        out_shape=[
            jax.ShapeDtypeStruct((B, T, D), F32),
            jax.ShapeDtypeStruct((B, KV_W, WINDOW), F32),
            jax.ShapeDtypeStruct((B, KV_W, WINDOW), F32),
            jax.ShapeDtypeStruct((s_x1.size // LANES, LANES), F32),
        ],
        scratch_shapes=[
            pltpu.VMEM((KV_W, D), BF16),
            pltpu.VMEM((2 * ATTN_W + KV_W, D), BF16),
            pltpu.VMEM((D, ATTN_W), BF16),
            pltpu.VMEM((2, ATTN_W, tm), BF16),
            pltpu.VMEM((2, WINDOW + tm, KV_W), BF16),
            pltpu.VMEM((2, KV_W, WINDOW + tm), BF16),
            pltpu.VMEM((2, ATTN_W, tm), F32),
            pltpu.VMEM((ATTN_W, tm), BF16),
            pltpu.VMEM((N_HEADS // 2, WINDOW, 2 * WINDOW), F32),
            pltpu.VMEM((N_HEADS // 2, WINDOW, 2 * WINDOW), BF16),
            pltpu.VMEM((N_HEADS // 2, 2 * WINDOW), F32),
        ],
        compiler_params=pltpu.CompilerParams(
            dimension_semantics=("arbitrary",), vmem_limit_bytes=VMEM_LIMIT_BYTES),
        name="odd_prompt",
    )(x, x, ng, w_in, w_out, small, cos_t, sin_t, sinks, s_olo, s_ohi, s_gate, s_x1)


def _dup_heads(x):
    lane = lax.broadcasted_iota(jnp.int32, x.shape, 1)
    rolled = pltpu.roll(x, HEAD_DIM, 1)
    first = lane < HEAD_DIM
    return jnp.where(first, x, rolled), jnp.where(first, rolled, x)


def _sample_dense_kernel(x_ref, sp_ref, sc_ref, ng_ref, win0_ref, pw_ref, ps_ref, cw_ref, wout0_ref,
                         win1_ref, qgain_ref, kgain_ref, cos1_ref, sin1_ref,
                         x1_ref, pool_ref, conv_ref, qh_ref, kt_ref, vt_ref, gate_ref, small_ref,
                         *scratch):
    hbm = _HbmOperands(sp_ref, win0_ref, wout0_ref, win1_ref, pool_ref, *scratch)
    sp_ref, win0_ref, wout0_ref, win1_ref = hbm.sp, hbm.win0, hbm.wout0, hbm.win1
    w_pool = ps_ref.shape[1]
    w_conv = (win0_ref.shape[1] - 2 * w_pool) // 4
    half = HEAD_DIM // 2
    qg_t, cos_t1, sin_t1 = _col_table(qgain_ref[...]), _col_table(cos1_ref[...]), _col_table(sin1_ref[...])
    small_ref[_QG_T:_QG_T + HEAD_DIM, :] = qg_t
    small_ref[_KG_T:_KG_T + HEAD_DIM, :] = _col_table(kgain_ref[...])
    small_ref[_COS_T:_COS_T + half, :] = cos_t1
    small_ref[_SIN_T:_SIN_T + half, :] = sin_t1
    second_half = lax.broadcasted_iota(jnp.int32, (1, LANES), 1) % HEAD_DIM >= half
    small_ref[_QG_ROW:_SMALL_ROWS, :] = jnp.concatenate(
        [_tiled_row(qg_t), _tiled_row(cos_t1), jnp.where(second_half, 1.0, -1.0) * _tiled_row(sin_t1),
         jnp.zeros((_SMALL_ROWS - _SIN_ROW - 1, LANES), F32)], axis=0)
    g0_ref, g1_ref = ng_ref.at[0:1], ng_ref.at[1:2]
    qg_ref, kgt_ref = small_ref.at[_QG_ROW:_QG_ROW + 1], small_ref.at[_KG_T:_KG_T + HEAD_DIM]
    cos_ref, sin_ref = small_ref.at[_COS_ROW:_COS_ROW + 1], small_ref.at[_SIN_ROW:_SIN_ROW + 1]
    cost_ref, sint_ref = small_ref.at[_COS_T:_COS_T + HEAD_DIM // 2], small_ref.at[_SIN_T:_SIN_T + HEAD_DIM // 2]

    n = x1_ref.shape[0]
    chunks = x1_ref.shape[1] // LANES
    x = jnp.concatenate([x_ref[pl.ds(c, n, stride=chunks), :] for c in range(chunks)], axis=1)
    h = _rms(x, g0_ref[...]).astype(win0_ref.dtype)

    def proj(c, width):
        return jnp.dot(h, win0_ref[:, c:c + width], preferred_element_type=F32)

    hbm.wait_win0()
    u = proj(0, w_pool)
    hbm.put_newest_pool_row(u)
    z_a = proj(w_pool, w_pool)
    hbm.wait_sp()
    ys = []
    for g, w in enumerate(POOL_WINDOWS):
        cols = slice(g * POOL_GROUP, (g + 1) * POOL_GROUP)
        wsum = u[:, cols]
        for r in range(POOL_BUF - (w - 1), POOL_BUF):
            wsum = wsum + sp_ref[r, :, cols]
        cnt = float(min(PAST_LEN + 1, w))
        d = wsum * (1.0 / cnt) - u[:, cols]
        ya = jnp.dot(d.astype(pw_ref.dtype), pw_ref[g], preferred_element_type=F32)
        ys.append((ya * ps_ref[:, cols] * _silu(z_a[:, cols])).astype(wout0_ref.dtype))

    b_gate = proj(2 * w_pool, w_conv)
    c_gate = proj(2 * w_pool + w_conv, w_conv)
    v_in = proj(2 * w_pool + 2 * w_conv, w_conv)
    v = c_gate * v_in
    cchunks = w_conv // LANES
    cstride = cchunks * CONV_BUF

    def conv_row(r):
        return jnp.concatenate(
            [sc_ref[pl.ds(j * CONV_BUF + r, n, stride=cstride), :] for j in range(cchunks)], axis=1)

    def put_conv_row(r, val):
        for j in range(cchunks):
            conv_ref[pl.ds(j * CONV_BUF + r, n, stride=cstride), :] = val[:, j * LANES:(j + 1) * LANES]

    conv = v * _conv_tap(cw_ref, CONV_WIDTH - 1, w_conv)
    for k in range(CONV_BUF):
        conv = conv + conv_row(k) * _conv_tap(cw_ref, k, w_conv)
    for r in range(CONV_BUF - 1):
        put_conv_row(r, conv_row(r + 1))
    put_conv_row(CONV_BUF - 1, v)
    z_b = proj(2 * w_pool + 3 * w_conv, w_conv)
    ys.append((b_gate * conv * _silu(z_b)).astype(wout0_ref.dtype))
    y = jnp.concatenate(ys, axis=1)
    hbm.wait_wout0()
    x1 = x + jnp.dot(y, wout0_ref[...], preferred_element_type=F32)
    x1_ref[...] = x1

    h1 = _rms(x1, g1_ref[...])
    hbm.wait_win1()

    def proj1(lo, width):
        return jnp.dot(h1, win1_ref[:, lo:lo + width], preferred_element_type=F32)

    q = proj1(0, ATTN_W)
    for c in range(ATTN_W // LANES):
        qc = _head_norm_rope(q[:, c * LANES:(c + 1) * LANES], qg_ref[...], cos_ref[...], sin_ref[...])
        d0, d1 = _dup_heads(qc * (HEAD_DIM ** -0.5))
        qh_ref[2 * c * n:(2 * c + 1) * n, :] = d0
        qh_ref[(2 * c + 1) * n:(2 * c + 2) * n, :] = d1
    kt = proj1(ATTN_W, KV_W).T
    cos_t = cost_ref[...]
    sin_t = sint_ref[...]
    for kv in range(N_KV_HEADS):
        kh = kt[kv * HEAD_DIM:(kv + 1) * HEAD_DIM, :]
        ms = jnp.sum(kh * kh, axis=0, keepdims=True) * (1.0 / HEAD_DIM)
        kn = kh * lax.rsqrt(ms + RMS_EPS) * kgt_ref[...]
        x1h, x2h = kn[0:half, :], kn[half:HEAD_DIM, :]
        kt_ref[kv * HEAD_DIM:(kv + 1) * HEAD_DIM, :] = jnp.concatenate(
            [x1h * cos_t - x2h * sin_t, x2h * cos_t + x1h * sin_t], axis=0)
    vt_ref[...] = proj1(ATTN_W + KV_W, KV_W).T
    gate_ref[...] = _silu(proj1(ATTN_W + 2 * KV_W, ATTN_W))
    hbm.finish()


class _HbmOperands:
    def __init__(self, sp_hbm, win0_hbm, wout0_hbm, win1_hbm, pool_hbm, sp, win0, wout0, win1, newest, sems):
        self.sp, self.win0, self.wout0, self.win1, self.newest = sp, win0, wout0, win1, newest
        self.pool_hbm = pool_hbm
        self.sems = sems
        rows = sp_hbm.shape[0]
        self.shift = pltpu.make_async_copy(sp_hbm.at[1:rows], pool_hbm.at[0:rows - 1], sems.at[0])
        self.fetch = {name: pltpu.make_async_copy(src, dst, sems.at[i + 1]) for i, (name, src, dst) in enumerate(
            (("win0", win0_hbm, win0), ("sp", sp_hbm, sp), ("wout0", wout0_hbm, wout0), ("win1", win1_hbm, win1)))}
        self.shift.start()
        for copy in self.fetch.values():
            copy.start()

    def wait_win0(self):
        self.fetch["win0"].wait()

    def wait_sp(self):
        self.fetch["sp"].wait()

    def wait_wout0(self):
        self.fetch["wout0"].wait()

    def wait_win1(self):
        self.fetch["win1"].wait()

    def put_newest_pool_row(self, u):
        rows = self.pool_hbm.shape[0]
        self.newest[...] = u
        self.put = pltpu.make_async_copy(self.newest, self.pool_hbm.at[rows - 1], self.sems.at[5])
        self.put.start()

    def finish(self):
        self.shift.wait()
        self.put.wait()


def _sample_dense(x, sp, sc, ng, win0, pw, ps, cw, wout0, win1, qgain, kgain, cos1, sin1):
    D = win0.shape[0]
    n = x.shape[0] * LANES // D
    assert n == LANES
    assert qgain.shape == kgain.shape == (1, HEAD_DIM) and cos1.shape == sin1.shape == (1, HEAD_DIM // 2)
    vmem = pl.BlockSpec(memory_space=pltpu.VMEM)
    hbm = pl.BlockSpec(memory_space=pl.ANY)
    in_specs = [vmem, hbm, vmem, vmem, hbm, vmem, vmem, vmem, hbm, hbm] + [vmem] * 4
    return pl.pallas_call(
        _sample_dense_kernel,
        in_specs=in_specs,
        out_specs=[vmem, hbm] + [vmem] * 6,
        scratch_shapes=[
            pltpu.VMEM(sp.shape, F32),
            pltpu.VMEM(win0.shape, F32),
            pltpu.VMEM(wout0.shape, F32),
            pltpu.VMEM(win1.shape, F32),
            pltpu.VMEM(sp.shape[1:], F32),
            pltpu.SemaphoreType.DMA((6,)),
        ],
        out_shape=[
            jax.ShapeDtypeStruct((n, D), F32),
            jax.ShapeDtypeStruct(sp.shape, F32),
            jax.ShapeDtypeStruct(sc.shape, F32),
            jax.ShapeDtypeStruct((N_HEADS * n, LANES), F32),
            jax.ShapeDtypeStruct((KV_W, n), F32),
            jax.ShapeDtypeStruct((KV_W, n), F32),
            jax.ShapeDtypeStruct((n, ATTN_W), F32),
            jax.ShapeDtypeStruct((_SMALL_ROWS, LANES), F32),
        ],
        compiler_params=pltpu.CompilerParams(vmem_limit_bytes=VMEM_LIMIT_BYTES),
        name="sample_dense",
    )(x, sp, sc, ng, win0, pw, ps, cw, wout0, win1, qgain, kgain, cos1, sin1)


def _sample_out(olo_ref, ohi_ref, gate_ref, x1_ref, wout_ref, y_ref):
    n = x1_ref.shape[0]
    blocks = LANES // HEAD_DIM
    lane_n = lax.broadcasted_iota(jnp.int32, (n, LANES), 1)
    chunks = []
    for c in range(ATTN_W // LANES):
        parts = []
        for hd in (2 * c, 2 * c + 1):
            kv = hd // GROUP
            slab = (olo_ref, ohi_ref)[kv // blocks]
            a = slab[pl.ds(hd, n, stride=N_HEADS), :]
            parts.append(a if kv % blocks == hd % blocks else pltpu.roll(a, HEAD_DIM, 1))
        chunks.append(jnp.where(lane_n < HEAD_DIM, parts[0], parts[1]))
    y = jnp.concatenate(chunks, axis=1) * gate_ref[...]
    out = x1_ref[...] + jnp.dot(y, wout_ref[...], preferred_element_type=F32)
    chunks = out.shape[1] // LANES
    for c in range(chunks):
        y_ref[pl.ds(c, n, stride=chunks), :] = out[:, c * LANES:(c + 1) * LANES]


def _rope_tables(pos):
    half = HEAD_DIM // 2
    inv = ROPE_THETA ** (-jnp.arange(half, dtype=F32) / half)
    ang = pos.astype(F32)[:, None] * inv[None, :]
    return jnp.cos(ang), jnp.sin(ang)


def kernel(x_prompt, x_sample, state_pool, state_conv, cache_k, cache_v, norm_g, w_in_even, pool_w, pool_scale,
           conv_w, w_out_even, w_in_odd, q_norm_g, k_norm_g, attn_sinks, w_out_odd):
    B, T, D = x_prompt.shape
    n_s, t_s, _ = x_sample.shape
    assert norm_g.shape[0] == 2 and w_in_even.shape[0] == 1 and w_in_odd.shape[0] == 1
    assert t_s == 1 and cache_k.shape[2] == WINDOW and T >= WINDOW
    assert cache_k.shape[3] * cache_k.shape[4] == KV_W and pool_w.shape[1:] == (len(POOL_WINDOWS), POOL_GROUP, POOL_GROUP)

    win0 = w_in_even[0]
    wout0 = w_out_even[0]
    win1 = w_in_odd[0]
    wout1 = w_out_odd[0]
    pw = pool_w[0]
    ps = pool_scale[0][None, :]
    cw = conv_w[0].reshape(-1, LANES)
    sinks = attn_sinks[0]

    tm_odd = 512
    cos_b, sin_b = _rope_tables(jnp.arange(T // tm_odd) * tm_odd)
    rope_base = jnp.broadcast_to(jnp.concatenate([cos_b, sin_b], axis=1)[:, :, None], (T // tm_odd, HEAD_DIM, LANES))
    cos_o, sin_o = _rope_tables(jnp.arange(tm_odd))
    rope_off = jnp.concatenate([cos_o, sin_o], axis=1).T
    cos_1, sin_1 = _rope_tables(PAST_LEN + jnp.arange(t_s))

    w_conv = state_conv.shape[-1]
    sp = jnp.transpose(state_pool[0], (1, 0, 2))
    sc = jnp.transpose(state_conv[0].reshape(n_s, CONV_BUF, w_conv // LANES, LANES), (0, 2, 1, 3)).reshape(-1, LANES)
    x1s, pool_s, conv_s, qh_s, kt_s, vt_s, gate_s, small = _sample_dense(
        x_sample.reshape(n_s * D // LANES, LANES), sp, sc, norm_g, win0, pw, ps, cw, wout0, win1,
        q_norm_g, k_norm_g, cos_1, sin_1)
    pool_s = jnp.transpose(pool_s, (1, 0, 2))
    conv_s = jnp.transpose(conv_s.reshape(n_s, w_conv // LANES, CONV_BUF, LANES), (0, 2, 1, 3)).reshape(
        n_s, CONV_BUF, w_conv)

    ckt = jnp.transpose(cache_k[0], (0, 2, 3, 1)).reshape(n_s, KV_W, WINDOW)
    cvt = jnp.transpose(cache_v[0], (0, 2, 3, 1)).reshape(n_s, KV_W, WINDOW)
    x1p, pool_p, conv_p, nkt_s, nvt_s, olo_s, ohi_s = _even_prompt(
        x_prompt, norm_g, win0, pw, ps, cw, wout0, qh_s, kt_s, vt_s, ckt, cvt, attn_sinks, tm=1024)
    y_p, k_p, v_p, y_s = _odd_prompt(x1p, norm_g, win1, wout1, small, rope_base, rope_off, sinks,
                                     olo_s, ohi_s, gate_s, x1s, tm=tm_odd)

    def window_major(a):
        return jnp.transpose(a.reshape(-1, N_KV_HEADS, HEAD_DIM, WINDOW), (0, 3, 1, 2))[None]

    return (y_p, y_s.reshape(n_s, 1, D), pool_p[None], pool_s[None], conv_p[None], conv_s[None],
            window_major(k_p), window_major(v_p), window_major(nkt_s), window_major(nvt_s))
```

```python
import functools

import jax
import jax.numpy as jnp
from jax import lax
from jax.experimental import pallas as pl
from jax.experimental.pallas import tpu as pltpu

F32 = jnp.float32
BF16 = jnp.bfloat16

POOL_WINDOWS = (2, 4, 8, 16)
POOL_GROUP = 128
POOL_BUF = max(POOL_WINDOWS) - 1
CONV_WIDTH = 3
CONV_BUF = CONV_WIDTH - 1
N_HEADS = 16
HEAD_DIM = 64
N_KV_HEADS = 4
GROUP = N_HEADS // N_KV_HEADS
WINDOW = 128
ROPE_THETA = 10000.0
RMS_EPS = 1e-6
PAST_LEN = 16384
ATTN_W = N_HEADS * HEAD_DIM
KV_W = N_KV_HEADS * HEAD_DIM

LANES = 128
POOL_HALO = 16
CONV_HALO = 8
VMEM_LIMIT_BYTES = 60 * 1024 * 1024
LOG2_E = 1.4426950408889634
_QG_T, _KG_T, _COS_T, _SIN_T = 0, HEAD_DIM, 2 * HEAD_DIM, 2 * HEAD_DIM + HEAD_DIM // 2
_QG_ROW, _COS_ROW, _SIN_ROW, _SMALL_ROWS = 3 * HEAD_DIM, 3 * HEAD_DIM + 1, 3 * HEAD_DIM + 2, 3 * HEAD_DIM + 8


def _rms(x, g):
    ms = jnp.mean(x * x, axis=-1, keepdims=True)
    return x * lax.rsqrt(ms + RMS_EPS) * g


def _silu(z):
    return z * jax.nn.sigmoid(z)


def _head_norm_rope(x, gain, cos, sin):
    lane = lax.broadcasted_iota(jnp.int32, x.shape, 1)
    first = lane < HEAD_DIM
    x2 = x * x
    ss0 = jnp.sum(jnp.where(first, x2, 0.0), axis=-1, keepdims=True)
    ss1 = jnp.sum(jnp.where(first, 0.0, x2), axis=-1, keepdims=True)
    r = jnp.where(first, lax.rsqrt(ss0 * (1.0 / HEAD_DIM) + RMS_EPS), lax.rsqrt(ss1 * (1.0 / HEAD_DIM) + RMS_EPS))
    xn = x * r * gain
    half = HEAD_DIM // 2
    swapped = jnp.where((lane % HEAD_DIM) < half, pltpu.roll(xn, LANES - half, 1), pltpu.roll(xn, half, 1))
    return xn * cos + swapped * sin


def _nt_dot(a, b):
    return lax.dot_general(a, b, (((1,), (1,)), ((), ())), preferred_element_type=F32)


def _col_table(row):
    m = row.shape[1]
    diag = lax.broadcasted_iota(jnp.int32, (m, m), 0) == lax.broadcasted_iota(jnp.int32, (m, m), 1)
    col = jnp.sum(jnp.where(diag, jnp.broadcast_to(row, (m, m)), 0.0), axis=1, keepdims=True)
    return jnp.broadcast_to(col, (m, LANES))


def _tiled_row(table):
    m = table.shape[0]
    pick = lax.broadcasted_iota(jnp.int32, table.shape, 0) == lax.broadcasted_iota(jnp.int32, table.shape, 1) % m
    return jnp.sum(jnp.where(pick, table, 0.0), axis=0, keepdims=True)


def _conv_tap(cw_ref, k, width):
    chunks = width // LANES
    return jnp.concatenate([cw_ref[k * chunks + j:k * chunks + j + 1, :] for j in range(chunks)], axis=1)


class _SampleAttention:
    def __init__(self, qh_ref, knt_ref, vnt_ref, ck_ref, cv_ref, sink_ref, nk_ref, nv_ref, olo_ref, ohi_ref, *, step):
        self.refs = (qh_ref, knt_ref, vnt_ref, ck_ref, cv_ref, sink_ref, nk_ref, nv_ref, olo_ref, ohi_ref)
        self.bb = ck_ref.shape[0]
        self.first = step * self.bb

    def roll(self):
        _, knt_ref, vnt_ref, ck_ref, cv_ref, _, nk_ref, nv_ref, _, _ = self.refs
        win = ck_ref.shape[2]
        newest = lax.broadcasted_iota(jnp.int32, (KV_W, win), 1) == win - 1
        for i in range(self.bb):
            b = self.first + i
            nk_ref[i] = jnp.where(newest, pltpu.roll(knt_ref[...], win - 1 - b, 1), pltpu.roll(ck_ref[i], win - 1, 1))
            nv_ref[i] = jnp.where(newest, pltpu.roll(vnt_ref[...], win - 1 - b, 1), pltpu.roll(cv_ref[i], win - 1, 1))

    def scores(self):
        qh_ref, nk_ref = self.refs[0], self.refs[6]
        n = qh_ref.shape[0] // N_HEADS
        blocks = LANES // HEAD_DIM
        hrow = lax.broadcasted_iota(jnp.int32, (N_HEADS, LANES), 0) // GROUP
        hcol = lax.broadcasted_iota(jnp.int32, (N_HEADS, LANES), 1) // HEAD_DIM
        self.s = []
        for i in range(self.bb):
            qd = qh_ref[pl.ds(self.first + i, N_HEADS, stride=n), :]
            qx = jnp.concatenate(
                [jnp.where(hcol + blocks * c == hrow, qd, 0.0) for c in range(KV_W // LANES)], axis=1)
            self.s.append(jnp.dot(qx.astype(BF16), nk_ref[i].astype(BF16), preferred_element_type=F32))

    def softmax(self):
        sink = _col_table(self.refs[5][...])[:, 0:1]
        self.p = []
        for s in self.s:
            m = jnp.maximum(jnp.max(s, axis=-1, keepdims=True), sink)
            p = jnp.exp(s - m)
            denom = jnp.sum(p, axis=-1, keepdims=True) + jnp.exp(sink - m)
            self.p.append((p * (1.0 / denom)).astype(BF16))

    def outputs(self):
        nv_ref, olo_ref, ohi_ref = self.refs[7:10]
        for i in range(self.bb):
            o = _nt_dot(self.p[i], nv_ref[i].astype(BF16))
            rows = pl.ds(pl.multiple_of((self.first + i) * N_HEADS, N_HEADS), N_HEADS)
            olo_ref[rows, :] = o[:, 0:LANES]
            ohi_ref[rows, :] = o[:, LANES:2 * LANES]


def _even_prompt_kernel(x_ref, g_ref, win_ref, pw_ref, ps_ref, cw_ref, wout_ref,
                        qh_ref, knt_ref, vnt_ref, ck_ref, cv_ref, sink_ref,
                        o_ref, pool_ref, conv_ref, nk_ref, nv_ref, olo_ref, ohi_ref,
                        uext, vext, sa, sb, y_scr, *, tm):
    t = pl.program_id(1)
    w_pool = uext.shape[1]
    w_conv = vext.shape[1]
    base = 2 * POOL_HALO
    side = _SampleAttention(qh_ref, knt_ref, vnt_ref, ck_ref, cv_ref, sink_ref, nk_ref, nv_ref, olo_ref, ohi_ref,
                            step=pl.program_id(0) * pl.num_programs(1) + t)

    @pl.when(t == 0)
    def _():
        uext[0:base, :] = jnp.zeros((base, w_pool), F32)
        vext[0:CONV_HALO, :] = jnp.zeros((CONV_HALO, w_conv), F32)
        sa[0:POOL_HALO, :] = jnp.zeros((POOL_HALO, POOL_GROUP), F32)
        sb[0:POOL_HALO, :] = jnp.zeros((POOL_HALO, POOL_GROUP), F32)

    x = x_ref[0]
    h = _rms(x, g_ref[0:1, :]).astype(win_ref.dtype)

    def proj(c, width):
        return jnp.dot(h, win_ref[:, c:c + width], preferred_element_type=F32)

    side.roll()
    u = proj(0, w_pool)
    uext[base:base + tm, :] = u
    pos = t * tm + lax.broadcasted_iota(jnp.int32, (tm, 1), 0)
    side.scores()
    z_a = proj(w_pool, w_pool)
    n_ext = POOL_HALO + tm

    for g, w in enumerate(POOL_WINDOWS):
        lo = g * POOL_GROUP
        cols = slice(lo, lo + POOL_GROUP)
        src = uext
        src_cols = cols
        step = 1
        bufs = (sa, sb)
        nbuf = 0
        while 2 * step < w:
            dst = bufs[nbuf % 2]
            dst[POOL_HALO:POOL_HALO + n_ext, :] = (src[POOL_HALO:POOL_HALO + n_ext, src_cols]
                                                   + src[POOL_HALO - step:POOL_HALO - step + n_ext, src_cols])
            src, src_cols = dst, slice(0, POOL_GROUP)
            step *= 2
            nbuf += 1
        wsum = src[base:base + tm, src_cols] + src[base - step:base - step + tm, src_cols]
        cnt = jnp.minimum(pos + 1, w).astype(F32)
        d = wsum * (1.0 / cnt) - u[:, cols]
        ya = jnp.dot(d.astype(pw_ref.dtype), pw_ref[g], preferred_element_type=F32)
        ya = ya * ps_ref[:, cols] * _silu(z_a[:, cols])
        y_scr[:, cols] = ya.astype(y_scr.dtype)

    b_gate = proj(2 * w_pool, w_conv)
    side.softmax()
    c_gate = proj(2 * w_pool + w_conv, w_conv)
    side.outputs()
    v_in = proj(2 * w_pool + 2 * w_conv, w_conv)
    v = c_gate * v_in
    vext[CONV_HALO:CONV_HALO + tm, :] = v
    conv = v * _conv_tap(cw_ref, CONV_WIDTH - 1, w_conv)
    for k in range(CONV_WIDTH - 1):
        shift = CONV_WIDTH - 1 - k
        conv = conv + vext[CONV_HALO - shift:CONV_HALO - shift + tm, :] * _conv_tap(cw_ref, k, w_conv)
    z_b = proj(2 * w_pool + 3 * w_conv, w_conv)
    y_b = b_gate * conv * _silu(z_b)
    y_scr[:, w_pool:w_pool + w_conv] = y_b.astype(y_scr.dtype)

    o_ref[0] = x + jnp.dot(y_scr[...], wout_ref[...], preferred_element_type=F32)
    pool_ref[0] = uext[base + tm - POOL_BUF:base + tm, :]
    conv_ref[0] = vext[CONV_HALO + tm - CONV_BUF:CONV_HALO + tm, :]
    uext[POOL_HALO:base, :] = uext[POOL_HALO + tm:base + tm, :]
    vext[0:CONV_HALO, :] = vext[tm:tm + CONV_HALO, :]


def _even_prompt(x, g, win, pw, ps, cw, wout, qh, knt, vnt, ckt, cvt, sink_col, *, tm):
    B, T, D = x.shape
    w_pool = ps.shape[-1]
    w_conv = (win.shape[1] - 2 * w_pool) // 4
    assert cw.shape == (CONV_WIDTH * w_conv // LANES, LANES)
    assert T % tm == 0 and tm % 16 == 0 and tm >= POOL_HALO
    nt = T // tm
    n, kvw, win_len = ckt.shape
    assert n % (B * nt) == 0 and kvw == KV_W == 2 * LANES and win_len == LANES and n == LANES
    bb = n // (B * nt)
    const = lambda *shape: pl.BlockSpec(shape, lambda b, t: (0,) * len(shape))
    cache_block = pl.BlockSpec((bb, kvw, win_len), lambda b, t: (b * nt + t, 0, 0))
    return pl.pallas_call(
        functools.partial(_even_prompt_kernel, tm=tm),
        grid=(B, nt),
        in_specs=[
            pl.BlockSpec((1, tm, D), lambda b, t: (b, t, 0)),
            const(*g.shape),
            const(*win.shape),
            const(*pw.shape),
            const(1, w_pool),
            const(*cw.shape),
            const(*wout.shape),
            const(*qh.shape),
            const(*knt.shape),
            const(*vnt.shape),
            cache_block,
            cache_block,
            const(*sink_col.shape),
        ],
        out_specs=[
            pl.BlockSpec((1, tm, D), lambda b, t: (b, t, 0)),
            pl.BlockSpec((1, POOL_BUF, w_pool), lambda b, t: (b, 0, 0)),
            pl.BlockSpec((1, CONV_BUF, w_conv), lambda b, t: (b, 0, 0)),
            cache_block,
            cache_block,
            const(n * N_HEADS, LANES),
            const(n * N_HEADS, LANES),
        ],
        out_shape=[
            jax.ShapeDtypeStruct((B, T, D), F32),
            jax.ShapeDtypeStruct((B, POOL_BUF, w_pool), F32),
            jax.ShapeDtypeStruct((B, CONV_BUF, w_conv), F32),
            jax.ShapeDtypeStruct((n, kvw, win_len), F32),
            jax.ShapeDtypeStruct((n, kvw, win_len), F32),
            jax.ShapeDtypeStruct((n * N_HEADS, LANES), F32),
            jax.ShapeDtypeStruct((n * N_HEADS, LANES), F32),
        ],
        scratch_shapes=[
            pltpu.VMEM((2 * POOL_HALO + tm, w_pool), F32),
            pltpu.VMEM((CONV_HALO + tm, w_conv), F32),
            pltpu.VMEM((2 * POOL_HALO + tm, POOL_GROUP), F32),
            pltpu.VMEM((2 * POOL_HALO + tm, POOL_GROUP), F32),
            pltpu.VMEM((tm, w_pool + w_conv), wout.dtype),
        ],
        compiler_params=pltpu.CompilerParams(
            dimension_semantics=("arbitrary", "arbitrary"), vmem_limit_bytes=VMEM_LIMIT_BYTES),
        name="even_prompt",
    )(x, g, win, pw, ps, cw, wout, qh, knt, vnt, ckt, cvt, sink_col)


def _odd_prompt_kernel(*refs, tm, nt, n_steps):
    (x_ref, xres_ref, ng_ref, win_ref, wo_ref, small_ref, ropeb_ref, ropeo_ref, sink_ref,
     olo_ref, ohi_ref, sgate_ref, sx1_ref,
     o_ref, knew_ref, vnew_ref, sy_ref, wkt_scr, wqvz_scr, wout_scr, qt_scr, kext, vt_ext, gate_scr, *rest) = refs
    g_ref = ng_ref.at[1:2]
    qg_ref, kg_ref = small_ref.at[_QG_T:_QG_T + HEAD_DIM], small_ref.at[_KG_T:_KG_T + HEAD_DIM]

    step = pl.program_id(0)
    last = n_steps - 1

    @pl.when(step == 0)
    def _():
        _sample_out(olo_ref, ohi_ref, sgate_ref, sx1_ref, wo_ref, sy_ref)
        kext[...] = jnp.zeros(kext.shape, BF16)
        vt_ext[...] = jnp.zeros(vt_ext.shape, BF16)
        blk = 2 * LANES
        for src, dst, width in ((0, 0, ATTN_W), (ATTN_W + KV_W, ATTN_W, KV_W + ATTN_W)):
            for c in range(0, width, blk):
                wqvz_scr[dst + c:dst + c + blk, :] = win_ref[:, src + c:src + c + blk].T.astype(BF16)
        wkt_scr[...] = win_ref[:, ATTN_W:ATTN_W + KV_W].T.astype(BF16)
        for c in range(0, wo_ref.shape[1], blk):
            wout_scr[c:c + blk, :] = wo_ref[:, c:c + blk].T.astype(BF16)

    def run(cur, **halves):
        _odd_prompt_step(x_ref, xres_ref, g_ref, wkt_scr, wqvz_scr, qg_ref, kg_ref, ropeb_ref, ropeo_ref, sink_ref,
                         wout_scr, o_ref, knew_ref, vnew_ref, qt_scr, kext, vt_ext, gate_scr, *rest,
                         tm=tm, nt=nt, cur=cur, prev=1 - cur, **halves)

    pl.when(step == 0)(lambda: run(0, do_attn=False))
    for parity in range(2):
        pl.when((step % 2 == parity) & (step > 0) & (step < last))(lambda parity=parity: run(parity))
    pl.when(step == last)(lambda: run((n_steps - 1) % 2, do_proj=False))


def _odd_prompt_step(x_ref, xres_ref, g_ref, wkt_ref, wqvz_ref, qg_ref, kg_ref, ropeb_ref, ropeo_ref, sink_ref,
                     wout_ref, o_ref, knew_ref, vnew_ref,
                     qt_scr, kext, vt_ext, gate_scr, yt_scr, s_scr, p_scr, esink_scr, *, tm, nt, cur, prev,
                     do_proj=True, do_attn=True):
    step = pl.program_id(0)
    proj_t = step % nt
    attn_t = jnp.maximum(step - 1, 0) % nt
    nblk = tm // WINDOW
    half = HEAD_DIM // 2

    h = _rms(x_ref[0], g_ref[...]).astype(BF16) if do_proj else None
    reps = tm // LANES
    if do_proj:
        cos_b = jnp.concatenate([ropeb_ref[0, 0:half, :]] * reps, axis=1)
        sin_b = jnp.concatenate([ropeb_ref[0, half:HEAD_DIM, :]] * reps, axis=1)
        cos_o, sin_o = ropeo_ref[0:half, :], ropeo_ref[half:HEAD_DIM, :]
        cos_t = cos_b * cos_o - sin_b * sin_o
        sin_t = sin_b * cos_o + cos_b * sin_o

    def norm_rope(xt, gain):
        ms = jnp.sum(xt * xt, axis=0, keepdims=True) * (1.0 / HEAD_DIM)
        xn = xt * lax.rsqrt(ms + RMS_EPS) * gain
        x1, x2 = xn[0:half, :], xn[half:HEAD_DIM, :]
        return jnp.concatenate([x1 * cos_t - x2 * sin_t, x2 * cos_t + x1 * sin_t], axis=0)

    has_past = proj_t > 0

    def proj_kv():
        kt = _nt_dot(wkt_ref[...], h)
        kgain = jnp.concatenate([kg_ref[...]] * reps, axis=1)
        kt = jnp.concatenate(
            [norm_rope(kt[kv * HEAD_DIM:(kv + 1) * HEAD_DIM, :], kgain) for kv in range(N_KV_HEADS)], axis=0)
        knew_ref[0] = kt[:, tm - WINDOW:tm]
        kext[cur, 0:WINDOW, :] = jnp.where(has_past, kext[prev, tm:tm + WINDOW, :], jnp.zeros((WINDOW, KV_W), BF16))
        kext[cur, WINDOW:WINDOW + tm, :] = kt.T.astype(BF16)
        vt = _nt_dot(wqvz_ref[ATTN_W:ATTN_W + KV_W, :], h)
        vnew_ref[0] = vt[:, tm - WINDOW:tm]
        vt_ext[cur, :, 0:WINDOW] = jnp.where(has_past, vt_ext[prev, :, tm:tm + WINDOW],
                                             jnp.zeros((KV_W, WINDOW), BF16))
        vt_ext[cur, :, WINDOW:WINDOW + tm] = vt.astype(BF16)

    def proj_gate(lo, hi):
        zt = _nt_dot(wqvz_ref[ATTN_W + KV_W + lo:ATTN_W + KV_W + hi, :], h)
        gate_scr[cur, lo:hi, :] = _silu(zt)

    q_raw = {}

    def proj_q(lo, hi):
        q_raw[lo] = _nt_dot(wqvz_ref[lo:hi, :], h)

    def finish_q():
        qgain = jnp.concatenate([qg_ref[...]] * reps, axis=1)
        for lo, qt in q_raw.items():
            for r0 in range(0, qt.shape[0], HEAD_DIM):
                qt_scr[cur, lo + r0:lo + r0 + HEAD_DIM, :] = (
                    norm_rope(qt[r0:r0 + HEAD_DIM, :], qgain) * (LOG2_E * HEAD_DIM ** -0.5)).astype(BF16)
        q_raw.clear()

    pieces = [proj_kv,
              lambda: proj_gate(0, ATTN_W // 2),
              lambda: proj_gate(ATTN_W // 2, ATTN_W),
              lambda: proj_q(0, ATTN_W // 2)]
    last_piece = lambda: proj_q(ATTN_W // 2, ATTN_W)
    if not do_proj:
        pieces, last_piece = [], lambda: None
    if not do_attn:
        for piece in pieces:
            piece()
        finish_q()
        last_piece()
        finish_q()
        return

    ri = lax.broadcasted_iota(jnp.int32, (WINDOW, 2 * WINDOW), 0)
    qq = lax.broadcasted_iota(jnp.int32, (WINDOW, 2 * WINDOW), 1) % WINDOW
    from_prev = ri > qq
    keep_prev = from_prev.astype(BF16)
    keep_cur = 1.0 - keep_prev
    lane = lax.broadcasted_iota(jnp.int32, (1, 2 * WINDOW), 1)
    zeros = jnp.zeros((HEAD_DIM, 2 * WINDOW), BF16)
    ones = jnp.ones((16, 2 * WINDOW), BF16)

    for i in range(nblk):
        c0 = i * WINDOW
        qcols = slice(c0, c0 + WINDOW)
        pairs = [(kv, kv * GROUP + 2 * pr) for kv in range(N_KV_HEADS) for pr in range(GROUP // 2)]
        for j, (kv, ha) in enumerate(pairs):
            chunk, pos = divmod(kv, LANES // HEAD_DIM)
            kblk = kext[prev, c0:c0 + 2 * WINDOW, chunk * LANES:(chunk + 1) * LANES]
            qpair = jnp.concatenate([qt_scr[prev, ha * HEAD_DIM:(ha + 1) * HEAD_DIM, qcols],
                                     qt_scr[prev, (ha + 1) * HEAD_DIM:(ha + 2) * HEAD_DIM, qcols]], axis=1)
            rhs = jnp.concatenate([qpair, zeros] if pos == 0 else [zeros, qpair], axis=0)
            s = jnp.dot(kblk, rhs, preferred_element_type=F32)
            s_prev = s[0:WINDOW, :]
            if i == 0:
                s_prev = jnp.where(attn_t == 0, -jnp.inf, s_prev)
            s_scr[j] = jnp.where(from_prev, s_prev, s[WINDOW:2 * WINDOW, :])
        for c, piece in enumerate(pieces):
            if c * nblk // len(pieces) == i:
                piece()
        for j, (kv, ha) in enumerate(pairs):
            s = s_scr[j]
            sink = jnp.where(lane < WINDOW, sink_ref[ha], sink_ref[ha + 1]) * LOG2_E
            m = jnp.maximum(jnp.max(s, axis=0, keepdims=True), sink)
            p_scr[j] = jnp.exp2(s - m).astype(BF16)
            esink_scr[j:j + 1, :] = jnp.exp2(sink - m)
        for j, (kv, ha) in enumerate(pairs):
            vg = vt_ext[prev, kv * HEAD_DIM:(kv + 1) * HEAD_DIM, c0:c0 + 2 * WINDOW]
            p = p_scr[j]
            p_keys = jnp.concatenate([p * keep_prev, p * keep_cur], axis=0)
            oa = jnp.dot(jnp.concatenate([vg, ones], axis=0), p_keys, preferred_element_type=F32)
            denom = oa[HEAD_DIM:HEAD_DIM + 1, :] + esink_scr[j:j + 1, :]
            o = oa[0:HEAD_DIM, :] * (1.0 / denom)
            for hd, part in ((ha, o[:, 0:WINDOW]), (ha + 1, o[:, WINDOW:2 * WINDOW])):
                rows = slice(hd * HEAD_DIM, (hd + 1) * HEAD_DIM)
                yt_scr[rows, qcols] = (part * gate_scr[prev, rows, qcols]).astype(BF16)

    out_t = jnp.dot(wout_ref[...], yt_scr[...], preferred_element_type=F32)
    finish_q()
    last_piece()
    finish_q()
    o_ref[0] = xres_ref[0] + out_t.T


def _odd_prompt(x, ng, w_in, w_out, small, cos_t, sin_t, sinks, s_olo, s_ohi, s_gate, s_x1, *, tm):
    B, T, D = x.shape
    assert T % tm == 0 and tm % WINDOW == 0 and w_in.shape == (D, 2 * ATTN_W + 2 * KV_W) and w_out.shape == (ATTN_W, D)
    nt = T // tm
    half = HEAD_DIM // 2
    const = lambda *shape: pl.BlockSpec(shape, lambda s: (0,) * len(shape))
    n_tiles = B * nt
    proj_tile = lambda s: jnp.minimum(s, n_tiles - 1)
    attn_tile = lambda s: jnp.maximum(s - 1, 0)
    return pl.pallas_call(
        functools.partial(_odd_prompt_kernel, tm=tm, nt=nt, n_steps=n_tiles + 1),
        grid=(n_tiles + 1,),
        in_specs=[
            pl.BlockSpec((1, tm, D), lambda s: (proj_tile(s) // nt, proj_tile(s) % nt, 0)),
            pl.BlockSpec((1, tm, D), lambda s: (attn_tile(s) // nt, attn_tile(s) % nt, 0)),
            const(*ng.shape),
            const(*w_in.shape),
            const(*w_out.shape),
            const(*small.shape),
            pl.BlockSpec((1, HEAD_DIM, LANES), lambda s: (proj_tile(s) % nt, 0, 0)),
            const(HEAD_DIM, tm),
            pl.BlockSpec(memory_space=pltpu.SMEM),
            const(*s_olo.shape),
            const(*s_ohi.shape),
            const(*s_gate.shape),
            const(*s_x1.shape),
        ],
        out_specs=[
            pl.BlockSpec((1, tm, D), lambda s: (attn_tile(s) // nt, attn_tile(s) % nt, 0)),
            pl.BlockSpec((1, KV_W, WINDOW), lambda s: (proj_tile(s) // nt, 0, 0)),
            pl.BlockSpec((1, KV_W, WINDOW), lambda s: (proj_tile(s) // nt, 0, 0)),
            const(s_x1.size // LANES, LANES),
        ],
        out_shape=[
            jax.ShapeDtypeStruct((B, T, D), F32),
            jax.ShapeDtypeStruct((B, KV_W, WINDOW), F32),
            jax.ShapeDtypeStruct((B, KV_W, WINDOW), F32),
            jax.ShapeDtypeStruct((s_x1.size // LANES, LANES), F32),
        ],
        scratch_shapes=[
            pltpu.VMEM((KV_W, D), BF16),
            pltpu.VMEM((2 * ATTN_W + KV_W, D), BF16),
            pltpu.VMEM((D, ATTN_W), BF16),
            pltpu.VMEM((2, ATTN_W, tm), BF16),
            pltpu.VMEM((2, WINDOW + tm, KV_W), BF16),
            pltpu.VMEM((2, KV_W, WINDOW + tm), BF16),
            pltpu.VMEM((2, ATTN_W, tm), F32),
            pltpu.VMEM((ATTN_W, tm), BF16),
            pltpu.VMEM((N_HEADS // 2, WINDOW, 2 * WINDOW), F32),
            pltpu.VMEM((N_HEADS // 2, WINDOW, 2 * WINDOW), BF16),
            pltpu.VMEM((N_HEADS // 2, 2 * WINDOW), F32),
        ],
        compiler_params=pltpu.CompilerParams(
            dimension_semantics=("arbitrary",), vmem_limit_bytes=VMEM_LIMIT_BYTES),
        name="odd_prompt",
    )(x, x, ng, w_in, w_out, small, cos_t, sin_t, sinks, s_olo, s_ohi, s_gate, s_x1)


def _dup_heads(x):
    lane = lax.broadcasted_iota(jnp.int32, x.shape, 1)
    rolled = pltpu.roll(x, HEAD_DIM, 1)
    first = lane < HEAD_DIM
    return jnp.where(first, x, rolled), jnp.where(first, rolled, x)


def _sample_dense_kernel(x_ref, sp_ref, sc_ref, ng_ref, win0_ref, pw_ref, ps_ref, cw_ref, wout0_ref,
                         win1_ref, qgain_ref, kgain_ref, cos1_ref, sin1_ref,
                         x1_ref, pool_ref, conv_ref, qh_ref, kt_ref, vt_ref, gate_ref, small_ref,
                         *scratch):
    hbm = _HbmOperands(sp_ref, win0_ref, wout0_ref, win1_ref, pool_ref, *scratch)
    sp_ref, win0_ref, wout0_ref, win1_ref = hbm.sp, hbm.win0, hbm.wout0, hbm.win1
    w_pool = ps_ref.shape[1]
    w_conv = (win0_ref.shape[1] - 2 * w_pool) // 4
    half = HEAD_DIM // 2
    qg_t, cos_t1, sin_t1 = _col_table(qgain_ref[...]), _col_table(cos1_ref[...]), _col_table(sin1_ref[...])
    small_ref[_QG_T:_QG_T + HEAD_DIM, :] = qg_t
    small_ref[_KG_T:_KG_T + HEAD_DIM, :] = _col_table(kgain_ref[...])
    small_ref[_COS_T:_COS_T + half, :] = cos_t1
    small_ref[_SIN_T:_SIN_T + half, :] = sin_t1
    second_half = lax.broadcasted_iota(jnp.int32, (1, LANES), 1) % HEAD_DIM >= half
    small_ref[_QG_ROW:_SMALL_ROWS, :] = jnp.concatenate(
        [_tiled_row(qg_t), _tiled_row(cos_t1), jnp.where(second_half, 1.0, -1.0) * _tiled_row(sin_t1),
         jnp.zeros((_SMALL_ROWS - _SIN_ROW - 1, LANES), F32)], axis=0)
    g0_ref, g1_ref = ng_ref.at[0:1], ng_ref.at[1:2]
    qg_ref, kgt_ref = small_ref.at[_QG_ROW:_QG_ROW + 1], small_ref.at[_KG_T:_KG_T + HEAD_DIM]
    cos_ref, sin_ref = small_ref.at[_COS_ROW:_COS_ROW + 1], small_ref.at[_SIN_ROW:_SIN_ROW + 1]
    cost_ref, sint_ref = small_ref.at[_COS_T:_COS_T + HEAD_DIM // 2], small_ref.at[_SIN_T:_SIN_T + HEAD_DIM // 2]

    n = x1_ref.shape[0]
    chunks = x1_ref.shape[1] // LANES
    x = jnp.concatenate([x_ref[pl.ds(c, n, stride=chunks), :] for c in range(chunks)], axis=1)
    h = _rms(x, g0_ref[...]).astype(win0_ref.dtype)

    def proj(c, width):
        return jnp.dot(h, win0_ref[:, c:c + width], preferred_element_type=F32)

    hbm.wait_win0()
    u = proj(0, w_pool)
    hbm.put_newest_pool_row(u)
    z_a = proj(w_pool, w_pool)
    hbm.wait_sp()
    ys = []
    for g, w in enumerate(POOL_WINDOWS):
        cols = slice(g * POOL_GROUP, (g + 1) * POOL_GROUP)
        wsum = u[:, cols]
        for r in range(POOL_BUF - (w - 1), POOL_BUF):
            wsum = wsum + sp_ref[r, :, cols]
        cnt = float(min(PAST_LEN + 1, w))
        d = wsum * (1.0 / cnt) - u[:, cols]
        ya = jnp.dot(d.astype(pw_ref.dtype), pw_ref[g], preferred_element_type=F32)
        ys.append((ya * ps_ref[:, cols] * _silu(z_a[:, cols])).astype(wout0_ref.dtype))

    b_gate = proj(2 * w_pool, w_conv)
    c_gate = proj(2 * w_pool + w_conv, w_conv)
    v_in = proj(2 * w_pool + 2 * w_conv, w_conv)
    v = c_gate * v_in
    cchunks = w_conv // LANES
    cstride = cchunks * CONV_BUF

    def conv_row(r):
        return jnp.concatenate(
            [sc_ref[pl.ds(j * CONV_BUF + r, n, stride=cstride), :] for j in range(cchunks)], axis=1)

    def put_conv_row(r, val):
        for j in range(cchunks):
            conv_ref[pl.ds(j * CONV_BUF + r, n, stride=cstride), :] = val[:, j * LANES:(j + 1) * LANES]

    conv = v * _conv_tap(cw_ref, CONV_WIDTH - 1, w_conv)
    for k in range(CONV_BUF):
        conv = conv + conv_row(k) * _conv_tap(cw_ref, k, w_conv)
    for r in range(CONV_BUF - 1):
        put_conv_row(r, conv_row(r + 1))
    put_conv_row(CONV_BUF - 1, v)
    z_b = proj(2 * w_pool + 3 * w_conv, w_conv)
    ys.append((b_gate * conv * _silu(z_b)).astype(wout0_ref.dtype))
    y = jnp.concatenate(ys, axis=1)
    hbm.wait_wout0()
    x1 = x + jnp.dot(y, wout0_ref[...], preferred_element_type=F32)
    x1_ref[...] = x1

    h1 = _rms(x1, g1_ref[...])
    hbm.wait_win1()

    def proj1(lo, width):
        return jnp.dot(h1, win1_ref[:, lo:lo + width], preferred_element_type=F32)

    q = proj1(0, ATTN_W)
    for c in range(ATTN_W // LANES):
        qc = _head_norm_rope(q[:, c * LANES:(c + 1) * LANES], qg_ref[...], cos_ref[...], sin_ref[...])
        d0, d1 = _dup_heads(qc * (HEAD_DIM ** -0.5))
        qh_ref[2 * c * n:(2 * c + 1) * n, :] = d0
        qh_ref[(2 * c + 1) * n:(2 * c + 2) * n, :] = d1
    kt = proj1(ATTN_W, KV_W).T
    cos_t = cost_ref[...]
    sin_t = sint_ref[...]
    for kv in range(N_KV_HEADS):
        kh = kt[kv * HEAD_DIM:(kv + 1) * HEAD_DIM, :]
        ms = jnp.sum(kh * kh, axis=0, keepdims=True) * (1.0 / HEAD_DIM)
        kn = kh * lax.rsqrt(ms + RMS_EPS) * kgt_ref[...]
        x1h, x2h = kn[0:half, :], kn[half:HEAD_DIM, :]
        kt_ref[kv * HEAD_DIM:(kv + 1) * HEAD_DIM, :] = jnp.concatenate(
            [x1h * cos_t - x2h * sin_t, x2h * cos_t + x1h * sin_t], axis=0)
    vt_ref[...] = proj1(ATTN_W + KV_W, KV_W).T
    gate_ref[...] = _silu(proj1(ATTN_W + 2 * KV_W, ATTN_W))
    hbm.finish()


class _HbmOperands:
    def __init__(self, sp_hbm, win0_hbm, wout0_hbm, win1_hbm, pool_hbm, sp, win0, wout0, win1, newest, sems):
        self.sp, self.win0, self.wout0, self.win1, self.newest = sp, win0, wout0, win1, newest
        self.pool_hbm = pool_hbm
        self.sems = sems
        rows = sp_hbm.shape[0]
        self.shift = pltpu.make_async_copy(sp.at[1:rows], pool_hbm.at[0:rows - 1], sems.at[0])
        self.fetch = {name: pltpu.make_async_copy(src, dst, sems.at[i + 1]) for i, (name, src, dst) in enumerate(
            (("win0", win0_hbm, win0), ("sp", sp_hbm, sp), ("wout0", wout0_hbm, wout0), ("win1", win1_hbm, win1)))}
        for copy in self.fetch.values():
            copy.start()

    def wait_win0(self):
        self.fetch["win0"].wait()

    def wait_sp(self):
        self.fetch["sp"].wait()
        self.shift.start()

    def wait_wout0(self):
        self.fetch["wout0"].wait()

    def wait_win1(self):
        self.fetch["win1"].wait()

    def put_newest_pool_row(self, u):
        rows = self.pool_hbm.shape[0]
        self.newest[...] = u
        self.put = pltpu.make_async_copy(self.newest, self.pool_hbm.at[rows - 1], self.sems.at[5])
        self.put.start()

    def finish(self):
        self.shift.wait()
        self.put.wait()


def _sample_dense(x, sp, sc, ng, win0, pw, ps, cw, wout0, win1, qgain, kgain, cos1, sin1):
    D = win0.shape[0]
    n = x.shape[0] * LANES // D
    assert n == LANES
    assert qgain.shape == kgain.shape == (1, HEAD_DIM) and cos1.shape == sin1.shape == (1, HEAD_DIM // 2)
    vmem = pl.BlockSpec(memory_space=pltpu.VMEM)
    hbm = pl.BlockSpec(memory_space=pl.ANY)
    in_specs = [vmem, hbm, vmem, vmem, hbm, vmem, vmem, vmem, hbm, hbm] + [vmem] * 4
    return pl.pallas_call(
        _sample_dense_kernel,
        in_specs=in_specs,
        out_specs=[vmem, hbm] + [vmem] * 6,
        scratch_shapes=[
            pltpu.VMEM(sp.shape, F32),
            pltpu.VMEM(win0.shape, F32),
            pltpu.VMEM(wout0.shape, F32),
            pltpu.VMEM(win1.shape, F32),
            pltpu.VMEM(sp.shape[1:], F32),
            pltpu.SemaphoreType.DMA((6,)),
        ],
        out_shape=[
            jax.ShapeDtypeStruct((n, D), F32),
            jax.ShapeDtypeStruct(sp.shape, F32),
            jax.ShapeDtypeStruct(sc.shape, F32),
            jax.ShapeDtypeStruct((N_HEADS * n, LANES), F32),
            jax.ShapeDtypeStruct((KV_W, n), F32),
            jax.ShapeDtypeStruct((KV_W, n), F32),
            jax.ShapeDtypeStruct((n, ATTN_W), F32),
            jax.ShapeDtypeStruct((_SMALL_ROWS, LANES), F32),
        ],
        compiler_params=pltpu.CompilerParams(vmem_limit_bytes=VMEM_LIMIT_BYTES),
        name="sample_dense",
    )(x, sp, sc, ng, win0, pw, ps, cw, wout0, win1, qgain, kgain, cos1, sin1)


def _sample_out(olo_ref, ohi_ref, gate_ref, x1_ref, wout_ref, y_ref):
    n = x1_ref.shape[0]
    blocks = LANES // HEAD_DIM
    lane_n = lax.broadcasted_iota(jnp.int32, (n, LANES), 1)
    chunks = []
    for c in range(ATTN_W // LANES):
        parts = []
        for hd in (2 * c, 2 * c + 1):
            kv = hd // GROUP
            slab = (olo_ref, ohi_ref)[kv // blocks]
            a = slab[pl.ds(hd, n, stride=N_HEADS), :]
            parts.append(a if kv % blocks == hd % blocks else pltpu.roll(a, HEAD_DIM, 1))
        chunks.append(jnp.where(lane_n < HEAD_DIM, parts[0], parts[1]))
    y = jnp.concatenate(chunks, axis=1) * gate_ref[...]
    out = x1_ref[...] + jnp.dot(y, wout_ref[...], preferred_element_type=F32)
    chunks = out.shape[1] // LANES
    for c in range(chunks):
        y_ref[pl.ds(c, n, stride=chunks), :] = out[:, c * LANES:(c + 1) * LANES]


def _rope_tables(pos):
    half = HEAD_DIM // 2
    inv = ROPE_THETA ** (-jnp.arange(half, dtype=F32) / half)
    ang = pos.astype(F32)[:, None] * inv[None, :]
    return jnp.cos(ang), jnp.sin(ang)


def kernel(x_prompt, x_sample, state_pool, state_conv, cache_k, cache_v, norm_g, w_in_even, pool_w, pool_scale,
           conv_w, w_out_even, w_in_odd, q_norm_g, k_norm_g, attn_sinks, w_out_odd):
    B, T, D = x_prompt.shape
    n_s, t_s, _ = x_sample.shape
    assert norm_g.shape[0] == 2 and w_in_even.shape[0] == 1 and w_in_odd.shape[0] == 1
    assert t_s == 1 and cache_k.shape[2] == WINDOW and T >= WINDOW
    assert cache_k.shape[3] * cache_k.shape[4] == KV_W and pool_w.shape[1:] == (len(POOL_WINDOWS), POOL_GROUP, POOL_GROUP)

    win0 = w_in_even[0]
    wout0 = w_out_even[0]
    win1 = w_in_odd[0]
    wout1 = w_out_odd[0]
    pw = pool_w[0]
    ps = pool_scale[0][None, :]
    cw = conv_w[0].reshape(-1, LANES)
    sinks = attn_sinks[0]

    tm_odd = 512
    cos_b, sin_b = _rope_tables(jnp.arange(T // tm_odd) * tm_odd)
    rope_base = jnp.broadcast_to(jnp.concatenate([cos_b, sin_b], axis=1)[:, :, None], (T // tm_odd, HEAD_DIM, LANES))
    cos_o, sin_o = _rope_tables(jnp.arange(tm_odd))
    rope_off = jnp.concatenate([cos_o, sin_o], axis=1).T
    cos_1, sin_1 = _rope_tables(PAST_LEN + jnp.arange(t_s))

    w_conv = state_conv.shape[-1]
    sp = jnp.transpose(state_pool[0], (1, 0, 2))
    sc = jnp.transpose(state_conv[0].reshape(n_s, CONV_BUF, w_conv // LANES, LANES), (0, 2, 1, 3)).reshape(-1, LANES)
    x1s, pool_s, conv_s, qh_s, kt_s, vt_s, gate_s, small = _sample_dense(
        x_sample.reshape(n_s * D // LANES, LANES), sp, sc, norm_g, win0, pw, ps, cw, wout0, win1,
        q_norm_g, k_norm_g, cos_1, sin_1)
    pool_s = jnp.transpose(pool_s, (1, 0, 2))
    conv_s = jnp.transpose(conv_s.reshape(n_s, w_conv // LANES, CONV_BUF, LANES), (0, 2, 1, 3)).reshape(
        n_s, CONV_BUF, w_conv)

    ckt = jnp.transpose(cache_k[0], (0, 2, 3, 1)).reshape(n_s, KV_W, WINDOW)
    cvt = jnp.transpose(cache_v[0], (0, 2, 3, 1)).reshape(n_s, KV_W, WINDOW)
    x1p, pool_p, conv_p, nkt_s, nvt_s, olo_s, ohi_s = _even_prompt(
        x_prompt, norm_g, win0, pw, ps, cw, wout0, qh_s, kt_s, vt_s, ckt, cvt, attn_sinks, tm=1024)
    y_p, k_p, v_p, y_s = _odd_prompt(x1p, norm_g, win1, wout1, small, rope_base, rope_off, sinks,
                                     olo_s, ohi_s, gate_s, x1s, tm=tm_odd)

    def window_major(a):
        return jnp.transpose(a.reshape(-1, N_KV_HEADS, HEAD_DIM, WINDOW), (0, 3, 1, 2))[None]

    return (y_p, y_s.reshape(n_s, 1, D), pool_p[None], pool_s[None], conv_p[None], conv_s[None],
            window_major(k_p), window_major(v_p), window_major(nkt_s), window_major(nvt_s))
```

```python
import functools

import jax
import jax.numpy as jnp
from jax import lax
from jax.experimental import pallas as pl
from jax.experimental.pallas import tpu as pltpu

F32 = jnp.float32
BF16 = jnp.bfloat16

POOL_WINDOWS = (2, 4, 8, 16)
POOL_GROUP = 128
POOL_BUF = max(POOL_WINDOWS) - 1
CONV_WIDTH = 3
CONV_BUF = CONV_WIDTH - 1
N_HEADS = 16
HEAD_DIM = 64
N_KV_HEADS = 4
GROUP = N_HEADS // N_KV_HEADS
WINDOW = 128
ROPE_THETA = 10000.0
RMS_EPS = 1e-6
PAST_LEN = 16384
ATTN_W = N_HEADS * HEAD_DIM
KV_W = N_KV_HEADS * HEAD_DIM

LANES = 128
POOL_HALO = 16
CONV_HALO = 8
VMEM_LIMIT_BYTES = 60 * 1024 * 1024
EVEN_TILE = 1024
ODD_TILE = 512
BF16_SUBLANES = 16
LOG2_E = 1.4426950408889634
_QG_T, _KG_T, _COS_T, _SIN_T = 0, HEAD_DIM, 2 * HEAD_DIM, 2 * HEAD_DIM + HEAD_DIM // 2
_QG_ROW, _COS_ROW, _SIN_ROW, _SMALL_ROWS = 3 * HEAD_DIM, 3 * HEAD_DIM + 1, 3 * HEAD_DIM + 2, 3 * HEAD_DIM + 8


def _rms(x, g):
    ms = jnp.mean(x * x, axis=-1, keepdims=True)
    return x * lax.rsqrt(ms + RMS_EPS) * g


def _silu(z):
    return z * jax.nn.sigmoid(z)


def _head_norm_rope(x, gain, cos, sin):
    lane = lax.broadcasted_iota(jnp.int32, x.shape, 1)
    first = lane < HEAD_DIM
    x2 = x * x
    ss0 = jnp.sum(jnp.where(first, x2, 0.0), axis=-1, keepdims=True)
    ss1 = jnp.sum(jnp.where(first, 0.0, x2), axis=-1, keepdims=True)
    r = jnp.where(first, lax.rsqrt(ss0 * (1.0 / HEAD_DIM) + RMS_EPS), lax.rsqrt(ss1 * (1.0 / HEAD_DIM) + RMS_EPS))
    xn = x * r * gain
    half = HEAD_DIM // 2
    swapped = jnp.where((lane % HEAD_DIM) < half, pltpu.roll(xn, LANES - half, 1), pltpu.roll(xn, half, 1))
    return xn * cos + swapped * sin


def _nt_dot(a, b):
    return lax.dot_general(a, b, (((1,), (1,)), ((), ())), preferred_element_type=F32)


def _col_table(row):
    m = row.shape[1]
    diag = lax.broadcasted_iota(jnp.int32, (m, m), 0) == lax.broadcasted_iota(jnp.int32, (m, m), 1)
    col = jnp.sum(jnp.where(diag, jnp.broadcast_to(row, (m, m)), 0.0), axis=1, keepdims=True)
    return jnp.broadcast_to(col, (m, LANES))


def _tiled_row(table):
    m = table.shape[0]
    pick = lax.broadcasted_iota(jnp.int32, table.shape, 0) == lax.broadcasted_iota(jnp.int32, table.shape, 1) % m
    return jnp.sum(jnp.where(pick, table, 0.0), axis=0, keepdims=True)


def _conv_tap(cw_ref, k, width):
    chunks = width // LANES
    return jnp.concatenate([cw_ref[k * chunks + j:k * chunks + j + 1, :] for j in range(chunks)], axis=1)


class _SampleAttention:
    def __init__(self, qh_ref, knt_ref, vnt_ref, ck_ref, cv_ref, sink_ref, nk_ref, nv_ref, olo_ref, ohi_ref, *, step):
        self.refs = (qh_ref, knt_ref, vnt_ref, ck_ref, cv_ref, sink_ref, nk_ref, nv_ref, olo_ref, ohi_ref)
        self.bb = ck_ref.shape[0]
        self.first = step * self.bb

    def roll(self):
        _, knt_ref, vnt_ref, ck_ref, cv_ref, _, nk_ref, nv_ref, _, _ = self.refs
        win = ck_ref.shape[2]
        newest = lax.broadcasted_iota(jnp.int32, (KV_W, win), 1) == win - 1
        for i in range(self.bb):
            b = self.first + i
            nk_ref[i] = jnp.where(newest, pltpu.roll(knt_ref[...], win - 1 - b, 1), pltpu.roll(ck_ref[i], win - 1, 1))
            nv_ref[i] = jnp.where(newest, pltpu.roll(vnt_ref[...], win - 1 - b, 1), pltpu.roll(cv_ref[i], win - 1, 1))

    def scores(self):
        qh_ref, nk_ref = self.refs[0], self.refs[6]
        n = qh_ref.shape[0] // N_HEADS
        blocks = LANES // HEAD_DIM
        hrow = lax.broadcasted_iota(jnp.int32, (N_HEADS, LANES), 0) // GROUP
        hcol = lax.broadcasted_iota(jnp.int32, (N_HEADS, LANES), 1) // HEAD_DIM
        self.s = []
        for i in range(self.bb):
            qd = qh_ref[pl.ds(self.first + i, N_HEADS, stride=n), :]
            qx = jnp.concatenate(
                [jnp.where(hcol + blocks * c == hrow, qd, 0.0) for c in range(KV_W // LANES)], axis=1)
            self.s.append(jnp.dot(qx.astype(BF16), nk_ref[i].astype(BF16), preferred_element_type=F32))

    def softmax(self):
        sink = _col_table(self.refs[5][...])[:, 0:1]
        self.p = []
        for s in self.s:
            m = jnp.maximum(jnp.max(s, axis=-1, keepdims=True), sink)
            p = jnp.exp(s - m)
            denom = jnp.sum(p, axis=-1, keepdims=True) + jnp.exp(sink - m)
            self.p.append((p * (1.0 / denom)).astype(BF16))

    def outputs(self):
        nv_ref, olo_ref, ohi_ref = self.refs[7:10]
        for i in range(self.bb):
            o = _nt_dot(self.p[i], nv_ref[i].astype(BF16))
            rows = pl.ds(pl.multiple_of((self.first + i) * N_HEADS, N_HEADS), N_HEADS)
            olo_ref[rows, :] = o[:, 0:LANES]
            ohi_ref[rows, :] = o[:, LANES:2 * LANES]


def _even_prompt_kernel(x_ref, g_ref, win_ref, pw_ref, ps_ref, cw_ref, wout_ref,
                        qh_ref, knt_ref, vnt_ref, ck_ref, cv_ref, sink_ref,
                        o_ref, pool_ref, conv_ref, nk_ref, nv_ref, olo_ref, ohi_ref,
                        uext, vext, sa, sb, y_scr, *, tm):
    t = pl.program_id(1)
    w_pool = uext.shape[1]
    w_conv = vext.shape[1]
    base = 2 * POOL_HALO
    side = _SampleAttention(qh_ref, knt_ref, vnt_ref, ck_ref, cv_ref, sink_ref, nk_ref, nv_ref, olo_ref, ohi_ref,
                            step=pl.program_id(0) * pl.num_programs(1) + t)

    @pl.when(t == 0)
    def _():
        uext[0:base, :] = jnp.zeros((base, w_pool), F32)
        vext[0:CONV_HALO, :] = jnp.zeros((CONV_HALO, w_conv), F32)
        sa[0:POOL_HALO, :] = jnp.zeros((POOL_HALO, POOL_GROUP), F32)
        sb[0:POOL_HALO, :] = jnp.zeros((POOL_HALO, POOL_GROUP), F32)

    x = x_ref[0]
    h = _rms(x, g_ref[0:1, :]).astype(win_ref.dtype)

    def proj(c, width):
        return jnp.dot(h, win_ref[:, c:c + width], preferred_element_type=F32)

    side.roll()
    u = proj(0, w_pool)
    uext[base:base + tm, :] = u
    pos = t * tm + lax.broadcasted_iota(jnp.int32, (tm, 1), 0)
    side.scores()
    z_a = proj(w_pool, w_pool)
    n_ext = POOL_HALO + tm

    for g, w in enumerate(POOL_WINDOWS):
        lo = g * POOL_GROUP
        cols = slice(lo, lo + POOL_GROUP)
        src = uext
        src_cols = cols
        step = 1
        bufs = (sa, sb)
        nbuf = 0
        while 2 * step < w:
            dst = bufs[nbuf % 2]
            dst[POOL_HALO:POOL_HALO + n_ext, :] = (src[POOL_HALO:POOL_HALO + n_ext, src_cols]
                                                   + src[POOL_HALO - step:POOL_HALO - step + n_ext, src_cols])
            src, src_cols = dst, slice(0, POOL_GROUP)
            step *= 2
            nbuf += 1
        wsum = src[base:base + tm, src_cols] + src[base - step:base - step + tm, src_cols]
        cnt = jnp.minimum(pos + 1, w).astype(F32)
        d = wsum * (1.0 / cnt) - u[:, cols]
        ya = jnp.dot(d.astype(pw_ref.dtype), pw_ref[g], preferred_element_type=F32)
        ya = ya * ps_ref[:, cols] * _silu(z_a[:, cols])
        y_scr[:, cols] = ya.astype(y_scr.dtype)

    b_gate = proj(2 * w_pool, w_conv)
    side.softmax()
    c_gate = proj(2 * w_pool + w_conv, w_conv)
    side.outputs()
    v_in = proj(2 * w_pool + 2 * w_conv, w_conv)
    v = c_gate * v_in
    vext[CONV_HALO:CONV_HALO + tm, :] = v
    conv = v * _conv_tap(cw_ref, CONV_WIDTH - 1, w_conv)
    for k in range(CONV_WIDTH - 1):
        shift = CONV_WIDTH - 1 - k
        conv = conv + vext[CONV_HALO - shift:CONV_HALO - shift + tm, :] * _conv_tap(cw_ref, k, w_conv)
    z_b = proj(2 * w_pool + 3 * w_conv, w_conv)
    y_b = b_gate * conv * _silu(z_b)
    y_scr[:, w_pool:w_pool + w_conv] = y_b.astype(y_scr.dtype)

    o_ref[0] = x + jnp.dot(y_scr[...], wout_ref[...], preferred_element_type=F32)
    pool_ref[0] = uext[base + tm - POOL_BUF:base + tm, :]
    conv_ref[0] = vext[CONV_HALO + tm - CONV_BUF:CONV_HALO + tm, :]
    uext[POOL_HALO:base, :] = uext[POOL_HALO + tm:base + tm, :]
    vext[0:CONV_HALO, :] = vext[tm:tm + CONV_HALO, :]


def _even_prompt(x, g, win, pw, ps, cw, wout, qh, knt, vnt, ckt, cvt, sink_col, *, tm):
    B, T, D = x.shape
    w_pool = ps.shape[-1]
    w_conv = (win.shape[1] - 2 * w_pool) // 4
    assert cw.shape == (CONV_WIDTH * w_conv // LANES, LANES)
    assert T % tm == 0 and tm % 16 == 0 and tm >= POOL_HALO
    nt = T // tm
    n, kvw, win_len = ckt.shape
    assert n % (B * nt) == 0 and kvw == KV_W == 2 * LANES and win_len == LANES and n == LANES
    bb = n // (B * nt)
    const = lambda *shape: pl.BlockSpec(shape, lambda b, t: (0,) * len(shape))
    cache_block = pl.BlockSpec((bb, kvw, win_len), lambda b, t: (b * nt + t, 0, 0))
    return pl.pallas_call(
        functools.partial(_even_prompt_kernel, tm=tm),
        grid=(B, nt),
        in_specs=[
            pl.BlockSpec((1, tm, D), lambda b, t: (b, t, 0)),
            const(*g.shape),
            const(*win.shape),
            const(*pw.shape),
            const(1, w_pool),
            const(*cw.shape),
            const(*wout.shape),
            const(*qh.shape),
            const(*knt.shape),
            const(*vnt.shape),
            cache_block,
            cache_block,
            const(*sink_col.shape),
        ],
        out_specs=[
            pl.BlockSpec((1, tm, D), lambda b, t: (b, t, 0)),
            pl.BlockSpec((1, POOL_BUF, w_pool), lambda b, t: (b, 0, 0)),
            pl.BlockSpec((1, CONV_BUF, w_conv), lambda b, t: (b, 0, 0)),
            cache_block,
            cache_block,
            const(n * N_HEADS, LANES),
            const(n * N_HEADS, LANES),
        ],
        out_shape=[
            jax.ShapeDtypeStruct((B, T, D), F32),
            jax.ShapeDtypeStruct((B, POOL_BUF, w_pool), F32),
            jax.ShapeDtypeStruct((B, CONV_BUF, w_conv), F32),
            jax.ShapeDtypeStruct((n, kvw, win_len), F32),
            jax.ShapeDtypeStruct((n, kvw, win_len), F32),
            jax.ShapeDtypeStruct((n * N_HEADS, LANES), F32),
            jax.ShapeDtypeStruct((n * N_HEADS, LANES), F32),
        ],
        scratch_shapes=[
            pltpu.VMEM((2 * POOL_HALO + tm, w_pool), F32),
            pltpu.VMEM((CONV_HALO + tm, w_conv), F32),
            pltpu.VMEM((2 * POOL_HALO + tm, POOL_GROUP), F32),
            pltpu.VMEM((2 * POOL_HALO + tm, POOL_GROUP), F32),
            pltpu.VMEM((tm, w_pool + w_conv), wout.dtype),
        ],
        compiler_params=pltpu.CompilerParams(
            dimension_semantics=("arbitrary", "arbitrary"), vmem_limit_bytes=VMEM_LIMIT_BYTES),
        name="even_prompt",
    )(x, g, win, pw, ps, cw, wout, qh, knt, vnt, ckt, cvt, sink_col)


def _odd_prompt_kernel(*refs, tm, nt, n_steps):
    (x_ref, xres_ref, ng_ref, win_ref, wo_ref, small_ref, ropeb_ref, ropeo_ref, sink_ref,
     olo_ref, ohi_ref, sgate_ref, sx1_ref,
     o_ref, knew_ref, vnew_ref, sy_ref, wkt_scr, wqvz_scr, wout_scr, qt_scr, kext, vt_ext, gate_scr, *rest) = refs
    g_ref = ng_ref.at[1:2]
    qg_ref, kg_ref = small_ref.at[_QG_T:_QG_T + HEAD_DIM], small_ref.at[_KG_T:_KG_T + HEAD_DIM]

    step = pl.program_id(0)
    last = n_steps - 1

    @pl.when(step == 0)
    def _():
        _sample_out(olo_ref, ohi_ref, sgate_ref, sx1_ref, wo_ref, sy_ref)
        kext[...] = jnp.zeros(kext.shape, BF16)
        vt_ext[...] = jnp.zeros(vt_ext.shape, BF16)
        blk = 2 * LANES
        for src, dst, width in ((0, 0, ATTN_W), (ATTN_W + KV_W, ATTN_W, KV_W + ATTN_W)):
            for c in range(0, width, blk):
                wqvz_scr[dst + c:dst + c + blk, :] = win_ref[:, src + c:src + c + blk].T.astype(BF16)
        wkt_scr[...] = win_ref[:, ATTN_W:ATTN_W + KV_W].T.astype(BF16)
        for c in range(0, wo_ref.shape[1], blk):
            wout_scr[c:c + blk, :] = wo_ref[:, c:c + blk].T.astype(BF16)

    def run(cur, **halves):
        _odd_prompt_step(x_ref, xres_ref, g_ref, wkt_scr, wqvz_scr, qg_ref, kg_ref, ropeb_ref, ropeo_ref, sink_ref,
                         wout_scr, o_ref, knew_ref, vnew_ref, qt_scr, kext, vt_ext, gate_scr, *rest,
                         tm=tm, nt=nt, cur=cur, prev=1 - cur, **halves)

    pl.when(step == 0)(lambda: run(0, do_attn=False))
    for parity in range(2):
        pl.when((step % 2 == parity) & (step > 0) & (step < last))(lambda parity=parity: run(parity))
    pl.when(step == last)(lambda: run((n_steps - 1) % 2, do_proj=False))


def _odd_prompt_step(x_ref, xres_ref, g_ref, wkt_ref, wqvz_ref, qg_ref, kg_ref, ropeb_ref, ropeo_ref, sink_ref,
                     wout_ref, o_ref, knew_ref, vnew_ref,
                     qt_scr, kext, vt_ext, gate_scr, yt_scr, s_scr, p_scr, esink_scr, *, tm, nt, cur, prev,
                     do_proj=True, do_attn=True):
    step = pl.program_id(0)
    proj_t = step % nt
    attn_t = jnp.maximum(step - 1, 0) % nt
    nblk = tm // WINDOW
    half = HEAD_DIM // 2

    h = _rms(x_ref[0], g_ref[...]).astype(BF16) if do_proj else None
    reps = tm // LANES
    if do_proj:
        cos_b = jnp.concatenate([ropeb_ref[0, 0:half, :]] * reps, axis=1)
        sin_b = jnp.concatenate([ropeb_ref[0, half:HEAD_DIM, :]] * reps, axis=1)
        cos_o, sin_o = ropeo_ref[0:half, :], ropeo_ref[half:HEAD_DIM, :]
        cos_t = cos_b * cos_o - sin_b * sin_o
        sin_t = sin_b * cos_o + cos_b * sin_o

    def norm_rope(xt, gain):
        ms = jnp.sum(xt * xt, axis=0, keepdims=True) * (1.0 / HEAD_DIM)
        xn = xt * lax.rsqrt(ms + RMS_EPS) * gain
        x1, x2 = xn[0:half, :], xn[half:HEAD_DIM, :]
        return jnp.concatenate([x1 * cos_t - x2 * sin_t, x2 * cos_t + x1 * sin_t], axis=0)

    has_past = proj_t > 0

    def proj_kv():
        kt = _nt_dot(wkt_ref[...], h)
        kgain = jnp.concatenate([kg_ref[...]] * reps, axis=1)
        kt = jnp.concatenate(
            [norm_rope(kt[kv * HEAD_DIM:(kv + 1) * HEAD_DIM, :], kgain) for kv in range(N_KV_HEADS)], axis=0)
        knew_ref[0] = kt[:, tm - WINDOW:tm]
        kext[cur, 0:WINDOW, :] = jnp.where(has_past, kext[prev, tm:tm + WINDOW, :], jnp.zeros((WINDOW, KV_W), BF16))
        kext[cur, WINDOW:WINDOW + tm, :] = kt.T.astype(BF16)
        vt = _nt_dot(wqvz_ref[ATTN_W:ATTN_W + KV_W, :], h)
        vnew_ref[0] = vt[:, tm - WINDOW:tm]
        vt_ext[cur, :, 0:WINDOW] = jnp.where(has_past, vt_ext[prev, :, tm:tm + WINDOW],
                                             jnp.zeros((KV_W, WINDOW), BF16))
        vt_ext[cur, :, WINDOW:WINDOW + tm] = vt.astype(BF16)

    def proj_gate(lo, hi):
        zt = _nt_dot(wqvz_ref[ATTN_W + KV_W + lo:ATTN_W + KV_W + hi, :], h)
        gate_scr[cur, lo:hi, :] = _silu(zt)

    q_raw = {}

    def proj_q(lo, hi):
        q_raw[lo] = _nt_dot(wqvz_ref[lo:hi, :], h)

    def finish_q():
        qgain = jnp.concatenate([qg_ref[...]] * reps, axis=1)
        for lo, qt in q_raw.items():
            for r0 in range(0, qt.shape[0], HEAD_DIM):
                qt_scr[cur, lo + r0:lo + r0 + HEAD_DIM, :] = (
                    norm_rope(qt[r0:r0 + HEAD_DIM, :], qgain) * (LOG2_E * HEAD_DIM ** -0.5)).astype(BF16)
        q_raw.clear()

    pieces = [proj_kv,
              lambda: proj_gate(0, ATTN_W // 2),
              lambda: proj_gate(ATTN_W // 2, ATTN_W),
              lambda: proj_q(0, ATTN_W // 2)]
    last_piece = lambda: proj_q(ATTN_W // 2, ATTN_W)
    if not do_proj:
        pieces, last_piece = [], lambda: None
    if not do_attn:
        for piece in pieces:
            piece()
        finish_q()
        last_piece()
        finish_q()
        return

    ri = lax.broadcasted_iota(jnp.int32, (WINDOW, 2 * WINDOW), 0)
    qq = lax.broadcasted_iota(jnp.int32, (WINDOW, 2 * WINDOW), 1) % WINDOW
    from_prev = ri > qq
    keep_prev = from_prev.astype(BF16)
    keep_cur = 1.0 - keep_prev
    lane = lax.broadcasted_iota(jnp.int32, (1, 2 * WINDOW), 1)
    zeros = jnp.zeros((HEAD_DIM, 2 * WINDOW), BF16)
    ones = jnp.ones((BF16_SUBLANES, 2 * WINDOW), BF16)

    for i in range(nblk):
        c0 = i * WINDOW
        qcols = slice(c0, c0 + WINDOW)
        pairs = [(kv, kv * GROUP + 2 * pr) for kv in range(N_KV_HEADS) for pr in range(GROUP // 2)]
        for j, (kv, ha) in enumerate(pairs):
            chunk, pos = divmod(kv, LANES // HEAD_DIM)
            kblk = kext[prev, c0:c0 + 2 * WINDOW, chunk * LANES:(chunk + 1) * LANES]
            qpair = jnp.concatenate([qt_scr[prev, ha * HEAD_DIM:(ha + 1) * HEAD_DIM, qcols],
                                     qt_scr[prev, (ha + 1) * HEAD_DIM:(ha + 2) * HEAD_DIM, qcols]], axis=1)
            rhs = jnp.concatenate([qpair, zeros] if pos == 0 else [zeros, qpair], axis=0)
            s = jnp.dot(kblk, rhs, preferred_element_type=F32)
            s_prev = s[0:WINDOW, :]
            if i == 0:
                s_prev = jnp.where(attn_t == 0, -jnp.inf, s_prev)
            s_scr[j] = jnp.where(from_prev, s_prev, s[WINDOW:2 * WINDOW, :])
        for c, piece in enumerate(pieces):
            if c * nblk // len(pieces) == i:
                piece()
        for j, (kv, ha) in enumerate(pairs):
            s = s_scr[j]
            sink = jnp.where(lane < WINDOW, sink_ref[ha], sink_ref[ha + 1]) * LOG2_E
            m = jnp.maximum(jnp.max(s, axis=0, keepdims=True), sink)
            p_scr[j] = jnp.exp2(s - m).astype(BF16)
            esink_scr[j:j + 1, :] = jnp.exp2(sink - m)
        for j, (kv, ha) in enumerate(pairs):
            vg = vt_ext[prev, kv * HEAD_DIM:(kv + 1) * HEAD_DIM, c0:c0 + 2 * WINDOW]
            p = p_scr[j]
            p_keys = jnp.concatenate([p * keep_prev, p * keep_cur], axis=0)
            oa = jnp.dot(jnp.concatenate([vg, ones], axis=0), p_keys, preferred_element_type=F32)
            denom = oa[HEAD_DIM:HEAD_DIM + 1, :] + esink_scr[j:j + 1, :]
            o = oa[0:HEAD_DIM, :] * (1.0 / denom)
            for hd, part in ((ha, o[:, 0:WINDOW]), (ha + 1, o[:, WINDOW:2 * WINDOW])):
                rows = slice(hd * HEAD_DIM, (hd + 1) * HEAD_DIM)
                yt_scr[rows, qcols] = (part * gate_scr[prev, rows, qcols]).astype(BF16)

    out_t = jnp.dot(wout_ref[...], yt_scr[...], preferred_element_type=F32)
    finish_q()
    last_piece()
    finish_q()
    o_ref[0] = xres_ref[0] + out_t.T


def _odd_prompt(x, ng, w_in, w_out, small, cos_t, sin_t, sinks, s_olo, s_ohi, s_gate, s_x1, *, tm):
    B, T, D = x.shape
    assert T % tm == 0 and tm % WINDOW == 0 and w_in.shape == (D, 2 * ATTN_W + 2 * KV_W) and w_out.shape == (ATTN_W, D)
    nt = T // tm
    half = HEAD_DIM // 2
    const = lambda *shape: pl.BlockSpec(shape, lambda s: (0,) * len(shape))
    n_tiles = B * nt
    proj_tile = lambda s: jnp.minimum(s, n_tiles - 1)
    attn_tile = lambda s: jnp.maximum(s - 1, 0)
    return pl.pallas_call(
        functools.partial(_odd_prompt_kernel, tm=tm, nt=nt, n_steps=n_tiles + 1),
        grid=(n_tiles + 1,),
        in_specs=[
            pl.BlockSpec((1, tm, D), lambda s: (proj_tile(s) // nt, proj_tile(s) % nt, 0)),
            pl.BlockSpec((1, tm, D), lambda s: (attn_tile(s) // nt, attn_tile(s) % nt, 0)),
            const(*ng.shape),
            const(*w_in.shape),
            const(*w_out.shape),
            const(*small.shape),
            pl.BlockSpec((1, HEAD_DIM, LANES), lambda s: (proj_tile(s) % nt, 0, 0)),
            const(HEAD_DIM, tm),
            pl.BlockSpec(memory_space=pltpu.SMEM),
            const(*s_olo.shape),
            const(*s_ohi.shape),
            const(*s_gate.shape),
            const(*s_x1.shape),
        ],
        out_specs=[
            pl.BlockSpec((1, tm, D), lambda s: (attn_tile(s) // nt, attn_tile(s) % nt, 0)),
            pl.BlockSpec((1, KV_W, WINDOW), lambda s: (proj_tile(s) // nt, 0, 0)),
            pl.BlockSpec((1, KV_W, WINDOW), lambda s: (proj_tile(s) // nt, 0, 0)),
            const(s_x1.size // LANES, LANES),
        ],
        out_shape=[
            jax.ShapeDtypeStruct((B, T, D), F32),
            jax.ShapeDtypeStruct((B, KV_W, WINDOW), F32),
            jax.ShapeDtypeStruct((B, KV_W, WINDOW), F32),
            jax.ShapeDtypeStruct((s_x1.size // LANES, LANES), F32),
        ],
        scratch_shapes=[
            pltpu.VMEM((KV_W, D), BF16),
            pltpu.VMEM((2 * ATTN_W + KV_W, D), BF16),
            pltpu.VMEM((D, ATTN_W), BF16),
            pltpu.VMEM((2, ATTN_W, tm), BF16),
            pltpu.VMEM((2, WINDOW + tm, KV_W), BF16),
            pltpu.VMEM((2, KV_W, WINDOW + tm), BF16),
            pltpu.VMEM((2, ATTN_W, tm), F32),
            pltpu.VMEM((ATTN_W, tm), BF16),
            pltpu.VMEM((N_HEADS // 2, WINDOW, 2 * WINDOW), F32),
            pltpu.VMEM((N_HEADS // 2, WINDOW, 2 * WINDOW), BF16),
            pltpu.VMEM((N_HEADS // 2, 2 * WINDOW), F32),
        ],
        compiler_params=pltpu.CompilerParams(
            dimension_semantics=("arbitrary",), vmem_limit_bytes=VMEM_LIMIT_BYTES),
        name="odd_prompt",
    )(x, x, ng, w_in, w_out, small, cos_t, sin_t, sinks, s_olo, s_ohi, s_gate, s_x1)


def _dup_heads(x):
    lane = lax.broadcasted_iota(jnp.int32, x.shape, 1)
    rolled = pltpu.roll(x, HEAD_DIM, 1)
    first = lane < HEAD_DIM
    return jnp.where(first, x, rolled), jnp.where(first, rolled, x)


def _sample_dense_kernel(x_ref, sp_ref, sc_ref, ng_ref, win0_ref, pw_ref, ps_ref, cw_ref, wout0_ref,
                         win1_ref, qgain_ref, kgain_ref, cos1_ref, sin1_ref,
                         x1_ref, pool_ref, conv_ref, qh_ref, kt_ref, vt_ref, gate_ref, small_ref,
                         *scratch):
    hbm = _HbmOperands(sp_ref, win0_ref, wout0_ref, win1_ref, pool_ref, *scratch)
    sp_ref, win0_ref, wout0_ref, win1_ref = hbm.sp, hbm.win0, hbm.wout0, hbm.win1
    w_pool = ps_ref.shape[1]
    w_conv = (win0_ref.shape[1] - 2 * w_pool) // 4
    half = HEAD_DIM // 2
    qg_t, cos_t1, sin_t1 = _col_table(qgain_ref[...]), _col_table(cos1_ref[...]), _col_table(sin1_ref[...])
    small_ref[_QG_T:_QG_T + HEAD_DIM, :] = qg_t
    small_ref[_KG_T:_KG_T + HEAD_DIM, :] = _col_table(kgain_ref[...])
    small_ref[_COS_T:_COS_T + half, :] = cos_t1
    small_ref[_SIN_T:_SIN_T + half, :] = sin_t1
    second_half = lax.broadcasted_iota(jnp.int32, (1, LANES), 1) % HEAD_DIM >= half
    small_ref[_QG_ROW:_SMALL_ROWS, :] = jnp.concatenate(
        [_tiled_row(qg_t), _tiled_row(cos_t1), jnp.where(second_half, 1.0, -1.0) * _tiled_row(sin_t1),
         jnp.zeros((_SMALL_ROWS - _SIN_ROW - 1, LANES), F32)], axis=0)
    g0_ref, g1_ref = ng_ref.at[0:1], ng_ref.at[1:2]
    qg_ref, kgt_ref = small_ref.at[_QG_ROW:_QG_ROW + 1], small_ref.at[_KG_T:_KG_T + HEAD_DIM]
    cos_ref, sin_ref = small_ref.at[_COS_ROW:_COS_ROW + 1], small_ref.at[_SIN_ROW:_SIN_ROW + 1]
    cost_ref, sint_ref = small_ref.at[_COS_T:_COS_T + HEAD_DIM // 2], small_ref.at[_SIN_T:_SIN_T + HEAD_DIM // 2]

    n = x1_ref.shape[0]
    chunks = x1_ref.shape[1] // LANES
    x = jnp.concatenate([x_ref[pl.ds(c, n, stride=chunks), :] for c in range(chunks)], axis=1)
    h = _rms(x, g0_ref[...]).astype(win0_ref.dtype)

    def proj(c, width):
        return jnp.dot(h, win0_ref[:, c:c + width], preferred_element_type=F32)

    hbm.wait_win0()
    u = proj(0, w_pool)
    hbm.put_newest_pool_row(u)
    z_a = proj(w_pool, w_pool)
    hbm.wait_sp()
    ys = []
    for g, w in enumerate(POOL_WINDOWS):
        cols = slice(g * POOL_GROUP, (g + 1) * POOL_GROUP)
        wsum = u[:, cols]
        for r in range(POOL_BUF - (w - 1), POOL_BUF):
            wsum = wsum + sp_ref[r, :, cols]
        cnt = float(min(PAST_LEN + 1, w))
        d = wsum * (1.0 / cnt) - u[:, cols]
        ya = jnp.dot(d.astype(pw_ref.dtype), pw_ref[g], preferred_element_type=F32)
        ys.append((ya * ps_ref[:, cols] * _silu(z_a[:, cols])).astype(wout0_ref.dtype))

    b_gate = proj(2 * w_pool, w_conv)
    c_gate = proj(2 * w_pool + w_conv, w_conv)
    v_in = proj(2 * w_pool + 2 * w_conv, w_conv)
    v = c_gate * v_in
    cchunks = w_conv // LANES
    cstride = cchunks * CONV_BUF

    def conv_row(r):
        return jnp.concatenate(
            [sc_ref[pl.ds(j * CONV_BUF + r, n, stride=cstride), :] for j in range(cchunks)], axis=1)

    def put_conv_row(r, val):
        for j in range(cchunks):
            conv_ref[pl.ds(j * CONV_BUF + r, n, stride=cstride), :] = val[:, j * LANES:(j + 1) * LANES]

    conv = v * _conv_tap(cw_ref, CONV_WIDTH - 1, w_conv)
    for k in range(CONV_BUF):
        conv = conv + conv_row(k) * _conv_tap(cw_ref, k, w_conv)
    for r in range(CONV_BUF - 1):
        put_conv_row(r, conv_row(r + 1))
    put_conv_row(CONV_BUF - 1, v)
    z_b = proj(2 * w_pool + 3 * w_conv, w_conv)
    ys.append((b_gate * conv * _silu(z_b)).astype(wout0_ref.dtype))
    y = jnp.concatenate(ys, axis=1)
    hbm.wait_wout0()
    x1 = x + jnp.dot(y, wout0_ref[...], preferred_element_type=F32)
    x1_ref[...] = x1

    h1 = _rms(x1, g1_ref[...])
    hbm.wait_win1()

    def proj1(lo, width):
        return jnp.dot(h1, win1_ref[:, lo:lo + width], preferred_element_type=F32)

    q = proj1(0, ATTN_W)
    for c in range(ATTN_W // LANES):
        qc = _head_norm_rope(q[:, c * LANES:(c + 1) * LANES], qg_ref[...], cos_ref[...], sin_ref[...])
        d0, d1 = _dup_heads(qc * (HEAD_DIM ** -0.5))
        qh_ref[2 * c * n:(2 * c + 1) * n, :] = d0
        qh_ref[(2 * c + 1) * n:(2 * c + 2) * n, :] = d1
    kt = proj1(ATTN_W, KV_W).T
    cos_t = cost_ref[...]
    sin_t = sint_ref[...]
    for kv in range(N_KV_HEADS):
        kh = kt[kv * HEAD_DIM:(kv + 1) * HEAD_DIM, :]
        ms = jnp.sum(kh * kh, axis=0, keepdims=True) * (1.0 / HEAD_DIM)
        kn = kh * lax.rsqrt(ms + RMS_EPS) * kgt_ref[...]
        x1h, x2h = kn[0:half, :], kn[half:HEAD_DIM, :]
        kt_ref[kv * HEAD_DIM:(kv + 1) * HEAD_DIM, :] = jnp.concatenate(
            [x1h * cos_t - x2h * sin_t, x2h * cos_t + x1h * sin_t], axis=0)
    vt_ref[...] = proj1(ATTN_W + KV_W, KV_W).T
    gate_ref[...] = _silu(proj1(ATTN_W + 2 * KV_W, ATTN_W))
    hbm.finish()


class _HbmOperands:
    def __init__(self, sp_hbm, win0_hbm, wout0_hbm, win1_hbm, pool_hbm, sp, win0, wout0, win1, newest, sems):
        self.sp, self.win0, self.wout0, self.win1, self.newest = sp, win0, wout0, win1, newest
        self.pool_hbm = pool_hbm
        self.sems = sems
        rows = sp_hbm.shape[0]
        self.shift = pltpu.make_async_copy(sp.at[1:rows], pool_hbm.at[0:rows - 1], sems.at[0])
        self.fetch = {name: pltpu.make_async_copy(src, dst, sems.at[i + 1]) for i, (name, src, dst) in enumerate(
            (("win0", win0_hbm, win0), ("sp", sp_hbm, sp), ("wout0", wout0_hbm, wout0), ("win1", win1_hbm, win1)))}
        for copy in self.fetch.values():
            copy.start()

    def wait_win0(self):
        self.fetch["win0"].wait()

    def wait_sp(self):
        self.fetch["sp"].wait()
        self.shift.start()

    def wait_wout0(self):
        self.fetch["wout0"].wait()

    def wait_win1(self):
        self.fetch["win1"].wait()

    def put_newest_pool_row(self, u):
        rows = self.pool_hbm.shape[0]
        self.newest[...] = u
        self.put = pltpu.make_async_copy(self.newest, self.pool_hbm.at[rows - 1], self.sems.at[5])
        self.put.start()

    def finish(self):
        self.shift.wait()
        self.put.wait()


def _sample_dense(x, sp, sc, ng, win0, pw, ps, cw, wout0, win1, qgain, kgain, cos1, sin1):
    D = win0.shape[0]
    n = x.shape[0] * LANES // D
    assert n == LANES
    assert qgain.shape == kgain.shape == (1, HEAD_DIM) and cos1.shape == sin1.shape == (1, HEAD_DIM // 2)
    vmem = pl.BlockSpec(memory_space=pltpu.VMEM)
    hbm = pl.BlockSpec(memory_space=pl.ANY)
    in_specs = [vmem, hbm, vmem, vmem, hbm, vmem, vmem, vmem, hbm, hbm] + [vmem] * 4
    return pl.pallas_call(
        _sample_dense_kernel,
        in_specs=in_specs,
        out_specs=[vmem, hbm] + [vmem] * 6,
        scratch_shapes=[
            pltpu.VMEM(sp.shape, F32),
            pltpu.VMEM(win0.shape, F32),
            pltpu.VMEM(wout0.shape, F32),
            pltpu.VMEM(win1.shape, F32),
            pltpu.VMEM(sp.shape[1:], F32),
            pltpu.SemaphoreType.DMA((6,)),
        ],
        out_shape=[
            jax.ShapeDtypeStruct((n, D), F32),
            jax.ShapeDtypeStruct(sp.shape, F32),
            jax.ShapeDtypeStruct(sc.shape, F32),
            jax.ShapeDtypeStruct((N_HEADS * n, LANES), F32),
            jax.ShapeDtypeStruct((KV_W, n), F32),
            jax.ShapeDtypeStruct((KV_W, n), F32),
            jax.ShapeDtypeStruct((n, ATTN_W), F32),
            jax.ShapeDtypeStruct((_SMALL_ROWS, LANES), F32),
        ],
        compiler_params=pltpu.CompilerParams(vmem_limit_bytes=VMEM_LIMIT_BYTES),
        name="sample_dense",
    )(x, sp, sc, ng, win0, pw, ps, cw, wout0, win1, qgain, kgain, cos1, sin1)


def _sample_out(olo_ref, ohi_ref, gate_ref, x1_ref, wout_ref, y_ref):
    n = x1_ref.shape[0]
    blocks = LANES // HEAD_DIM
    lane_n = lax.broadcasted_iota(jnp.int32, (n, LANES), 1)
    chunks = []
    for c in range(ATTN_W // LANES):
        parts = []
        for hd in (2 * c, 2 * c + 1):
            kv = hd // GROUP
            slab = (olo_ref, ohi_ref)[kv // blocks]
            a = slab[pl.ds(hd, n, stride=N_HEADS), :]
            parts.append(a if kv % blocks == hd % blocks else pltpu.roll(a, HEAD_DIM, 1))
        chunks.append(jnp.where(lane_n < HEAD_DIM, parts[0], parts[1]))
    y = jnp.concatenate(chunks, axis=1) * gate_ref[...]
    out = x1_ref[...] + jnp.dot(y, wout_ref[...], preferred_element_type=F32)
    chunks = out.shape[1] // LANES
    for c in range(chunks):
        y_ref[pl.ds(c, n, stride=chunks), :] = out[:, c * LANES:(c + 1) * LANES]


def _rope_tables(pos):
    half = HEAD_DIM // 2
    inv = ROPE_THETA ** (-jnp.arange(half, dtype=F32) / half)
    ang = pos.astype(F32)[:, None] * inv[None, :]
    return jnp.cos(ang), jnp.sin(ang)


def kernel(x_prompt, x_sample, state_pool, state_conv, cache_k, cache_v, norm_g, w_in_even, pool_w, pool_scale,
           conv_w, w_out_even, w_in_odd, q_norm_g, k_norm_g, attn_sinks, w_out_odd):
    B, T, D = x_prompt.shape
    n_s, t_s, _ = x_sample.shape
    assert norm_g.shape[0] == 2 and w_in_even.shape[0] == 1 and w_in_odd.shape[0] == 1
    assert t_s == 1 and cache_k.shape[2] == WINDOW and T >= WINDOW
    assert cache_k.shape[3] * cache_k.shape[4] == KV_W and pool_w.shape[1:] == (len(POOL_WINDOWS), POOL_GROUP, POOL_GROUP)

    win0 = w_in_even[0]
    wout0 = w_out_even[0]
    win1 = w_in_odd[0]
    wout1 = w_out_odd[0]
    pw = pool_w[0]
    ps = pool_scale[0][None, :]
    cw = conv_w[0].reshape(-1, LANES)
    sinks = attn_sinks[0]

    cos_b, sin_b = _rope_tables(jnp.arange(T // ODD_TILE) * ODD_TILE)
    rope_base = jnp.broadcast_to(jnp.concatenate([cos_b, sin_b], axis=1)[:, :, None],
                                 (T // ODD_TILE, HEAD_DIM, LANES))
    cos_o, sin_o = _rope_tables(jnp.arange(ODD_TILE))
    rope_off = jnp.concatenate([cos_o, sin_o], axis=1).T
    cos_1, sin_1 = _rope_tables(PAST_LEN + jnp.arange(t_s))

    w_conv = state_conv.shape[-1]
    sp = jnp.transpose(state_pool[0], (1, 0, 2))
    sc = jnp.transpose(state_conv[0].reshape(n_s, CONV_BUF, w_conv // LANES, LANES), (0, 2, 1, 3)).reshape(-1, LANES)
    x1s, pool_s, conv_s, qh_s, kt_s, vt_s, gate_s, small = _sample_dense(
        x_sample.reshape(n_s * D // LANES, LANES), sp, sc, norm_g, win0, pw, ps, cw, wout0, win1,
        q_norm_g, k_norm_g, cos_1, sin_1)
    pool_s = jnp.transpose(pool_s, (1, 0, 2))
    conv_s = jnp.transpose(conv_s.reshape(n_s, w_conv // LANES, CONV_BUF, LANES), (0, 2, 1, 3)).reshape(
        n_s, CONV_BUF, w_conv)

    ckt = jnp.transpose(cache_k[0], (0, 2, 3, 1)).reshape(n_s, KV_W, WINDOW)
    cvt = jnp.transpose(cache_v[0], (0, 2, 3, 1)).reshape(n_s, KV_W, WINDOW)
    x1p, pool_p, conv_p, nkt_s, nvt_s, olo_s, ohi_s = _even_prompt(
        x_prompt, norm_g, win0, pw, ps, cw, wout0, qh_s, kt_s, vt_s, ckt, cvt, attn_sinks, tm=EVEN_TILE)
    y_p, k_p, v_p, y_s = _odd_prompt(x1p, norm_g, win1, wout1, small, rope_base, rope_off, sinks,
                                     olo_s, ohi_s, gate_s, x1s, tm=ODD_TILE)

    def window_major(a):
        return jnp.transpose(a.reshape(-1, N_KV_HEADS, HEAD_DIM, WINDOW), (0, 3, 1, 2))[None]

    return (y_p, y_s.reshape(n_s, 1, D), pool_p[None], pool_s[None], conv_p[None], conv_s[None],
            window_major(k_p), window_major(v_p), window_major(nkt_s), window_major(nvt_s))
```

```python
import functools

import jax
import jax.numpy as jnp
from jax import lax
from jax.experimental import pallas as pl
from jax.experimental.pallas import tpu as pltpu

F32 = jnp.float32
BF16 = jnp.bfloat16

POOL_WINDOWS = (2, 4, 8, 16)
POOL_GROUP = 128
POOL_BUF = max(POOL_WINDOWS) - 1
CONV_WIDTH = 3
CONV_BUF = CONV_WIDTH - 1
N_HEADS = 16
HEAD_DIM = 64
N_KV_HEADS = 4
GROUP = N_HEADS // N_KV_HEADS
WINDOW = 128
ROPE_THETA = 10000.0
RMS_EPS = 1e-6
PAST_LEN = 16384
ATTN_W = N_HEADS * HEAD_DIM
KV_W = N_KV_HEADS * HEAD_DIM

LANES = 128
POOL_HALO = 16
CONV_HALO = 8
VMEM_LIMIT_BYTES = 60 * 1024 * 1024
EVEN_TILE = 1024
ODD_TILE = 512
BF16_SUBLANES = 16
LOG2_E = 1.4426950408889634
_QG_T, _KG_T, _COS_T, _SIN_T = 0, HEAD_DIM, 2 * HEAD_DIM, 2 * HEAD_DIM + HEAD_DIM // 2
_QG_ROW, _COS_ROW, _SIN_ROW, _SMALL_ROWS = 3 * HEAD_DIM, 3 * HEAD_DIM + 1, 3 * HEAD_DIM + 2, 3 * HEAD_DIM + 8


def _rms(x, g):
    ms = jnp.mean(x * x, axis=-1, keepdims=True)
    return x * lax.rsqrt(ms + RMS_EPS) * g


def _silu(z):
    return z * jax.nn.sigmoid(z)


def _head_norm_rope(x, gain, cos, sin):
    lane = lax.broadcasted_iota(jnp.int32, x.shape, 1)
    first = lane < HEAD_DIM
    x2 = x * x
    ss0 = jnp.sum(jnp.where(first, x2, 0.0), axis=-1, keepdims=True)
    ss1 = jnp.sum(jnp.where(first, 0.0, x2), axis=-1, keepdims=True)
    r = jnp.where(first, lax.rsqrt(ss0 * (1.0 / HEAD_DIM) + RMS_EPS), lax.rsqrt(ss1 * (1.0 / HEAD_DIM) + RMS_EPS))
    xn = x * r * gain
    half = HEAD_DIM // 2
    swapped = jnp.where((lane % HEAD_DIM) < half, pltpu.roll(xn, LANES - half, 1), pltpu.roll(xn, half, 1))
    return xn * cos + swapped * sin


def _nt_dot(a, b):
    return lax.dot_general(a, b, (((1,), (1,)), ((), ())), preferred_element_type=F32)


def _col_table(row):
    m = row.shape[1]
    diag = lax.broadcasted_iota(jnp.int32, (m, m), 0) == lax.broadcasted_iota(jnp.int32, (m, m), 1)
    col = jnp.sum(jnp.where(diag, jnp.broadcast_to(row, (m, m)), 0.0), axis=1, keepdims=True)
    return jnp.broadcast_to(col, (m, LANES))


def _tiled_row(table):
    m = table.shape[0]
    pick = lax.broadcasted_iota(jnp.int32, table.shape, 0) == lax.broadcasted_iota(jnp.int32, table.shape, 1) % m
    return jnp.sum(jnp.where(pick, table, 0.0), axis=0, keepdims=True)


def _conv_tap(cw_ref, k, width):
    chunks = width // LANES
    return jnp.concatenate([cw_ref[k * chunks + j:k * chunks + j + 1, :] for j in range(chunks)], axis=1)


class _SampleAttention:
    def __init__(self, qh_ref, knt_ref, vnt_ref, ck_ref, cv_ref, sink_ref, nk_ref, nv_ref, olo_ref, ohi_ref, *, step):
        self.refs = (qh_ref, knt_ref, vnt_ref, ck_ref, cv_ref, sink_ref, nk_ref, nv_ref, olo_ref, ohi_ref)
        self.bb = ck_ref.shape[0]
        self.first = step * self.bb

    def roll(self):
        _, knt_ref, vnt_ref, ck_ref, cv_ref, _, nk_ref, nv_ref, _, _ = self.refs
        win = ck_ref.shape[2]
        newest = lax.broadcasted_iota(jnp.int32, (KV_W, win), 1) == win - 1
        for i in range(self.bb):
            b = self.first + i
            nk_ref[i] = jnp.where(newest, pltpu.roll(knt_ref[...], win - 1 - b, 1), pltpu.roll(ck_ref[i], win - 1, 1))
            nv_ref[i] = jnp.where(newest, pltpu.roll(vnt_ref[...], win - 1 - b, 1), pltpu.roll(cv_ref[i], win - 1, 1))

    def scores(self):
        qh_ref, nk_ref = self.refs[0], self.refs[6]
        n = qh_ref.shape[0] // N_HEADS
        blocks = LANES // HEAD_DIM
        hrow = lax.broadcasted_iota(jnp.int32, (N_HEADS, LANES), 0) // GROUP
        hcol = lax.broadcasted_iota(jnp.int32, (N_HEADS, LANES), 1) // HEAD_DIM
        self.s = []
        for i in range(self.bb):
            qd = qh_ref[pl.ds(self.first + i, N_HEADS, stride=n), :]
            qx = jnp.concatenate(
                [jnp.where(hcol + blocks * c == hrow, qd, 0.0) for c in range(KV_W // LANES)], axis=1)
            self.s.append(jnp.dot(qx.astype(BF16), nk_ref[i].astype(BF16), preferred_element_type=F32))

    def softmax(self):
        sink = _col_table(self.refs[5][...])[:, 0:1]
        self.p = []
        for s in self.s:
            m = jnp.maximum(jnp.max(s, axis=-1, keepdims=True), sink)
            p = jnp.exp(s - m)
            denom = jnp.sum(p, axis=-1, keepdims=True) + jnp.exp(sink - m)
            self.p.append((p * (1.0 / denom)).astype(BF16))

    def outputs(self):
        nv_ref, olo_ref, ohi_ref = self.refs[7:10]
        for i in range(self.bb):
            o = _nt_dot(self.p[i], nv_ref[i].astype(BF16))
            rows = pl.ds(pl.multiple_of((self.first + i) * N_HEADS, N_HEADS), N_HEADS)
            olo_ref[rows, :] = o[:, 0:LANES]
            ohi_ref[rows, :] = o[:, LANES:2 * LANES]


def _even_prompt_kernel(x_ref, g_ref, win_ref, pw_ref, ps_ref, cw_ref, wout_ref,
                        qh_ref, knt_ref, vnt_ref, ck_ref, cv_ref, sink_ref,
                        o_ref, pool_ref, conv_ref, nk_ref, nv_ref, olo_ref, ohi_ref,
                        uext, vext, sa, sb, y_scr, *, tm):
    t = pl.program_id(1)
    w_pool = uext.shape[1]
    w_conv = vext.shape[1]
    base = 2 * POOL_HALO
    side = _SampleAttention(qh_ref, knt_ref, vnt_ref, ck_ref, cv_ref, sink_ref, nk_ref, nv_ref, olo_ref, ohi_ref,
                            step=pl.program_id(0) * pl.num_programs(1) + t)

    @pl.when(t == 0)
    def _():
        uext[0:base, :] = jnp.zeros((base, w_pool), F32)
        vext[0:CONV_HALO, :] = jnp.zeros((CONV_HALO, w_conv), F32)
        sa[0:POOL_HALO, :] = jnp.zeros((POOL_HALO, POOL_GROUP), F32)
        sb[0:POOL_HALO, :] = jnp.zeros((POOL_HALO, POOL_GROUP), F32)

    x = x_ref[0]
    h = _rms(x, g_ref[0:1, :]).astype(win_ref.dtype)

    def proj(c, width):
        return jnp.dot(h, win_ref[:, c:c + width], preferred_element_type=F32)

    side.roll()
    u = proj(0, w_pool)
    uext[base:base + tm, :] = u
    pos = t * tm + lax.broadcasted_iota(jnp.int32, (tm, 1), 0)
    side.scores()
    z_a = proj(w_pool, w_pool)
    n_ext = POOL_HALO + tm

    for g, w in enumerate(POOL_WINDOWS):
        lo = g * POOL_GROUP
        cols = slice(lo, lo + POOL_GROUP)
        src = uext
        src_cols = cols
        step = 1
        bufs = (sa, sb)
        nbuf = 0
        while 2 * step < w:
            dst = bufs[nbuf % 2]
            dst[POOL_HALO:POOL_HALO + n_ext, :] = (src[POOL_HALO:POOL_HALO + n_ext, src_cols]
                                                   + src[POOL_HALO - step:POOL_HALO - step + n_ext, src_cols])
            src, src_cols = dst, slice(0, POOL_GROUP)
            step *= 2
            nbuf += 1
        wsum = src[base:base + tm, src_cols] + src[base - step:base - step + tm, src_cols]
        cnt = jnp.minimum(pos + 1, w).astype(F32)
        d = wsum * (1.0 / cnt) - u[:, cols]
        ya = jnp.dot(d.astype(pw_ref.dtype), pw_ref[g], preferred_element_type=F32)
        ya = ya * ps_ref[:, cols] * _silu(z_a[:, cols])
        y_scr[:, cols] = ya.astype(y_scr.dtype)

    b_gate = proj(2 * w_pool, w_conv)
    side.softmax()
    c_gate = proj(2 * w_pool + w_conv, w_conv)
    side.outputs()
    v_in = proj(2 * w_pool + 2 * w_conv, w_conv)
    v = c_gate * v_in
    vext[CONV_HALO:CONV_HALO + tm, :] = v
    conv = v * _conv_tap(cw_ref, CONV_WIDTH - 1, w_conv)
    for k in range(CONV_WIDTH - 1):
        shift = CONV_WIDTH - 1 - k
        conv = conv + vext[CONV_HALO - shift:CONV_HALO - shift + tm, :] * _conv_tap(cw_ref, k, w_conv)
    z_b = proj(2 * w_pool + 3 * w_conv, w_conv)
    y_b = b_gate * conv * _silu(z_b)
    y_scr[:, w_pool:w_pool + w_conv] = y_b.astype(y_scr.dtype)

    o_ref[0] = x + jnp.dot(y_scr[...], wout_ref[...], preferred_element_type=F32)
    pool_ref[0] = uext[base + tm - POOL_BUF:base + tm, :]
    conv_ref[0] = vext[CONV_HALO + tm - CONV_BUF:CONV_HALO + tm, :]
    uext[POOL_HALO:base, :] = uext[POOL_HALO + tm:base + tm, :]
    vext[0:CONV_HALO, :] = vext[tm:tm + CONV_HALO, :]


def _even_prompt(x, g, win, pw, ps, cw, wout, qh, knt, vnt, ckt, cvt, sink_col, *, tm):
    B, T, D = x.shape
    w_pool = ps.shape[-1]
    w_conv = (win.shape[1] - 2 * w_pool) // 4
    assert cw.shape == (CONV_WIDTH * w_conv // LANES, LANES)
    assert T % tm == 0 and tm % 16 == 0 and tm >= POOL_HALO
    nt = T // tm
    n, kvw, win_len = ckt.shape
    assert n % (B * nt) == 0 and kvw == KV_W == 2 * LANES and win_len == LANES and n == LANES
    bb = n // (B * nt)
    const = lambda *shape: pl.BlockSpec(shape, lambda b, t: (0,) * len(shape))
    cache_block = pl.BlockSpec((bb, kvw, win_len), lambda b, t: (b * nt + t, 0, 0))
    return pl.pallas_call(
        functools.partial(_even_prompt_kernel, tm=tm),
        grid=(B, nt),
        in_specs=[
            pl.BlockSpec((1, tm, D), lambda b, t: (b, t, 0)),
            const(*g.shape),
            const(*win.shape),
            const(*pw.shape),
            const(1, w_pool),
            const(*cw.shape),
            const(*wout.shape),
            const(*qh.shape),
            const(*knt.shape),
            const(*vnt.shape),
            cache_block,
            cache_block,
            const(*sink_col.shape),
        ],
        out_specs=[
            pl.BlockSpec((1, tm, D), lambda b, t: (b, t, 0)),
            pl.BlockSpec((1, POOL_BUF, w_pool), lambda b, t: (b, 0, 0)),
            pl.BlockSpec((1, CONV_BUF, w_conv), lambda b, t: (b, 0, 0)),
            cache_block,
            cache_block,
            const(n * N_HEADS, LANES),
            const(n * N_HEADS, LANES),
        ],
        out_shape=[
            jax.ShapeDtypeStruct((B, T, D), F32),
            jax.ShapeDtypeStruct((B, POOL_BUF, w_pool), F32),
            jax.ShapeDtypeStruct((B, CONV_BUF, w_conv), F32),
            jax.ShapeDtypeStruct((n, kvw, win_len), F32),
            jax.ShapeDtypeStruct((n, kvw, win_len), F32),
            jax.ShapeDtypeStruct((n * N_HEADS, LANES), F32),
            jax.ShapeDtypeStruct((n * N_HEADS, LANES), F32),
        ],
        scratch_shapes=[
            pltpu.VMEM((2 * POOL_HALO + tm, w_pool), F32),
            pltpu.VMEM((CONV_HALO + tm, w_conv), F32),
            pltpu.VMEM((2 * POOL_HALO + tm, POOL_GROUP), F32),
            pltpu.VMEM((2 * POOL_HALO + tm, POOL_GROUP), F32),
            pltpu.VMEM((tm, w_pool + w_conv), wout.dtype),
        ],
        compiler_params=pltpu.CompilerParams(
            dimension_semantics=("arbitrary", "arbitrary"), vmem_limit_bytes=VMEM_LIMIT_BYTES),
        name="even_prompt",
    )(x, g, win, pw, ps, cw, wout, qh, knt, vnt, ckt, cvt, sink_col)


def _odd_prompt_kernel(*refs, tm, nt, n_steps):
    (x_ref, ng_ref, win_ref, wo_ref, small_ref, ropeb_ref, ropeo_ref, sink_ref,
     olo_ref, ohi_ref, sgate_ref, sx1_ref,
     o_ref, knew_ref, vnew_ref, sy_ref, xres_ref, wkt_scr, wqvz_scr, wout_scr, qt_scr, kext, vt_ext, gate_scr,
     *rest) = refs
    g_ref = ng_ref.at[1:2]
    qg_ref, kg_ref = small_ref.at[_QG_T:_QG_T + HEAD_DIM], small_ref.at[_KG_T:_KG_T + HEAD_DIM]

    step = pl.program_id(0)
    last = n_steps - 1

    @pl.when(step == 0)
    def _():
        _sample_out(olo_ref, ohi_ref, sgate_ref, sx1_ref, wo_ref, sy_ref)
        kext[...] = jnp.zeros(kext.shape, BF16)
        vt_ext[...] = jnp.zeros(vt_ext.shape, BF16)
        blk = 2 * LANES
        for src, dst, width in ((0, 0, ATTN_W), (ATTN_W + KV_W, ATTN_W, KV_W + ATTN_W)):
            for c in range(0, width, blk):
                wqvz_scr[dst + c:dst + c + blk, :] = win_ref[:, src + c:src + c + blk].T.astype(BF16)
        wkt_scr[...] = win_ref[:, ATTN_W:ATTN_W + KV_W].T.astype(BF16)
        for c in range(0, wo_ref.shape[1], blk):
            wout_scr[c:c + blk, :] = wo_ref[:, c:c + blk].T.astype(BF16)

    def run(cur, **halves):
        _odd_prompt_step(x_ref, xres_ref, g_ref, wkt_scr, wqvz_scr, qg_ref, kg_ref, ropeb_ref, ropeo_ref, sink_ref,
                         wout_scr, o_ref, knew_ref, vnew_ref, qt_scr, kext, vt_ext, gate_scr, *rest,
                         tm=tm, nt=nt, cur=cur, prev=1 - cur, **halves)

    pl.when(step == 0)(lambda: run(0, do_attn=False))
    for parity in range(2):
        pl.when((step % 2 == parity) & (step > 0) & (step < last))(lambda parity=parity: run(parity))
    pl.when(step == last)(lambda: run((n_steps - 1) % 2, do_proj=False))


def _odd_prompt_step(x_ref, xres_ref, g_ref, wkt_ref, wqvz_ref, qg_ref, kg_ref, ropeb_ref, ropeo_ref, sink_ref,
                     wout_ref, o_ref, knew_ref, vnew_ref,
                     qt_scr, kext, vt_ext, gate_scr, yt_scr, s_scr, p_scr, esink_scr, *, tm, nt, cur, prev,
                     do_proj=True, do_attn=True):
    step = pl.program_id(0)
    proj_t = step % nt
    attn_t = jnp.maximum(step - 1, 0) % nt
    nblk = tm // WINDOW
    half = HEAD_DIM // 2

    h = _rms(x_ref[0], g_ref[...]).astype(BF16) if do_proj else None
    reps = tm // LANES
    if do_proj:
        xres_ref[cur] = x_ref[0]
        cos_b = jnp.concatenate([ropeb_ref[0, 0:half, :]] * reps, axis=1)
        sin_b = jnp.concatenate([ropeb_ref[0, half:HEAD_DIM, :]] * reps, axis=1)
        cos_o, sin_o = ropeo_ref[0:half, :], ropeo_ref[half:HEAD_DIM, :]
        cos_t = cos_b * cos_o - sin_b * sin_o
        sin_t = sin_b * cos_o + cos_b * sin_o

    def norm_rope(xt, gain):
        ms = jnp.sum(xt * xt, axis=0, keepdims=True) * (1.0 / HEAD_DIM)
        xn = xt * lax.rsqrt(ms + RMS_EPS) * gain
        x1, x2 = xn[0:half, :], xn[half:HEAD_DIM, :]
        return jnp.concatenate([x1 * cos_t - x2 * sin_t, x2 * cos_t + x1 * sin_t], axis=0)

    has_past = proj_t > 0

    def proj_kv():
        kt = _nt_dot(wkt_ref[...], h)
        kgain = jnp.concatenate([kg_ref[...]] * reps, axis=1)
        kt = jnp.concatenate(
            [norm_rope(kt[kv * HEAD_DIM:(kv + 1) * HEAD_DIM, :], kgain) for kv in range(N_KV_HEADS)], axis=0)
        knew_ref[0] = kt[:, tm - WINDOW:tm]
        kext[cur, 0:WINDOW, :] = jnp.where(has_past, kext[prev, tm:tm + WINDOW, :], jnp.zeros((WINDOW, KV_W), BF16))
        kext[cur, WINDOW:WINDOW + tm, :] = kt.T.astype(BF16)
        vt = _nt_dot(wqvz_ref[ATTN_W:ATTN_W + KV_W, :], h)
        vnew_ref[0] = vt[:, tm - WINDOW:tm]
        vt_ext[cur, :, 0:WINDOW] = jnp.where(has_past, vt_ext[prev, :, tm:tm + WINDOW],
                                             jnp.zeros((KV_W, WINDOW), BF16))
        vt_ext[cur, :, WINDOW:WINDOW + tm] = vt.astype(BF16)

    def proj_gate(lo, hi):
        zt = _nt_dot(wqvz_ref[ATTN_W + KV_W + lo:ATTN_W + KV_W + hi, :], h)
        gate_scr[cur, lo:hi, :] = _silu(zt)

    q_raw = {}

    def proj_q(lo, hi):
        q_raw[lo] = _nt_dot(wqvz_ref[lo:hi, :], h)

    def finish_q():
        qgain = jnp.concatenate([qg_ref[...]] * reps, axis=1)
        for lo, qt in q_raw.items():
            for r0 in range(0, qt.shape[0], HEAD_DIM):
                qt_scr[cur, lo + r0:lo + r0 + HEAD_DIM, :] = (
                    norm_rope(qt[r0:r0 + HEAD_DIM, :], qgain) * (LOG2_E * HEAD_DIM ** -0.5)).astype(BF16)
        q_raw.clear()

    pieces = [proj_kv,
              lambda: proj_gate(0, ATTN_W // 2),
              lambda: proj_gate(ATTN_W // 2, ATTN_W),
              lambda: proj_q(0, ATTN_W // 2)]
    last_piece = lambda: proj_q(ATTN_W // 2, ATTN_W)
    if not do_proj:
        pieces, last_piece = [], lambda: None
    if not do_attn:
        for piece in pieces:
            piece()
        finish_q()
        last_piece()
        finish_q()
        return

    ri = lax.broadcasted_iota(jnp.int32, (WINDOW, 2 * WINDOW), 0)
    qq = lax.broadcasted_iota(jnp.int32, (WINDOW, 2 * WINDOW), 1) % WINDOW
    from_prev = ri > qq
    keep_prev = from_prev.astype(BF16)
    keep_cur = 1.0 - keep_prev
    lane = lax.broadcasted_iota(jnp.int32, (1, 2 * WINDOW), 1)
    zeros = jnp.zeros((HEAD_DIM, 2 * WINDOW), BF16)
    ones = jnp.ones((BF16_SUBLANES, 2 * WINDOW), BF16)

    for i in range(nblk):
        c0 = i * WINDOW
        qcols = slice(c0, c0 + WINDOW)
        pairs = [(kv, kv * GROUP + 2 * pr) for kv in range(N_KV_HEADS) for pr in range(GROUP // 2)]
        for j, (kv, ha) in enumerate(pairs):
            chunk, pos = divmod(kv, LANES // HEAD_DIM)
            kblk = kext[prev, c0:c0 + 2 * WINDOW, chunk * LANES:(chunk + 1) * LANES]
            qpair = jnp.concatenate([qt_scr[prev, ha * HEAD_DIM:(ha + 1) * HEAD_DIM, qcols],
                                     qt_scr[prev, (ha + 1) * HEAD_DIM:(ha + 2) * HEAD_DIM, qcols]], axis=1)
            rhs = jnp.concatenate([qpair, zeros] if pos == 0 else [zeros, qpair], axis=0)
            s = jnp.dot(kblk, rhs, preferred_element_type=F32)
            s_prev = s[0:WINDOW, :]
            if i == 0:
                s_prev = jnp.where(attn_t == 0, -jnp.inf, s_prev)
            s_scr[j] = jnp.where(from_prev, s_prev, s[WINDOW:2 * WINDOW, :])
        for c, piece in enumerate(pieces):
            if c * nblk // len(pieces) == i:
                piece()
        for j, (kv, ha) in enumerate(pairs):
            s = s_scr[j]
            sink = jnp.where(lane < WINDOW, sink_ref[ha], sink_ref[ha + 1]) * LOG2_E
            m = jnp.maximum(jnp.max(s, axis=0, keepdims=True), sink)
            p_scr[j] = jnp.exp2(s - m).astype(BF16)
            esink_scr[j:j + 1, :] = jnp.exp2(sink - m)
        for j, (kv, ha) in enumerate(pairs):
            vg = vt_ext[prev, kv * HEAD_DIM:(kv + 1) * HEAD_DIM, c0:c0 + 2 * WINDOW]
            p = p_scr[j]
            p_keys = jnp.concatenate([p * keep_prev, p * keep_cur], axis=0)
            oa = jnp.dot(jnp.concatenate([vg, ones], axis=0), p_keys, preferred_element_type=F32)
            denom = oa[HEAD_DIM:HEAD_DIM + 1, :] + esink_scr[j:j + 1, :]
            o = oa[0:HEAD_DIM, :] * (1.0 / denom)
            for hd, part in ((ha, o[:, 0:WINDOW]), (ha + 1, o[:, WINDOW:2 * WINDOW])):
                rows = slice(hd * HEAD_DIM, (hd + 1) * HEAD_DIM)
                yt_scr[rows, qcols] = (part * gate_scr[prev, rows, qcols]).astype(BF16)

    out_t = jnp.dot(wout_ref[...], yt_scr[...], preferred_element_type=F32)
    finish_q()
    last_piece()
    finish_q()
    o_ref[0] = xres_ref[prev] + out_t.T


def _odd_prompt(x, ng, w_in, w_out, small, cos_t, sin_t, sinks, s_olo, s_ohi, s_gate, s_x1, *, tm):
    B, T, D = x.shape
    assert T % tm == 0 and tm % WINDOW == 0 and w_in.shape == (D, 2 * ATTN_W + 2 * KV_W) and w_out.shape == (ATTN_W, D)
    nt = T // tm
    half = HEAD_DIM // 2
    const = lambda *shape: pl.BlockSpec(shape, lambda s: (0,) * len(shape))
    n_tiles = B * nt
    proj_tile = lambda s: jnp.minimum(s, n_tiles - 1)
    attn_tile = lambda s: jnp.maximum(s - 1, 0)
    return pl.pallas_call(
        functools.partial(_odd_prompt_kernel, tm=tm, nt=nt, n_steps=n_tiles + 1),
        grid=(n_tiles + 1,),
        in_specs=[
            pl.BlockSpec((1, tm, D), lambda s: (proj_tile(s) // nt, proj_tile(s) % nt, 0)),
            const(*ng.shape),
            const(*w_in.shape),
            const(*w_out.shape),
            const(*small.shape),
            pl.BlockSpec((1, HEAD_DIM, LANES), lambda s: (proj_tile(s) % nt, 0, 0)),
            const(HEAD_DIM, tm),
            pl.BlockSpec(memory_space=pltpu.SMEM),
            const(*s_olo.shape),
            const(*s_ohi.shape),
            const(*s_gate.shape),
            const(*s_x1.shape),
        ],
        out_specs=[
            pl.BlockSpec((1, tm, D), lambda s: (attn_tile(s) // nt, attn_tile(s) % nt, 0)),
            pl.BlockSpec((1, KV_W, WINDOW), lambda s: (proj_tile(s) // nt, 0, 0)),
            pl.BlockSpec((1, KV_W, WINDOW), lambda s: (proj_tile(s) // nt, 0, 0)),
            const(s_x1.size // LANES, LANES),
        ],
        out_shape=[
            jax.ShapeDtypeStruct((B, T, D), F32),
            jax.ShapeDtypeStruct((B, KV_W, WINDOW), F32),
            jax.ShapeDtypeStruct((B, KV_W, WINDOW), F32),
            jax.ShapeDtypeStruct((s_x1.size // LANES, LANES), F32),
        ],
        scratch_shapes=[
            pltpu.VMEM((2, tm, D), F32),
            pltpu.VMEM((KV_W, D), BF16),
            pltpu.VMEM((2 * ATTN_W + KV_W, D), BF16),
            pltpu.VMEM((D, ATTN_W), BF16),
            pltpu.VMEM((2, ATTN_W, tm), BF16),
            pltpu.VMEM((2, WINDOW + tm, KV_W), BF16),
            pltpu.VMEM((2, KV_W, WINDOW + tm), BF16),
            pltpu.VMEM((2, ATTN_W, tm), F32),
            pltpu.VMEM((ATTN_W, tm), BF16),
            pltpu.VMEM((N_HEADS // 2, WINDOW, 2 * WINDOW), F32),
            pltpu.VMEM((N_HEADS // 2, WINDOW, 2 * WINDOW), BF16),
            pltpu.VMEM((N_HEADS // 2, 2 * WINDOW), F32),
        ],
        compiler_params=pltpu.CompilerParams(
            dimension_semantics=("arbitrary",), vmem_limit_bytes=VMEM_LIMIT_BYTES),
        name="odd_prompt",
    )(x, ng, w_in, w_out, small, cos_t, sin_t, sinks, s_olo, s_ohi, s_gate, s_x1)


def _dup_heads(x):
    lane = lax.broadcasted_iota(jnp.int32, x.shape, 1)
    rolled = pltpu.roll(x, HEAD_DIM, 1)
    first = lane < HEAD_DIM
    return jnp.where(first, x, rolled), jnp.where(first, rolled, x)


def _sample_dense_kernel(x_ref, sp_ref, sc_ref, ng_ref, win0_ref, pw_ref, ps_ref, cw_ref, wout0_ref,
                         win1_ref, qgain_ref, kgain_ref, cos1_ref, sin1_ref,
                         x1_ref, pool_ref, conv_ref, qh_ref, kt_ref, vt_ref, gate_ref, small_ref,
                         *scratch):
    hbm = _HbmOperands(sp_ref, win0_ref, wout0_ref, win1_ref, pool_ref, *scratch)
    sp_ref, win0_ref, wout0_ref, win1_ref = hbm.sp, hbm.win0, hbm.wout0, hbm.win1
    w_pool = ps_ref.shape[1]
    w_conv = (win0_ref.shape[1] - 2 * w_pool) // 4
    half = HEAD_DIM // 2
    qg_t, cos_t1, sin_t1 = _col_table(qgain_ref[...]), _col_table(cos1_ref[...]), _col_table(sin1_ref[...])
    small_ref[_QG_T:_QG_T + HEAD_DIM, :] = qg_t
    small_ref[_KG_T:_KG_T + HEAD_DIM, :] = _col_table(kgain_ref[...])
    small_ref[_COS_T:_COS_T + half, :] = cos_t1
    small_ref[_SIN_T:_SIN_T + half, :] = sin_t1
    second_half = lax.broadcasted_iota(jnp.int32, (1, LANES), 1) % HEAD_DIM >= half
    small_ref[_QG_ROW:_SMALL_ROWS, :] = jnp.concatenate(
        [_tiled_row(qg_t), _tiled_row(cos_t1), jnp.where(second_half, 1.0, -1.0) * _tiled_row(sin_t1),
         jnp.zeros((_SMALL_ROWS - _SIN_ROW - 1, LANES), F32)], axis=0)
    g0_ref, g1_ref = ng_ref.at[0:1], ng_ref.at[1:2]
    qg_ref, kgt_ref = small_ref.at[_QG_ROW:_QG_ROW + 1], small_ref.at[_KG_T:_KG_T + HEAD_DIM]
    cos_ref, sin_ref = small_ref.at[_COS_ROW:_COS_ROW + 1], small_ref.at[_SIN_ROW:_SIN_ROW + 1]
    cost_ref, sint_ref = small_ref.at[_COS_T:_COS_T + HEAD_DIM // 2], small_ref.at[_SIN_T:_SIN_T + HEAD_DIM // 2]

    n = x1_ref.shape[0]
    chunks = x1_ref.shape[1] // LANES
    x = jnp.concatenate([x_ref[pl.ds(c, n, stride=chunks), :] for c in range(chunks)], axis=1)
    h = _rms(x, g0_ref[...]).astype(win0_ref.dtype)

    def proj(c, width):
        return jnp.dot(h, win0_ref[:, c:c + width], preferred_element_type=F32)

    hbm.wait_win0()
    u = proj(0, w_pool)
    hbm.put_newest_pool_row(u)
    z_a = proj(w_pool, w_pool)
    hbm.wait_sp()
    ys = []
    for g, w in enumerate(POOL_WINDOWS):
        cols = slice(g * POOL_GROUP, (g + 1) * POOL_GROUP)
        wsum = u[:, cols]
        for r in range(POOL_BUF - (w - 1), POOL_BUF):
            wsum = wsum + sp_ref[r, :, cols]
        cnt = float(min(PAST_LEN + 1, w))
        d = wsum * (1.0 / cnt) - u[:, cols]
        ya = jnp.dot(d.astype(pw_ref.dtype), pw_ref[g], preferred_element_type=F32)
        ys.append((ya * ps_ref[:, cols] * _silu(z_a[:, cols])).astype(wout0_ref.dtype))

    b_gate = proj(2 * w_pool, w_conv)
    c_gate = proj(2 * w_pool + w_conv, w_conv)
    v_in = proj(2 * w_pool + 2 * w_conv, w_conv)
    v = c_gate * v_in
    cchunks = w_conv // LANES
    cstride = cchunks * CONV_BUF

    def conv_row(r):
        return jnp.concatenate(
            [sc_ref[pl.ds(j * CONV_BUF + r, n, stride=cstride), :] for j in range(cchunks)], axis=1)

    def put_conv_row(r, val):
        for j in range(cchunks):
            conv_ref[pl.ds(j * CONV_BUF + r, n, stride=cstride), :] = val[:, j * LANES:(j + 1) * LANES]

    conv = v * _conv_tap(cw_ref, CONV_WIDTH - 1, w_conv)
    for k in range(CONV_BUF):
        conv = conv + conv_row(k) * _conv_tap(cw_ref, k, w_conv)
    for r in range(CONV_BUF - 1):
        put_conv_row(r, conv_row(r + 1))
    put_conv_row(CONV_BUF - 1, v)
    z_b = proj(2 * w_pool + 3 * w_conv, w_conv)
    ys.append((b_gate * conv * _silu(z_b)).astype(wout0_ref.dtype))
    y = jnp.concatenate(ys, axis=1)
    hbm.wait_wout0()
    x1 = x + jnp.dot(y, wout0_ref[...], preferred_element_type=F32)
    x1_ref[...] = x1

    h1 = _rms(x1, g1_ref[...])
    hbm.wait_win1()

    def proj1(lo, width):
        return jnp.dot(h1, win1_ref[:, lo:lo + width], preferred_element_type=F32)

    q = proj1(0, ATTN_W)
    for c in range(ATTN_W // LANES):
        qc = _head_norm_rope(q[:, c * LANES:(c + 1) * LANES], qg_ref[...], cos_ref[...], sin_ref[...])
        d0, d1 = _dup_heads(qc * (HEAD_DIM ** -0.5))
        qh_ref[2 * c * n:(2 * c + 1) * n, :] = d0
        qh_ref[(2 * c + 1) * n:(2 * c + 2) * n, :] = d1
    kt = proj1(ATTN_W, KV_W).T
    cos_t = cost_ref[...]
    sin_t = sint_ref[...]
    for kv in range(N_KV_HEADS):
        kh = kt[kv * HEAD_DIM:(kv + 1) * HEAD_DIM, :]
        ms = jnp.sum(kh * kh, axis=0, keepdims=True) * (1.0 / HEAD_DIM)
        kn = kh * lax.rsqrt(ms + RMS_EPS) * kgt_ref[...]
        x1h, x2h = kn[0:half, :], kn[half:HEAD_DIM, :]
        kt_ref[kv * HEAD_DIM:(kv + 1) * HEAD_DIM, :] = jnp.concatenate(
            [x1h * cos_t - x2h * sin_t, x2h * cos_t + x1h * sin_t], axis=0)
    vt_ref[...] = proj1(ATTN_W + KV_W, KV_W).T
    gate_ref[...] = _silu(proj1(ATTN_W + 2 * KV_W, ATTN_W))
    hbm.finish()


class _HbmOperands:
    def __init__(self, sp_hbm, win0_hbm, wout0_hbm, win1_hbm, pool_hbm, sp, win0, wout0, win1, newest, sems):
        self.sp, self.win0, self.wout0, self.win1, self.newest = sp, win0, wout0, win1, newest
        self.pool_hbm = pool_hbm
        self.sems = sems
        rows = sp_hbm.shape[0]
        self.shift = pltpu.make_async_copy(sp.at[1:rows], pool_hbm.at[0:rows - 1], sems.at[0])
        self.fetch = {name: pltpu.make_async_copy(src, dst, sems.at[i + 1]) for i, (name, src, dst) in enumerate(
            (("win0", win0_hbm, win0), ("sp", sp_hbm, sp), ("wout0", wout0_hbm, wout0), ("win1", win1_hbm, win1)))}
        for copy in self.fetch.values():
            copy.start()

    def wait_win0(self):
        self.fetch["win0"].wait()

    def wait_sp(self):
        self.fetch["sp"].wait()
        self.shift.start()

    def wait_wout0(self):
        self.fetch["wout0"].wait()

    def wait_win1(self):
        self.fetch["win1"].wait()

    def put_newest_pool_row(self, u):
        rows = self.pool_hbm.shape[0]
        self.newest[...] = u
        self.put = pltpu.make_async_copy(self.newest, self.pool_hbm.at[rows - 1], self.sems.at[5])
        self.put.start()

    def finish(self):
        self.shift.wait()
        self.put.wait()


def _sample_dense(x, sp, sc, ng, win0, pw, ps, cw, wout0, win1, qgain, kgain, cos1, sin1):
    D = win0.shape[0]
    n = x.shape[0] * LANES // D
    assert n == LANES
    assert qgain.shape == kgain.shape == (1, HEAD_DIM) and cos1.shape == sin1.shape == (1, HEAD_DIM // 2)
    vmem = pl.BlockSpec(memory_space=pltpu.VMEM)
    hbm = pl.BlockSpec(memory_space=pl.ANY)
    in_specs = [vmem, hbm, vmem, vmem, hbm, vmem, vmem, vmem, hbm, hbm] + [vmem] * 4
    return pl.pallas_call(
        _sample_dense_kernel,
        in_specs=in_specs,
        out_specs=[vmem, hbm] + [vmem] * 6,
        scratch_shapes=[
            pltpu.VMEM(sp.shape, F32),
            pltpu.VMEM(win0.shape, F32),
            pltpu.VMEM(wout0.shape, F32),
            pltpu.VMEM(win1.shape, F32),
            pltpu.VMEM(sp.shape[1:], F32),
            pltpu.SemaphoreType.DMA((6,)),
        ],
        out_shape=[
            jax.ShapeDtypeStruct((n, D), F32),
            jax.ShapeDtypeStruct(sp.shape, F32),
            jax.ShapeDtypeStruct(sc.shape, F32),
            jax.ShapeDtypeStruct((N_HEADS * n, LANES), F32),
            jax.ShapeDtypeStruct((KV_W, n), F32),
            jax.ShapeDtypeStruct((KV_W, n), F32),
            jax.ShapeDtypeStruct((n, ATTN_W), F32),
            jax.ShapeDtypeStruct((_SMALL_ROWS, LANES), F32),
        ],
        compiler_params=pltpu.CompilerParams(vmem_limit_bytes=VMEM_LIMIT_BYTES),
        name="sample_dense",
    )(x, sp, sc, ng, win0, pw, ps, cw, wout0, win1, qgain, kgain, cos1, sin1)


def _sample_out(olo_ref, ohi_ref, gate_ref, x1_ref, wout_ref, y_ref):
    n = x1_ref.shape[0]
    blocks = LANES // HEAD_DIM
    lane_n = lax.broadcasted_iota(jnp.int32, (n, LANES), 1)
    chunks = []
    for c in range(ATTN_W // LANES):
        parts = []
        for hd in (2 * c, 2 * c + 1):
            kv = hd // GROUP
            slab = (olo_ref, ohi_ref)[kv // blocks]
            a = slab[pl.ds(hd, n, stride=N_HEADS), :]
            parts.append(a if kv % blocks == hd % blocks else pltpu.roll(a, HEAD_DIM, 1))
        chunks.append(jnp.where(lane_n < HEAD_DIM, parts[0], parts[1]))
    y = jnp.concatenate(chunks, axis=1) * gate_ref[...]
    out = x1_ref[...] + jnp.dot(y, wout_ref[...], preferred_element_type=F32)
    chunks = out.shape[1] // LANES
    for c in range(chunks):
        y_ref[pl.ds(c, n, stride=chunks), :] = out[:, c * LANES:(c + 1) * LANES]


def _rope_tables(pos):
    half = HEAD_DIM // 2
    inv = ROPE_THETA ** (-jnp.arange(half, dtype=F32) / half)
    ang = pos.astype(F32)[:, None] * inv[None, :]
    return jnp.cos(ang), jnp.sin(ang)


def kernel(x_prompt, x_sample, state_pool, state_conv, cache_k, cache_v, norm_g, w_in_even, pool_w, pool_scale,
           conv_w, w_out_even, w_in_odd, q_norm_g, k_norm_g, attn_sinks, w_out_odd):
    B, T, D = x_prompt.shape
    n_s, t_s, _ = x_sample.shape
    assert norm_g.shape[0] == 2 and w_in_even.shape[0] == 1 and w_in_odd.shape[0] == 1
    assert t_s == 1 and cache_k.shape[2] == WINDOW and T >= WINDOW
    assert cache_k.shape[3] * cache_k.shape[4] == KV_W and pool_w.shape[1:] == (len(POOL_WINDOWS), POOL_GROUP, POOL_GROUP)

    win0 = w_in_even[0]
    wout0 = w_out_even[0]
    win1 = w_in_odd[0]
    wout1 = w_out_odd[0]
    pw = pool_w[0]
    ps = pool_scale[0][None, :]
    cw = conv_w[0].reshape(-1, LANES)
    sinks = attn_sinks[0]

    cos_b, sin_b = _rope_tables(jnp.arange(T // ODD_TILE) * ODD_TILE)
    rope_base = jnp.broadcast_to(jnp.concatenate([cos_b, sin_b], axis=1)[:, :, None],
                                 (T // ODD_TILE, HEAD_DIM, LANES))
    cos_o, sin_o = _rope_tables(jnp.arange(ODD_TILE))
    rope_off = jnp.concatenate([cos_o, sin_o], axis=1).T
    cos_1, sin_1 = _rope_tables(PAST_LEN + jnp.arange(t_s))

    w_conv = state_conv.shape[-1]
    sp = jnp.transpose(state_pool[0], (1, 0, 2))
    sc = jnp.transpose(state_conv[0].reshape(n_s, CONV_BUF, w_conv // LANES, LANES), (0, 2, 1, 3)).reshape(-1, LANES)
    x1s, pool_s, conv_s, qh_s, kt_s, vt_s, gate_s, small = _sample_dense(
        x_sample.reshape(n_s * D // LANES, LANES), sp, sc, norm_g, win0, pw, ps, cw, wout0, win1,
        q_norm_g, k_norm_g, cos_1, sin_1)
    pool_s = jnp.transpose(pool_s, (1, 0, 2))
    conv_s = jnp.transpose(conv_s.reshape(n_s, w_conv // LANES, CONV_BUF, LANES), (0, 2, 1, 3)).reshape(
        n_s, CONV_BUF, w_conv)

    ckt = jnp.transpose(cache_k[0], (0, 2, 3, 1)).reshape(n_s, KV_W, WINDOW)
    cvt = jnp.transpose(cache_v[0], (0, 2, 3, 1)).reshape(n_s, KV_W, WINDOW)
    x1p, pool_p, conv_p, nkt_s, nvt_s, olo_s, ohi_s = _even_prompt(
        x_prompt, norm_g, win0, pw, ps, cw, wout0, qh_s, kt_s, vt_s, ckt, cvt, attn_sinks, tm=EVEN_TILE)
    y_p, k_p, v_p, y_s = _odd_prompt(x1p, norm_g, win1, wout1, small, rope_base, rope_off, sinks,
                                     olo_s, ohi_s, gate_s, x1s, tm=ODD_TILE)

    def window_major(a):
        return jnp.transpose(a.reshape(-1, N_KV_HEADS, HEAD_DIM, WINDOW), (0, 3, 1, 2))[None]

    return (y_p, y_s.reshape(n_s, 1, D), pool_p[None], pool_s[None], conv_p[None], conv_s[None],
            window_major(k_p), window_major(v_p), window_major(nkt_s), window_major(nvt_s))
```

```python
import functools

import jax
import jax.numpy as jnp
from jax import lax
from jax.experimental import pallas as pl
from jax.experimental.pallas import tpu as pltpu

F32 = jnp.float32
BF16 = jnp.bfloat16

POOL_WINDOWS = (2, 4, 8, 16)
POOL_GROUP = 128
POOL_BUF = max(POOL_WINDOWS) - 1
CONV_WIDTH = 3
CONV_BUF = CONV_WIDTH - 1
N_HEADS = 16
HEAD_DIM = 64
N_KV_HEADS = 4
GROUP = N_HEADS // N_KV_HEADS
WINDOW = 128
ROPE_THETA = 10000.0
RMS_EPS = 1e-6
PAST_LEN = 16384
ATTN_W = N_HEADS * HEAD_DIM
KV_W = N_KV_HEADS * HEAD_DIM

LANES = 128
POOL_HALO = 16
CONV_HALO = 8
VMEM_LIMIT_BYTES = 60 * 1024 * 1024
EVEN_TILE = 1024
ODD_TILE = 1024
BF16_SUBLANES = 16
LOG2_E = 1.4426950408889634
_QG_T, _KG_T, _COS_T, _SIN_T = 0, HEAD_DIM, 2 * HEAD_DIM, 2 * HEAD_DIM + HEAD_DIM // 2
_QG_ROW, _COS_ROW, _SIN_ROW, _SMALL_ROWS = 3 * HEAD_DIM, 3 * HEAD_DIM + 1, 3 * HEAD_DIM + 2, 3 * HEAD_DIM + 8


def _rms(x, g):
    ms = jnp.mean(x * x, axis=-1, keepdims=True)
    return x * lax.rsqrt(ms + RMS_EPS) * g


def _silu(z):
    return z * jax.nn.sigmoid(z)


def _head_norm_rope(x, gain, cos, sin):
    lane = lax.broadcasted_iota(jnp.int32, x.shape, 1)
    first = lane < HEAD_DIM
    x2 = x * x
    ss0 = jnp.sum(jnp.where(first, x2, 0.0), axis=-1, keepdims=True)
    ss1 = jnp.sum(jnp.where(first, 0.0, x2), axis=-1, keepdims=True)
    r = jnp.where(first, lax.rsqrt(ss0 * (1.0 / HEAD_DIM) + RMS_EPS), lax.rsqrt(ss1 * (1.0 / HEAD_DIM) + RMS_EPS))
    xn = x * r * gain
    half = HEAD_DIM // 2
    swapped = jnp.where((lane % HEAD_DIM) < half, pltpu.roll(xn, LANES - half, 1), pltpu.roll(xn, half, 1))
    return xn * cos + swapped * sin


def _nt_dot(a, b):
    return lax.dot_general(a, b, (((1,), (1,)), ((), ())), preferred_element_type=F32)


def _col_table(row):
    m = row.shape[1]
    diag = lax.broadcasted_iota(jnp.int32, (m, m), 0) == lax.broadcasted_iota(jnp.int32, (m, m), 1)
    col = jnp.sum(jnp.where(diag, jnp.broadcast_to(row, (m, m)), 0.0), axis=1, keepdims=True)
    return jnp.broadcast_to(col, (m, LANES))


def _tiled_row(table):
    m = table.shape[0]
    pick = lax.broadcasted_iota(jnp.int32, table.shape, 0) == lax.broadcasted_iota(jnp.int32, table.shape, 1) % m
    return jnp.sum(jnp.where(pick, table, 0.0), axis=0, keepdims=True)


def _conv_tap(cw_ref, k, width):
    chunks = width // LANES
    return jnp.concatenate([cw_ref[k * chunks + j:k * chunks + j + 1, :] for j in range(chunks)], axis=1)


class _SampleAttention:
    def __init__(self, qh_ref, knt_ref, vnt_ref, ck_ref, cv_ref, sink_ref, nk_ref, nv_ref, olo_ref, ohi_ref, *, step):
        self.refs = (qh_ref, knt_ref, vnt_ref, ck_ref, cv_ref, sink_ref, nk_ref, nv_ref, olo_ref, ohi_ref)
        self.bb = ck_ref.shape[0]
        self.first = step * self.bb

    def roll(self):
        _, knt_ref, vnt_ref, ck_ref, cv_ref, _, nk_ref, nv_ref, _, _ = self.refs
        win = ck_ref.shape[2]
        newest = lax.broadcasted_iota(jnp.int32, (KV_W, win), 1) == win - 1
        for i in range(self.bb):
            b = self.first + i
            nk_ref[i] = jnp.where(newest, pltpu.roll(knt_ref[...], win - 1 - b, 1), pltpu.roll(ck_ref[i], win - 1, 1))
            nv_ref[i] = jnp.where(newest, pltpu.roll(vnt_ref[...], win - 1 - b, 1), pltpu.roll(cv_ref[i], win - 1, 1))

    def scores(self):
        qh_ref, nk_ref = self.refs[0], self.refs[6]
        n = qh_ref.shape[0] // N_HEADS
        blocks = LANES // HEAD_DIM
        hrow = lax.broadcasted_iota(jnp.int32, (N_HEADS, LANES), 0) // GROUP
        hcol = lax.broadcasted_iota(jnp.int32, (N_HEADS, LANES), 1) // HEAD_DIM
        self.s = []
        for i in range(self.bb):
            qd = qh_ref[pl.ds(self.first + i, N_HEADS, stride=n), :]
            qx = jnp.concatenate(
                [jnp.where(hcol + blocks * c == hrow, qd, 0.0) for c in range(KV_W // LANES)], axis=1)
            self.s.append(jnp.dot(qx.astype(BF16), nk_ref[i].astype(BF16), preferred_element_type=F32))

    def softmax(self):
        sink = _col_table(self.refs[5][...])[:, 0:1]
        self.p = []
        for s in self.s:
            m = jnp.maximum(jnp.max(s, axis=-1, keepdims=True), sink)
            p = jnp.exp(s - m)
            denom = jnp.sum(p, axis=-1, keepdims=True) + jnp.exp(sink - m)
            self.p.append((p * (1.0 / denom)).astype(BF16))

    def outputs(self):
        nv_ref, olo_ref, ohi_ref = self.refs[7:10]
        for i in range(self.bb):
            o = _nt_dot(self.p[i], nv_ref[i].astype(BF16))
            rows = pl.ds(pl.multiple_of((self.first + i) * N_HEADS, N_HEADS), N_HEADS)
            olo_ref[rows, :] = o[:, 0:LANES]
            ohi_ref[rows, :] = o[:, LANES:2 * LANES]


def _even_prompt_kernel(x_ref, g_ref, win_ref, pw_ref, ps_ref, cw_ref, wout_ref,
                        qh_ref, knt_ref, vnt_ref, ck_ref, cv_ref, sink_ref,
                        o_ref, pool_ref, conv_ref, nk_ref, nv_ref, olo_ref, ohi_ref,
                        uext, vext, sa, sb, y_scr, *, tm):
    t = pl.program_id(1)
    w_pool = uext.shape[1]
    w_conv = vext.shape[1]
    base = 2 * POOL_HALO
    side = _SampleAttention(qh_ref, knt_ref, vnt_ref, ck_ref, cv_ref, sink_ref, nk_ref, nv_ref, olo_ref, ohi_ref,
                            step=pl.program_id(0) * pl.num_programs(1) + t)

    @pl.when(t == 0)
    def _():
        uext[0:base, :] = jnp.zeros((base, w_pool), F32)
        vext[0:CONV_HALO, :] = jnp.zeros((CONV_HALO, w_conv), F32)
        sa[0:POOL_HALO, :] = jnp.zeros((POOL_HALO, POOL_GROUP), F32)
        sb[0:POOL_HALO, :] = jnp.zeros((POOL_HALO, POOL_GROUP), F32)

    x = x_ref[0]
    h = _rms(x, g_ref[0:1, :]).astype(win_ref.dtype)

    def proj(c, width):
        return jnp.dot(h, win_ref[:, c:c + width], preferred_element_type=F32)

    side.roll()
    u = proj(0, w_pool)
    uext[base:base + tm, :] = u
    pos = t * tm + lax.broadcasted_iota(jnp.int32, (tm, 1), 0)
    side.scores()
    z_a = proj(w_pool, w_pool)
    n_ext = POOL_HALO + tm

    for g, w in enumerate(POOL_WINDOWS):
        lo = g * POOL_GROUP
        cols = slice(lo, lo + POOL_GROUP)
        src = uext
        src_cols = cols
        step = 1
        bufs = (sa, sb)
        nbuf = 0
        while 2 * step < w:
            dst = bufs[nbuf % 2]
            dst[POOL_HALO:POOL_HALO + n_ext, :] = (src[POOL_HALO:POOL_HALO + n_ext, src_cols]
                                                   + src[POOL_HALO - step:POOL_HALO - step + n_ext, src_cols])
            src, src_cols = dst, slice(0, POOL_GROUP)
            step *= 2
            nbuf += 1
        wsum = src[base:base + tm, src_cols] + src[base - step:base - step + tm, src_cols]
        cnt = jnp.minimum(pos + 1, w).astype(F32)
        d = wsum * (1.0 / cnt) - u[:, cols]
        ya = jnp.dot(d.astype(pw_ref.dtype), pw_ref[g], preferred_element_type=F32)
        ya = ya * ps_ref[:, cols] * _silu(z_a[:, cols])
        y_scr[:, cols] = ya.astype(y_scr.dtype)

    b_gate = proj(2 * w_pool, w_conv)
    side.softmax()
    c_gate = proj(2 * w_pool + w_conv, w_conv)
    side.outputs()
    v_in = proj(2 * w_pool + 2 * w_conv, w_conv)
    v = c_gate * v_in
    vext[CONV_HALO:CONV_HALO + tm, :] = v
    conv = v * _conv_tap(cw_ref, CONV_WIDTH - 1, w_conv)
    for k in range(CONV_WIDTH - 1):
        shift = CONV_WIDTH - 1 - k
        conv = conv + vext[CONV_HALO - shift:CONV_HALO - shift + tm, :] * _conv_tap(cw_ref, k, w_conv)
    z_b = proj(2 * w_pool + 3 * w_conv, w_conv)
    y_b = b_gate * conv * _silu(z_b)
    y_scr[:, w_pool:w_pool + w_conv] = y_b.astype(y_scr.dtype)

    o_ref[0] = x + jnp.dot(y_scr[...], wout_ref[...], preferred_element_type=F32)
    pool_ref[0] = uext[base + tm - POOL_BUF:base + tm, :]
    conv_ref[0] = vext[CONV_HALO + tm - CONV_BUF:CONV_HALO + tm, :]
    uext[POOL_HALO:base, :] = uext[POOL_HALO + tm:base + tm, :]
    vext[0:CONV_HALO, :] = vext[tm:tm + CONV_HALO, :]


def _even_prompt(x, g, win, pw, ps, cw, wout, qh, knt, vnt, ckt, cvt, sink_col, *, tm):
    B, T, D = x.shape
    w_pool = ps.shape[-1]
    w_conv = (win.shape[1] - 2 * w_pool) // 4
    assert cw.shape == (CONV_WIDTH * w_conv // LANES, LANES)
    assert T % tm == 0 and tm % 16 == 0 and tm >= POOL_HALO
    nt = T // tm
    n, kvw, win_len = ckt.shape
    assert n % (B * nt) == 0 and kvw == KV_W == 2 * LANES and win_len == LANES and n == LANES
    bb = n // (B * nt)
    const = lambda *shape: pl.BlockSpec(shape, lambda b, t: (0,) * len(shape))
    cache_block = pl.BlockSpec((bb, kvw, win_len), lambda b, t: (b * nt + t, 0, 0))
    return pl.pallas_call(
        functools.partial(_even_prompt_kernel, tm=tm),
        grid=(B, nt),
        in_specs=[
            pl.BlockSpec((1, tm, D), lambda b, t: (b, t, 0)),
            const(*g.shape),
            const(*win.shape),
            const(*pw.shape),
            const(1, w_pool),
            const(*cw.shape),
            const(*wout.shape),
            const(*qh.shape),
            const(*knt.shape),
            const(*vnt.shape),
            cache_block,
            cache_block,
            const(*sink_col.shape),
        ],
        out_specs=[
            pl.BlockSpec((1, tm, D), lambda b, t: (b, t, 0)),
            pl.BlockSpec((1, POOL_BUF, w_pool), lambda b, t: (b, 0, 0)),
            pl.BlockSpec((1, CONV_BUF, w_conv), lambda b, t: (b, 0, 0)),
            cache_block,
            cache_block,
            const(n * N_HEADS, LANES),
            const(n * N_HEADS, LANES),
        ],
        out_shape=[
            jax.ShapeDtypeStruct((B, T, D), F32),
            jax.ShapeDtypeStruct((B, POOL_BUF, w_pool), F32),
            jax.ShapeDtypeStruct((B, CONV_BUF, w_conv), F32),
            jax.ShapeDtypeStruct((n, kvw, win_len), F32),
            jax.ShapeDtypeStruct((n, kvw, win_len), F32),
            jax.ShapeDtypeStruct((n * N_HEADS, LANES), F32),
            jax.ShapeDtypeStruct((n * N_HEADS, LANES), F32),
        ],
        scratch_shapes=[
            pltpu.VMEM((2 * POOL_HALO + tm, w_pool), F32),
            pltpu.VMEM((CONV_HALO + tm, w_conv), F32),
            pltpu.VMEM((2 * POOL_HALO + tm, POOL_GROUP), F32),
            pltpu.VMEM((2 * POOL_HALO + tm, POOL_GROUP), F32),
            pltpu.VMEM((tm, w_pool + w_conv), wout.dtype),
        ],
        compiler_params=pltpu.CompilerParams(
            dimension_semantics=("arbitrary", "arbitrary"), vmem_limit_bytes=VMEM_LIMIT_BYTES),
        name="even_prompt",
    )(x, g, win, pw, ps, cw, wout, qh, knt, vnt, ckt, cvt, sink_col)


def _odd_prompt_kernel(*refs, tm, nt, n_steps):
    (x_ref, xres_ref, ng_ref, wkt_ref, wqvz_ref, wout_ref, small_ref, ropeb_ref, ropeo_ref, sink_ref,
     olo_ref, ohi_ref, sgate_ref, sx1_ref,
     o_ref, knew_ref, vnew_ref, sy_ref, qt_scr, kext, vt_ext, gate_scr, *rest) = refs
    g_ref = ng_ref.at[1:2]
    qg_ref, kg_ref = small_ref.at[_QG_T:_QG_T + HEAD_DIM], small_ref.at[_KG_T:_KG_T + HEAD_DIM]

    step = pl.program_id(0)
    last = n_steps - 1

    @pl.when(step == 0)
    def _():
        _sample_out(olo_ref, ohi_ref, sgate_ref, sx1_ref, wout_ref, sy_ref)
        kext[...] = jnp.zeros(kext.shape, BF16)
        vt_ext[...] = jnp.zeros(vt_ext.shape, BF16)

    def run(cur, **halves):
        _odd_prompt_step(x_ref, xres_ref, g_ref, wkt_ref, wqvz_ref, qg_ref, kg_ref, ropeb_ref, ropeo_ref, sink_ref,
                         wout_ref, o_ref, knew_ref, vnew_ref, qt_scr, kext, vt_ext, gate_scr, *rest,
                         tm=tm, nt=nt, cur=cur, prev=1 - cur, **halves)

    pl.when(step == 0)(lambda: run(0, do_attn=False))
    for parity in range(2):
        pl.when((step % 2 == parity) & (step > 0) & (step < last))(lambda parity=parity: run(parity))
    pl.when(step == last)(lambda: run((n_steps - 1) % 2, do_proj=False))


def _odd_prompt_step(x_ref, xres_ref, g_ref, wkt_ref, wqvz_ref, qg_ref, kg_ref, ropeb_ref, ropeo_ref, sink_ref,
                     wout_ref, o_ref, knew_ref, vnew_ref,
                     qt_scr, kext, vt_ext, gate_scr, yt_scr, s_scr, p_scr, esink_scr, *, tm, nt, cur, prev,
                     do_proj=True, do_attn=True):
    step = pl.program_id(0)
    proj_t = step % nt
    attn_t = jnp.maximum(step - 1, 0) % nt
    nblk = tm // WINDOW
    half = HEAD_DIM // 2

    h = _rms(x_ref[0], g_ref[...]).astype(BF16) if do_proj else None
    reps = tm // LANES
    if do_proj:
        cos_b = jnp.concatenate([ropeb_ref[0, 0:half, :]] * reps, axis=1)
        sin_b = jnp.concatenate([ropeb_ref[0, half:HEAD_DIM, :]] * reps, axis=1)
        cos_o, sin_o = ropeo_ref[0:half, :], ropeo_ref[half:HEAD_DIM, :]
        cos_t = cos_b * cos_o - sin_b * sin_o
        sin_t = sin_b * cos_o + cos_b * sin_o

    def norm_rope(xt, gain):
        ms = jnp.sum(xt * xt, axis=0, keepdims=True) * (1.0 / HEAD_DIM)
        xn = xt * lax.rsqrt(ms + RMS_EPS) * gain
        x1, x2 = xn[0:half, :], xn[half:HEAD_DIM, :]
        return jnp.concatenate([x1 * cos_t - x2 * sin_t, x2 * cos_t + x1 * sin_t], axis=0)

    has_past = proj_t > 0

    def proj_kv():
        kt = _nt_dot(wkt_ref[...], h)
        kgain = jnp.concatenate([kg_ref[...]] * reps, axis=1)
        kt = jnp.concatenate(
            [norm_rope(kt[kv * HEAD_DIM:(kv + 1) * HEAD_DIM, :], kgain) for kv in range(N_KV_HEADS)], axis=0)
        knew_ref[0] = kt[:, tm - WINDOW:tm]
        kext[cur, 0:WINDOW, :] = jnp.where(has_past, kext[prev, tm:tm + WINDOW, :], jnp.zeros((WINDOW, KV_W), BF16))
        kext[cur, WINDOW:WINDOW + tm, :] = kt.T.astype(BF16)
        vt = _nt_dot(wqvz_ref[ATTN_W:ATTN_W + KV_W, :], h)
        vnew_ref[0] = vt[:, tm - WINDOW:tm]
        vt_ext[cur, :, 0:WINDOW] = jnp.where(has_past, vt_ext[prev, :, tm:tm + WINDOW],
                                             jnp.zeros((KV_W, WINDOW), BF16))
        vt_ext[cur, :, WINDOW:WINDOW + tm] = vt.astype(BF16)

    def proj_gate(lo, hi):
        zt = _nt_dot(wqvz_ref[ATTN_W + KV_W + lo:ATTN_W + KV_W + hi, :], h)
        gate_scr[cur, lo:hi, :] = _silu(zt)

    q_raw = {}

    def proj_q(lo, hi):
        q_raw[lo] = _nt_dot(wqvz_ref[lo:hi, :], h)

    def finish_q():
        qgain = jnp.concatenate([qg_ref[...]] * reps, axis=1)
        for lo, qt in q_raw.items():
            for r0 in range(0, qt.shape[0], HEAD_DIM):
                qt_scr[cur, lo + r0:lo + r0 + HEAD_DIM, :] = (
                    norm_rope(qt[r0:r0 + HEAD_DIM, :], qgain) * (LOG2_E * HEAD_DIM ** -0.5)).astype(BF16)
        q_raw.clear()

    pieces = [proj_kv,
              lambda: proj_gate(0, ATTN_W // 2),
              lambda: proj_gate(ATTN_W // 2, ATTN_W),
              lambda: proj_q(0, ATTN_W // 2)]
    last_piece = lambda: proj_q(ATTN_W // 2, ATTN_W)
    if not do_proj:
        pieces, last_piece = [], lambda: None
    if not do_attn:
        for piece in pieces:
            piece()
        finish_q()
        last_piece()
        finish_q()
        return

    ri = lax.broadcasted_iota(jnp.int32, (WINDOW, 2 * WINDOW), 0)
    qq = lax.broadcasted_iota(jnp.int32, (WINDOW, 2 * WINDOW), 1) % WINDOW
    from_prev = ri > qq
    keep_prev = from_prev.astype(BF16)
    keep_cur = 1.0 - keep_prev
    lane = lax.broadcasted_iota(jnp.int32, (1, 2 * WINDOW), 1)
    zeros = jnp.zeros((HEAD_DIM, 2 * WINDOW), BF16)
    ones = jnp.ones((BF16_SUBLANES, 2 * WINDOW), BF16)

    for i in range(nblk):
        c0 = i * WINDOW
        qcols = slice(c0, c0 + WINDOW)
        pairs = [(kv, kv * GROUP + 2 * pr) for kv in range(N_KV_HEADS) for pr in range(GROUP // 2)]
        for j, (kv, ha) in enumerate(pairs):
            chunk, pos = divmod(kv, LANES // HEAD_DIM)
            kblk = kext[prev, c0:c0 + 2 * WINDOW, chunk * LANES:(chunk + 1) * LANES]
            qpair = jnp.concatenate([qt_scr[prev, ha * HEAD_DIM:(ha + 1) * HEAD_DIM, qcols],
                                     qt_scr[prev, (ha + 1) * HEAD_DIM:(ha + 2) * HEAD_DIM, qcols]], axis=1)
            rhs = jnp.concatenate([qpair, zeros] if pos == 0 else [zeros, qpair], axis=0)
            s = jnp.dot(kblk, rhs, preferred_element_type=F32)
            s_prev = s[0:WINDOW, :]
            if i == 0:
                s_prev = jnp.where(attn_t == 0, -jnp.inf, s_prev)
            s_scr[j] = jnp.where(from_prev, s_prev, s[WINDOW:2 * WINDOW, :])
        for c, piece in enumerate(pieces):
            if c * nblk // len(pieces) == i:
                piece()
        for j, (kv, ha) in enumerate(pairs):
            s = s_scr[j]
            sink = jnp.where(lane < WINDOW, sink_ref[ha], sink_ref[ha + 1]) * LOG2_E
            m = jnp.maximum(jnp.max(s, axis=0, keepdims=True), sink)
            p_scr[j] = jnp.exp2(s - m).astype(BF16)
            esink_scr[j:j + 1, :] = jnp.exp2(sink - m)
        for j, (kv, ha) in enumerate(pairs):
            vg = vt_ext[prev, kv * HEAD_DIM:(kv + 1) * HEAD_DIM, c0:c0 + 2 * WINDOW]
            p = p_scr[j]
            p_keys = jnp.concatenate([p * keep_prev, p * keep_cur], axis=0)
            oa = jnp.dot(jnp.concatenate([vg, ones], axis=0), p_keys, preferred_element_type=F32)
            denom = oa[HEAD_DIM:HEAD_DIM + 1, :] + esink_scr[j:j + 1, :]
            o = oa[0:HEAD_DIM, :] * (1.0 / denom)
            for hd, part in ((ha, o[:, 0:WINDOW]), (ha + 1, o[:, WINDOW:2 * WINDOW])):
                rows = slice(hd * HEAD_DIM, (hd + 1) * HEAD_DIM)
                yt_scr[rows, qcols] = (part * gate_scr[prev, rows, qcols]).astype(BF16)

    out_t = jnp.dot(wout_ref[...], yt_scr[...], preferred_element_type=F32)
    finish_q()
    last_piece()
    finish_q()
    o_ref[0] = xres_ref[0] + out_t.T


def _odd_prompt(x, ng, wk_t, wqvz_t, wout_t, small, cos_t, sin_t, sinks, s_olo, s_ohi, s_gate, s_x1, *, tm):
    B, T, D = x.shape
    assert T % tm == 0 and tm % WINDOW == 0
    assert wk_t.shape == (KV_W, D) and wqvz_t.shape == (2 * ATTN_W + KV_W, D) and wout_t.shape == (D, ATTN_W)
    nt = T // tm
    half = HEAD_DIM // 2
    const = lambda *shape: pl.BlockSpec(shape, lambda s: (0,) * len(shape))
    n_tiles = B * nt
    proj_tile = lambda s: jnp.minimum(s, n_tiles - 1)
    attn_tile = lambda s: jnp.maximum(s - 1, 0)
    return pl.pallas_call(
        functools.partial(_odd_prompt_kernel, tm=tm, nt=nt, n_steps=n_tiles + 1),
        grid=(n_tiles + 1,),
        in_specs=[
            pl.BlockSpec((1, tm, D), lambda s: (proj_tile(s) // nt, proj_tile(s) % nt, 0)),
            pl.BlockSpec((1, tm, D), lambda s: (attn_tile(s) // nt, attn_tile(s) % nt, 0)),
            const(*ng.shape),
            const(*wk_t.shape),
            const(*wqvz_t.shape),
            const(*wout_t.shape),
            const(*small.shape),
            pl.BlockSpec((1, HEAD_DIM, LANES), lambda s: (proj_tile(s) % nt, 0, 0)),
            const(HEAD_DIM, tm),
            pl.BlockSpec(memory_space=pltpu.SMEM),
            const(*s_olo.shape),
            const(*s_ohi.shape),
            const(*s_gate.shape),
            const(*s_x1.shape),
        ],
        out_specs=[
            pl.BlockSpec((1, tm, D), lambda s: (attn_tile(s) // nt, attn_tile(s) % nt, 0)),
            pl.BlockSpec((1, KV_W, WINDOW), lambda s: (proj_tile(s) // nt, 0, 0)),
            pl.BlockSpec((1, KV_W, WINDOW), lambda s: (proj_tile(s) // nt, 0, 0)),
            const(s_x1.size // LANES, LANES),
        ],
        out_shape=[
            jax.ShapeDtypeStruct((B, T, D), F32),
            jax.ShapeDtypeStruct((B, KV_W, WINDOW), F32),
            jax.ShapeDtypeStruct((B, KV_W, WINDOW), F32),
            jax.ShapeDtypeStruct((s_x1.size // LANES, LANES), F32),
        ],
        scratch_shapes=[
            pltpu.VMEM((2, ATTN_W, tm), BF16),
            pltpu.VMEM((2, WINDOW + tm, KV_W), BF16),
            pltpu.VMEM((2, KV_W, WINDOW + tm), BF16),
            pltpu.VMEM((2, ATTN_W, tm), F32),
            pltpu.VMEM((ATTN_W, tm), BF16),
            pltpu.VMEM((N_HEADS // 2, WINDOW, 2 * WINDOW), F32),
            pltpu.VMEM((N_HEADS // 2, WINDOW, 2 * WINDOW), BF16),
            pltpu.VMEM((N_HEADS // 2, 2 * WINDOW), F32),
        ],
        compiler_params=pltpu.CompilerParams(
            dimension_semantics=("arbitrary",), vmem_limit_bytes=VMEM_LIMIT_BYTES),
        name="odd_prompt",
    )(x, x, ng, wk_t, wqvz_t, wout_t, small, cos_t, sin_t, sinks, s_olo, s_ohi, s_gate, s_x1)


def _dup_heads(x):
    lane = lax.broadcasted_iota(jnp.int32, x.shape, 1)
    rolled = pltpu.roll(x, HEAD_DIM, 1)
    first = lane < HEAD_DIM
    return jnp.where(first, x, rolled), jnp.where(first, rolled, x)


def _sample_dense_kernel(x_ref, sp_ref, sc_ref, ng_ref, win0_ref, pw_ref, ps_ref, cw_ref, wout0_ref,
                         win1_ref, wout1_ref, qgain_ref, kgain_ref, cos1_ref, sin1_ref,
                         x1_ref, pool_ref, conv_ref, qh_ref, kt_ref, vt_ref, gate_ref, small_ref,
                         wqvz_t_ref, wk_t_ref, wout1_t_ref, *scratch):
    hbm = _HbmOperands(sp_ref, win0_ref, wout0_ref, win1_ref, wout1_ref, pool_ref, *scratch)
    sp_ref, win0_ref, wout0_ref, win1_ref, wout1_ref = hbm.sp, hbm.win0, hbm.wout0, hbm.win1, hbm.wout1
    w_pool = ps_ref.shape[1]
    w_conv = (win0_ref.shape[1] - 2 * w_pool) // 4
    half = HEAD_DIM // 2
    qg_t, cos_t1, sin_t1 = _col_table(qgain_ref[...]), _col_table(cos1_ref[...]), _col_table(sin1_ref[...])
    small_ref[_QG_T:_QG_T + HEAD_DIM, :] = qg_t
    small_ref[_KG_T:_KG_T + HEAD_DIM, :] = _col_table(kgain_ref[...])
    small_ref[_COS_T:_COS_T + half, :] = cos_t1
    small_ref[_SIN_T:_SIN_T + half, :] = sin_t1
    second_half = lax.broadcasted_iota(jnp.int32, (1, LANES), 1) % HEAD_DIM >= half
    small_ref[_QG_ROW:_SMALL_ROWS, :] = jnp.concatenate(
        [_tiled_row(qg_t), _tiled_row(cos_t1), jnp.where(second_half, 1.0, -1.0) * _tiled_row(sin_t1),
         jnp.zeros((_SMALL_ROWS - _SIN_ROW - 1, LANES), F32)], axis=0)
    g0_ref, g1_ref = ng_ref.at[0:1], ng_ref.at[1:2]
    qg_ref, kgt_ref = small_ref.at[_QG_ROW:_QG_ROW + 1], small_ref.at[_KG_T:_KG_T + HEAD_DIM]
    cos_ref, sin_ref = small_ref.at[_COS_ROW:_COS_ROW + 1], small_ref.at[_SIN_ROW:_SIN_ROW + 1]
    cost_ref, sint_ref = small_ref.at[_COS_T:_COS_T + HEAD_DIM // 2], small_ref.at[_SIN_T:_SIN_T + HEAD_DIM // 2]

    n = x1_ref.shape[0]
    chunks = x1_ref.shape[1] // LANES
    x = jnp.concatenate([x_ref[pl.ds(c, n, stride=chunks), :] for c in range(chunks)], axis=1)
    h = _rms(x, g0_ref[...]).astype(win0_ref.dtype)

    def proj(c, width):
        return jnp.dot(h, win0_ref[:, c:c + width], preferred_element_type=F32)

    hbm.wait_win0()
    u = proj(0, w_pool)
    hbm.put_newest_pool_row(u)
    z_a = proj(w_pool, w_pool)
    hbm.wait_sp()
    ys = []
    for g, w in enumerate(POOL_WINDOWS):
        cols = slice(g * POOL_GROUP, (g + 1) * POOL_GROUP)
        wsum = u[:, cols]
        for r in range(POOL_BUF - (w - 1), POOL_BUF):
            wsum = wsum + sp_ref[r, :, cols]
        cnt = float(min(PAST_LEN + 1, w))
        d = wsum * (1.0 / cnt) - u[:, cols]
        ya = jnp.dot(d.astype(pw_ref.dtype), pw_ref[g], preferred_element_type=F32)
        ys.append((ya * ps_ref[:, cols] * _silu(z_a[:, cols])).astype(wout0_ref.dtype))

    b_gate = proj(2 * w_pool, w_conv)
    c_gate = proj(2 * w_pool + w_conv, w_conv)
    v_in = proj(2 * w_pool + 2 * w_conv, w_conv)
    v = c_gate * v_in
    cchunks = w_conv // LANES
    cstride = cchunks * CONV_BUF

    def conv_row(r):
        return jnp.concatenate(
            [sc_ref[pl.ds(j * CONV_BUF + r, n, stride=cstride), :] for j in range(cchunks)], axis=1)

    def put_conv_row(r, val):
        for j in range(cchunks):
            conv_ref[pl.ds(j * CONV_BUF + r, n, stride=cstride), :] = val[:, j * LANES:(j + 1) * LANES]

    conv = v * _conv_tap(cw_ref, CONV_WIDTH - 1, w_conv)
    for k in range(CONV_BUF):
        conv = conv + conv_row(k) * _conv_tap(cw_ref, k, w_conv)
    for r in range(CONV_BUF - 1):
        put_conv_row(r, conv_row(r + 1))
    put_conv_row(CONV_BUF - 1, v)
    z_b = proj(2 * w_pool + 3 * w_conv, w_conv)
    ys.append((b_gate * conv * _silu(z_b)).astype(wout0_ref.dtype))
    y = jnp.concatenate(ys, axis=1)
    hbm.wait_wout0()
    x1 = x + jnp.dot(y, wout0_ref[...], preferred_element_type=F32)
    x1_ref[...] = x1

    h1 = _rms(x1, g1_ref[...])
    hbm.wait_win1()

    def proj1(lo, width):
        return jnp.dot(h1, win1_ref[:, lo:lo + width], preferred_element_type=F32)

    q = proj1(0, ATTN_W)
    for c in range(ATTN_W // LANES):
        qc = _head_norm_rope(q[:, c * LANES:(c + 1) * LANES], qg_ref[...], cos_ref[...], sin_ref[...])
        d0, d1 = _dup_heads(qc * (HEAD_DIM ** -0.5))
        qh_ref[2 * c * n:(2 * c + 1) * n, :] = d0
        qh_ref[(2 * c + 1) * n:(2 * c + 2) * n, :] = d1
    kt = proj1(ATTN_W, KV_W).T
    cos_t = cost_ref[...]
    sin_t = sint_ref[...]
    for kv in range(N_KV_HEADS):
        kh = kt[kv * HEAD_DIM:(kv + 1) * HEAD_DIM, :]
        ms = jnp.sum(kh * kh, axis=0, keepdims=True) * (1.0 / HEAD_DIM)
        kn = kh * lax.rsqrt(ms + RMS_EPS) * kgt_ref[...]
        x1h, x2h = kn[0:half, :], kn[half:HEAD_DIM, :]
        kt_ref[kv * HEAD_DIM:(kv + 1) * HEAD_DIM, :] = jnp.concatenate(
            [x1h * cos_t - x2h * sin_t, x2h * cos_t + x1h * sin_t], axis=0)
    vt_ref[...] = proj1(ATTN_W + KV_W, KV_W).T
    gate_ref[...] = _silu(proj1(ATTN_W + 2 * KV_W, ATTN_W))

    blk = 2 * LANES
    for src, dst, width in ((0, 0, ATTN_W), (ATTN_W + KV_W, ATTN_W, KV_W + ATTN_W)):
        for c in range(0, width, blk):
            wqvz_t_ref[dst + c:dst + c + blk, :] = win1_ref[:, src + c:src + c + blk].T.astype(BF16)
    wk_t_ref[...] = win1_ref[:, ATTN_W:ATTN_W + KV_W].T.astype(BF16)
    hbm.wait_wout1()
    for c in range(0, wout1_ref.shape[1], blk):
        wout1_t_ref[c:c + blk, :] = wout1_ref[:, c:c + blk].T.astype(BF16)
    hbm.finish()


class _HbmOperands:
    def __init__(self, sp_hbm, win0_hbm, wout0_hbm, win1_hbm, wout1_hbm, pool_hbm,
                 sp, win0, wout0, win1, wout1, newest, sems):
        self.sp, self.win0, self.wout0, self.win1, self.wout1, self.newest = sp, win0, wout0, win1, wout1, newest
        self.pool_hbm = pool_hbm
        self.sems = sems
        rows = sp_hbm.shape[0]
        self.shift = pltpu.make_async_copy(sp.at[1:rows], pool_hbm.at[0:rows - 1], sems.at[0])
        self.fetch = {name: pltpu.make_async_copy(src, dst, sems.at[i + 2]) for i, (name, src, dst) in enumerate(
            (("win0", win0_hbm, win0), ("sp", sp_hbm, sp), ("wout0", wout0_hbm, wout0), ("win1", win1_hbm, win1),
             ("wout1", wout1_hbm, wout1)))}
        for copy in self.fetch.values():
            copy.start()

    def wait_wout1(self):
        self.fetch["wout1"].wait()

    def wait_win0(self):
        self.fetch["win0"].wait()

    def wait_sp(self):
        self.fetch["sp"].wait()
        self.shift.start()

    def wait_wout0(self):
        self.fetch["wout0"].wait()

    def wait_win1(self):
        self.fetch["win1"].wait()

    def put_newest_pool_row(self, u):
        rows = self.pool_hbm.shape[0]
        self.newest[...] = u
        self.put = pltpu.make_async_copy(self.newest, self.pool_hbm.at[rows - 1], self.sems.at[1])
        self.put.start()

    def finish(self):
        self.shift.wait()
        self.put.wait()


def _sample_dense(x, sp, sc, ng, win0, pw, ps, cw, wout0, win1, wout1, qgain, kgain, cos1, sin1):
    D = win0.shape[0]
    n = x.shape[0] * LANES // D
    assert n == LANES
    assert qgain.shape == kgain.shape == (1, HEAD_DIM) and cos1.shape == sin1.shape == (1, HEAD_DIM // 2)
    assert win1.shape == (D, 2 * ATTN_W + 2 * KV_W) and wout1.shape == (ATTN_W, D)
    vmem = pl.BlockSpec(memory_space=pltpu.VMEM)
    hbm = pl.BlockSpec(memory_space=pl.ANY)
    in_specs = [vmem, hbm, vmem, vmem, hbm, vmem, vmem, vmem, hbm, hbm, hbm] + [vmem] * 4
    return pl.pallas_call(
        _sample_dense_kernel,
        in_specs=in_specs,
        out_specs=[vmem, hbm] + [vmem] * 9,
        scratch_shapes=[
            pltpu.VMEM(sp.shape, F32),
            pltpu.VMEM(win0.shape, F32),
            pltpu.VMEM(wout0.shape, F32),
            pltpu.VMEM(win1.shape, F32),
            pltpu.VMEM(wout1.shape, F32),
            pltpu.VMEM(sp.shape[1:], F32),
            pltpu.SemaphoreType.DMA((7,)),
        ],
        out_shape=[
            jax.ShapeDtypeStruct((n, D), F32),
            jax.ShapeDtypeStruct(sp.shape, F32),
            jax.ShapeDtypeStruct(sc.shape, F32),
            jax.ShapeDtypeStruct((N_HEADS * n, LANES), F32),
            jax.ShapeDtypeStruct((KV_W, n), F32),
            jax.ShapeDtypeStruct((KV_W, n), F32),
            jax.ShapeDtypeStruct((n, ATTN_W), F32),
            jax.ShapeDtypeStruct((_SMALL_ROWS, LANES), F32),
            jax.ShapeDtypeStruct((2 * ATTN_W + KV_W, D), BF16),
            jax.ShapeDtypeStruct((KV_W, D), BF16),
            jax.ShapeDtypeStruct((D, ATTN_W), BF16),
        ],
        compiler_params=pltpu.CompilerParams(vmem_limit_bytes=VMEM_LIMIT_BYTES),
        name="sample_dense",
    )(x, sp, sc, ng, win0, pw, ps, cw, wout0, win1, wout1, qgain, kgain, cos1, sin1)


def _sample_out(olo_ref, ohi_ref, gate_ref, x1_ref, wout_ref, y_ref):
    n = x1_ref.shape[0]
    blocks = LANES // HEAD_DIM
    lane_n = lax.broadcasted_iota(jnp.int32, (n, LANES), 1)
    chunks = []
    for c in range(ATTN_W // LANES):
        parts = []
        for hd in (2 * c, 2 * c + 1):
            kv = hd // GROUP
            slab = (olo_ref, ohi_ref)[kv // blocks]
            a = slab[pl.ds(hd, n, stride=N_HEADS), :]
            parts.append(a if kv % blocks == hd % blocks else pltpu.roll(a, HEAD_DIM, 1))
        chunks.append(jnp.where(lane_n < HEAD_DIM, parts[0], parts[1]))
    y = jnp.concatenate(chunks, axis=1) * gate_ref[...]
    out = x1_ref[...] + _nt_dot(y.astype(wout_ref.dtype), wout_ref[...])
    chunks = out.shape[1] // LANES
    for c in range(chunks):
        y_ref[pl.ds(c, n, stride=chunks), :] = out[:, c * LANES:(c + 1) * LANES]


def _rope_tables(pos):
    half = HEAD_DIM // 2
    inv = ROPE_THETA ** (-jnp.arange(half, dtype=F32) / half)
    ang = pos.astype(F32)[:, None] * inv[None, :]
    return jnp.cos(ang), jnp.sin(ang)


def kernel(x_prompt, x_sample, state_pool, state_conv, cache_k, cache_v, norm_g, w_in_even, pool_w, pool_scale,
           conv_w, w_out_even, w_in_odd, q_norm_g, k_norm_g, attn_sinks, w_out_odd):
    B, T, D = x_prompt.shape
    n_s, t_s, _ = x_sample.shape
    assert norm_g.shape[0] == 2 and w_in_even.shape[0] == 1 and w_in_odd.shape[0] == 1
    assert t_s == 1 and cache_k.shape[2] == WINDOW and T >= WINDOW
    assert cache_k.shape[3] * cache_k.shape[4] == KV_W and pool_w.shape[1:] == (len(POOL_WINDOWS), POOL_GROUP, POOL_GROUP)

    win0 = w_in_even[0]
    wout0 = w_out_even[0]
    win1 = w_in_odd[0]
    wout1 = w_out_odd[0]
    pw = pool_w[0]
    ps = pool_scale[0][None, :]
    cw = conv_w[0].reshape(-1, LANES)
    sinks = attn_sinks[0]

    cos_b, sin_b = _rope_tables(jnp.arange(T // ODD_TILE) * ODD_TILE)
    rope_base = jnp.broadcast_to(jnp.concatenate([cos_b, sin_b], axis=1)[:, :, None],
                                 (T // ODD_TILE, HEAD_DIM, LANES))
    cos_o, sin_o = _rope_tables(jnp.arange(ODD_TILE))
    rope_off = jnp.concatenate([cos_o, sin_o], axis=1).T
    cos_1, sin_1 = _rope_tables(PAST_LEN + jnp.arange(t_s))

    w_conv = state_conv.shape[-1]
    sp = jnp.transpose(state_pool[0], (1, 0, 2))
    sc = jnp.transpose(state_conv[0].reshape(n_s, CONV_BUF, w_conv // LANES, LANES), (0, 2, 1, 3)).reshape(-1, LANES)
    x1s, pool_s, conv_s, qh_s, kt_s, vt_s, gate_s, small, wqvz_t, wk_t, wout1_t = _sample_dense(
        x_sample.reshape(n_s * D // LANES, LANES), sp, sc, norm_g, win0, pw, ps, cw, wout0, win1, wout1,
        q_norm_g, k_norm_g, cos_1, sin_1)
    pool_s = jnp.transpose(pool_s, (1, 0, 2))
    conv_s = jnp.transpose(conv_s.reshape(n_s, w_conv // LANES, CONV_BUF, LANES), (0, 2, 1, 3)).reshape(
        n_s, CONV_BUF, w_conv)

    ckt = jnp.transpose(cache_k[0], (0, 2, 3, 1)).reshape(n_s, KV_W, WINDOW)
    cvt = jnp.transpose(cache_v[0], (0, 2, 3, 1)).reshape(n_s, KV_W, WINDOW)
    x1p, pool_p, conv_p, nkt_s, nvt_s, olo_s, ohi_s = _even_prompt(
        x_prompt, norm_g, win0, pw, ps, cw, wout0, qh_s, kt_s, vt_s, ckt, cvt, attn_sinks, tm=EVEN_TILE)
    y_p, k_p, v_p, y_s = _odd_prompt(x1p, norm_g, wk_t, wqvz_t, wout1_t, small, rope_base, rope_off, sinks,
                                     olo_s, ohi_s, gate_s, x1s, tm=ODD_TILE)

    def window_major(a):
        return jnp.transpose(a.reshape(-1, N_KV_HEADS, HEAD_DIM, WINDOW), (0, 3, 1, 2))[None]

    return (y_p, y_s.reshape(n_s, 1, D), pool_p[None], pool_s[None], conv_p[None], conv_s[None],
            window_major(k_p), window_major(v_p), window_major(nkt_s), window_major(nvt_s))
```

```python
import functools

import jax
import jax.numpy as jnp
from jax import lax
from jax.experimental import pallas as pl
from jax.experimental.pallas import tpu as pltpu

F32 = jnp.float32
BF16 = jnp.bfloat16

POOL_WINDOWS = (2, 4, 8, 16)
POOL_GROUP = 128
POOL_BUF = max(POOL_WINDOWS) - 1
CONV_WIDTH = 3
CONV_BUF = CONV_WIDTH - 1
N_HEADS = 16
HEAD_DIM = 64
N_KV_HEADS = 4
GROUP = N_HEADS // N_KV_HEADS
WINDOW = 128
ROPE_THETA = 10000.0
RMS_EPS = 1e-6
PAST_LEN = 16384
ATTN_W = N_HEADS * HEAD_DIM
KV_W = N_KV_HEADS * HEAD_DIM

LANES = 128
POOL_HALO = 16
CONV_HALO = 8
VMEM_LIMIT_BYTES = 60 * 1024 * 1024
EVEN_TILE = 1024
ODD_TILE = 512
BF16_SUBLANES = 16
LOG2_E = 1.4426950408889634
_QG_T, _KG_T, _COS_T, _SIN_T = 0, HEAD_DIM, 2 * HEAD_DIM, 2 * HEAD_DIM + HEAD_DIM // 2
_QG_ROW, _COS_ROW, _SIN_ROW, _SMALL_ROWS = 3 * HEAD_DIM, 3 * HEAD_DIM + 1, 3 * HEAD_DIM + 2, 3 * HEAD_DIM + 8


def _rms(x, g):
    ms = jnp.mean(x * x, axis=-1, keepdims=True)
    return x * lax.rsqrt(ms + RMS_EPS) * g


def _silu(z):
    return z * jax.nn.sigmoid(z)


def _head_norm_rope(x, gain, cos, sin):
    lane = lax.broadcasted_iota(jnp.int32, x.shape, 1)
    first = lane < HEAD_DIM
    x2 = x * x
    ss0 = jnp.sum(jnp.where(first, x2, 0.0), axis=-1, keepdims=True)
    ss1 = jnp.sum(jnp.where(first, 0.0, x2), axis=-1, keepdims=True)
    r = jnp.where(first, lax.rsqrt(ss0 * (1.0 / HEAD_DIM) + RMS_EPS), lax.rsqrt(ss1 * (1.0 / HEAD_DIM) + RMS_EPS))
    xn = x * r * gain
    half = HEAD_DIM // 2
    swapped = jnp.where((lane % HEAD_DIM) < half, pltpu.roll(xn, LANES - half, 1), pltpu.roll(xn, half, 1))
    return xn * cos + swapped * sin


def _nt_dot(a, b):
    return lax.dot_general(a, b, (((1,), (1,)), ((), ())), preferred_element_type=F32)


def _col_table(row):
    m = row.shape[1]
    diag = lax.broadcasted_iota(jnp.int32, (m, m), 0) == lax.broadcasted_iota(jnp.int32, (m, m), 1)
    col = jnp.sum(jnp.where(diag, jnp.broadcast_to(row, (m, m)), 0.0), axis=1, keepdims=True)
    return jnp.broadcast_to(col, (m, LANES))


def _tiled_row(table):
    m = table.shape[0]
    pick = lax.broadcasted_iota(jnp.int32, table.shape, 0) == lax.broadcasted_iota(jnp.int32, table.shape, 1) % m
    return jnp.sum(jnp.where(pick, table, 0.0), axis=0, keepdims=True)


def _conv_tap(cw_ref, k, width):
    chunks = width // LANES
    return jnp.concatenate([cw_ref[k * chunks + j:k * chunks + j + 1, :] for j in range(chunks)], axis=1)


class _SampleAttention:
    def __init__(self, qh_ref, knt_ref, vnt_ref, ck_ref, cv_ref, sink_ref, nk_ref, nv_ref, olo_ref, ohi_ref, *, step):
        self.refs = (qh_ref, knt_ref, vnt_ref, ck_ref, cv_ref, sink_ref, nk_ref, nv_ref, olo_ref, ohi_ref)
        self.bb = ck_ref.shape[0]
        self.first = step * self.bb

    def roll(self):
        _, knt_ref, vnt_ref, ck_ref, cv_ref, _, nk_ref, nv_ref, _, _ = self.refs
        win = ck_ref.shape[2]
        newest = lax.broadcasted_iota(jnp.int32, (KV_W, win), 1) == win - 1
        for i in range(self.bb):
            b = self.first + i
            nk_ref[i] = jnp.where(newest, pltpu.roll(knt_ref[...], win - 1 - b, 1), pltpu.roll(ck_ref[i], win - 1, 1))
            nv_ref[i] = jnp.where(newest, pltpu.roll(vnt_ref[...], win - 1 - b, 1), pltpu.roll(cv_ref[i], win - 1, 1))

    def scores(self):
        qh_ref, nk_ref = self.refs[0], self.refs[6]
        n = qh_ref.shape[0] // N_HEADS
        blocks = LANES // HEAD_DIM
        hrow = lax.broadcasted_iota(jnp.int32, (N_HEADS, LANES), 0) // GROUP
        hcol = lax.broadcasted_iota(jnp.int32, (N_HEADS, LANES), 1) // HEAD_DIM
        self.s = []
        for i in range(self.bb):
            qd = qh_ref[pl.ds(self.first + i, N_HEADS, stride=n), :]
            qx = jnp.concatenate(
                [jnp.where(hcol + blocks * c == hrow, qd, 0.0) for c in range(KV_W // LANES)], axis=1)
            self.s.append(jnp.dot(qx.astype(BF16), nk_ref[i].astype(BF16), preferred_element_type=F32))

    def softmax(self):
        sink = _col_table(self.refs[5][...])[:, 0:1]
        self.p = []
        for s in self.s:
            m = jnp.maximum(jnp.max(s, axis=-1, keepdims=True), sink)
            p = jnp.exp(s - m)
            denom = jnp.sum(p, axis=-1, keepdims=True) + jnp.exp(sink - m)
            self.p.append((p * (1.0 / denom)).astype(BF16))

    def outputs(self):
        nv_ref, olo_ref, ohi_ref = self.refs[7:10]
        for i in range(self.bb):
            o = _nt_dot(self.p[i], nv_ref[i].astype(BF16))
            rows = pl.ds(pl.multiple_of((self.first + i) * N_HEADS, N_HEADS), N_HEADS)
            olo_ref[rows, :] = o[:, 0:LANES]
            ohi_ref[rows, :] = o[:, LANES:2 * LANES]


def _even_prompt_kernel(x_ref, g_ref, win_ref, pw_ref, ps_ref, cw_ref, wout_ref,
                        qh_ref, knt_ref, vnt_ref, ck_ref, cv_ref, sink_ref,
                        o_ref, pool_ref, conv_ref, nk_ref, nv_ref, olo_ref, ohi_ref,
                        uext, vext, sa, sb, y_scr, *, tm):
    t = pl.program_id(1)
    w_pool = uext.shape[1]
    w_conv = vext.shape[1]
    base = 2 * POOL_HALO
    side = _SampleAttention(qh_ref, knt_ref, vnt_ref, ck_ref, cv_ref, sink_ref, nk_ref, nv_ref, olo_ref, ohi_ref,
                            step=pl.program_id(0) * pl.num_programs(1) + t)

    @pl.when(t == 0)
    def _():
        uext[0:base, :] = jnp.zeros((base, w_pool), F32)
        vext[0:CONV_HALO, :] = jnp.zeros((CONV_HALO, w_conv), F32)
        sa[0:POOL_HALO, :] = jnp.zeros((POOL_HALO, POOL_GROUP), F32)
        sb[0:POOL_HALO, :] = jnp.zeros((POOL_HALO, POOL_GROUP), F32)

    x = x_ref[0]
    h = _rms(x, g_ref[0:1, :]).astype(win_ref.dtype)

    def proj(c, width):
        return jnp.dot(h, win_ref[:, c:c + width], preferred_element_type=F32)

    side.roll()
    u = proj(0, w_pool)
    uext[base:base + tm, :] = u
    pos = t * tm + lax.broadcasted_iota(jnp.int32, (tm, 1), 0)
    side.scores()
    z_a = proj(w_pool, w_pool)
    n_ext = POOL_HALO + tm

    for g, w in enumerate(POOL_WINDOWS):
        lo = g * POOL_GROUP
        cols = slice(lo, lo + POOL_GROUP)
        src = uext
        src_cols = cols
        step = 1
        bufs = (sa, sb)
        nbuf = 0
        while 2 * step < w:
            dst = bufs[nbuf % 2]
            dst[POOL_HALO:POOL_HALO + n_ext, :] = (src[POOL_HALO:POOL_HALO + n_ext, src_cols]
                                                   + src[POOL_HALO - step:POOL_HALO - step + n_ext, src_cols])
            src, src_cols = dst, slice(0, POOL_GROUP)
            step *= 2
            nbuf += 1
        wsum = src[base:base + tm, src_cols] + src[base - step:base - step + tm, src_cols]
        cnt = jnp.minimum(pos + 1, w).astype(F32)
        d = wsum * (1.0 / cnt) - u[:, cols]
        ya = jnp.dot(d.astype(pw_ref.dtype), pw_ref[g], preferred_element_type=F32)
        ya = ya * ps_ref[:, cols] * _silu(z_a[:, cols])
        y_scr[:, cols] = ya.astype(y_scr.dtype)

    b_gate = proj(2 * w_pool, w_conv)
    side.softmax()
    c_gate = proj(2 * w_pool + w_conv, w_conv)
    side.outputs()
    v_in = proj(2 * w_pool + 2 * w_conv, w_conv)
    v = c_gate * v_in
    vext[CONV_HALO:CONV_HALO + tm, :] = v
    conv = v * _conv_tap(cw_ref, CONV_WIDTH - 1, w_conv)
    for k in range(CONV_WIDTH - 1):
        shift = CONV_WIDTH - 1 - k
        conv = conv + vext[CONV_HALO - shift:CONV_HALO - shift + tm, :] * _conv_tap(cw_ref, k, w_conv)
    z_b = proj(2 * w_pool + 3 * w_conv, w_conv)
    y_b = b_gate * conv * _silu(z_b)
    y_scr[:, w_pool:w_pool + w_conv] = y_b.astype(y_scr.dtype)

    o_ref[0] = x + jnp.dot(y_scr[...], wout_ref[...], preferred_element_type=F32)
    pool_ref[0] = uext[base + tm - POOL_BUF:base + tm, :]
    conv_ref[0] = vext[CONV_HALO + tm - CONV_BUF:CONV_HALO + tm, :]
    uext[POOL_HALO:base, :] = uext[POOL_HALO + tm:base + tm, :]
    vext[0:CONV_HALO, :] = vext[tm:tm + CONV_HALO, :]


def _even_prompt(x, g, win, pw, ps, cw, wout, qh, knt, vnt, ckt, cvt, sink_col, *, tm):
    B, T, D = x.shape
    w_pool = ps.shape[-1]
    w_conv = (win.shape[1] - 2 * w_pool) // 4
    assert cw.shape == (CONV_WIDTH * w_conv // LANES, LANES)
    assert T % tm == 0 and tm % 16 == 0 and tm >= POOL_HALO
    nt = T // tm
    n, kvw, win_len = ckt.shape
    assert n % (B * nt) == 0 and kvw == KV_W == 2 * LANES and win_len == LANES and n == LANES
    bb = n // (B * nt)
    const = lambda *shape: pl.BlockSpec(shape, lambda b, t: (0,) * len(shape))
    cache_block = pl.BlockSpec((bb, kvw, win_len), lambda b, t: (b * nt + t, 0, 0))
    return pl.pallas_call(
        functools.partial(_even_prompt_kernel, tm=tm),
        grid=(B, nt),
        in_specs=[
            pl.BlockSpec((1, tm, D), lambda b, t: (b, t, 0)),
            const(*g.shape),
            const(*win.shape),
            const(*pw.shape),
            const(1, w_pool),
            const(*cw.shape),
            const(*wout.shape),
            const(*qh.shape),
            const(*knt.shape),
            const(*vnt.shape),
            cache_block,
            cache_block,
            const(*sink_col.shape),
        ],
        out_specs=[
            pl.BlockSpec((1, tm, D), lambda b, t: (b, t, 0)),
            pl.BlockSpec((1, POOL_BUF, w_pool), lambda b, t: (b, 0, 0)),
            pl.BlockSpec((1, CONV_BUF, w_conv), lambda b, t: (b, 0, 0)),
            cache_block,
            cache_block,
            const(n * N_HEADS, LANES),
            const(n * N_HEADS, LANES),
        ],
        out_shape=[
            jax.ShapeDtypeStruct((B, T, D), F32),
            jax.ShapeDtypeStruct((B, POOL_BUF, w_pool), F32),
            jax.ShapeDtypeStruct((B, CONV_BUF, w_conv), F32),
            jax.ShapeDtypeStruct((n, kvw, win_len), F32),
            jax.ShapeDtypeStruct((n, kvw, win_len), F32),
            jax.ShapeDtypeStruct((n * N_HEADS, LANES), F32),
            jax.ShapeDtypeStruct((n * N_HEADS, LANES), F32),
        ],
        scratch_shapes=[
            pltpu.VMEM((2 * POOL_HALO + tm, w_pool), F32),
            pltpu.VMEM((CONV_HALO + tm, w_conv), F32),
            pltpu.VMEM((2 * POOL_HALO + tm, POOL_GROUP), F32),
            pltpu.VMEM((2 * POOL_HALO + tm, POOL_GROUP), F32),
            pltpu.VMEM((tm, w_pool + w_conv), wout.dtype),
        ],
        compiler_params=pltpu.CompilerParams(
            dimension_semantics=("arbitrary", "arbitrary"), vmem_limit_bytes=VMEM_LIMIT_BYTES),
        name="even_prompt",
    )(x, g, win, pw, ps, cw, wout, qh, knt, vnt, ckt, cvt, sink_col)


def _odd_prompt_kernel(*refs, tm, nt):
    (x_ref, xres_ref, ng_ref, win_ref, wo_ref, small_ref, ropeb_ref, ropeo_ref, sink_ref,
     olo_ref, ohi_ref, sgate_ref, sx1_ref,
     o_ref, knew_ref, vnew_ref, sy_ref, wkt_scr, wqvz_scr, wout_scr, qt_scr, kext, vt_ext, gate_scr, *rest) = refs
    g_ref = ng_ref.at[1:2]
    qg_ref, kg_ref = small_ref.at[_QG_T:_QG_T + HEAD_DIM], small_ref.at[_KG_T:_KG_T + HEAD_DIM]

    step = pl.program_id(0)

    @pl.when(step == 0)
    def _():
        _sample_out(olo_ref, ohi_ref, sgate_ref, sx1_ref, wo_ref, sy_ref)
        kext[...] = jnp.zeros(kext.shape, BF16)
        vt_ext[...] = jnp.zeros(vt_ext.shape, BF16)
        qt_scr[1] = jnp.zeros(qt_scr.shape[1:], BF16)
        gate_scr[1] = jnp.zeros(gate_scr.shape[1:], F32)
        blk = 2 * LANES
        for src, dst, width in ((0, 0, ATTN_W), (ATTN_W + KV_W, ATTN_W, KV_W + ATTN_W)):
            for c in range(0, width, blk):
                wqvz_scr[dst + c:dst + c + blk, :] = win_ref[:, src + c:src + c + blk].T.astype(BF16)
        wkt_scr[...] = win_ref[:, ATTN_W:ATTN_W + KV_W].T.astype(BF16)
        for c in range(0, wo_ref.shape[1], blk):
            wout_scr[c:c + blk, :] = wo_ref[:, c:c + blk].T.astype(BF16)

    def run(cur):
        _odd_prompt_step(x_ref, xres_ref, g_ref, wkt_scr, wqvz_scr, qg_ref, kg_ref, ropeb_ref, ropeo_ref, sink_ref,
                         wout_scr, o_ref, knew_ref, vnew_ref, qt_scr, kext, vt_ext, gate_scr, *rest,
                         tm=tm, nt=nt, cur=cur, prev=1 - cur)

    for parity in range(2):
        pl.when(step % 2 == parity)(lambda parity=parity: run(parity))


def _odd_prompt_step(x_ref, xres_ref, g_ref, wkt_ref, wqvz_ref, qg_ref, kg_ref, ropeb_ref, ropeo_ref, sink_ref,
                     wout_ref, o_ref, knew_ref, vnew_ref,
                     qt_scr, kext, vt_ext, gate_scr, yt_scr, s_scr, p_scr, esink_scr, *, tm, nt, cur, prev):
    step = pl.program_id(0)
    proj_t = step % nt
    attn_t = jnp.maximum(step - 1, 0) % nt
    nblk = tm // WINDOW
    half = HEAD_DIM // 2

    h = _rms(x_ref[0], g_ref[...]).astype(BF16)
    reps = tm // LANES
    cos_b = jnp.concatenate([ropeb_ref[0, 0:half, :]] * reps, axis=1)
    sin_b = jnp.concatenate([ropeb_ref[0, half:HEAD_DIM, :]] * reps, axis=1)
    cos_o, sin_o = ropeo_ref[0:half, :], ropeo_ref[half:HEAD_DIM, :]
    cos_t = cos_b * cos_o - sin_b * sin_o
    sin_t = sin_b * cos_o + cos_b * sin_o

    def norm_rope(xt, gain):
        ms = jnp.sum(xt * xt, axis=0, keepdims=True) * (1.0 / HEAD_DIM)
        xn = xt * lax.rsqrt(ms + RMS_EPS) * gain
        x1, x2 = xn[0:half, :], xn[half:HEAD_DIM, :]
        return jnp.concatenate([x1 * cos_t - x2 * sin_t, x2 * cos_t + x1 * sin_t], axis=0)

    has_past = proj_t > 0

    def proj_kv():
        kt = _nt_dot(wkt_ref[...], h)
        kgain = jnp.concatenate([kg_ref[...]] * reps, axis=1)
        kt = jnp.concatenate(
            [norm_rope(kt[kv * HEAD_DIM:(kv + 1) * HEAD_DIM, :], kgain) for kv in range(N_KV_HEADS)], axis=0)
        knew_ref[0] = kt[:, tm - WINDOW:tm]
        kext[cur, 0:WINDOW, :] = jnp.where(has_past, kext[prev, tm:tm + WINDOW, :], jnp.zeros((WINDOW, KV_W), BF16))
        kext[cur, WINDOW:WINDOW + tm, :] = kt.T.astype(BF16)
        vt = _nt_dot(wqvz_ref[ATTN_W:ATTN_W + KV_W, :], h)
        vnew_ref[0] = vt[:, tm - WINDOW:tm]
        vt_ext[cur, :, 0:WINDOW] = jnp.where(has_past, vt_ext[prev, :, tm:tm + WINDOW],
                                             jnp.zeros((KV_W, WINDOW), BF16))
        vt_ext[cur, :, WINDOW:WINDOW + tm] = vt.astype(BF16)

    def proj_gate(lo, hi):
        zt = _nt_dot(wqvz_ref[ATTN_W + KV_W + lo:ATTN_W + KV_W + hi, :], h)
        gate_scr[cur, lo:hi, :] = _silu(zt)

    q_raw = {}

    def proj_q(lo, hi):
        q_raw[lo] = _nt_dot(wqvz_ref[lo:hi, :], h)

    def finish_q():
        qgain = jnp.concatenate([qg_ref[...]] * reps, axis=1)
        for lo, qt in q_raw.items():
            for r0 in range(0, qt.shape[0], HEAD_DIM):
                qt_scr[cur, lo + r0:lo + r0 + HEAD_DIM, :] = (
                    norm_rope(qt[r0:r0 + HEAD_DIM, :], qgain) * (LOG2_E * HEAD_DIM ** -0.5)).astype(BF16)
        q_raw.clear()

    pieces = [proj_kv,
              lambda: proj_gate(0, ATTN_W // 2),
              lambda: proj_gate(ATTN_W // 2, ATTN_W),
              lambda: proj_q(0, ATTN_W // 2)]
    last_piece = lambda: proj_q(ATTN_W // 2, ATTN_W)

    ri = lax.broadcasted_iota(jnp.int32, (WINDOW, 2 * WINDOW), 0)
    qq = lax.broadcasted_iota(jnp.int32, (WINDOW, 2 * WINDOW), 1) % WINDOW
    from_prev = ri > qq
    keep_prev = from_prev.astype(BF16)
    keep_cur = 1.0 - keep_prev
    lane = lax.broadcasted_iota(jnp.int32, (1, 2 * WINDOW), 1)
    zeros = jnp.zeros((HEAD_DIM, 2 * WINDOW), BF16)
    ones = jnp.ones((BF16_SUBLANES, 2 * WINDOW), BF16)

    for i in range(nblk):
        c0 = i * WINDOW
        qcols = slice(c0, c0 + WINDOW)
        pairs = [(kv, kv * GROUP + 2 * pr) for kv in range(N_KV_HEADS) for pr in range(GROUP // 2)]
        for j, (kv, ha) in enumerate(pairs):
            chunk, pos = divmod(kv, LANES // HEAD_DIM)
            kblk = kext[prev, c0:c0 + 2 * WINDOW, chunk * LANES:(chunk + 1) * LANES]
            qpair = jnp.concatenate([qt_scr[prev, ha * HEAD_DIM:(ha + 1) * HEAD_DIM, qcols],
                                     qt_scr[prev, (ha + 1) * HEAD_DIM:(ha + 2) * HEAD_DIM, qcols]], axis=1)
            rhs = jnp.concatenate([qpair, zeros] if pos == 0 else [zeros, qpair], axis=0)
            s = jnp.dot(kblk, rhs, preferred_element_type=F32)
            s_prev = s[0:WINDOW, :]
            if i == 0:
                s_prev = jnp.where(attn_t == 0, -jnp.inf, s_prev)
            s_scr[j] = jnp.where(from_prev, s_prev, s[WINDOW:2 * WINDOW, :])
        for c, piece in enumerate(pieces):
            if c * nblk // len(pieces) == i:
                piece()
        for j, (kv, ha) in enumerate(pairs):
            s = s_scr[j]
            sink = jnp.where(lane < WINDOW, sink_ref[ha], sink_ref[ha + 1]) * LOG2_E
            m = jnp.maximum(jnp.max(s, axis=0, keepdims=True), sink)
            p_scr[j] = jnp.exp2(s - m).astype(BF16)
            esink_scr[j:j + 1, :] = jnp.exp2(sink - m)
        for j, (kv, ha) in enumerate(pairs):
            vg = vt_ext[prev, kv * HEAD_DIM:(kv + 1) * HEAD_DIM, c0:c0 + 2 * WINDOW]
            p = p_scr[j]
            p_keys = jnp.concatenate([p * keep_prev, p * keep_cur], axis=0)
            oa = jnp.dot(jnp.concatenate([vg, ones], axis=0), p_keys, preferred_element_type=F32)
            denom = oa[HEAD_DIM:HEAD_DIM + 1, :] + esink_scr[j:j + 1, :]
            o = oa[0:HEAD_DIM, :] * (1.0 / denom)
            for hd, part in ((ha, o[:, 0:WINDOW]), (ha + 1, o[:, WINDOW:2 * WINDOW])):
                rows = slice(hd * HEAD_DIM, (hd + 1) * HEAD_DIM)
                yt_scr[rows, qcols] = (part * gate_scr[prev, rows, qcols]).astype(BF16)

    out_t = jnp.dot(wout_ref[...], yt_scr[...], preferred_element_type=F32)
    finish_q()
    last_piece()
    finish_q()
    o_ref[0] = xres_ref[0] + out_t.T


def _odd_prompt(x, ng, w_in, w_out, small, cos_t, sin_t, sinks, s_olo, s_ohi, s_gate, s_x1, *, tm):
    B, T, D = x.shape
    assert T % tm == 0 and tm % WINDOW == 0 and w_in.shape == (D, 2 * ATTN_W + 2 * KV_W) and w_out.shape == (ATTN_W, D)
    nt = T // tm
    half = HEAD_DIM // 2
    const = lambda *shape: pl.BlockSpec(shape, lambda s: (0,) * len(shape))
    n_tiles = B * nt
    proj_tile = lambda s: jnp.minimum(s, n_tiles - 1)
    attn_tile = lambda s: jnp.maximum(s - 1, 0)
    return pl.pallas_call(
        functools.partial(_odd_prompt_kernel, tm=tm, nt=nt),
        grid=(n_tiles + 1,),
        in_specs=[
            pl.BlockSpec((1, tm, D), lambda s: (proj_tile(s) // nt, proj_tile(s) % nt, 0)),
            pl.BlockSpec((1, tm, D), lambda s: (attn_tile(s) // nt, attn_tile(s) % nt, 0)),
            const(*ng.shape),
            const(*w_in.shape),
            const(*w_out.shape),
            const(*small.shape),
            pl.BlockSpec((1, HEAD_DIM, LANES), lambda s: (proj_tile(s) % nt, 0, 0)),
            const(HEAD_DIM, tm),
            pl.BlockSpec(memory_space=pltpu.SMEM),
            const(*s_olo.shape),
            const(*s_ohi.shape),
            const(*s_gate.shape),
            const(*s_x1.shape),
        ],
        out_specs=[
            pl.BlockSpec((1, tm, D), lambda s: (attn_tile(s) // nt, attn_tile(s) % nt, 0)),
            pl.BlockSpec((1, KV_W, WINDOW), lambda s: (proj_tile(s) // nt, 0, 0)),
            pl.BlockSpec((1, KV_W, WINDOW), lambda s: (proj_tile(s) // nt, 0, 0)),
            const(s_x1.size // LANES, LANES),
        ],
        out_shape=[
            jax.ShapeDtypeStruct((B, T, D), F32),
            jax.ShapeDtypeStruct((B, KV_W, WINDOW), F32),
            jax.ShapeDtypeStruct((B, KV_W, WINDOW), F32),
            jax.ShapeDtypeStruct((s_x1.size // LANES, LANES), F32),
        ],
        scratch_shapes=[
            pltpu.VMEM((KV_W, D), BF16),
            pltpu.VMEM((2 * ATTN_W + KV_W, D), BF16),
            pltpu.VMEM((D, ATTN_W), BF16),
            pltpu.VMEM((2, ATTN_W, tm), BF16),
            pltpu.VMEM((2, WINDOW + tm, KV_W), BF16),
            pltpu.VMEM((2, KV_W, WINDOW + tm), BF16),
            pltpu.VMEM((2, ATTN_W, tm), F32),
            pltpu.VMEM((ATTN_W, tm), BF16),
            pltpu.VMEM((N_HEADS // 2, WINDOW, 2 * WINDOW), F32),
            pltpu.VMEM((N_HEADS // 2, WINDOW, 2 * WINDOW), BF16),
            pltpu.VMEM((N_HEADS // 2, 2 * WINDOW), F32),
        ],
        compiler_params=pltpu.CompilerParams(
            dimension_semantics=("arbitrary",), vmem_limit_bytes=VMEM_LIMIT_BYTES),
        name="odd_prompt",
    )(x, x, ng, w_in, w_out, small, cos_t, sin_t, sinks, s_olo, s_ohi, s_gate, s_x1)


def _dup_heads(x):
    lane = lax.broadcasted_iota(jnp.int32, x.shape, 1)
    rolled = pltpu.roll(x, HEAD_DIM, 1)
    first = lane < HEAD_DIM
    return jnp.where(first, x, rolled), jnp.where(first, rolled, x)


def _sample_dense_kernel(x_ref, sp_ref, sc_ref, ng_ref, win0_ref, pw_ref, ps_ref, cw_ref, wout0_ref,
                         win1_ref, qgain_ref, kgain_ref, cos1_ref, sin1_ref,
                         x1_ref, pool_ref, conv_ref, qh_ref, kt_ref, vt_ref, gate_ref, small_ref,
                         *scratch):
    hbm = _HbmOperands(sp_ref, win0_ref, wout0_ref, win1_ref, pool_ref, *scratch)
    sp_ref, win0_ref, wout0_ref, win1_ref = hbm.sp, hbm.win0, hbm.wout0, hbm.win1
    w_pool = ps_ref.shape[1]
    w_conv = (win0_ref.shape[1] - 2 * w_pool) // 4
    half = HEAD_DIM // 2
    qg_t, cos_t1, sin_t1 = _col_table(qgain_ref[...]), _col_table(cos1_ref[...]), _col_table(sin1_ref[...])
    small_ref[_QG_T:_QG_T + HEAD_DIM, :] = qg_t
    small_ref[_KG_T:_KG_T + HEAD_DIM, :] = _col_table(kgain_ref[...])
    small_ref[_COS_T:_COS_T + half, :] = cos_t1
    small_ref[_SIN_T:_SIN_T + half, :] = sin_t1
    second_half = lax.broadcasted_iota(jnp.int32, (1, LANES), 1) % HEAD_DIM >= half
    small_ref[_QG_ROW:_SMALL_ROWS, :] = jnp.concatenate(
        [_tiled_row(qg_t), _tiled_row(cos_t1), jnp.where(second_half, 1.0, -1.0) * _tiled_row(sin_t1),
         jnp.zeros((_SMALL_ROWS - _SIN_ROW - 1, LANES), F32)], axis=0)
    g0_ref, g1_ref = ng_ref.at[0:1], ng_ref.at[1:2]
    qg_ref, kgt_ref = small_ref.at[_QG_ROW:_QG_ROW + 1], small_ref.at[_KG_T:_KG_T + HEAD_DIM]
    cos_ref, sin_ref = small_ref.at[_COS_ROW:_COS_ROW + 1], small_ref.at[_SIN_ROW:_SIN_ROW + 1]
    cost_ref, sint_ref = small_ref.at[_COS_T:_COS_T + HEAD_DIM // 2], small_ref.at[_SIN_T:_SIN_T + HEAD_DIM // 2]

    n = x1_ref.shape[0]
    chunks = x1_ref.shape[1] // LANES
    x = jnp.concatenate([x_ref[pl.ds(c, n, stride=chunks), :] for c in range(chunks)], axis=1)
    h = _rms(x, g0_ref[...]).astype(win0_ref.dtype)

    def proj(c, width):
        return jnp.dot(h, win0_ref[:, c:c + width], preferred_element_type=F32)

    hbm.wait_win0()
    u = proj(0, w_pool)
    hbm.put_newest_pool_row(u)
    z_a = proj(w_pool, w_pool)
    hbm.wait_sp()
    ys = []
    for g, w in enumerate(POOL_WINDOWS):
        cols = slice(g * POOL_GROUP, (g + 1) * POOL_GROUP)
        wsum = u[:, cols]
        for r in range(POOL_BUF - (w - 1), POOL_BUF):
            wsum = wsum + sp_ref[r, :, cols]
        cnt = float(min(PAST_LEN + 1, w))
        d = wsum * (1.0 / cnt) - u[:, cols]
        ya = jnp.dot(d.astype(pw_ref.dtype), pw_ref[g], preferred_element_type=F32)
        ys.append((ya * ps_ref[:, cols] * _silu(z_a[:, cols])).astype(wout0_ref.dtype))

    b_gate = proj(2 * w_pool, w_conv)
    c_gate = proj(2 * w_pool + w_conv, w_conv)
    v_in = proj(2 * w_pool + 2 * w_conv, w_conv)
    v = c_gate * v_in
    cchunks = w_conv // LANES
    cstride = cchunks * CONV_BUF

    def conv_row(r):
        return jnp.concatenate(
            [sc_ref[pl.ds(j * CONV_BUF + r, n, stride=cstride), :] for j in range(cchunks)], axis=1)

    def put_conv_row(r, val):
        for j in range(cchunks):
            conv_ref[pl.ds(j * CONV_BUF + r, n, stride=cstride), :] = val[:, j * LANES:(j + 1) * LANES]

    conv = v * _conv_tap(cw_ref, CONV_WIDTH - 1, w_conv)
    for k in range(CONV_BUF):
        conv = conv + conv_row(k) * _conv_tap(cw_ref, k, w_conv)
    for r in range(CONV_BUF - 1):
        put_conv_row(r, conv_row(r + 1))
    put_conv_row(CONV_BUF - 1, v)
    z_b = proj(2 * w_pool + 3 * w_conv, w_conv)
    ys.append((b_gate * conv * _silu(z_b)).astype(wout0_ref.dtype))
    y = jnp.concatenate(ys, axis=1)
    hbm.wait_wout0()
    x1 = x + jnp.dot(y, wout0_ref[...], preferred_element_type=F32)
    x1_ref[...] = x1

    h1 = _rms(x1, g1_ref[...])
    hbm.wait_win1()

    def proj1(lo, width):
        return jnp.dot(h1, win1_ref[:, lo:lo + width], preferred_element_type=F32)

    q = proj1(0, ATTN_W)
    for c in range(ATTN_W // LANES):
        qc = _head_norm_rope(q[:, c * LANES:(c + 1) * LANES], qg_ref[...], cos_ref[...], sin_ref[...])
        d0, d1 = _dup_heads(qc * (HEAD_DIM ** -0.5))
        qh_ref[2 * c * n:(2 * c + 1) * n, :] = d0
        qh_ref[(2 * c + 1) * n:(2 * c + 2) * n, :] = d1
    kt = proj1(ATTN_W, KV_W).T
    cos_t = cost_ref[...]
    sin_t = sint_ref[...]
    for kv in range(N_KV_HEADS):
        kh = kt[kv * HEAD_DIM:(kv + 1) * HEAD_DIM, :]
        ms = jnp.sum(kh * kh, axis=0, keepdims=True) * (1.0 / HEAD_DIM)
        kn = kh * lax.rsqrt(ms + RMS_EPS) * kgt_ref[...]
        x1h, x2h = kn[0:half, :], kn[half:HEAD_DIM, :]
        kt_ref[kv * HEAD_DIM:(kv + 1) * HEAD_DIM, :] = jnp.concatenate(
            [x1h * cos_t - x2h * sin_t, x2h * cos_t + x1h * sin_t], axis=0)
    vt_ref[...] = proj1(ATTN_W + KV_W, KV_W).T
    gate_ref[...] = _silu(proj1(ATTN_W + 2 * KV_W, ATTN_W))
    hbm.finish()


class _HbmOperands:
    def __init__(self, sp_hbm, win0_hbm, wout0_hbm, win1_hbm, pool_hbm, sp, win0, wout0, win1, newest, sems):
        self.sp, self.win0, self.wout0, self.win1, self.newest = sp, win0, wout0, win1, newest
        self.pool_hbm = pool_hbm
        self.sems = sems
        rows = sp_hbm.shape[0]
        self.shift = pltpu.make_async_copy(sp.at[1:rows], pool_hbm.at[0:rows - 1], sems.at[0])
        self.fetch = {name: pltpu.make_async_copy(src, dst, sems.at[i + 1]) for i, (name, src, dst) in enumerate(
            (("win0", win0_hbm, win0), ("sp", sp_hbm, sp), ("wout0", wout0_hbm, wout0), ("win1", win1_hbm, win1)))}
        for copy in self.fetch.values():
            copy.start()

    def wait_win0(self):
        self.fetch["win0"].wait()

    def wait_sp(self):
        self.fetch["sp"].wait()
        self.shift.start()

    def wait_wout0(self):
        self.fetch["wout0"].wait()

    def wait_win1(self):
        self.fetch["win1"].wait()

    def put_newest_pool_row(self, u):
        rows = self.pool_hbm.shape[0]
        self.newest[...] = u
        self.put = pltpu.make_async_copy(self.newest, self.pool_hbm.at[rows - 1], self.sems.at[5])
        self.put.start()

    def finish(self):
        self.shift.wait()
        self.put.wait()


def _sample_dense(x, sp, sc, ng, win0, pw, ps, cw, wout0, win1, qgain, kgain, cos1, sin1):
    D = win0.shape[0]
    n = x.shape[0] * LANES // D
    assert n == LANES
    assert qgain.shape == kgain.shape == (1, HEAD_DIM) and cos1.shape == sin1.shape == (1, HEAD_DIM // 2)
    vmem = pl.BlockSpec(memory_space=pltpu.VMEM)
    hbm = pl.BlockSpec(memory_space=pl.ANY)
    in_specs = [vmem, hbm, vmem, vmem, hbm, vmem, vmem, vmem, hbm, hbm] + [vmem] * 4
    return pl.pallas_call(
        _sample_dense_kernel,
        in_specs=in_specs,
        out_specs=[vmem, hbm] + [vmem] * 6,
        scratch_shapes=[
            pltpu.VMEM(sp.shape, F32),
            pltpu.VMEM(win0.shape, F32),
            pltpu.VMEM(wout0.shape, F32),
            pltpu.VMEM(win1.shape, F32),
            pltpu.VMEM(sp.shape[1:], F32),
            pltpu.SemaphoreType.DMA((6,)),
        ],
        out_shape=[
            jax.ShapeDtypeStruct((n, D), F32),
            jax.ShapeDtypeStruct(sp.shape, F32),
            jax.ShapeDtypeStruct(sc.shape, F32),
            jax.ShapeDtypeStruct((N_HEADS * n, LANES), F32),
            jax.ShapeDtypeStruct((KV_W, n), F32),
            jax.ShapeDtypeStruct((KV_W, n), F32),
            jax.ShapeDtypeStruct((n, ATTN_W), F32),
            jax.ShapeDtypeStruct((_SMALL_ROWS, LANES), F32),
        ],
        compiler_params=pltpu.CompilerParams(vmem_limit_bytes=VMEM_LIMIT_BYTES),
        name="sample_dense",
    )(x, sp, sc, ng, win0, pw, ps, cw, wout0, win1, qgain, kgain, cos1, sin1)


def _sample_out(olo_ref, ohi_ref, gate_ref, x1_ref, wout_ref, y_ref):
    n = x1_ref.shape[0]
    blocks = LANES // HEAD_DIM
    lane_n = lax.broadcasted_iota(jnp.int32, (n, LANES), 1)
    chunks = []
    for c in range(ATTN_W // LANES):
        parts = []
        for hd in (2 * c, 2 * c + 1):
            kv = hd // GROUP
            slab = (olo_ref, ohi_ref)[kv // blocks]
            a = slab[pl.ds(hd, n, stride=N_HEADS), :]
            parts.append(a if kv % blocks == hd % blocks else pltpu.roll(a, HEAD_DIM, 1))
        chunks.append(jnp.where(lane_n < HEAD_DIM, parts[0], parts[1]))
    y = jnp.concatenate(chunks, axis=1) * gate_ref[...]
    out = x1_ref[...] + jnp.dot(y, wout_ref[...], preferred_element_type=F32)
    chunks = out.shape[1] // LANES
    for c in range(chunks):
        y_ref[pl.ds(c, n, stride=chunks), :] = out[:, c * LANES:(c + 1) * LANES]


def _rope_tables(pos):
    half = HEAD_DIM // 2
    inv = ROPE_THETA ** (-jnp.arange(half, dtype=F32) / half)
    ang = pos.astype(F32)[:, None] * inv[None, :]
    return jnp.cos(ang), jnp.sin(ang)


def kernel(x_prompt, x_sample, state_pool, state_conv, cache_k, cache_v, norm_g, w_in_even, pool_w, pool_scale,
           conv_w, w_out_even, w_in_odd, q_norm_g, k_norm_g, attn_sinks, w_out_odd):
    B, T, D = x_prompt.shape
    n_s, t_s, _ = x_sample.shape
    assert norm_g.shape[0] == 2 and w_in_even.shape[0] == 1 and w_in_odd.shape[0] == 1
    assert t_s == 1 and cache_k.shape[2] == WINDOW and T >= WINDOW
    assert cache_k.shape[3] * cache_k.shape[4] == KV_W and pool_w.shape[1:] == (len(POOL_WINDOWS), POOL_GROUP, POOL_GROUP)

    win0 = w_in_even[0]
    wout0 = w_out_even[0]
    win1 = w_in_odd[0]
    wout1 = w_out_odd[0]
    pw = pool_w[0]
    ps = pool_scale[0][None, :]
    cw = conv_w[0].reshape(-1, LANES)
    sinks = attn_sinks[0]

    cos_b, sin_b = _rope_tables(jnp.arange(T // ODD_TILE) * ODD_TILE)
    rope_base = jnp.broadcast_to(jnp.concatenate([cos_b, sin_b], axis=1)[:, :, None],
                                 (T // ODD_TILE, HEAD_DIM, LANES))
    cos_o, sin_o = _rope_tables(jnp.arange(ODD_TILE))
    rope_off = jnp.concatenate([cos_o, sin_o], axis=1).T
    cos_1, sin_1 = _rope_tables(PAST_LEN + jnp.arange(t_s))

    w_conv = state_conv.shape[-1]
    sp = jnp.transpose(state_pool[0], (1, 0, 2))
    sc = jnp.transpose(state_conv[0].reshape(n_s, CONV_BUF, w_conv // LANES, LANES), (0, 2, 1, 3)).reshape(-1, LANES)
    x1s, pool_s, conv_s, qh_s, kt_s, vt_s, gate_s, small = _sample_dense(
        x_sample.reshape(n_s * D // LANES, LANES), sp, sc, norm_g, win0, pw, ps, cw, wout0, win1,
        q_norm_g, k_norm_g, cos_1, sin_1)
    pool_s = jnp.transpose(pool_s, (1, 0, 2))
    conv_s = jnp.transpose(conv_s.reshape(n_s, w_conv // LANES, CONV_BUF, LANES), (0, 2, 1, 3)).reshape(
        n_s, CONV_BUF, w_conv)

    ckt = jnp.transpose(cache_k[0], (0, 2, 3, 1)).reshape(n_s, KV_W, WINDOW)
    cvt = jnp.transpose(cache_v[0], (0, 2, 3, 1)).reshape(n_s, KV_W, WINDOW)
    x1p, pool_p, conv_p, nkt_s, nvt_s, olo_s, ohi_s = _even_prompt(
        x_prompt, norm_g, win0, pw, ps, cw, wout0, qh_s, kt_s, vt_s, ckt, cvt, attn_sinks, tm=EVEN_TILE)
    y_p, k_p, v_p, y_s = _odd_prompt(x1p, norm_g, win1, wout1, small, rope_base, rope_off, sinks,
                                     olo_s, ohi_s, gate_s, x1s, tm=ODD_TILE)

    def window_major(a):
        return jnp.transpose(a.reshape(-1, N_KV_HEADS, HEAD_DIM, WINDOW), (0, 3, 1, 2))[None]

    return (y_p, y_s.reshape(n_s, 1, D), pool_p[None], pool_s[None], conv_p[None], conv_s[None],
            window_major(k_p), window_major(v_p), window_major(nkt_s), window_major(nvt_s))
```

```python
import functools

import jax
import jax.numpy as jnp
from jax import lax
from jax.experimental import pallas as pl
from jax.experimental.pallas import tpu as pltpu

F32 = jnp.float32
BF16 = jnp.bfloat16

POOL_WINDOWS = (2, 4, 8, 16)
POOL_GROUP = 128
POOL_BUF = max(POOL_WINDOWS) - 1
CONV_WIDTH = 3
CONV_BUF = CONV_WIDTH - 1
N_HEADS = 16
HEAD_DIM = 64
N_KV_HEADS = 4
GROUP = N_HEADS // N_KV_HEADS
WINDOW = 128
ROPE_THETA = 10000.0
RMS_EPS = 1e-6
PAST_LEN = 16384
ATTN_W = N_HEADS * HEAD_DIM
KV_W = N_KV_HEADS * HEAD_DIM

LANES = 128
POOL_HALO = 16
CONV_HALO = 8
VMEM_LIMIT_BYTES = 60 * 1024 * 1024
EVEN_TILE = 1024
ODD_TILE = 1024
BF16_SUBLANES = 16
LOG2_E = 1.4426950408889634
_QG_T, _KG_T, _COS_T, _SIN_T = 0, HEAD_DIM, 2 * HEAD_DIM, 2 * HEAD_DIM + HEAD_DIM // 2
_QG_ROW, _COS_ROW, _SIN_ROW, _SMALL_ROWS = 3 * HEAD_DIM, 3 * HEAD_DIM + 1, 3 * HEAD_DIM + 2, 3 * HEAD_DIM + 8


def _rms(x, g):
    ms = jnp.mean(x * x, axis=-1, keepdims=True)
    return x * lax.rsqrt(ms + RMS_EPS) * g


def _silu(z):
    return z * jax.nn.sigmoid(z)


def _head_norm_rope(x, gain, cos, sin):
    lane = lax.broadcasted_iota(jnp.int32, x.shape, 1)
    first = lane < HEAD_DIM
    x2 = x * x
    ss0 = jnp.sum(jnp.where(first, x2, 0.0), axis=-1, keepdims=True)
    ss1 = jnp.sum(jnp.where(first, 0.0, x2), axis=-1, keepdims=True)
    r = jnp.where(first, lax.rsqrt(ss0 * (1.0 / HEAD_DIM) + RMS_EPS), lax.rsqrt(ss1 * (1.0 / HEAD_DIM) + RMS_EPS))
    xn = x * r * gain
    half = HEAD_DIM // 2
    swapped = jnp.where((lane % HEAD_DIM) < half, pltpu.roll(xn, LANES - half, 1), pltpu.roll(xn, half, 1))
    return xn * cos + swapped * sin


def _nt_dot(a, b):
    return lax.dot_general(a, b, (((1,), (1,)), ((), ())), preferred_element_type=F32)


def _col_table(row):
    m = row.shape[1]
    diag = lax.broadcasted_iota(jnp.int32, (m, m), 0) == lax.broadcasted_iota(jnp.int32, (m, m), 1)
    col = jnp.sum(jnp.where(diag, jnp.broadcast_to(row, (m, m)), 0.0), axis=1, keepdims=True)
    return jnp.broadcast_to(col, (m, LANES))


def _tiled_row(table):
    m = table.shape[0]
    pick = lax.broadcasted_iota(jnp.int32, table.shape, 0) == lax.broadcasted_iota(jnp.int32, table.shape, 1) % m
    return jnp.sum(jnp.where(pick, table, 0.0), axis=0, keepdims=True)


def _conv_tap(cw_ref, k, width):
    chunks = width // LANES
    return jnp.concatenate([cw_ref[k * chunks + j:k * chunks + j + 1, :] for j in range(chunks)], axis=1)


class _SampleAttention:
    def __init__(self, qh_ref, knt_ref, vnt_ref, ck_ref, cv_ref, sink_ref, nk_ref, nv_ref, olo_ref, ohi_ref, *, step):
        self.refs = (qh_ref, knt_ref, vnt_ref, ck_ref, cv_ref, sink_ref, nk_ref, nv_ref, olo_ref, ohi_ref)
        self.bb = ck_ref.shape[0]
        self.first = step * self.bb

    def roll(self):
        _, knt_ref, vnt_ref, ck_ref, cv_ref, _, nk_ref, nv_ref, _, _ = self.refs
        win = ck_ref.shape[2]
        newest = lax.broadcasted_iota(jnp.int32, (KV_W, win), 1) == win - 1
        for i in range(self.bb):
            b = self.first + i
            nk_ref[i] = jnp.where(newest, pltpu.roll(knt_ref[...], win - 1 - b, 1), pltpu.roll(ck_ref[i], win - 1, 1))
            nv_ref[i] = jnp.where(newest, pltpu.roll(vnt_ref[...], win - 1 - b, 1), pltpu.roll(cv_ref[i], win - 1, 1))

    def scores(self):
        qh_ref, nk_ref = self.refs[0], self.refs[6]
        n = qh_ref.shape[0] // N_HEADS
        blocks = LANES // HEAD_DIM
        hrow = lax.broadcasted_iota(jnp.int32, (N_HEADS, LANES), 0) // GROUP
        hcol = lax.broadcasted_iota(jnp.int32, (N_HEADS, LANES), 1) // HEAD_DIM
        self.s = []
        for i in range(self.bb):
            qd = qh_ref[pl.ds(self.first + i, N_HEADS, stride=n), :]
            qx = jnp.concatenate(
                [jnp.where(hcol + blocks * c == hrow, qd, 0.0) for c in range(KV_W // LANES)], axis=1)
            self.s.append(jnp.dot(qx.astype(BF16), nk_ref[i].astype(BF16), preferred_element_type=F32))

    def softmax(self):
        sink = _col_table(self.refs[5][...])[:, 0:1]
        self.p = []
        for s in self.s:
            m = jnp.maximum(jnp.max(s, axis=-1, keepdims=True), sink)
            p = jnp.exp(s - m)
            denom = jnp.sum(p, axis=-1, keepdims=True) + jnp.exp(sink - m)
            self.p.append((p * (1.0 / denom)).astype(BF16))

    def outputs(self):
        nv_ref, olo_ref, ohi_ref = self.refs[7:10]
        for i in range(self.bb):
            o = _nt_dot(self.p[i], nv_ref[i].astype(BF16))
            rows = pl.ds(pl.multiple_of((self.first + i) * N_HEADS, N_HEADS), N_HEADS)
            olo_ref[rows, :] = o[:, 0:LANES]
            ohi_ref[rows, :] = o[:, LANES:2 * LANES]


def _even_prompt_kernel(x_ref, g_ref, win_ref, pw_ref, ps_ref, cw_ref, wout_ref,
                        qh_ref, knt_ref, vnt_ref, ck_ref, cv_ref, sink_ref,
                        o_ref, pool_ref, conv_ref, nk_ref, nv_ref, olo_ref, ohi_ref,
                        uext, vext, sa, sb, y_scr, *, tm):
    t = pl.program_id(1)
    w_pool = uext.shape[1]
    w_conv = vext.shape[1]
    base = 2 * POOL_HALO
    side = _SampleAttention(qh_ref, knt_ref, vnt_ref, ck_ref, cv_ref, sink_ref, nk_ref, nv_ref, olo_ref, ohi_ref,
                            step=pl.program_id(0) * pl.num_programs(1) + t)

    @pl.when(t == 0)
    def _():
        uext[0:base, :] = jnp.zeros((base, w_pool), F32)
        vext[0:CONV_HALO, :] = jnp.zeros((CONV_HALO, w_conv), F32)
        sa[0:POOL_HALO, :] = jnp.zeros((POOL_HALO, POOL_GROUP), F32)
        sb[0:POOL_HALO, :] = jnp.zeros((POOL_HALO, POOL_GROUP), F32)

    x = x_ref[0]
    h = _rms(x, g_ref[0:1, :]).astype(win_ref.dtype)

    def proj(c, width):
        return jnp.dot(h, win_ref[:, c:c + width], preferred_element_type=F32)

    side.roll()
    u = proj(0, w_pool)
    uext[base:base + tm, :] = u
    pos = t * tm + lax.broadcasted_iota(jnp.int32, (tm, 1), 0)
    side.scores()
    z_a = proj(w_pool, w_pool)
    n_ext = POOL_HALO + tm

    for g, w in enumerate(POOL_WINDOWS):
        lo = g * POOL_GROUP
        cols = slice(lo, lo + POOL_GROUP)
        src = uext
        src_cols = cols
        step = 1
        bufs = (sa, sb)
        nbuf = 0
        while 2 * step < w:
            dst = bufs[nbuf % 2]
            dst[POOL_HALO:POOL_HALO + n_ext, :] = (src[POOL_HALO:POOL_HALO + n_ext, src_cols]
                                                   + src[POOL_HALO - step:POOL_HALO - step + n_ext, src_cols])
            src, src_cols = dst, slice(0, POOL_GROUP)
            step *= 2
            nbuf += 1
        wsum = src[base:base + tm, src_cols] + src[base - step:base - step + tm, src_cols]
        cnt = jnp.minimum(pos + 1, w).astype(F32)
        d = wsum * (1.0 / cnt) - u[:, cols]
        ya = jnp.dot(d.astype(pw_ref.dtype), pw_ref[g], preferred_element_type=F32)
        ya = ya * ps_ref[:, cols] * _silu(z_a[:, cols])
        y_scr[:, cols] = ya.astype(y_scr.dtype)

    b_gate = proj(2 * w_pool, w_conv)
    side.softmax()
    c_gate = proj(2 * w_pool + w_conv, w_conv)
    side.outputs()
    v_in = proj(2 * w_pool + 2 * w_conv, w_conv)
    v = c_gate * v_in
    vext[CONV_HALO:CONV_HALO + tm, :] = v
    conv = v * _conv_tap(cw_ref, CONV_WIDTH - 1, w_conv)
    for k in range(CONV_WIDTH - 1):
        shift = CONV_WIDTH - 1 - k
        conv = conv + vext[CONV_HALO - shift:CONV_HALO - shift + tm, :] * _conv_tap(cw_ref, k, w_conv)
    z_b = proj(2 * w_pool + 3 * w_conv, w_conv)
    y_b = b_gate * conv * _silu(z_b)
    y_scr[:, w_pool:w_pool + w_conv] = y_b.astype(y_scr.dtype)

    o_ref[0] = x + jnp.dot(y_scr[...], wout_ref[...], preferred_element_type=F32)
    pool_ref[0] = uext[base + tm - POOL_BUF:base + tm, :]
    conv_ref[0] = vext[CONV_HALO + tm - CONV_BUF:CONV_HALO + tm, :]
    uext[POOL_HALO:base, :] = uext[POOL_HALO + tm:base + tm, :]
    vext[0:CONV_HALO, :] = vext[tm:tm + CONV_HALO, :]


def _even_prompt(x, g, win, pw, ps, cw, wout, qh, knt, vnt, ckt, cvt, sink_col, *, tm):
    B, T, D = x.shape
    w_pool = ps.shape[-1]
    w_conv = (win.shape[1] - 2 * w_pool) // 4
    assert cw.shape == (CONV_WIDTH * w_conv // LANES, LANES)
    assert T % tm == 0 and tm % 16 == 0 and tm >= POOL_HALO
    nt = T // tm
    n, kvw, win_len = ckt.shape
    assert n % (B * nt) == 0 and kvw == KV_W == 2 * LANES and win_len == LANES and n == LANES
    bb = n // (B * nt)
    const = lambda *shape: pl.BlockSpec(shape, lambda b, t: (0,) * len(shape))
    cache_block = pl.BlockSpec((bb, kvw, win_len), lambda b, t: (b * nt + t, 0, 0))
    return pl.pallas_call(
        functools.partial(_even_prompt_kernel, tm=tm),
        grid=(B, nt),
        in_specs=[
            pl.BlockSpec((1, tm, D), lambda b, t: (b, t, 0)),
            const(*g.shape),
            const(*win.shape),
            const(*pw.shape),
            const(1, w_pool),
            const(*cw.shape),
            const(*wout.shape),
            const(*qh.shape),
            const(*knt.shape),
            const(*vnt.shape),
            cache_block,
            cache_block,
            const(*sink_col.shape),
        ],
        out_specs=[
            pl.BlockSpec((1, tm, D), lambda b, t: (b, t, 0)),
            pl.BlockSpec((1, POOL_BUF, w_pool), lambda b, t: (b, 0, 0)),
            pl.BlockSpec((1, CONV_BUF, w_conv), lambda b, t: (b, 0, 0)),
            cache_block,
            cache_block,
            const(n * N_HEADS, LANES),
            const(n * N_HEADS, LANES),
        ],
        out_shape=[
            jax.ShapeDtypeStruct((B, T, D), F32),
            jax.ShapeDtypeStruct((B, POOL_BUF, w_pool), F32),
            jax.ShapeDtypeStruct((B, CONV_BUF, w_conv), F32),
            jax.ShapeDtypeStruct((n, kvw, win_len), F32),
            jax.ShapeDtypeStruct((n, kvw, win_len), F32),
            jax.ShapeDtypeStruct((n * N_HEADS, LANES), F32),
            jax.ShapeDtypeStruct((n * N_HEADS, LANES), F32),
        ],
        scratch_shapes=[
            pltpu.VMEM((2 * POOL_HALO + tm, w_pool), F32),
            pltpu.VMEM((CONV_HALO + tm, w_conv), F32),
            pltpu.VMEM((2 * POOL_HALO + tm, POOL_GROUP), F32),
            pltpu.VMEM((2 * POOL_HALO + tm, POOL_GROUP), F32),
            pltpu.VMEM((tm, w_pool + w_conv), wout.dtype),
        ],
        compiler_params=pltpu.CompilerParams(
            dimension_semantics=("arbitrary", "arbitrary"), vmem_limit_bytes=VMEM_LIMIT_BYTES),
        name="even_prompt",
    )(x, g, win, pw, ps, cw, wout, qh, knt, vnt, ckt, cvt, sink_col)


def _odd_prompt_kernel(*refs, tm, nt):
    (x_ref, xres_ref, ng_ref, wkt_ref, wqvz_ref, wout_ref, small_ref, ropeb_ref, ropeo_ref, sink_ref,
     olo_ref, ohi_ref, sgate_ref, sx1_ref,
     o_ref, knew_ref, vnew_ref, sy_ref, qt_scr, kext, vt_ext, gate_scr, *rest) = refs
    g_ref = ng_ref.at[1:2]
    qg_ref, kg_ref = small_ref.at[_QG_T:_QG_T + HEAD_DIM], small_ref.at[_KG_T:_KG_T + HEAD_DIM]

    step = pl.program_id(0)

    @pl.when(step == 0)
    def _():
        _sample_out(olo_ref, ohi_ref, sgate_ref, sx1_ref, wout_ref, sy_ref)
        kext[...] = jnp.zeros(kext.shape, BF16)
        vt_ext[...] = jnp.zeros(vt_ext.shape, BF16)
        qt_scr[1] = jnp.zeros(qt_scr.shape[1:], BF16)
        gate_scr[1] = jnp.zeros(gate_scr.shape[1:], F32)

    def run(cur):
        _odd_prompt_step(x_ref, xres_ref, g_ref, wkt_ref, wqvz_ref, qg_ref, kg_ref, ropeb_ref, ropeo_ref, sink_ref,
                         wout_ref, o_ref, knew_ref, vnew_ref, qt_scr, kext, vt_ext, gate_scr, *rest,
                         tm=tm, nt=nt, cur=cur, prev=1 - cur)

    for parity in range(2):
        pl.when(step % 2 == parity)(lambda parity=parity: run(parity))


def _odd_prompt_step(x_ref, xres_ref, g_ref, wkt_ref, wqvz_ref, qg_ref, kg_ref, ropeb_ref, ropeo_ref, sink_ref,
                     wout_ref, o_ref, knew_ref, vnew_ref,
                     qt_scr, kext, vt_ext, gate_scr, yt_scr, s_scr, p_scr, esink_scr, *, tm, nt, cur, prev):
    step = pl.program_id(0)
    proj_t = step % nt
    attn_t = jnp.maximum(step - 1, 0) % nt
    nblk = tm // WINDOW
    half = HEAD_DIM // 2

    h = _rms(x_ref[0], g_ref[...]).astype(BF16)
    reps = tm // LANES
    cos_b = jnp.concatenate([ropeb_ref[0, 0:half, :]] * reps, axis=1)
    sin_b = jnp.concatenate([ropeb_ref[0, half:HEAD_DIM, :]] * reps, axis=1)
    cos_o, sin_o = ropeo_ref[0:half, :], ropeo_ref[half:HEAD_DIM, :]
    cos_t = cos_b * cos_o - sin_b * sin_o
    sin_t = sin_b * cos_o + cos_b * sin_o

    def norm_rope(xt, gain):
        ms = jnp.sum(xt * xt, axis=0, keepdims=True) * (1.0 / HEAD_DIM)
        xn = xt * lax.rsqrt(ms + RMS_EPS) * gain
        x1, x2 = xn[0:half, :], xn[half:HEAD_DIM, :]
        return jnp.concatenate([x1 * cos_t - x2 * sin_t, x2 * cos_t + x1 * sin_t], axis=0)

    has_past = proj_t > 0

    def proj_kv():
        kt = _nt_dot(wkt_ref[...], h)
        kgain = jnp.concatenate([kg_ref[...]] * reps, axis=1)
        kt = jnp.concatenate(
            [norm_rope(kt[kv * HEAD_DIM:(kv + 1) * HEAD_DIM, :], kgain) for kv in range(N_KV_HEADS)], axis=0)
        knew_ref[0] = kt[:, tm - WINDOW:tm]
        kext[cur, 0:WINDOW, :] = jnp.where(has_past, kext[prev, tm:tm + WINDOW, :], jnp.zeros((WINDOW, KV_W), BF16))
        kext[cur, WINDOW:WINDOW + tm, :] = kt.T.astype(BF16)
        vt = _nt_dot(wqvz_ref[ATTN_W:ATTN_W + KV_W, :], h)
        vnew_ref[0] = vt[:, tm - WINDOW:tm]
        vt_ext[cur, :, 0:WINDOW] = jnp.where(has_past, vt_ext[prev, :, tm:tm + WINDOW],
                                             jnp.zeros((KV_W, WINDOW), BF16))
        vt_ext[cur, :, WINDOW:WINDOW + tm] = vt.astype(BF16)

    def proj_gate(lo, hi):
        zt = _nt_dot(wqvz_ref[ATTN_W + KV_W + lo:ATTN_W + KV_W + hi, :], h)
        gate_scr[cur, lo:hi, :] = _silu(zt)

    q_raw = {}

    def proj_q(lo, hi):
        q_raw[lo] = _nt_dot(wqvz_ref[lo:hi, :], h)

    def finish_q():
        qgain = jnp.concatenate([qg_ref[...]] * reps, axis=1)
        for lo, qt in q_raw.items():
            for r0 in range(0, qt.shape[0], HEAD_DIM):
                qt_scr[cur, lo + r0:lo + r0 + HEAD_DIM, :] = (
                    norm_rope(qt[r0:r0 + HEAD_DIM, :], qgain) * (LOG2_E * HEAD_DIM ** -0.5)).astype(BF16)
        q_raw.clear()

    pieces = [proj_kv,
              lambda: proj_gate(0, ATTN_W // 2),
              lambda: proj_gate(ATTN_W // 2, ATTN_W),
              lambda: proj_q(0, ATTN_W // 2)]
    last_piece = lambda: proj_q(ATTN_W // 2, ATTN_W)

    ri = lax.broadcasted_iota(jnp.int32, (WINDOW, 2 * WINDOW), 0)
    qq = lax.broadcasted_iota(jnp.int32, (WINDOW, 2 * WINDOW), 1) % WINDOW
    from_prev = ri > qq
    keep_prev = from_prev.astype(BF16)
    keep_cur = 1.0 - keep_prev
    lane = lax.broadcasted_iota(jnp.int32, (1, 2 * WINDOW), 1)
    zeros = jnp.zeros((HEAD_DIM, 2 * WINDOW), BF16)
    ones = jnp.ones((BF16_SUBLANES, 2 * WINDOW), BF16)

    for i in range(nblk):
        c0 = i * WINDOW
        qcols = slice(c0, c0 + WINDOW)
        pairs = [(kv, kv * GROUP + 2 * pr) for kv in range(N_KV_HEADS) for pr in range(GROUP // 2)]
        for j, (kv, ha) in enumerate(pairs):
            chunk, pos = divmod(kv, LANES // HEAD_DIM)
            kblk = kext[prev, c0:c0 + 2 * WINDOW, chunk * LANES:(chunk + 1) * LANES]
            qpair = jnp.concatenate([qt_scr[prev, ha * HEAD_DIM:(ha + 1) * HEAD_DIM, qcols],
                                     qt_scr[prev, (ha + 1) * HEAD_DIM:(ha + 2) * HEAD_DIM, qcols]], axis=1)
            rhs = jnp.concatenate([qpair, zeros] if pos == 0 else [zeros, qpair], axis=0)
            s = jnp.dot(kblk, rhs, preferred_element_type=F32)
            s_prev = s[0:WINDOW, :]
            if i == 0:
                s_prev = jnp.where(attn_t == 0, -jnp.inf, s_prev)
            s_scr[j] = jnp.where(from_prev, s_prev, s[WINDOW:2 * WINDOW, :])
        for c, piece in enumerate(pieces):
            if c * nblk // len(pieces) == i:
                piece()
        for j, (kv, ha) in enumerate(pairs):
            s = s_scr[j]
            sink = jnp.where(lane < WINDOW, sink_ref[ha], sink_ref[ha + 1]) * LOG2_E
            m = jnp.maximum(jnp.max(s, axis=0, keepdims=True), sink)
            p_scr[j] = jnp.exp2(s - m).astype(BF16)
            esink_scr[j:j + 1, :] = jnp.exp2(sink - m)
        for j, (kv, ha) in enumerate(pairs):
            vg = vt_ext[prev, kv * HEAD_DIM:(kv + 1) * HEAD_DIM, c0:c0 + 2 * WINDOW]
            p = p_scr[j]
            p_keys = jnp.concatenate([p * keep_prev, p * keep_cur], axis=0)
            oa = jnp.dot(jnp.concatenate([vg, ones], axis=0), p_keys, preferred_element_type=F32)
            denom = oa[HEAD_DIM:HEAD_DIM + 1, :] + esink_scr[j:j + 1, :]
            o = oa[0:HEAD_DIM, :] * (1.0 / denom)
            for hd, part in ((ha, o[:, 0:WINDOW]), (ha + 1, o[:, WINDOW:2 * WINDOW])):
                rows = slice(hd * HEAD_DIM, (hd + 1) * HEAD_DIM)
                yt_scr[rows, qcols] = (part * gate_scr[prev, rows, qcols]).astype(BF16)

    out_t = jnp.dot(wout_ref[...], yt_scr[...], preferred_element_type=F32)
    finish_q()
    last_piece()
    finish_q()
    o_ref[0] = xres_ref[0] + out_t.T


def _odd_prompt(x, ng, wk_t, wqvz_t, wout_t, small, cos_t, sin_t, sinks, s_olo, s_ohi, s_gate, s_x1, *, tm):
    B, T, D = x.shape
    assert T % tm == 0 and tm % WINDOW == 0
    assert wk_t.shape == (KV_W, D) and wqvz_t.shape == (2 * ATTN_W + KV_W, D) and wout_t.shape == (D, ATTN_W)
    nt = T // tm
    half = HEAD_DIM // 2
    const = lambda *shape: pl.BlockSpec(shape, lambda s: (0,) * len(shape))
    n_tiles = B * nt
    proj_tile = lambda s: jnp.minimum(s, n_tiles - 1)
    attn_tile = lambda s: jnp.maximum(s - 1, 0)
    return pl.pallas_call(
        functools.partial(_odd_prompt_kernel, tm=tm, nt=nt),
        grid=(n_tiles + 1,),
        in_specs=[
            pl.BlockSpec((1, tm, D), lambda s: (proj_tile(s) // nt, proj_tile(s) % nt, 0)),
            pl.BlockSpec((1, tm, D), lambda s: (attn_tile(s) // nt, attn_tile(s) % nt, 0)),
            const(*ng.shape),
            const(*wk_t.shape),
            const(*wqvz_t.shape),
            const(*wout_t.shape),
            const(*small.shape),
            pl.BlockSpec((1, HEAD_DIM, LANES), lambda s: (proj_tile(s) % nt, 0, 0)),
            const(HEAD_DIM, tm),
            pl.BlockSpec(memory_space=pltpu.SMEM),
            const(*s_olo.shape),
            const(*s_ohi.shape),
            const(*s_gate.shape),
            const(*s_x1.shape),
        ],
        out_specs=[
            pl.BlockSpec((1, tm, D), lambda s: (attn_tile(s) // nt, attn_tile(s) % nt, 0)),
            pl.BlockSpec((1, KV_W, WINDOW), lambda s: (proj_tile(s) // nt, 0, 0)),
            pl.BlockSpec((1, KV_W, WINDOW), lambda s: (proj_tile(s) // nt, 0, 0)),
            const(s_x1.size // LANES, LANES),
        ],
        out_shape=[
            jax.ShapeDtypeStruct((B, T, D), F32),
            jax.ShapeDtypeStruct((B, KV_W, WINDOW), F32),
            jax.ShapeDtypeStruct((B, KV_W, WINDOW), F32),
            jax.ShapeDtypeStruct((s_x1.size // LANES, LANES), F32),
        ],
        scratch_shapes=[
            pltpu.VMEM((2, ATTN_W, tm), BF16),
            pltpu.VMEM((2, WINDOW + tm, KV_W), BF16),
            pltpu.VMEM((2, KV_W, WINDOW + tm), BF16),
            pltpu.VMEM((2, ATTN_W, tm), F32),
            pltpu.VMEM((ATTN_W, tm), BF16),
            pltpu.VMEM((N_HEADS // 2, WINDOW, 2 * WINDOW), F32),
            pltpu.VMEM((N_HEADS // 2, WINDOW, 2 * WINDOW), BF16),
            pltpu.VMEM((N_HEADS // 2, 2 * WINDOW), F32),
        ],
        compiler_params=pltpu.CompilerParams(
            dimension_semantics=("arbitrary",), vmem_limit_bytes=VMEM_LIMIT_BYTES),
        name="odd_prompt",
    )(x, x, ng, wk_t, wqvz_t, wout_t, small, cos_t, sin_t, sinks, s_olo, s_ohi, s_gate, s_x1)


def _dup_heads(x):
    lane = lax.broadcasted_iota(jnp.int32, x.shape, 1)
    rolled = pltpu.roll(x, HEAD_DIM, 1)
    first = lane < HEAD_DIM
    return jnp.where(first, x, rolled), jnp.where(first, rolled, x)


def _sample_dense_kernel(x_ref, sp_ref, sc_ref, ng_ref, win0_ref, pw_ref, ps_ref, cw_ref, wout0_ref,
                         win1_ref, wout1_ref, qgain_ref, kgain_ref, cos1_ref, sin1_ref,
                         x1_ref, pool_ref, conv_ref, qh_ref, kt_ref, vt_ref, gate_ref, small_ref,
                         wqvz_t_ref, wk_t_ref, wout1_t_ref, *scratch):
    hbm = _HbmOperands(sp_ref, win0_ref, wout0_ref, win1_ref, wout1_ref, pool_ref, *scratch)
    sp_ref, win0_ref, wout0_ref, win1_ref, wout1_ref = hbm.sp, hbm.win0, hbm.wout0, hbm.win1, hbm.wout1
    w_pool = ps_ref.shape[1]
    w_conv = (win0_ref.shape[1] - 2 * w_pool) // 4
    half = HEAD_DIM // 2
    qg_t, cos_t1, sin_t1 = _col_table(qgain_ref[...]), _col_table(cos1_ref[...]), _col_table(sin1_ref[...])
    small_ref[_QG_T:_QG_T + HEAD_DIM, :] = qg_t
    small_ref[_KG_T:_KG_T + HEAD_DIM, :] = _col_table(kgain_ref[...])
    small_ref[_COS_T:_COS_T + half, :] = cos_t1
    small_ref[_SIN_T:_SIN_T + half, :] = sin_t1
    second_half = lax.broadcasted_iota(jnp.int32, (1, LANES), 1) % HEAD_DIM >= half
    small_ref[_QG_ROW:_SMALL_ROWS, :] = jnp.concatenate(
        [_tiled_row(qg_t), _tiled_row(cos_t1), jnp.where(second_half, 1.0, -1.0) * _tiled_row(sin_t1),
         jnp.zeros((_SMALL_ROWS - _SIN_ROW - 1, LANES), F32)], axis=0)
    g0_ref, g1_ref = ng_ref.at[0:1], ng_ref.at[1:2]
    qg_ref, kgt_ref = small_ref.at[_QG_ROW:_QG_ROW + 1], small_ref.at[_KG_T:_KG_T + HEAD_DIM]
    cos_ref, sin_ref = small_ref.at[_COS_ROW:_COS_ROW + 1], small_ref.at[_SIN_ROW:_SIN_ROW + 1]
    cost_ref, sint_ref = small_ref.at[_COS_T:_COS_T + HEAD_DIM // 2], small_ref.at[_SIN_T:_SIN_T + HEAD_DIM // 2]

    n = x1_ref.shape[0]
    chunks = x1_ref.shape[1] // LANES
    x = jnp.concatenate([x_ref[pl.ds(c, n, stride=chunks), :] for c in range(chunks)], axis=1)
    h = _rms(x, g0_ref[...]).astype(win0_ref.dtype)

    def proj(c, width):
        return jnp.dot(h, win0_ref[:, c:c + width], preferred_element_type=F32)

    hbm.wait_win0()
    u = proj(0, w_pool)
    hbm.put_newest_pool_row(u)
    z_a = proj(w_pool, w_pool)
    hbm.wait_sp()
    ys = []
    for g, w in enumerate(POOL_WINDOWS):
        cols = slice(g * POOL_GROUP, (g + 1) * POOL_GROUP)
        wsum = u[:, cols]
        for r in range(POOL_BUF - (w - 1), POOL_BUF):
            wsum = wsum + sp_ref[r, :, cols]
        cnt = float(min(PAST_LEN + 1, w))
        d = wsum * (1.0 / cnt) - u[:, cols]
        ya = jnp.dot(d.astype(pw_ref.dtype), pw_ref[g], preferred_element_type=F32)
        ys.append((ya * ps_ref[:, cols] * _silu(z_a[:, cols])).astype(wout0_ref.dtype))

    b_gate = proj(2 * w_pool, w_conv)
    c_gate = proj(2 * w_pool + w_conv, w_conv)
    v_in = proj(2 * w_pool + 2 * w_conv, w_conv)
    v = c_gate * v_in
    cchunks = w_conv // LANES
    cstride = cchunks * CONV_BUF

    def conv_row(r):
        return jnp.concatenate(
            [sc_ref[pl.ds(j * CONV_BUF + r, n, stride=cstride), :] for j in range(cchunks)], axis=1)

    def put_conv_row(r, val):
        for j in range(cchunks):
            conv_ref[pl.ds(j * CONV_BUF + r, n, stride=cstride), :] = val[:, j * LANES:(j + 1) * LANES]

    conv = v * _conv_tap(cw_ref, CONV_WIDTH - 1, w_conv)
    for k in range(CONV_BUF):
        conv = conv + conv_row(k) * _conv_tap(cw_ref, k, w_conv)
    for r in range(CONV_BUF - 1):
        put_conv_row(r, conv_row(r + 1))
    put_conv_row(CONV_BUF - 1, v)
    z_b = proj(2 * w_pool + 3 * w_conv, w_conv)
    ys.append((b_gate * conv * _silu(z_b)).astype(wout0_ref.dtype))
    y = jnp.concatenate(ys, axis=1)
    hbm.wait_wout0()
    x1 = x + jnp.dot(y, wout0_ref[...], preferred_element_type=F32)
    x1_ref[...] = x1

    h1 = _rms(x1, g1_ref[...])
    hbm.wait_win1()

    def proj1(lo, width):
        return jnp.dot(h1, win1_ref[:, lo:lo + width], preferred_element_type=F32)

    q = proj1(0, ATTN_W)
    for c in range(ATTN_W // LANES):
        qc = _head_norm_rope(q[:, c * LANES:(c + 1) * LANES], qg_ref[...], cos_ref[...], sin_ref[...])
        d0, d1 = _dup_heads(qc * (HEAD_DIM ** -0.5))
        qh_ref[2 * c * n:(2 * c + 1) * n, :] = d0
        qh_ref[(2 * c + 1) * n:(2 * c + 2) * n, :] = d1
    kt = proj1(ATTN_W, KV_W).T
    cos_t = cost_ref[...]
    sin_t = sint_ref[...]
    for kv in range(N_KV_HEADS):
        kh = kt[kv * HEAD_DIM:(kv + 1) * HEAD_DIM, :]
        ms = jnp.sum(kh * kh, axis=0, keepdims=True) * (1.0 / HEAD_DIM)
        kn = kh * lax.rsqrt(ms + RMS_EPS) * kgt_ref[...]
        x1h, x2h = kn[0:half, :], kn[half:HEAD_DIM, :]
        kt_ref[kv * HEAD_DIM:(kv + 1) * HEAD_DIM, :] = jnp.concatenate(
            [x1h * cos_t - x2h * sin_t, x2h * cos_t + x1h * sin_t], axis=0)
    vt_ref[...] = proj1(ATTN_W + KV_W, KV_W).T
    gate_ref[...] = _silu(proj1(ATTN_W + 2 * KV_W, ATTN_W))

    blk = 2 * LANES
    for src, dst, width in ((0, 0, ATTN_W), (ATTN_W + KV_W, ATTN_W, KV_W + ATTN_W)):
        for c in range(0, width, blk):
            wqvz_t_ref[dst + c:dst + c + blk, :] = win1_ref[:, src + c:src + c + blk].T.astype(BF16)
    wk_t_ref[...] = win1_ref[:, ATTN_W:ATTN_W + KV_W].T.astype(BF16)
    hbm.wait_wout1()
    for c in range(0, wout1_ref.shape[1], blk):
        wout1_t_ref[c:c + blk, :] = wout1_ref[:, c:c + blk].T.astype(BF16)
    hbm.finish()


class _HbmOperands:
    def __init__(self, sp_hbm, win0_hbm, wout0_hbm, win1_hbm, wout1_hbm, pool_hbm,
                 sp, win0, wout0, win1, wout1, newest, sems):
        self.sp, self.win0, self.wout0, self.win1, self.wout1, self.newest = sp, win0, wout0, win1, wout1, newest
        self.pool_hbm = pool_hbm
        self.sems = sems
        rows = sp_hbm.shape[0]
        self.shift = pltpu.make_async_copy(sp.at[1:rows], pool_hbm.at[0:rows - 1], sems.at[0])
        self.fetch = {name: pltpu.make_async_copy(src, dst, sems.at[i + 2]) for i, (name, src, dst) in enumerate(
            (("win0", win0_hbm, win0), ("sp", sp_hbm, sp), ("wout0", wout0_hbm, wout0), ("win1", win1_hbm, win1),
             ("wout1", wout1_hbm, wout1)))}
        for copy in self.fetch.values():
            copy.start()

    def wait_wout1(self):
        self.fetch["wout1"].wait()

    def wait_win0(self):
        self.fetch["win0"].wait()

    def wait_sp(self):
        self.fetch["sp"].wait()
        self.shift.start()

    def wait_wout0(self):
        self.fetch["wout0"].wait()

    def wait_win1(self):
        self.fetch["win1"].wait()

    def put_newest_pool_row(self, u):
        rows = self.pool_hbm.shape[0]
        self.newest[...] = u
        self.put = pltpu.make_async_copy(self.newest, self.pool_hbm.at[rows - 1], self.sems.at[1])
        self.put.start()

    def finish(self):
        self.shift.wait()
        self.put.wait()


def _sample_dense(x, sp, sc, ng, win0, pw, ps, cw, wout0, win1, wout1, qgain, kgain, cos1, sin1):
    D = win0.shape[0]
    n = x.shape[0] * LANES // D
    assert n == LANES
    assert qgain.shape == kgain.shape == (1, HEAD_DIM) and cos1.shape == sin1.shape == (1, HEAD_DIM // 2)
    assert win1.shape == (D, 2 * ATTN_W + 2 * KV_W) and wout1.shape == (ATTN_W, D)
    vmem = pl.BlockSpec(memory_space=pltpu.VMEM)
    hbm = pl.BlockSpec(memory_space=pl.ANY)
    in_specs = [vmem, hbm, vmem, vmem, hbm, vmem, vmem, vmem, hbm, hbm, hbm] + [vmem] * 4
    return pl.pallas_call(
        _sample_dense_kernel,
        in_specs=in_specs,
        out_specs=[vmem, hbm] + [vmem] * 9,
        scratch_shapes=[
            pltpu.VMEM(sp.shape, F32),
            pltpu.VMEM(win0.shape, F32),
            pltpu.VMEM(wout0.shape, F32),
            pltpu.VMEM(win1.shape, F32),
            pltpu.VMEM(wout1.shape, F32),
            pltpu.VMEM(sp.shape[1:], F32),
            pltpu.SemaphoreType.DMA((7,)),
        ],
        out_shape=[
            jax.ShapeDtypeStruct((n, D), F32),
            jax.ShapeDtypeStruct(sp.shape, F32),
            jax.ShapeDtypeStruct(sc.shape, F32),
            jax.ShapeDtypeStruct((N_HEADS * n, LANES), F32),
            jax.ShapeDtypeStruct((KV_W, n), F32),
            jax.ShapeDtypeStruct((KV_W, n), F32),
            jax.ShapeDtypeStruct((n, ATTN_W), F32),
            jax.ShapeDtypeStruct((_SMALL_ROWS, LANES), F32),
            jax.ShapeDtypeStruct((2 * ATTN_W + KV_W, D), BF16),
            jax.ShapeDtypeStruct((KV_W, D), BF16),
            jax.ShapeDtypeStruct((D, ATTN_W), BF16),
        ],
        compiler_params=pltpu.CompilerParams(vmem_limit_bytes=VMEM_LIMIT_BYTES),
        name="sample_dense",
    )(x, sp, sc, ng, win0, pw, ps, cw, wout0, win1, wout1, qgain, kgain, cos1, sin1)


def _sample_out(olo_ref, ohi_ref, gate_ref, x1_ref, wout_ref, y_ref):
    n = x1_ref.shape[0]
    blocks = LANES // HEAD_DIM
    lane_n = lax.broadcasted_iota(jnp.int32, (n, LANES), 1)
    chunks = []
    for c in range(ATTN_W // LANES):
        parts = []
        for hd in (2 * c, 2 * c + 1):
            kv = hd // GROUP
            slab = (olo_ref, ohi_ref)[kv // blocks]
            a = slab[pl.ds(hd, n, stride=N_HEADS), :]
            parts.append(a if kv % blocks == hd % blocks else pltpu.roll(a, HEAD_DIM, 1))
        chunks.append(jnp.where(lane_n < HEAD_DIM, parts[0], parts[1]))
    y = jnp.concatenate(chunks, axis=1) * gate_ref[...]
    out = x1_ref[...] + _nt_dot(y.astype(wout_ref.dtype), wout_ref[...])
    chunks = out.shape[1] // LANES
    for c in range(chunks):
        y_ref[pl.ds(c, n, stride=chunks), :] = out[:, c * LANES:(c + 1) * LANES]


def _rope_tables(pos):
    half = HEAD_DIM // 2
    inv = ROPE_THETA ** (-jnp.arange(half, dtype=F32) / half)
    ang = pos.astype(F32)[:, None] * inv[None, :]
    return jnp.cos(ang), jnp.sin(ang)


def kernel(x_prompt, x_sample, state_pool, state_conv, cache_k, cache_v, norm_g, w_in_even, pool_w, pool_scale,
           conv_w, w_out_even, w_in_odd, q_norm_g, k_norm_g, attn_sinks, w_out_odd):
    B, T, D = x_prompt.shape
    n_s, t_s, _ = x_sample.shape
    assert norm_g.shape[0] == 2 and w_in_even.shape[0] == 1 and w_in_odd.shape[0] == 1
    assert t_s == 1 and cache_k.shape[2] == WINDOW and T >= WINDOW
    assert cache_k.shape[3] * cache_k.shape[4] == KV_W and pool_w.shape[1:] == (len(POOL_WINDOWS), POOL_GROUP, POOL_GROUP)

    win0 = w_in_even[0]
    wout0 = w_out_even[0]
    win1 = w_in_odd[0]
    wout1 = w_out_odd[0]
    pw = pool_w[0]
    ps = pool_scale[0][None, :]
    cw = conv_w[0].reshape(-1, LANES)
    sinks = attn_sinks[0]

    cos_b, sin_b = _rope_tables(jnp.arange(T // ODD_TILE) * ODD_TILE)
    rope_base = jnp.broadcast_to(jnp.concatenate([cos_b, sin_b], axis=1)[:, :, None],
                                 (T // ODD_TILE, HEAD_DIM, LANES))
    cos_o, sin_o = _rope_tables(jnp.arange(ODD_TILE))
    rope_off = jnp.concatenate([cos_o, sin_o], axis=1).T
    cos_1, sin_1 = _rope_tables(PAST_LEN + jnp.arange(t_s))

    w_conv = state_conv.shape[-1]
    sp = jnp.transpose(state_pool[0], (1, 0, 2))
    sc = jnp.transpose(state_conv[0].reshape(n_s, CONV_BUF, w_conv // LANES, LANES), (0, 2, 1, 3)).reshape(-1, LANES)
    x1s, pool_s, conv_s, qh_s, kt_s, vt_s, gate_s, small, wqvz_t, wk_t, wout1_t = _sample_dense(
        x_sample.reshape(n_s * D // LANES, LANES), sp, sc, norm_g, win0, pw, ps, cw, wout0, win1, wout1,
        q_norm_g, k_norm_g, cos_1, sin_1)
    pool_s = jnp.transpose(pool_s, (1, 0, 2))
    conv_s = jnp.transpose(conv_s.reshape(n_s, w_conv // LANES, CONV_BUF, LANES), (0, 2, 1, 3)).reshape(
        n_s, CONV_BUF, w_conv)

    ckt = jnp.transpose(cache_k[0], (0, 2, 3, 1)).reshape(n_s, KV_W, WINDOW)
    cvt = jnp.transpose(cache_v[0], (0, 2, 3, 1)).reshape(n_s, KV_W, WINDOW)
    x1p, pool_p, conv_p, nkt_s, nvt_s, olo_s, ohi_s = _even_prompt(
        x_prompt, norm_g, win0, pw, ps, cw, wout0, qh_s, kt_s, vt_s, ckt, cvt, attn_sinks, tm=EVEN_TILE)
    y_p, k_p, v_p, y_s = _odd_prompt(x1p, norm_g, wk_t, wqvz_t, wout1_t, small, rope_base, rope_off, sinks,
                                     olo_s, ohi_s, gate_s, x1s, tm=ODD_TILE)

    def window_major(a):
        return jnp.transpose(a.reshape(-1, N_KV_HEADS, HEAD_DIM, WINDOW), (0, 3, 1, 2))[None]

    return (y_p, y_s.reshape(n_s, 1, D), pool_p[None], pool_s[None], conv_p[None], conv_s[None],
            window_major(k_p), window_major(v_p), window_major(nkt_s), window_major(nvt_s))
```
